```python
import jax, jax.numpy as jnp
from jax import lax
import numpy as np

D_MODEL = 1024
BATCH = 4
SEQ = 4096
DEPTH = 2

PLE_DIM = 256
R_HEADS = 4
R_DK = 64
R_DV = 128
R_CHUNK = 128
ROPE_BASE = 10000.0
M_HEADS = 4
M_DK = 64
M_DV = 128
M_CHUNK = 64
M_CONV = 4
AB_MIX = R_HEADS * R_DV + M_HEADS * M_DV
AB_SIZES = (R_HEADS * R_DK, R_HEADS * R_DK, R_HEADS * R_DV, R_HEADS * R_DV,
            M_HEADS * M_DK, M_HEADS * M_DK, M_HEADS * M_DV, M_HEADS * M_DV,
            M_HEADS, M_HEADS)
AB_COLS = sum(AB_SIZES)
N_HEADS = 16
N_KV_GROUPS = 2
HEAD_DIM = 64
CMP_BLOCK = 32
CMP_STRIDE = 16
CMP_HIDDEN = 256
SLC_BLOCK = 64
N_SELECT = 16
WINDOW = 512
Q_BLOCK = 128
NSA_SIZES = (N_HEADS * HEAD_DIM,) + (N_KV_GROUPS * HEAD_DIM,) * 6 + (N_HEADS * 3,)
NSA_COLS = sum(NSA_SIZES)
D_FF = 2816
FFN_CONV = 3

NEG = -1e30
BIG = 1e30
EPS = 1e-6

kernel_name = 'hybrid_retnet_mlstm_nsa_convffn'


def rms_norm(x, g):
    xf = x.astype(jnp.float32)
    y = xf * lax.rsqrt(jnp.mean(xf * xf, axis=-1, keepdims=True) + EPS)
    return (y * g.astype(jnp.float32)).astype(x.dtype)


def split_cols(z, sizes):
    return jnp.split(z, np.cumsum(sizes)[:-1].tolist(), axis=-1)


def causal_dwconv(x, w, b):
    K = w.shape[0]
    T = x.shape[1]
    xp = jnp.pad(x, ((0, 0), (K - 1, 0), (0, 0)))
    y = b
    for j in range(K):
        y = y + xp[:, j:j + T] * w[j]
    return y


def to_chunks(x, L):
    B, T = x.shape[:2]
    x = x.reshape((B, T // L, L) + x.shape[2:])
    return jnp.moveaxis(x, (1, 2), (0, 3))


def from_chunks(y):
    y = jnp.moveaxis(y, (0, 3), (1, 2))
    return y.reshape((y.shape[0], y.shape[1] * y.shape[2]) + y.shape[3:])


def rotary(x, pos):
    half = x.shape[-1] // 2
    inv = ROPE_BASE ** (-jnp.arange(half, dtype=jnp.float32) / half)
    ang = pos[:, None] * inv[None, :]
    cos = jnp.cos(ang)[:, None, :]
    sin = jnp.sin(ang)[:, None, :]
    x1, x2 = x[..., :half], x[..., half:]
    return jnp.concatenate([x1 * cos - x2 * sin, x1 * sin + x2 * cos], axis=-1)


def retention_chunkwise(q, k, v):
    B, T, H, DK = q.shape
    DV = v.shape[-1]
    L = R_CHUNK
    log_g = jnp.log1p(-jnp.exp2(-5.0 - jnp.arange(H, dtype=jnp.float32)))
    idx = jnp.arange(L, dtype=jnp.float32)
    diff = idx[:, None] - idx[None, :]
    causal = diff >= 0
    dmask = jnp.where(causal, jnp.exp(jnp.where(causal, diff, 0.0)[None] * log_g[:, None, None]), 0.0)
    q_dec = jnp.exp((idx + 1.0)[None, :] * log_g[:, None])
    k_dec = jnp.exp((L - 1.0 - idx)[None, :] * log_g[:, None])
    c_dec = jnp.exp(L * log_g)

    def step(R, xs):
        qc, kc, vc = xs
        s = jnp.einsum('bhld,bhmd->bhlm', qc, kc) * dmask
        o = (jnp.einsum('bhlm,bhme->bhle', s, vc)
             + jnp.einsum('bhld,bhde->bhle', qc, R) * q_dec[..., None])
        R = c_dec[:, None, None] * R + jnp.einsum('bhmd,bhme->bhde', kc * k_dec[..., None], vc)
        return R, o

    R0 = jnp.zeros((B, H, DK, DV), jnp.float32)
    _, o = lax.scan(step, R0, (to_chunks(q, L), to_chunks(k, L), to_chunks(v, L)))
    return from_chunks(o)


def mlstm_chunkwise(q, k, v, ig, lf):
    B, T, H, DK = q.shape
    DV = v.shape[-1]
    L = M_CHUNK
    causal = jnp.tril(jnp.ones((L, L), dtype=bool))

    def step(carry, xs):
        C, n, m = carry
        qc, kc, vc, ic, fc = xs
        b = jnp.cumsum(fc, axis=-1)
        dlog = jnp.where(causal, b[..., :, None] - b[..., None, :] + ic[..., None, :], NEG)
        inter = b + m[..., None]
        m_t = jnp.maximum(inter, jnp.max(dlog, axis=-1))
        s = jnp.einsum('bhld,bhsd->bhls', qc, kc) * jnp.exp(dlog - m_t[..., None])
        w_inter = jnp.exp(inter - m_t)
        num = (jnp.einsum('bhls,bhse->bhle', s, vc)
               + w_inter[..., None] * jnp.einsum('bhld,bhde->bhle', qc, C))
        den = jnp.sum(s, axis=-1) + w_inter * jnp.einsum('bhld,bhd->bhl', qc, n)
        h = num / jnp.maximum(jnp.abs(den), jnp.exp(-m_t))[..., None]
        b_last = b[..., -1]
        wlog = b_last[..., None] - b + ic
        m_new = jnp.maximum(b_last + m, jnp.max(wlog, axis=-1))
        decay = jnp.exp(b_last + m - m_new)
        wk = kc * jnp.exp(wlog - m_new[..., None])[..., None]
        C = decay[..., None, None] * C + jnp.einsum('bhsd,bhse->bhde', wk, vc)
        n = decay[..., None] * n + jnp.sum(wk, axis=2)
        return (C, n, m_new), h

    init = (jnp.zeros((B, H, DK, DV), jnp.float32), jnp.zeros((B, H, DK), jnp.float32),
            jnp.zeros((B, H), jnp.float32))
    xs = (to_chunks(q, L), to_chunks(k, L), to_chunks(v, L), to_chunks(ig, L), to_chunks(lf, L))
    _, h = lax.scan(step, init, xs)
    return from_chunks(h)


def ab_mixer(h, w_in, conv_w, conv_b, ret_g, ig_b, fg_b, m_g, w_out):
    B, T, _ = h.shape
    z = (h @ w_in).astype(jnp.float32)
    rq, rk, rv, rg, mq, mk, mv, mo, mi, mf = split_cols(z, AB_SIZES)
    pos = jnp.arange(T, dtype=jnp.float32)
    rq = rotary(rq.reshape(B, T, R_HEADS, R_DK), pos)
    rk = rotary(rk.reshape(B, T, R_HEADS, R_DK), pos) * (R_DK ** -0.5)
    ret = retention_chunkwise(rq, rk, rv.reshape(B, T, R_HEADS, R_DV))
    ret = rms_norm(ret, ret_g.reshape(R_HEADS, R_DV)).reshape(B, T, -1) * jax.nn.silu(rg)
    mqk = jax.nn.silu(causal_dwconv(jnp.concatenate([mq, mk], axis=-1), conv_w, conv_b))
    mq, mk = jnp.split(mqk, 2, axis=-1)
    hm = mlstm_chunkwise(mq.reshape(B, T, M_HEADS, M_DK) * (M_DK ** -0.5),
                         mk.reshape(B, T, M_HEADS, M_DK),
                         mv.reshape(B, T, M_HEADS, M_DV),
                         mi + ig_b,
                         jax.nn.log_sigmoid(mf + fg_b))
    mlstm = jax.nn.sigmoid(mo) * rms_norm(hm, m_g.reshape(M_HEADS, M_DV)).reshape(B, T, -1)
    y = jnp.concatenate([ret, mlstm], axis=-1) @ w_out
    return y.astype(h.dtype)


def nsa_mixer(h, w_in, q_g, k_g, pos_k, pos_v, w1k, w2k, w1v, w2v, gate_b, w_out):
    B, T, _ = h.shape
    G, HG, HD = N_KV_GROUPS, N_HEADS // N_KV_GROUPS, HEAD_DIM
    z = (h @ w_in).astype(jnp.float32)
    q, kc, vc, ks, vs, kw, vw, gt = split_cols(z, NSA_SIZES)
    q = rms_norm(q.reshape(B, T, G, HG, HD), q_g).transpose(0, 2, 3, 1, 4)

    def kv(t):
        return t.reshape(B, T, G, HD)

    nc = (T - CMP_BLOCK) // CMP_STRIDE + 1
    c_start = jnp.arange(nc) * CMP_STRIDE
    cidx = c_start[:, None] + jnp.arange(CMP_BLOCK)[None, :]

    def compress(t, pos, w1, w2):
        blk = kv(t)[:, cidx] + pos[:, None, :]
        blk = blk.transpose(0, 3, 1, 2, 4).reshape(B, G, nc, CMP_BLOCK * HD)
        return jax.nn.gelu(blk @ w1) @ w2

    kc = rms_norm(compress(kc, pos_k, w1k, w2k), k_g[0])
    vc = compress(vc, pos_v, w1v, w2v)
    ks = rms_norm(kv(ks), k_g[1]).transpose(0, 2, 1, 3)
    vs = kv(vs).transpose(0, 2, 1, 3)
    kw = rms_norm(kv(kw), k_g[2]).transpose(0, 2, 1, 3)
    vw = kv(vw).transpose(0, 2, 1, 3)
    ns = T // SLC_BLOCK
    n_sel = min(N_SELECT, ns)
    ks_blk = ks.reshape(B, G, ns, SLC_BLOCK, HD)
    vs_blk = vs.reshape(B, G, ns, SLC_BLOCK, HD)
    pad = ((0, 0), (0, 0), (WINDOW, 0), (0, 0))
    kw_pad = jnp.pad(kw, pad)
    vw_pad = jnp.pad(vw, pad)
    gates = jax.nn.sigmoid(gt.reshape(B, T, G, HG, 3) + gate_b.reshape(G, HG, 3)).transpose(0, 2, 3, 1, 4)
    cmp_end = c_start + CMP_BLOCK - 1
    sj = jnp.arange(ns)
    overlap = ((c_start[:, None] < (sj[None, :] + 1) * SLC_BLOCK)
               & (c_start[:, None] + CMP_BLOCK > sj[None, :] * SLC_BLOCK)).astype(jnp.float32)
    scale = HD ** -0.5
    bi = jnp.arange(B)[:, None, None, None]
    gi = jnp.arange(G)[None, :, None, None]

    def query_block(qb):
        t0 = qb * Q_BLOCK
        tpos = t0 + jnp.arange(Q_BLOCK)
        qq = lax.dynamic_slice_in_dim(q, t0, Q_BLOCK, axis=3)
        s = jnp.einsum('bghqd,bgnd->bghqn', qq, kc) * scale
        cvalid = cmp_end[None, :] <= tpos[:, None]
        p_c = jnp.where(cvalid, jax.nn.softmax(jnp.where(cvalid, s, NEG), axis=-1), 0.0)
        o_cmp = jnp.einsum('bghqn,bgnd->bghqd', p_c, vc)
        imp = jnp.einsum('bghqn,ns->bgqs', p_c, overlap)
        cur = tpos // SLC_BLOCK
        forced = (sj[None, :] == 0) | (sj[None, :] == cur[:, None]) | (sj[None, :] == cur[:, None] - 1)
        bvalid = sj[None, :] <= cur[:, None]
        score = jnp.where(forced, BIG, jnp.where(bvalid, imp, NEG))
        _, sel = lax.top_k(score, n_sel)
        kg = ks_blk[bi, gi, sel]
        vg = vs_blk[bi, gi, sel]
        kpos = sel[..., None] * SLC_BLOCK + jnp.arange(SLC_BLOCK)
        smask = (kpos <= tpos[:, None, None])[:, :, None]
        s = jnp.einsum('bghqd,bgqnkd->bghqnk', qq, kg) * scale
        s = jnp.where(smask, s, NEG).reshape(B, G, HG, Q_BLOCK, n_sel * SLC_BLOCK)
        p_s = jax.nn.softmax(s, axis=-1).reshape(B, G, HG, Q_BLOCK, n_sel, SLC_BLOCK)
        o_slc = jnp.einsum('bghqnk,bgqnkd->bghqd', p_s, vg)
        kwb = lax.dynamic_slice_in_dim(kw_pad, t0, WINDOW + Q_BLOCK, axis=2)
        vwb = lax.dynamic_slice_in_dim(vw_pad, t0, WINDOW + Q_BLOCK, axis=2)
        wpos = t0 - WINDOW + jnp.arange(WINDOW + Q_BLOCK)
        wmask = ((wpos[None, :] <= tpos[:, None]) & (wpos[None, :] > tpos[:, None] - WINDOW)
                 & (wpos[None, :] >= 0))
        s = jnp.einsum('bghqd,bgkd->bghqk', qq, kwb) * scale
        p_w = jax.nn.softmax(jnp.where(wmask, s, NEG), axis=-1)
        o_win = jnp.einsum('bghqk,bgkd->bghqd', p_w, vwb)
        g = lax.dynamic_slice_in_dim(gates, t0, Q_BLOCK, axis=3)
        return g[..., 0:1] * o_cmp + g[..., 1:2] * o_slc + g[..., 2:3] * o_win

    o = lax.map(query_block, jnp.arange(T // Q_BLOCK))
    o = o.transpose(1, 0, 4, 2, 3, 5).reshape(B, T, N_HEADS * HD)
    return (o @ w_out).astype(h.dtype)


def conv_ffn(h, w_up, conv_w, conv_b, w_down):
    a, b = jnp.split(h @ w_up, 2, axis=-1)
    a = causal_dwconv(a, conv_w, conv_b)
    return (jax.nn.gelu(a) * b) @ w_down


def setup_inputs(seed: int = 0) -> dict:
    key = jax.random.key(seed)
    keys = iter(jax.random.split(key, 64))
    f32 = jnp.float32
    NE = (DEPTH + 1) // 2
    NO = DEPTH // 2

    def nrm(shape, scale):
        return scale * jax.random.normal(next(keys), shape, f32)

    def gain(shape):
        return 1.0 + nrm(shape, 0.02)

    return {
        'x': nrm((BATCH, SEQ, D_MODEL), 1.0),
        'p': nrm((DEPTH, BATCH, SEQ, PLE_DIM), 1.0),
        'ab_norm_g': gain((NE, D_MODEL)),
        'ab_w_in': nrm((NE, D_MODEL, AB_COLS), D_MODEL ** -0.5),
        'ab_conv_w': nrm((NE, M_CONV, 2 * M_HEADS * M_DK), M_CONV ** -0.5),
        'ab_conv_b': nrm((NE, 2 * M_HEADS * M_DK), 0.02),
        'ab_ret_norm_g': gain((NE, R_HEADS * R_DV)),
        'ab_ig_b': nrm((NE, M_HEADS), 0.1),
        'ab_fg_b': jnp.linspace(3.0, 6.0, M_HEADS, dtype=f32)[None, :] + nrm((NE, M_HEADS), 0.1),
        'ab_m_norm_g': gain((NE, M_HEADS * M_DV)),
        'ab_w_out': nrm((NE, AB_MIX, D_MODEL), AB_MIX ** -0.5),
        'nsa_norm_g': gain((NO, D_MODEL)),
        'nsa_w_in': nrm((NO, D_MODEL, NSA_COLS), D_MODEL ** -0.5),
        'nsa_q_norm_g': gain((NO, HEAD_DIM)),
        'nsa_k_norm_g': gain((NO, 3, HEAD_DIM)),
        'nsa_cmp_pos_k': nrm((NO, CMP_BLOCK, HEAD_DIM), 0.1),
        'nsa_cmp_pos_v': nrm((NO, CMP_BLOCK, HEAD_DIM), 0.1),
        'nsa_cmp_w1k': nrm((NO, CMP_BLOCK * HEAD_DIM, CMP_HIDDEN), (CMP_BLOCK * HEAD_DIM) ** -0.5),
        'nsa_cmp_w2k': nrm((NO, CMP_HIDDEN, HEAD_DIM), CMP_HIDDEN ** -0.5),
        'nsa_cmp_w1v': nrm((NO, CMP_BLOCK * HEAD_DIM, CMP_HIDDEN), (CMP_BLOCK * HEAD_DIM) ** -0.5),
        'nsa_cmp_w2v': nrm((NO, CMP_HIDDEN, HEAD_DIM), CMP_HIDDEN ** -0.5),
        'nsa_gate_b': nrm((NO, N_HEADS * 3), 0.1),
        'nsa_w_out': nrm((NO, N_HEADS * HEAD_DIM, D_MODEL), (N_HEADS * HEAD_DIM) ** -0.5),
        'ffn_norm_g': gain((DEPTH, D_MODEL)),
        'ffn_w_up': nrm((DEPTH, D_MODEL, 2 * D_FF), D_MODEL ** -0.5),
        'ffn_conv_w': nrm((DEPTH, FFN_CONV, D_FF), FFN_CONV ** -0.5),
        'ffn_conv_b': nrm((DEPTH, D_FF), 0.02),
        'ffn_w_down': nrm((DEPTH, D_FF, D_MODEL), D_FF ** -0.5),
        'ple_w': nrm((DEPTH, PLE_DIM, D_MODEL), PLE_DIM ** -0.5),
        'ple_norm_g': gain((DEPTH, D_MODEL)),
        'ple_gate_norm_g': gain((DEPTH, D_MODEL)),
        'ple_w_gate': nrm((DEPTH, D_MODEL, D_MODEL), D_MODEL ** -0.5),
    }


def reference(x, p, ab_norm_g, ab_w_in, ab_conv_w, ab_conv_b, ab_ret_norm_g, ab_ig_b, ab_fg_b,
              ab_m_norm_g, ab_w_out, nsa_norm_g, nsa_w_in, nsa_q_norm_g, nsa_k_norm_g,
              nsa_cmp_pos_k, nsa_cmp_pos_v, nsa_cmp_w1k, nsa_cmp_w2k, nsa_cmp_w1v, nsa_cmp_w2v,
              nsa_gate_b, nsa_w_out, ffn_norm_g, ffn_w_up, ffn_conv_w, ffn_conv_b, ffn_w_down,
              ple_w, ple_norm_g, ple_gate_norm_g, ple_w_gate):
    h = x
    for i in range(DEPTH):
        j = i // 2
        if i % 2 == 0:
            h = h + ab_mixer(rms_norm(h, ab_norm_g[j]), ab_w_in[j], ab_conv_w[j], ab_conv_b[j],
                             ab_ret_norm_g[j], ab_ig_b[j], ab_fg_b[j], ab_m_norm_g[j], ab_w_out[j])
        else:
            h = h + nsa_mixer(rms_norm(h, nsa_norm_g[j]), nsa_w_in[j], nsa_q_norm_g[j], nsa_k_norm_g[j],
                              nsa_cmp_pos_k[j], nsa_cmp_pos_v[j], nsa_cmp_w1k[j], nsa_cmp_w2k[j],
                              nsa_cmp_w1v[j], nsa_cmp_w2v[j], nsa_gate_b[j], nsa_w_out[j])
        h = h + conv_ffn(rms_norm(h, ffn_norm_g[i]), ffn_w_up[i], ffn_conv_w[i], ffn_conv_b[i], ffn_w_down[i])
        e = rms_norm(p[i] @ ple_w[i], ple_norm_g[i])
        gate = jax.nn.sigmoid(rms_norm(h, ple_gate_norm_g[i]) @ ple_w_gate[i])
        h = h + gate * e
    return h
```

```python
import functools
import math

import numpy as np
import jax
import jax.numpy as jnp
from jax import lax
from jax.experimental import pallas as pl
from jax.experimental.pallas import tpu as pltpu

F32 = jnp.float32
BF16 = jnp.bfloat16

LANES_V7X = 128
VMEM_LIMIT_V7X = 56 * 1024 * 1024

D_MODEL = 1024
PLE_DIM = 256
R_HEADS, R_DK, R_DV, R_CHUNK = 4, 64, 128, 128
ROPE_BASE = 10000.0
M_HEADS, M_DK, M_DV, M_CHUNK, M_CONV = 4, 64, 128, 64, 4
AB_SIZES = (R_HEADS * R_DK, R_HEADS * R_DK, R_HEADS * R_DV, R_HEADS * R_DV,
            M_HEADS * M_DK, M_HEADS * M_DK, M_HEADS * M_DV, M_HEADS * M_DV, M_HEADS, M_HEADS)
AB_COLS = sum(AB_SIZES)
AB_COLS_PAD = 3200
N_HEADS, N_KV_GROUPS, HEAD_DIM = 16, 2, 64
HG = N_HEADS // N_KV_GROUPS
CMP_BLOCK, CMP_STRIDE, CMP_HIDDEN = 32, 16, 256
SLC_BLOCK, N_SELECT, WINDOW, Q_BLOCK = 64, 16, 512, 128
NSA_COLS = N_HEADS * HEAD_DIM + 6 * N_KV_GROUPS * HEAD_DIM + N_HEADS * 3
NSA_COLS_PAD = 1920
D_FF = 2816
FFN_CONV = 3
NEG = -1e30
BIG = 1e30
EPS = 1e-6
KEY_TILE = 128


def _cparams(sem, vmem=None):
    return pltpu.CompilerParams(dimension_semantics=sem, vmem_limit_bytes=vmem)


def _rms(x, g):
    ms = jnp.mean(x * x, axis=-1, keepdims=True)
    return x * lax.rsqrt(ms + EPS) * g


def _dot(a, b):
    return jnp.dot(a, b, preferred_element_type=F32)


def _dot_nt(a, b):
    return lax.dot_general(a, b, (((1,), (1,)), ((), ())), preferred_element_type=F32)


def _dot_f32(a, b):
    return jnp.dot(a, b, preferred_element_type=F32, precision=lax.Precision.HIGHEST)


def _norm_matmul_kernel(x_ref, g_ref, w_ref, o_ref):
    xn = _rms(x_ref[...], g_ref[...]).astype(BF16)
    o_ref[...] = _dot(xn, w_ref[...]).astype(o_ref.dtype)


def norm_matmul(x, g, w, tm=512, out_dtype=F32):
    n, d = x.shape
    nc = w.shape[1]
    return pl.pallas_call(
        _norm_matmul_kernel,
        out_shape=jax.ShapeDtypeStruct((n, nc), out_dtype),
        grid=(n // tm,),
        in_specs=[pl.BlockSpec((tm, d), lambda i: (i, 0)),
                  pl.BlockSpec((1, d), lambda i: (0, 0)),
                  pl.BlockSpec((d, nc), lambda i: (0, 0))],
        out_specs=pl.BlockSpec((tm, nc), lambda i: (i, 0)),
        compiler_params=_cparams(("parallel",), VMEM_LIMIT_V7X),
        name="norm_matmul",
    )(x, g.reshape(1, d), w)


def _proj_residual_kernel(*refs, n_in):
    h_ref = refs[0]
    o_ref = refs[1 + 2 * n_in]
    acc = h_ref[...]
    for k in range(n_in):
        acc = acc + _dot(refs[1 + 2 * k][...].astype(BF16), refs[2 + 2 * k][...])
    o_ref[...] = acc


def proj_residual(h, pairs, tm=512):
    n, d = h.shape
    in_specs = [pl.BlockSpec((tm, d), lambda i: (i, 0))]
    args = [h]
    for a, w in pairs:
        in_specs.append(pl.BlockSpec((tm, a.shape[1]), lambda i: (i, 0)))
        in_specs.append(pl.BlockSpec(w.shape, lambda i: (0, 0)))
        args += [a, w]
    return pl.pallas_call(
        functools.partial(_proj_residual_kernel, n_in=len(pairs)),
        out_shape=jax.ShapeDtypeStruct((n, d), F32),
        grid=(n // tm,),
        in_specs=in_specs,
        out_specs=pl.BlockSpec((tm, d), lambda i: (i, 0)),
        compiler_params=_cparams(("parallel",), VMEM_LIMIT_V7X),
        name="proj_residual",
    )(*args)


def _rope_table_kernel(inv_ref, cos_ref, sin_ref):
    c = pl.program_id(0)
    rows, width = cos_ref.shape
    pos = (c * rows + lax.broadcasted_iota(jnp.int32, (rows, width), 0)).astype(F32)
    lane = lax.broadcasted_iota(jnp.int32, (rows, width), 1)
    ang = pos * inv_ref[...]
    cos_ref[...] = jnp.cos(ang)
    sn = jnp.sin(ang)
    sin_ref[...] = jnp.where(lane % R_DK < R_DK // 2, -sn, sn)


def rope_tables(t):
    half = R_DK // 2
    inv = ROPE_BASE ** (-jnp.arange(half, dtype=F32) / half)
    inv = jnp.tile(inv, 2 * R_HEADS).reshape(1, R_HEADS * R_DK)
    width = R_HEADS * R_DK
    shp = jax.ShapeDtypeStruct((t, width), F32)
    return pl.pallas_call(
        _rope_table_kernel,
        out_shape=(shp, shp),
        grid=(t // R_CHUNK,),
        in_specs=[pl.BlockSpec((1, width), lambda c: (0, 0))],
        out_specs=(pl.BlockSpec((R_CHUNK, width), lambda c: (c, 0)),
                   pl.BlockSpec((R_CHUNK, width), lambda c: (c, 0))),
        compiler_params=_cparams(("parallel",)),
        name="rope_tables",
    )(inv)


def _retention_kernel(cos_ref, sin_ref, q_ref, k_ref, v_ref, g_ref, gain_ref, o_ref, r_ref):
    c = pl.program_id(1)
    L = R_CHUNK

    @pl.when(c == 0)
    def _():
        r_ref[...] = jnp.zeros_like(r_ref)

    cos = cos_ref[...]
    sin = sin_ref[...]
    lane = lax.broadcasted_iota(jnp.int32, cos.shape, 1)
    first_half = lane % R_DK < R_DK // 2
    width = R_HEADS * R_DK

    def rot(x):
        swapped = jnp.where(first_half, pltpu.roll(x, width - R_DK // 2, 1), pltpu.roll(x, R_DK // 2, 1))
        return x * cos + swapped * sin

    q = rot(q_ref[...])
    k = rot(k_ref[...]) * (R_DK ** -0.5)
    v = v_ref[...]
    gate = g_ref[...]
    gain = gain_ref[...]

    ri = lax.broadcasted_iota(jnp.int32, (L, L), 0)
    ci = lax.broadcasted_iota(jnp.int32, (L, L), 1)
    diff = (ri - ci).astype(F32)
    causal = ri >= ci
    idx = lax.broadcasted_iota(jnp.int32, (L, 1), 0).astype(F32)

    for h in range(R_HEADS):
        log_g = math.log1p(-2.0 ** (-5.0 - h))
        dmask = jnp.where(causal, jnp.exp(jnp.where(causal, diff, 0.0) * log_g), 0.0)
        q_dec = jnp.exp((idx + 1.0) * log_g)
        k_dec = jnp.exp((L - 1.0 - idx) * log_g)
        c_dec = math.exp(L * log_g)
        qh = q[:, h * R_DK:(h + 1) * R_DK]
        kh = k[:, h * R_DK:(h + 1) * R_DK]
        vh = v[:, h * R_DV:(h + 1) * R_DV].astype(BF16)
        qb = qh.astype(BF16)
        s = _dot_nt(qb, kh.astype(BF16)) * dmask
        rstate = r_ref[h]
        o = _dot(s.astype(BF16), vh) + _dot(qb, rstate.astype(BF16)) * q_dec
        kd = (kh * k_dec).T.astype(BF16)
        r_ref[h] = c_dec * rstate + _dot(kd, vh)
        sl = slice(h * R_DV, (h + 1) * R_DV)
        gh = gate[:, sl]
        o_ref[:, sl] = _rms(o, gain[:, sl]) * (gh * jax.nn.sigmoid(gh))


def retention(z3, cos_t, sin_t, gain):
    b, t, _ = z3.shape
    nc = t // R_CHUNK
    qk_w = R_HEADS * R_DK
    v_w = R_HEADS * R_DV
    return pl.pallas_call(
        _retention_kernel,
        out_shape=jax.ShapeDtypeStruct((b, t, v_w), F32),
        grid=(b, nc),
        in_specs=[pl.BlockSpec((R_CHUNK, qk_w), lambda i, c: (c, 0)),
                  pl.BlockSpec((R_CHUNK, qk_w), lambda i, c: (c, 0)),
                  pl.BlockSpec((None, R_CHUNK, qk_w), lambda i, c: (i, c, 0)),
                  pl.BlockSpec((None, R_CHUNK, qk_w), lambda i, c: (i, c, 1)),
                  pl.BlockSpec((None, R_CHUNK, v_w), lambda i, c: (i, c, 1)),
                  pl.BlockSpec((None, R_CHUNK, v_w), lambda i, c: (i, c, 2)),
                  pl.BlockSpec((1, v_w), lambda i, c: (0, 0))],
        out_specs=pl.BlockSpec((None, R_CHUNK, v_w), lambda i, c: (i, c, 0)),
        scratch_shapes=[pltpu.VMEM((R_HEADS, R_DK, R_DV), F32)],
        compiler_params=_cparams(("parallel", "arbitrary")),
        name="retention",
    )(cos_t, sin_t, z3, z3, z3, z3, gain.reshape(1, v_w))


def _mlstm_kernel(q_ref, k_ref, v_ref, og_ref, gc_ref, gr_ref, cw_ref, cb_ref, bc_ref, br_ref, gain_ref,
                  o_ref, xbuf, c_ref, n_ref, m_ref):
    c = pl.program_id(1)
    L = M_CHUNK
    H = M_HEADS
    qk_w = H * M_DK
    halo = 8

    @pl.when(c == 0)
    def _():
        xbuf[0:halo, :] = jnp.zeros((halo, 2 * qk_w), F32)
        c_ref[...] = jnp.zeros_like(c_ref)
        n_ref[...] = jnp.zeros_like(n_ref)
        m_ref[...] = jnp.zeros_like(m_ref)

    xbuf[halo:halo + L, 0:qk_w] = q_ref[...]
    xbuf[halo:halo + L, qk_w:2 * qk_w] = k_ref[...]
    conv = cb_ref[...]
    for j in range(M_CONV):
        conv = conv + xbuf[pl.ds(halo - (M_CONV - 1) + j, L), :] * cw_ref[j:j + 1, :]
    tail = xbuf[L:L + halo, :]
    xbuf[0:halo, :] = tail
    act = conv * jax.nn.sigmoid(conv)
    q = act[:, 0:qk_w] * (M_DK ** -0.5)
    k = act[:, qk_w:2 * qk_w]
    v = v_ref[...]
    og = og_ref[...]
    gain = gain_ref[...]

    gc = gc_ref[...][:, 0:2 * H] + bc_ref[...]
    gr = gr_ref[...] + br_ref[...]
    ig_c = gc[:, 0:H]
    lf_c = jax.nn.log_sigmoid(gc[:, H:2 * H])
    ig_r = gr[0:H, :]
    lf_r = jax.nn.log_sigmoid(gr[H:2 * H, :])
    ri = lax.broadcasted_iota(jnp.int32, (L, L), 0)
    ci = lax.broadcasted_iota(jnp.int32, (L, L), 1)
    causal = ri >= ci
    tril = causal.astype(F32)
    triu = (ri <= ci).astype(F32)
    b_c = _dot_f32(tril, lf_c)
    b_r = _dot_f32(lf_r, triu)

    for h in range(H):
        bh = b_c[:, h:h + 1]
        dlog = jnp.where(causal, bh - b_r[h:h + 1, :] + ig_r[h:h + 1, :], NEG)
        m_prev = m_ref[h][:, 0:1]
        inter = bh + m_prev
        m_t = jnp.maximum(inter, jnp.max(dlog, axis=-1, keepdims=True))
        qh = q[:, h * M_DK:(h + 1) * M_DK]
        kh = k[:, h * M_DK:(h + 1) * M_DK]
        vh = v[:, h * M_DV:(h + 1) * M_DV].astype(BF16)
        qb = qh.astype(BF16)
        s = _dot_nt(qb, kh.astype(BF16)) * jnp.exp(dlog - m_t)
        w_inter = jnp.exp(inter - m_t)
        cstate = c_ref[h]
        nstate = n_ref[h]
        num = _dot(s.astype(BF16), vh) + w_inter * _dot(qb, cstate.astype(BF16))
        den = jnp.sum(s, axis=-1, keepdims=True) + w_inter * jnp.sum(qh * nstate, axis=-1, keepdims=True)
        hh = num / jnp.maximum(jnp.abs(den), jnp.exp(-m_t))
        b_last = bh[L - 1:L, :]
        wlog = b_last - bh + ig_c[:, h:h + 1]
        m_new = jnp.maximum(b_last + m_prev, jnp.max(wlog, axis=0, keepdims=True))
        decay = jnp.exp(b_last + m_prev - m_new)
        wk = kh * jnp.exp(wlog - m_new)
        c_ref[h] = decay * cstate + _dot(wk.T.astype(BF16), vh)
        n_ref[h] = decay * nstate + jnp.sum(wk, axis=0, keepdims=True)
        m_ref[h] = jnp.broadcast_to(m_new, (1, LANES_V7X))
        sl = slice(h * M_DV, (h + 1) * M_DV)
        o_ref[:, sl] = jax.nn.sigmoid(og[:, sl]) * _rms(hh, gain[:, sl])


def mlstm(z3, gates_r, conv_w, conv_b, ig_b, fg_b, gain):
    b, t, _ = z3.shape
    nc = t // M_CHUNK
    H = M_HEADS
    qk_w = H * M_DK
    v_w = H * M_DV
    bias = jnp.concatenate([ig_b, fg_b])
    gate_blk = AB_COLS_PAD // LANES_V7X - 1
    return pl.pallas_call(
        _mlstm_kernel,
        out_shape=jax.ShapeDtypeStruct((b, t, v_w), F32),
        grid=(b, nc),
        in_specs=[pl.BlockSpec((None, M_CHUNK, qk_w), lambda i, c: (i, c, 6)),
                  pl.BlockSpec((None, M_CHUNK, qk_w), lambda i, c: (i, c, 7)),
                  pl.BlockSpec((None, M_CHUNK, v_w), lambda i, c: (i, c, 4)),
                  pl.BlockSpec((None, M_CHUNK, v_w), lambda i, c: (i, c, 5)),
                  pl.BlockSpec((None, M_CHUNK, LANES_V7X), lambda i, c: (i, c, gate_blk)),
                  pl.BlockSpec((None, None, 2 * H, M_CHUNK), lambda i, c: (i, c, 0, 0)),
                  pl.BlockSpec((M_CONV, 2 * qk_w), lambda i, c: (0, 0)),
                  pl.BlockSpec((1, 2 * qk_w), lambda i, c: (0, 0)),
                  pl.BlockSpec((1, 2 * H), lambda i, c: (0, 0)),
                  pl.BlockSpec((2 * H, 1), lambda i, c: (0, 0)),
                  pl.BlockSpec((1, v_w), lambda i, c: (0, 0))],
        out_specs=pl.BlockSpec((None, M_CHUNK, v_w), lambda i, c: (i, c, 0)),
        scratch_shapes=[pltpu.VMEM((8 + M_CHUNK, 2 * qk_w), F32),
                        pltpu.VMEM((H, M_DK, M_DV), F32),
                        pltpu.VMEM((H, 1, M_DK), F32),
                        pltpu.VMEM((H, 1, LANES_V7X), F32)],
        compiler_params=_cparams(("parallel", "arbitrary")),
        name="mlstm",
    )(z3, z3, z3, z3, z3, gates_r, conv_w, conv_b.reshape(1, -1), bias.reshape(1, -1), bias.reshape(-1, 1),
      gain.reshape(1, v_w))


def _ffn_kernel(xp_ref, x_ref, g_ref, wa_ref, wb_ref, cw_ref, cb_ref, wd_ref, o_ref, xn_ref, a_ref,
                *, tm, seq):
    i = pl.program_id(0)
    j = pl.program_id(1)
    halo = 8

    @pl.when(j == 0)
    def _():
        g = g_ref[...]
        x = x_ref[...]
        xn_ref[halo:halo + tm, :] = _rms(x, g).astype(BF16)
        prev = _rms(xp_ref[...], g)
        keep = ((i * tm) % seq != 0).astype(F32)
        xn_ref[0:halo, :] = (prev * keep).astype(BF16)
        o_ref[...] = x

    a_ref[...] = _dot(xn_ref[...], wa_ref[...])
    conv = cb_ref[...]
    for t in range(FFN_CONV):
        conv = conv + a_ref[pl.ds(halo - (FFN_CONV - 1) + t, tm), :] * cw_ref[t:t + 1, :]
    bgate = _dot(xn_ref[halo:halo + tm, :], wb_ref[...])
    act = (jax.nn.gelu(conv) * bgate).astype(BF16)
    o_ref[...] += _dot(act, wd_ref[...])


def conv_ffn(h, g, w_up, conv_w, conv_b, w_down, seq, tm=1024, tf=256):
    n, d = h.shape
    nj = D_FF // tf
    hb = tm // 8
    return pl.pallas_call(
        functools.partial(_ffn_kernel, tm=tm, seq=seq),
        out_shape=jax.ShapeDtypeStruct((n, d), F32),
        grid=(n // tm, nj),
        in_specs=[pl.BlockSpec((8, d), lambda i, j: (jnp.maximum(i * hb - 1, 0), 0)),
                  pl.BlockSpec((tm, d), lambda i, j: (i, 0)),
                  pl.BlockSpec((1, d), lambda i, j: (0, 0)),
                  pl.BlockSpec((d, tf), lambda i, j: (0, j)),
                  pl.BlockSpec((d, tf), lambda i, j: (0, nj + j)),
                  pl.BlockSpec((FFN_CONV, tf), lambda i, j: (0, j)),
                  pl.BlockSpec((1, tf), lambda i, j: (0, j)),
                  pl.BlockSpec((tf, d), lambda i, j: (j, 0))],
        out_specs=pl.BlockSpec((tm, d), lambda i, j: (i, 0)),
        scratch_shapes=[pltpu.VMEM((8 + tm, d), BF16),
                        pltpu.VMEM((8 + tm, tf), F32)],
        compiler_params=_cparams(("parallel", "arbitrary"), VMEM_LIMIT_V7X),
        name="conv_ffn",
    )(h, h, g.reshape(1, d), w_up, w_up, conv_w, conv_b.reshape(1, -1), w_down)


def _ple_kernel(h_ref, p_ref, wp_ref, ng_ref, gg_ref, wg_ref, o_ref):
    h = h_ref[...]
    e = _rms(_dot(p_ref[...].astype(BF16), wp_ref[...]), ng_ref[...])
    gate = jax.nn.sigmoid(_dot(_rms(h, gg_ref[...]).astype(BF16), wg_ref[...]))
    o_ref[...] = h + gate * e


def ple(h, p, wp, norm_g, gate_norm_g, wg, tm=512):
    n, d = h.shape
    pd = p.shape[1]
    return pl.pallas_call(
        _ple_kernel,
        out_shape=jax.ShapeDtypeStruct((n, d), F32),
        grid=(n // tm,),
        in_specs=[pl.BlockSpec((tm, d), lambda i: (i, 0)),
                  pl.BlockSpec((tm, pd), lambda i: (i, 0)),
                  pl.BlockSpec((pd, d), lambda i: (0, 0)),
                  pl.BlockSpec((1, d), lambda i: (0, 0)),
                  pl.BlockSpec((1, d), lambda i: (0, 0)),
                  pl.BlockSpec((d, d), lambda i: (0, 0))],
        out_specs=pl.BlockSpec((tm, d), lambda i: (i, 0)),
        compiler_params=_cparams(("parallel",), VMEM_LIMIT_V7X),
        name="ple",
    )(h, p, wp, norm_g.reshape(1, d), gate_norm_g.reshape(1, d), wg)


def _group_rms(x, g):
    lane = lax.broadcasted_iota(jnp.int32, x.shape, 1)
    x2 = x * x
    ms = jnp.zeros_like(x)
    for grp in range(N_KV_GROUPS):
        in_grp = (lane >= grp * HEAD_DIM) & (lane < (grp + 1) * HEAD_DIM)
        tot = jnp.sum(jnp.where(in_grp, x2, 0.0), axis=-1, keepdims=True)
        ms = jnp.where(in_grp, tot * (1.0 / HEAD_DIM), ms)
    return x * lax.rsqrt(ms + EPS) * g


def _kv_prep_kernel(c_ref, s_ref, w_ref, gs_ref, gw_ref, kc_ref, vc_ref, ks_ref, vs_ref, kw_ref, vw_ref):
    gw = N_KV_GROUPS * HEAD_DIM
    cc = c_ref[...]
    ss = s_ref[...]
    ww = w_ref[...]
    kc_ref[...] = cc[:, 0:gw].astype(BF16)
    vc_ref[...] = cc[:, gw:2 * gw].astype(BF16)
    ks_ref[...] = _group_rms(ss[:, 0:gw], gs_ref[...]).astype(BF16)
    vs_ref[...] = ss[:, gw:2 * gw].astype(BF16)
    kw_ref[...] = _group_rms(ww[:, 0:gw], gw_ref[...]).astype(BF16)
    vw_ref[...] = ww[:, gw:2 * gw].astype(BF16)


def kv_prep(z, k_g, tm=512):
    n = z.shape[0]
    gw = N_KV_GROUPS * HEAD_DIM
    base = N_HEADS * HEAD_DIM // (2 * gw)
    shp = jax.ShapeDtypeStruct((n, gw), BF16)
    ospec = pl.BlockSpec((tm, gw), lambda i: (i, 0))
    return pl.pallas_call(
        _kv_prep_kernel,
        out_shape=(shp,) * 6,
        grid=(n // tm,),
        in_specs=[pl.BlockSpec((tm, 2 * gw), lambda i: (i, base)),
                  pl.BlockSpec((tm, 2 * gw), lambda i: (i, base + 1)),
                  pl.BlockSpec((tm, 2 * gw), lambda i: (i, base + 2)),
                  pl.BlockSpec((1, gw), lambda i: (0, 0)),
                  pl.BlockSpec((1, gw), lambda i: (0, 0))],
        out_specs=(ospec,) * 6,
        compiler_params=_cparams(("parallel",)),
        name="kv_prep",
    )(z, z, z, jnp.tile(k_g[1], N_KV_GROUPS).reshape(1, gw), jnp.tile(k_g[2], N_KV_GROUPS).reshape(1, gw))


def _compress_kernel(x_ref, pos_ref, w1_ref, w2_ref, g_ref, o_ref, *, normalize):
    half = w1_ref.shape[0] // 2
    x = x_ref[...]
    u = _dot(x, w1_ref[0:half, :])
    v = _dot(x, w1_ref[half:2 * half, :])
    rows = u.shape[0]
    posc = _dot(pos_ref[...], w1_ref[...])[0:1, :]
    hid = u + pltpu.roll(v, rows - 1, 0) + posc
    out = _dot(jax.nn.gelu(hid).astype(BF16), w2_ref[...])
    if normalize:
        out = _rms(out, g_ref[...])
    o_ref[...] = out.astype(o_ref.dtype)


def compress(x, pos, w1, w2, g, normalize, ncb):
    n, kdim = x.shape
    posf = jnp.broadcast_to(pos.reshape(1, -1), (8, 2 * kdim)).astype(BF16)
    return pl.pallas_call(
        functools.partial(_compress_kernel, normalize=normalize),
        out_shape=jax.ShapeDtypeStruct((n, HEAD_DIM), BF16),
        grid=(n // ncb,),
        in_specs=[pl.BlockSpec((ncb, kdim), lambda i: (i, 0)),
                  pl.BlockSpec((8, 2 * kdim), lambda i: (0, 0)),
                  pl.BlockSpec((2 * kdim, CMP_HIDDEN), lambda i: (0, 0)),
                  pl.BlockSpec((CMP_HIDDEN, HEAD_DIM), lambda i: (0, 0)),
                  pl.BlockSpec((1, HEAD_DIM), lambda i: (0, 0))],
        out_specs=pl.BlockSpec((ncb, HEAD_DIM), lambda i: (i, 0)),
        compiler_params=_cparams(("parallel",)),
        name="compress",
    )(x, posf, w1.astype(BF16), w2.astype(BF16), g.reshape(1, HEAD_DIM))


def _nsa_kernel(zq_ref, gt_ref, gb_ref, qg_ref, ov_ref, kct_ref, vc_ref, kst_ref, vs_ref, kwt_ref, vw_ref,
                o_ref, q_scr, m_scr, l_scr, acc_scr, oc_scr):
    grp = pl.program_id(1)
    qb = pl.program_id(2)
    QB = Q_BLOCK
    t0 = qb * QB
    tpos = t0 + lax.broadcasted_iota(jnp.int32, (QB, 1), 0)
    ncp = kct_ref.shape[-1]
    ns = ov_ref.shape[-1]
    qg = qg_ref[...]

    cmp_end = lax.broadcasted_iota(jnp.int32, (1, ncp), 1) * CMP_STRIDE + (CMP_BLOCK - 1)
    cvalid = cmp_end <= tpos
    kct = kct_ref[...]
    vc = vc_ref[...]
    psum = jnp.zeros((QB, ncp), F32)
    for h in range(HG):
        qh = zq_ref[:, h * HEAD_DIM:(h + 1) * HEAD_DIM]
        qh = (_rms(qh, qg) * (HEAD_DIM ** -0.5)).astype(BF16)
        q_scr[h] = qh
        s = jnp.where(cvalid, _dot(qh, kct), NEG)
        e = jnp.exp(s - jnp.max(s, axis=-1, keepdims=True))
        p = jnp.where(cvalid, e / jnp.sum(e, axis=-1, keepdims=True), 0.0)
        oc_scr[h] = _dot(p.astype(BF16), vc)
        psum = psum + p

    p_hi = psum.astype(BF16)
    p_lo = (psum - p_hi.astype(F32)).astype(BF16)
    ov = ov_ref[...]
    imp = _dot(p_hi, ov) + _dot(p_lo, ov)
    sj = lax.broadcasted_iota(jnp.int32, (1, ns), 1)
    sj_f = sj.astype(F32)
    cur = tpos // SLC_BLOCK
    forced = (sj == 0) | (sj == cur) | (sj == cur - 1)
    bvalid = sj <= cur
    score = jnp.where(forced, BIG, jnp.where(bvalid, imp, NEG))
    sel = jnp.zeros((QB, ns), F32)
    for _ in range(min(N_SELECT, ns)):
        mx = jnp.max(score, axis=-1, keepdims=True)
        first = jnp.min(jnp.where(score == mx, sj_f, float(ns)), axis=-1, keepdims=True)
        pick = sj_f == first
        sel = jnp.where(pick, 1.0, sel)
        score = jnp.where(pick, -jnp.inf, score)
    sel = jnp.where(bvalid, sel, 0.0).astype(BF16)

    key_iota = lax.broadcasted_iota(jnp.int32, (1, KEY_TILE), 1)
    blk_iota = lax.broadcasted_iota(jnp.int32, (ns, 1), 0)
    blocks_per_tile = KEY_TILE // SLC_BLOCK

    def reset_state():
        m_scr[...] = jnp.full(m_scr.shape, NEG, F32)
        l_scr[...] = jnp.zeros(l_scr.shape, F32)
        acc_scr[...] = jnp.zeros(acc_scr.shape, F32)

    def attend(kt, v, mask):
        for h in range(HG):
            s = jnp.where(mask, _dot(q_scr[h], kt), NEG)
            m_old = m_scr[h]
            m_new = jnp.maximum(m_old, jnp.max(s, axis=-1, keepdims=True))
            alpha = jnp.exp(m_old - m_new)
            p = jnp.exp(s - m_new)
            l_scr[h] = alpha * l_scr[h] + jnp.sum(p, axis=-1, keepdims=True)
            acc_scr[h] = alpha * acc_scr[h] + _dot(p.astype(BF16), v)
            m_scr[h] = m_new

    reset_state()

    def sel_body(j, carry):
        expand = (blk_iota == j * blocks_per_tile + key_iota // SLC_BLOCK).astype(BF16)
        chosen = _dot(sel, expand) > 0.5
        kpos = j * KEY_TILE + key_iota
        attend(kst_ref[j], vs_ref[j], chosen & (kpos <= tpos))
        return carry

    lax.fori_loop(0, qb + 1, sel_body, 0)
    gates = jax.nn.sigmoid(gt_ref[...] + gb_ref[...])
    per_grp = HG * 3
    gsel = gates[:, 0:per_grp]
    for g2 in range(1, N_KV_GROUPS):
        gsel = jnp.where(grp == g2, gates[:, g2 * per_grp:(g2 + 1) * per_grp], gsel)
    for h in range(HG):
        o_sel = acc_scr[h] / l_scr[h]
        oc_scr[h] = gsel[:, 3 * h:3 * h + 1] * oc_scr[h] + gsel[:, 3 * h + 1:3 * h + 2] * o_sel

    reset_state()
    n_win = WINDOW // KEY_TILE

    def win_body(w, carry):
        j = qb - w
        kpos = j * KEY_TILE + key_iota
        attend(kwt_ref[j], vw_ref[j], (kpos <= tpos) & (kpos > tpos - WINDOW))
        return carry

    lax.fori_loop(0, jnp.minimum(qb, n_win) + 1, win_body, 0)
    for h in range(HG):
        o_win = acc_scr[h] / l_scr[h]
        o_ref[:, h * HEAD_DIM:(h + 1) * HEAD_DIM] = oc_scr[h] + gsel[:, 3 * h + 2:3 * h + 3] * o_win


def nsa_attention(z3, gate_b, q_g, overlap, kct, vc, kst, vs, kwt, vw):
    b, t, _ = z3.shape
    nq = t // Q_BLOCK
    nt = t // KEY_TILE
    qw = HG * HEAD_DIM
    ncp = kct.shape[-1]
    ns = overlap.shape[-1]
    gate_blk = (NSA_COLS_PAD // LANES_V7X) - 1
    gb = jnp.zeros((1, LANES_V7X), F32).at[0, :N_HEADS * 3].set(gate_b)
    full5 = lambda i, g, q: (i, g, 0, 0, 0)
    return pl.pallas_call(
        _nsa_kernel,
        out_shape=jax.ShapeDtypeStruct((b, t, N_HEADS * HEAD_DIM), F32),
        grid=(b, N_KV_GROUPS, nq),
        in_specs=[pl.BlockSpec((None, Q_BLOCK, qw), lambda i, g, q: (i, q, g)),
                  pl.BlockSpec((None, Q_BLOCK, LANES_V7X), lambda i, g, q: (i, q, gate_blk)),
                  pl.BlockSpec((1, LANES_V7X), lambda i, g, q: (0, 0)),
                  pl.BlockSpec((1, HEAD_DIM), lambda i, g, q: (0, 0)),
                  pl.BlockSpec((ncp, ns), lambda i, g, q: (0, 0)),
                  pl.BlockSpec((None, None, HEAD_DIM, ncp), lambda i, g, q: (i, g, 0, 0)),
                  pl.BlockSpec((None, None, ncp, HEAD_DIM), lambda i, g, q: (i, g, 0, 0)),
                  pl.BlockSpec((None, None, nt, HEAD_DIM, KEY_TILE), full5),
                  pl.BlockSpec((None, None, nt, KEY_TILE, HEAD_DIM), full5),
                  pl.BlockSpec((None, None, nt, HEAD_DIM, KEY_TILE), full5),
                  pl.BlockSpec((None, None, nt, KEY_TILE, HEAD_DIM), full5)],
        out_specs=pl.BlockSpec((None, Q_BLOCK, qw), lambda i, g, q: (i, q, g)),
        scratch_shapes=[pltpu.VMEM((HG, Q_BLOCK, HEAD_DIM), BF16),
                        pltpu.VMEM((HG, Q_BLOCK, 1), F32),
                        pltpu.VMEM((HG, Q_BLOCK, 1), F32),
                        pltpu.VMEM((HG, Q_BLOCK, HEAD_DIM), F32),
                        pltpu.VMEM((HG, Q_BLOCK, HEAD_DIM), F32)],
        compiler_params=_cparams(("parallel", "parallel", "arbitrary"), VMEM_LIMIT_V7X),
        name="nsa_attention",
    )(z3, z3, gb, q_g.reshape(1, HEAD_DIM), overlap, kct, vc, kst, vs, kwt, vw)


def _overlap_matrix(t):
    ncp = t // CMP_STRIDE
    ns = t // SLC_BLOCK
    c_start = np.arange(ncp) * CMP_STRIDE
    sj = np.arange(ns)
    ov = ((c_start[:, None] < (sj[None, :] + 1) * SLC_BLOCK)
          & (c_start[:, None] + CMP_BLOCK > sj[None, :] * SLC_BLOCK)
          & (c_start[:, None] + CMP_BLOCK <= t))
    return jnp.asarray(ov, dtype=BF16)


def ab_layer(h, b, t, norm_g, w_in, conv_w, conv_b, ret_g, ig_b, fg_b, m_g, w_out):
    n = b * t
    w_in_p = jnp.pad(w_in, ((0, 0), (0, AB_COLS_PAD - AB_COLS))).astype(BF16)
    z = norm_matmul(h, norm_g, w_in_p)
    z3 = z.reshape(b, t, AB_COLS_PAD)
    cos_t, sin_t = rope_tables(t)
    ret = retention(z3, cos_t, sin_t, ret_g)
    g0 = AB_COLS - 2 * M_HEADS
    gates_r = z3[:, :, g0:AB_COLS].reshape(b, t // M_CHUNK, M_CHUNK, 2 * M_HEADS).transpose(0, 1, 3, 2)
    ml = mlstm(z3, gates_r, conv_w, conv_b, ig_b, fg_b, m_g)
    w_out_b = w_out.astype(BF16)
    rw = R_HEADS * R_DV
    return proj_residual(h, [(ret.reshape(n, rw), w_out_b[:rw]), (ml.reshape(n, -1), w_out_b[rw:])])


def nsa_layer(h, b, t, norm_g, w_in, q_g, k_g, pos_k, pos_v, w1k, w2k, w1v, w2v, gate_b, w_out):
    n = b * t
    G = N_KV_GROUPS
    w_in_p = jnp.pad(w_in, ((0, 0), (0, NSA_COLS_PAD - NSA_COLS))).astype(BF16)
    z = norm_matmul(h, norm_g, w_in_p)
    kc_in, vc_in, ks, vs, kw, vw = kv_prep(z, k_g)
    nt = t // KEY_TILE
    ncb = t // CMP_STRIDE

    def cmp_rows(x):
        return x.reshape(b, t, G, HEAD_DIM).transpose(0, 2, 1, 3).reshape(b * G * ncb, CMP_STRIDE * HEAD_DIM)

    kc = compress(cmp_rows(kc_in), pos_k, w1k, w2k, k_g[0], True, ncb)
    vc = compress(cmp_rows(vc_in), pos_v, w1v, w2v, k_g[0], False, ncb)
    kct = kc.reshape(b, G, ncb, HEAD_DIM).transpose(0, 1, 3, 2)
    vc = vc.reshape(b, G, ncb, HEAD_DIM)

    def key_tiles(x):
        return x.reshape(b, nt, KEY_TILE, G, HEAD_DIM).transpose(0, 3, 1, 4, 2)

    def val_tiles(x):
        return x.reshape(b, nt, KEY_TILE, G, HEAD_DIM).transpose(0, 3, 1, 2, 4)

    o = nsa_attention(z.reshape(b, t, NSA_COLS_PAD), gate_b, q_g, _overlap_matrix(t), kct, vc,
                      key_tiles(ks), val_tiles(vs), key_tiles(kw), val_tiles(vw))
    return proj_residual(h, [(o.reshape(n, -1), w_out.astype(BF16))])


def kernel(x, p, ab_norm_g, ab_w_in, ab_conv_w, ab_conv_b, ab_ret_norm_g, ab_ig_b, ab_fg_b, ab_m_norm_g, ab_w_out, nsa_norm_g, nsa_w_in, nsa_q_norm_g, nsa_k_norm_g, nsa_cmp_pos_k, nsa_cmp_pos_v, nsa_cmp_w1k, nsa_cmp_w2k, nsa_cmp_w1v, nsa_cmp_w2v, nsa_gate_b, nsa_w_out, ffn_norm_g, ffn_w_up, ffn_conv_w, ffn_conv_b, ffn_w_down, ple_w, ple_norm_g, ple_gate_norm_g, ple_w_gate):
    b, t, d = x.shape
    n = b * t
    depth = p.shape[0]
    h = x.reshape(n, d)
    for i in range(depth):
        j = i // 2
        if i % 2 == 0:
            h = ab_layer(h, b, t, ab_norm_g[j], ab_w_in[j], ab_conv_w[j], ab_conv_b[j], ab_ret_norm_g[j],
                         ab_ig_b[j], ab_fg_b[j], ab_m_norm_g[j], ab_w_out[j])
        else:
            h = nsa_layer(h, b, t, nsa_norm_g[j], nsa_w_in[j], nsa_q_norm_g[j], nsa_k_norm_g[j],
                          nsa_cmp_pos_k[j], nsa_cmp_pos_v[j], nsa_cmp_w1k[j], nsa_cmp_w2k[j],
                          nsa_cmp_w1v[j], nsa_cmp_w2v[j], nsa_gate_b[j], nsa_w_out[j])
        h = conv_ffn(h, ffn_norm_g[i], ffn_w_up[i].astype(BF16), ffn_conv_w[i], ffn_conv_b[i],
                     ffn_w_down[i].astype(BF16), t)
        h = ple(h, p[i].reshape(n, -1), ple_w[i].astype(BF16), ple_norm_g[i], ple_gate_norm_g[i],
                ple_w_gate[i].astype(BF16))
    return h.reshape(b, t, d)
```

```python
import functools
import math

import numpy as np
import jax
import jax.numpy as jnp
from jax import lax
from jax.experimental import pallas as pl
from jax.experimental.pallas import tpu as pltpu

F32 = jnp.float32
BF16 = jnp.bfloat16

LANES_V7X = 128
VMEM_LIMIT_V7X = 56 * 1024 * 1024

D_MODEL = 1024
PLE_DIM = 256
R_HEADS, R_DK, R_DV, R_CHUNK = 4, 64, 128, 128
ROPE_BASE = 10000.0
M_HEADS, M_DK, M_DV, M_CHUNK, M_CONV = 4, 64, 128, 64, 4
AB_SIZES = (R_HEADS * R_DK, R_HEADS * R_DK, R_HEADS * R_DV, R_HEADS * R_DV,
            M_HEADS * M_DK, M_HEADS * M_DK, M_HEADS * M_DV, M_HEADS * M_DV, M_HEADS, M_HEADS)
AB_COLS = sum(AB_SIZES)
AB_COLS_PAD = 3200
N_HEADS, N_KV_GROUPS, HEAD_DIM = 16, 2, 64
HG = N_HEADS // N_KV_GROUPS
CMP_BLOCK, CMP_STRIDE, CMP_HIDDEN = 32, 16, 256
SLC_BLOCK, N_SELECT, WINDOW, Q_BLOCK = 64, 16, 512, 128
NSA_COLS = N_HEADS * HEAD_DIM + 6 * N_KV_GROUPS * HEAD_DIM + N_HEADS * 3
NSA_COLS_PAD = 1920
D_FF = 2816
FFN_CONV = 3
NEG = -1e30
BIG = 1e30
EPS = 1e-6
KEY_TILE = 128


def _cparams(sem, vmem=None):
    return pltpu.CompilerParams(dimension_semantics=sem, vmem_limit_bytes=vmem)


def _rms(x, g):
    ms = jnp.mean(x * x, axis=-1, keepdims=True)
    return x * lax.rsqrt(ms + EPS) * g


def _dot(a, b):
    return jnp.dot(a, b, preferred_element_type=F32)


def _dot_nt(a, b):
    return lax.dot_general(a, b, (((1,), (1,)), ((), ())), preferred_element_type=F32)


def _dot_f32(a, b):
    return jnp.dot(a, b, preferred_element_type=F32, precision=lax.Precision.HIGHEST)


def _norm_matmul_kernel(x_ref, g_ref, w_ref, o_ref):
    xn = _rms(x_ref[...], g_ref[...]).astype(BF16)
    o_ref[...] = _dot(xn, w_ref[...]).astype(o_ref.dtype)


def norm_matmul(x, g, w, tm=512, out_dtype=F32):
    n, d = x.shape
    nc = w.shape[1]
    return pl.pallas_call(
        _norm_matmul_kernel,
        out_shape=jax.ShapeDtypeStruct((n, nc), out_dtype),
        grid=(n // tm,),
        in_specs=[pl.BlockSpec((tm, d), lambda i: (i, 0)),
                  pl.BlockSpec((1, d), lambda i: (0, 0)),
                  pl.BlockSpec((d, nc), lambda i: (0, 0))],
        out_specs=pl.BlockSpec((tm, nc), lambda i: (i, 0)),
        compiler_params=_cparams(("parallel",), VMEM_LIMIT_V7X),
        name="norm_matmul",
    )(x, g.reshape(1, d), w)


def _proj_residual_kernel(*refs, n_in):
    h_ref = refs[0]
    o_ref = refs[1 + 2 * n_in]
    acc = h_ref[...]
    for k in range(n_in):
        acc = acc + _dot(refs[1 + 2 * k][...].astype(BF16), refs[2 + 2 * k][...])
    o_ref[...] = acc


def proj_residual(h, pairs, tm=512):
    n, d = h.shape
    in_specs = [pl.BlockSpec((tm, d), lambda i: (i, 0))]
    args = [h]
    for a, w in pairs:
        in_specs.append(pl.BlockSpec((tm, a.shape[1]), lambda i: (i, 0)))
        in_specs.append(pl.BlockSpec(w.shape, lambda i: (0, 0)))
        args += [a, w]
    return pl.pallas_call(
        functools.partial(_proj_residual_kernel, n_in=len(pairs)),
        out_shape=jax.ShapeDtypeStruct((n, d), F32),
        grid=(n // tm,),
        in_specs=in_specs,
        out_specs=pl.BlockSpec((tm, d), lambda i: (i, 0)),
        compiler_params=_cparams(("parallel",), VMEM_LIMIT_V7X),
        name="proj_residual",
    )(*args)


def _rope_table_kernel(inv_ref, cos_ref, sin_ref):
    c = pl.program_id(0)
    rows, width = cos_ref.shape
    pos = (c * rows + lax.broadcasted_iota(jnp.int32, (rows, width), 0)).astype(F32)
    lane = lax.broadcasted_iota(jnp.int32, (rows, width), 1)
    ang = pos * inv_ref[...]
    cos_ref[...] = jnp.cos(ang)
    sn = jnp.sin(ang)
    sin_ref[...] = jnp.where(lane % R_DK < R_DK // 2, -sn, sn)


def rope_tables(t):
    half = R_DK // 2
    inv = ROPE_BASE ** (-jnp.arange(half, dtype=F32) / half)
    inv = jnp.tile(inv, 2 * R_HEADS).reshape(1, R_HEADS * R_DK)
    width = R_HEADS * R_DK
    shp = jax.ShapeDtypeStruct((t, width), F32)
    return pl.pallas_call(
        _rope_table_kernel,
        out_shape=(shp, shp),
        grid=(t // R_CHUNK,),
        in_specs=[pl.BlockSpec((1, width), lambda c: (0, 0))],
        out_specs=(pl.BlockSpec((R_CHUNK, width), lambda c: (c, 0)),
                   pl.BlockSpec((R_CHUNK, width), lambda c: (c, 0))),
        compiler_params=_cparams(("parallel",)),
        name="rope_tables",
    )(inv)


def _retention_kernel(cos_ref, sin_ref, q_ref, k_ref, v_ref, g_ref, gain_ref, o_ref, r_ref):
    c = pl.program_id(1)
    L = R_CHUNK

    @pl.when(c == 0)
    def _():
        r_ref[...] = jnp.zeros_like(r_ref)

    cos = cos_ref[...]
    sin = sin_ref[...]
    lane = lax.broadcasted_iota(jnp.int32, cos.shape, 1)
    first_half = lane % R_DK < R_DK // 2
    width = R_HEADS * R_DK

    def rot(x):
        swapped = jnp.where(first_half, pltpu.roll(x, width - R_DK // 2, 1), pltpu.roll(x, R_DK // 2, 1))
        return x * cos + swapped * sin

    q = rot(q_ref[...])
    k = rot(k_ref[...]) * (R_DK ** -0.5)
    v = v_ref[...]
    gate = g_ref[...]
    gain = gain_ref[...]

    ri = lax.broadcasted_iota(jnp.int32, (L, L), 0)
    ci = lax.broadcasted_iota(jnp.int32, (L, L), 1)
    diff = (ri - ci).astype(F32)
    causal = ri >= ci
    idx = lax.broadcasted_iota(jnp.int32, (L, 1), 0).astype(F32)

    for h in range(R_HEADS):
        log_g = math.log1p(-2.0 ** (-5.0 - h))
        dmask = jnp.where(causal, jnp.exp(jnp.where(causal, diff, 0.0) * log_g), 0.0)
        q_dec = jnp.exp((idx + 1.0) * log_g)
        k_dec = jnp.exp((L - 1.0 - idx) * log_g)
        c_dec = math.exp(L * log_g)
        qh = q[:, h * R_DK:(h + 1) * R_DK]
        kh = k[:, h * R_DK:(h + 1) * R_DK]
        vh = v[:, h * R_DV:(h + 1) * R_DV].astype(BF16)
        qb = qh.astype(BF16)
        s = _dot_nt(qb, kh.astype(BF16)) * dmask
        rstate = r_ref[h]
        o = _dot(s.astype(BF16), vh) + _dot(qb, rstate.astype(BF16)) * q_dec
        kd = (kh * k_dec).T.astype(BF16)
        r_ref[h] = c_dec * rstate + _dot(kd, vh)
        sl = slice(h * R_DV, (h + 1) * R_DV)
        gh = gate[:, sl]
        o_ref[:, sl] = _rms(o, gain[:, sl]) * (gh * jax.nn.sigmoid(gh))


def retention(z3, cos_t, sin_t, gain):
    b, t, _ = z3.shape
    nc = t // R_CHUNK
    qk_w = R_HEADS * R_DK
    v_w = R_HEADS * R_DV
    return pl.pallas_call(
        _retention_kernel,
        out_shape=jax.ShapeDtypeStruct((b, t, v_w), F32),
        grid=(b, nc),
        in_specs=[pl.BlockSpec((R_CHUNK, qk_w), lambda i, c: (c, 0)),
                  pl.BlockSpec((R_CHUNK, qk_w), lambda i, c: (c, 0)),
                  pl.BlockSpec((None, R_CHUNK, qk_w), lambda i, c: (i, c, 0)),
                  pl.BlockSpec((None, R_CHUNK, qk_w), lambda i, c: (i, c, 1)),
                  pl.BlockSpec((None, R_CHUNK, v_w), lambda i, c: (i, c, 1)),
                  pl.BlockSpec((None, R_CHUNK, v_w), lambda i, c: (i, c, 2)),
                  pl.BlockSpec((1, v_w), lambda i, c: (0, 0))],
        out_specs=pl.BlockSpec((None, R_CHUNK, v_w), lambda i, c: (i, c, 0)),
        scratch_shapes=[pltpu.VMEM((R_HEADS, R_DK, R_DV), F32)],
        compiler_params=_cparams(("parallel", "arbitrary")),
        name="retention",
    )(cos_t, sin_t, z3, z3, z3, z3, gain.reshape(1, v_w))


def _mlstm_kernel(q_ref, k_ref, v_ref, og_ref, gc_ref, gr_ref, cw_ref, cb_ref, bc_ref, br_ref, gain_ref,
                  o_ref, xbuf, c_ref, n_ref, m_ref):
    c = pl.program_id(1)
    L = M_CHUNK
    H = M_HEADS
    qk_w = H * M_DK
    halo = 8

    @pl.when(c == 0)
    def _():
        xbuf[0:halo, :] = jnp.zeros((halo, 2 * qk_w), F32)
        c_ref[...] = jnp.zeros_like(c_ref)
        n_ref[...] = jnp.zeros_like(n_ref)
        m_ref[...] = jnp.zeros_like(m_ref)

    xbuf[halo:halo + L, 0:qk_w] = q_ref[...]
    xbuf[halo:halo + L, qk_w:2 * qk_w] = k_ref[...]
    conv = cb_ref[...]
    for j in range(M_CONV):
        conv = conv + xbuf[pl.ds(halo - (M_CONV - 1) + j, L), :] * cw_ref[j:j + 1, :]
    tail = xbuf[L:L + halo, :]
    xbuf[0:halo, :] = tail
    act = conv * jax.nn.sigmoid(conv)
    q = act[:, 0:qk_w] * (M_DK ** -0.5)
    k = act[:, qk_w:2 * qk_w]
    v = v_ref[...]
    og = og_ref[...]
    gain = gain_ref[...]

    gc = gc_ref[...][:, 0:2 * H] + bc_ref[...]
    gr = gr_ref[...] + br_ref[...]
    ig_c = gc[:, 0:H]
    lf_c = jax.nn.log_sigmoid(gc[:, H:2 * H])
    ig_r = gr[0:H, :]
    lf_r = jax.nn.log_sigmoid(gr[H:2 * H, :])
    ri = lax.broadcasted_iota(jnp.int32, (L, L), 0)
    ci = lax.broadcasted_iota(jnp.int32, (L, L), 1)
    causal = ri >= ci
    tril = causal.astype(F32)
    triu = (ri <= ci).astype(F32)
    b_c = _dot_f32(tril, lf_c)
    b_r = _dot_f32(lf_r, triu)

    for h in range(H):
        bh = b_c[:, h:h + 1]
        dlog = jnp.where(causal, bh - b_r[h:h + 1, :] + ig_r[h:h + 1, :], NEG)
        m_prev = m_ref[h][:, 0:1]
        inter = bh + m_prev
        m_t = jnp.maximum(inter, jnp.max(dlog, axis=-1, keepdims=True))
        qh = q[:, h * M_DK:(h + 1) * M_DK]
        kh = k[:, h * M_DK:(h + 1) * M_DK]
        vh = v[:, h * M_DV:(h + 1) * M_DV].astype(BF16)
        qb = qh.astype(BF16)
        s = _dot_nt(qb, kh.astype(BF16)) * jnp.exp(dlog - m_t)
        w_inter = jnp.exp(inter - m_t)
        cstate = c_ref[h]
        nstate = n_ref[h]
        num = _dot(s.astype(BF16), vh) + w_inter * _dot(qb, cstate.astype(BF16))
        den = jnp.sum(s, axis=-1, keepdims=True) + w_inter * jnp.sum(qh * nstate, axis=-1, keepdims=True)
        hh = num / jnp.maximum(jnp.abs(den), jnp.exp(-m_t))
        b_last = bh[L - 1:L, :]
        wlog = b_last - bh + ig_c[:, h:h + 1]
        m_new = jnp.maximum(b_last + m_prev, jnp.max(wlog, axis=0, keepdims=True))
        decay = jnp.exp(b_last + m_prev - m_new)
        wk = kh * jnp.exp(wlog - m_new)
        c_ref[h] = decay * cstate + _dot(wk.T.astype(BF16), vh)
        n_ref[h] = decay * nstate + jnp.sum(wk, axis=0, keepdims=True)
        m_ref[h] = jnp.broadcast_to(m_new, (1, LANES_V7X))
        sl = slice(h * M_DV, (h + 1) * M_DV)
        o_ref[:, sl] = jax.nn.sigmoid(og[:, sl]) * _rms(hh, gain[:, sl])


def mlstm(z3, gates_r, conv_w, conv_b, ig_b, fg_b, gain):
    b, t, _ = z3.shape
    nc = t // M_CHUNK
    H = M_HEADS
    qk_w = H * M_DK
    v_w = H * M_DV
    bias = jnp.concatenate([ig_b, fg_b])
    gate_blk = AB_COLS_PAD // LANES_V7X - 1
    return pl.pallas_call(
        _mlstm_kernel,
        out_shape=jax.ShapeDtypeStruct((b, t, v_w), F32),
        grid=(b, nc),
        in_specs=[pl.BlockSpec((None, M_CHUNK, qk_w), lambda i, c: (i, c, 6)),
                  pl.BlockSpec((None, M_CHUNK, qk_w), lambda i, c: (i, c, 7)),
                  pl.BlockSpec((None, M_CHUNK, v_w), lambda i, c: (i, c, 4)),
                  pl.BlockSpec((None, M_CHUNK, v_w), lambda i, c: (i, c, 5)),
                  pl.BlockSpec((None, M_CHUNK, LANES_V7X), lambda i, c: (i, c, gate_blk)),
                  pl.BlockSpec((None, None, 2 * H, M_CHUNK), lambda i, c: (i, c, 0, 0)),
                  pl.BlockSpec((M_CONV, 2 * qk_w), lambda i, c: (0, 0)),
                  pl.BlockSpec((1, 2 * qk_w), lambda i, c: (0, 0)),
                  pl.BlockSpec((1, 2 * H), lambda i, c: (0, 0)),
                  pl.BlockSpec((2 * H, 1), lambda i, c: (0, 0)),
                  pl.BlockSpec((1, v_w), lambda i, c: (0, 0))],
        out_specs=pl.BlockSpec((None, M_CHUNK, v_w), lambda i, c: (i, c, 0)),
        scratch_shapes=[pltpu.VMEM((8 + M_CHUNK, 2 * qk_w), F32),
                        pltpu.VMEM((H, M_DK, M_DV), F32),
                        pltpu.VMEM((H, 1, M_DK), F32),
                        pltpu.VMEM((H, 1, LANES_V7X), F32)],
        compiler_params=_cparams(("parallel", "arbitrary")),
        name="mlstm",
    )(z3, z3, z3, z3, z3, gates_r, conv_w, conv_b.reshape(1, -1), bias.reshape(1, -1), bias.reshape(-1, 1),
      gain.reshape(1, v_w))


def _ffn_kernel(xp_ref, x_ref, g_ref, wa_ref, wb_ref, cw_ref, cb_ref, wd_ref, o_ref, xn_ref, a_ref,
                *, tm, seq):
    i = pl.program_id(0)
    j = pl.program_id(1)
    halo = 8

    @pl.when(j == 0)
    def _():
        g = g_ref[...]
        x = x_ref[...]
        xn_ref[halo:halo + tm, :] = _rms(x, g).astype(BF16)
        prev = _rms(xp_ref[...], g)
        keep = ((i * tm) % seq != 0).astype(F32)
        xn_ref[0:halo, :] = (prev * keep).astype(BF16)
        o_ref[...] = x

    a_ref[...] = _dot(xn_ref[...], wa_ref[...])
    conv = cb_ref[...]
    for t in range(FFN_CONV):
        conv = conv + a_ref[pl.ds(halo - (FFN_CONV - 1) + t, tm), :] * cw_ref[t:t + 1, :]
    bgate = _dot(xn_ref[halo:halo + tm, :], wb_ref[...])
    act = (jax.nn.gelu(conv) * bgate).astype(BF16)
    o_ref[...] += _dot(act, wd_ref[...])


def conv_ffn(h, g, w_up, conv_w, conv_b, w_down, seq, tm=1024, tf=256):
    n, d = h.shape
    nj = D_FF // tf
    hb = tm // 8
    return pl.pallas_call(
        functools.partial(_ffn_kernel, tm=tm, seq=seq),
        out_shape=jax.ShapeDtypeStruct((n, d), F32),
        grid=(n // tm, nj),
        in_specs=[pl.BlockSpec((8, d), lambda i, j: (jnp.maximum(i * hb - 1, 0), 0)),
                  pl.BlockSpec((tm, d), lambda i, j: (i, 0)),
                  pl.BlockSpec((1, d), lambda i, j: (0, 0)),
                  pl.BlockSpec((d, tf), lambda i, j: (0, j)),
                  pl.BlockSpec((d, tf), lambda i, j: (0, nj + j)),
                  pl.BlockSpec((FFN_CONV, tf), lambda i, j: (0, j)),
                  pl.BlockSpec((1, tf), lambda i, j: (0, j)),
                  pl.BlockSpec((tf, d), lambda i, j: (j, 0))],
        out_specs=pl.BlockSpec((tm, d), lambda i, j: (i, 0)),
        scratch_shapes=[pltpu.VMEM((8 + tm, d), BF16),
                        pltpu.VMEM((8 + tm, tf), F32)],
        compiler_params=_cparams(("parallel", "arbitrary"), VMEM_LIMIT_V7X),
        name="conv_ffn",
    )(h, h, g.reshape(1, d), w_up, w_up, conv_w, conv_b.reshape(1, -1), w_down)


def _ple_kernel(h_ref, p_ref, wp_ref, ng_ref, gg_ref, wg_ref, o_ref):
    h = h_ref[...]
    e = _rms(_dot(p_ref[...].astype(BF16), wp_ref[...]), ng_ref[...])
    gate = jax.nn.sigmoid(_dot(_rms(h, gg_ref[...]).astype(BF16), wg_ref[...]))
    o_ref[...] = h + gate * e


def ple(h, p, wp, norm_g, gate_norm_g, wg, tm=512):
    n, d = h.shape
    pd = p.shape[1]
    return pl.pallas_call(
        _ple_kernel,
        out_shape=jax.ShapeDtypeStruct((n, d), F32),
        grid=(n // tm,),
        in_specs=[pl.BlockSpec((tm, d), lambda i: (i, 0)),
                  pl.BlockSpec((tm, pd), lambda i: (i, 0)),
                  pl.BlockSpec((pd, d), lambda i: (0, 0)),
                  pl.BlockSpec((1, d), lambda i: (0, 0)),
                  pl.BlockSpec((1, d), lambda i: (0, 0)),
                  pl.BlockSpec((d, d), lambda i: (0, 0))],
        out_specs=pl.BlockSpec((tm, d), lambda i: (i, 0)),
        compiler_params=_cparams(("parallel",), VMEM_LIMIT_V7X),
        name="ple",
    )(h, p, wp, norm_g.reshape(1, d), gate_norm_g.reshape(1, d), wg)


def _group_rms(x, g):
    lane = lax.broadcasted_iota(jnp.int32, x.shape, 1)
    x2 = x * x
    ms = jnp.zeros_like(x)
    for grp in range(N_KV_GROUPS):
        in_grp = (lane >= grp * HEAD_DIM) & (lane < (grp + 1) * HEAD_DIM)
        tot = jnp.sum(jnp.where(in_grp, x2, 0.0), axis=-1, keepdims=True)
        ms = jnp.where(in_grp, tot * (1.0 / HEAD_DIM), ms)
    return x * lax.rsqrt(ms + EPS) * g


def _kv_prep_kernel(c_ref, s_ref, w_ref, gs_ref, gw_ref, kc_ref, vc_ref, ks_ref, vs_ref, kw_ref, vw_ref):
    gw = N_KV_GROUPS * HEAD_DIM
    cc = c_ref[...]
    ss = s_ref[...]
    ww = w_ref[...]
    kc_ref[...] = cc[:, 0:gw].astype(BF16)
    vc_ref[...] = cc[:, gw:2 * gw].astype(BF16)
    ks_ref[...] = _group_rms(ss[:, 0:gw], gs_ref[...]).astype(BF16)
    vs_ref[...] = ss[:, gw:2 * gw].astype(BF16)
    kw_ref[...] = _group_rms(ww[:, 0:gw], gw_ref[...]).astype(BF16)
    vw_ref[...] = ww[:, gw:2 * gw].astype(BF16)


def kv_prep(z, k_g, tm=512):
    n = z.shape[0]
    gw = N_KV_GROUPS * HEAD_DIM
    base = N_HEADS * HEAD_DIM // (2 * gw)
    shp = jax.ShapeDtypeStruct((n, gw), BF16)
    ospec = pl.BlockSpec((tm, gw), lambda i: (i, 0))
    return pl.pallas_call(
        _kv_prep_kernel,
        out_shape=(shp,) * 6,
        grid=(n // tm,),
        in_specs=[pl.BlockSpec((tm, 2 * gw), lambda i: (i, base)),
                  pl.BlockSpec((tm, 2 * gw), lambda i: (i, base + 1)),
                  pl.BlockSpec((tm, 2 * gw), lambda i: (i, base + 2)),
                  pl.BlockSpec((1, gw), lambda i: (0, 0)),
                  pl.BlockSpec((1, gw), lambda i: (0, 0))],
        out_specs=(ospec,) * 6,
        compiler_params=_cparams(("parallel",)),
        name="kv_prep",
    )(z, z, z, jnp.tile(k_g[1], N_KV_GROUPS).reshape(1, gw), jnp.tile(k_g[2], N_KV_GROUPS).reshape(1, gw))


def _compress_kernel(x_ref, pos_ref, w1_ref, w2_ref, g_ref, o_ref, *, normalize):
    half = w1_ref.shape[0] // 2
    x = x_ref[...]
    u = _dot(x, w1_ref[0:half, :])
    v = _dot(x, w1_ref[half:2 * half, :])
    rows = u.shape[0]
    posc = _dot(pos_ref[...], w1_ref[...])[0:1, :]
    hid = u + pltpu.roll(v, rows - 1, 0) + posc
    out = _dot(jax.nn.gelu(hid).astype(BF16), w2_ref[...])
    if normalize:
        out = _rms(out, g_ref[...])
    o_ref[...] = out.astype(o_ref.dtype)


def compress(x, pos, w1, w2, g, normalize, ncb):
    n, kdim = x.shape
    posf = jnp.broadcast_to(pos.reshape(1, -1), (8, 2 * kdim)).astype(BF16)
    return pl.pallas_call(
        functools.partial(_compress_kernel, normalize=normalize),
        out_shape=jax.ShapeDtypeStruct((n, HEAD_DIM), BF16),
        grid=(n // ncb,),
        in_specs=[pl.BlockSpec((ncb, kdim), lambda i: (i, 0)),
                  pl.BlockSpec((8, 2 * kdim), lambda i: (0, 0)),
                  pl.BlockSpec((2 * kdim, CMP_HIDDEN), lambda i: (0, 0)),
                  pl.BlockSpec((CMP_HIDDEN, HEAD_DIM), lambda i: (0, 0)),
                  pl.BlockSpec((1, HEAD_DIM), lambda i: (0, 0))],
        out_specs=pl.BlockSpec((ncb, HEAD_DIM), lambda i: (i, 0)),
        compiler_params=_cparams(("parallel",)),
        name="compress",
    )(x, posf, w1.astype(BF16), w2.astype(BF16), g.reshape(1, HEAD_DIM))


def _nsa_kernel(zq_ref, gt_ref, gb_ref, qg_ref, ovt_ref, kc_ref, vct_ref, ks_ref, vst_ref, kw_ref, vwt_ref,
                o_ref, q_scr, sel_scr, m_scr, l_scr, acc_scr, oc_scr):
    grp = pl.program_id(1)
    qb = pl.program_id(2)
    QB = Q_BLOCK
    ncp = kc_ref.shape[0]
    ns = ovt_ref.shape[0]
    tpos = qb * QB + lax.broadcasted_iota(jnp.int32, (1, QB), 1)

    def lane_tile(x):
        return jnp.concatenate([x] * HG, axis=1)

    zt = zq_ref[...].T
    qg = qg_ref[...] * (HEAD_DIM ** -0.5)
    heads = []
    for h in range(HG):
        xh = zt[h * HEAD_DIM:(h + 1) * HEAD_DIM, :]
        ms = jnp.mean(xh * xh, axis=0, keepdims=True)
        heads.append(xh * lax.rsqrt(ms + EPS) * qg)
    q_scr[...] = jnp.concatenate(heads, axis=1).astype(BF16)
    q = q_scr[...]

    cmp_end = lax.broadcasted_iota(jnp.int32, (ncp, 1), 0) * CMP_STRIDE + (CMP_BLOCK - 1)
    cbias = jnp.where(cmp_end <= tpos, 0.0, NEG)
    s = _dot(kc_ref[...], q) + lane_tile(cbias)
    e = jnp.exp(s - jnp.max(s, axis=0, keepdims=True))
    inv = jnp.where(lane_tile(tpos) >= CMP_BLOCK - 1, 1.0 / jnp.sum(e, axis=0, keepdims=True), 0.0)
    p = e * inv
    oc_scr[...] = _dot(vct_ref[...], p.astype(BF16))
    psum = p[:, 0:QB]
    for h in range(1, HG):
        psum = psum + p[:, h * QB:(h + 1) * QB]

    p_hi = psum.astype(BF16)
    p_lo = (psum - p_hi.astype(F32)).astype(BF16)
    ovt = ovt_ref[...]
    imp = _dot(ovt, p_hi) + _dot(ovt, p_lo)
    blk = lax.broadcasted_iota(jnp.int32, (ns, 1), 0)
    blk_f = blk.astype(F32)
    cur = jnp.right_shift(tpos, SLC_BLOCK.bit_length() - 1)
    forced = (blk == 0) | (blk == cur) | (blk == cur - 1)
    bvalid = blk <= cur
    score = jnp.where(forced, BIG, jnp.where(bvalid, imp, NEG))
    sel = jnp.zeros((ns, QB), F32)
    for _ in range(min(N_SELECT, ns)):
        mx = jnp.max(score, axis=0, keepdims=True)
        first = jnp.min(jnp.where(score == mx, blk_f, float(ns)), axis=0, keepdims=True)
        pick = blk_f == first
        sel = jnp.where(pick, 1.0, sel)
        score = jnp.where(pick, -jnp.inf, score)
    sel_scr[...] = jnp.where(bvalid, sel, 0.0)

    gates = jax.nn.sigmoid(gt_ref[...].T + gb_ref[...])
    per_grp = HG * 3
    gsel = gates[0:per_grp, :]
    for g2 in range(1, N_KV_GROUPS):
        gsel = jnp.where(grp == g2, gates[g2 * per_grp:(g2 + 1) * per_grp, :], gsel)

    def gate_row(c):
        return jnp.concatenate([gsel[3 * h + c:3 * h + c + 1, :] for h in range(HG)], axis=1)

    key_sub = lax.broadcasted_iota(jnp.int32, (KEY_TILE, 1), 0)

    def reset_state():
        m_scr[...] = jnp.full(m_scr.shape, NEG, F32)
        l_scr[...] = jnp.zeros(l_scr.shape, F32)
        acc_scr[...] = jnp.zeros(acc_scr.shape, F32)

    def attend(k, vt, bias):
        st = _dot(k, q_scr[...]) + lane_tile(bias)
        m_old = m_scr[...]
        m_new = jnp.maximum(m_old, jnp.max(st, axis=0, keepdims=True))
        alpha = jnp.exp(m_old - m_new)
        pt = jnp.exp(st - m_new)
        l_scr[...] = alpha * l_scr[...] + jnp.sum(pt, axis=0, keepdims=True)
        acc_scr[...] = alpha * acc_scr[...] + _dot(vt, pt.astype(BF16))
        m_scr[...] = m_new

    reset_state()
    blocks_per_tile = KEY_TILE // SLC_BLOCK

    def sel_body(j, carry):
        chosen = sel_scr[pl.ds(j * blocks_per_tile, 1), :]
        for bi in range(1, blocks_per_tile):
            row = sel_scr[pl.ds(j * blocks_per_tile + bi, 1), :]
            chosen = jnp.where(key_sub < bi * SLC_BLOCK, chosen, row)
        kpos = j * KEY_TILE + key_sub
        bias = jnp.where((chosen > 0.5) & (kpos <= tpos), 0.0, NEG)
        attend(ks_ref[j], vst_ref[j], bias)
        return carry

    lax.fori_loop(0, qb + 1, sel_body, 0)
    oc_scr[...] = gate_row(0) * oc_scr[...] + (gate_row(1) / l_scr[...]) * acc_scr[...]

    reset_state()
    n_win = WINDOW // KEY_TILE

    def win_body(w, carry):
        j = qb - w
        kpos = j * KEY_TILE + key_sub
        bias = jnp.where((kpos <= tpos) & (kpos > tpos - WINDOW), 0.0, NEG)
        attend(kw_ref[j], vwt_ref[j], bias)
        return carry

    lax.fori_loop(0, jnp.minimum(qb, n_win) + 1, win_body, 0)
    ot = oc_scr[...] + (gate_row(2) / l_scr[...]) * acc_scr[...]
    stacked = jnp.concatenate([ot[:, h * QB:(h + 1) * QB] for h in range(HG)], axis=0)
    o_ref[...] = stacked.T


def nsa_attention(z3, gate_b, q_g, overlap_t, kc, vct, ks, vst, kw, vwt):
    b, t, _ = z3.shape
    nq = t // Q_BLOCK
    nt = t // KEY_TILE
    qw = HG * HEAD_DIM
    ns, ncp = overlap_t.shape
    gate_blk = (NSA_COLS_PAD // LANES_V7X) - 1
    gb = jnp.zeros((LANES_V7X, 1), F32).at[:N_HEADS * 3, 0].set(gate_b)
    full5 = lambda i, g, q: (i, g, 0, 0, 0)
    lanes = HG * Q_BLOCK
    return pl.pallas_call(
        _nsa_kernel,
        out_shape=jax.ShapeDtypeStruct((b, t, N_HEADS * HEAD_DIM), F32),
        grid=(b, N_KV_GROUPS, nq),
        in_specs=[pl.BlockSpec((None, Q_BLOCK, qw), lambda i, g, q: (i, q, g)),
                  pl.BlockSpec((None, Q_BLOCK, LANES_V7X), lambda i, g, q: (i, q, gate_blk)),
                  pl.BlockSpec((LANES_V7X, 1), lambda i, g, q: (0, 0)),
                  pl.BlockSpec((HEAD_DIM, 1), lambda i, g, q: (0, 0)),
                  pl.BlockSpec((ns, ncp), lambda i, g, q: (0, 0)),
                  pl.BlockSpec((None, None, ncp, HEAD_DIM), lambda i, g, q: (i, g, 0, 0)),
                  pl.BlockSpec((None, None, HEAD_DIM, ncp), lambda i, g, q: (i, g, 0, 0)),
                  pl.BlockSpec((None, None, nt, KEY_TILE, HEAD_DIM), full5),
                  pl.BlockSpec((None, None, nt, HEAD_DIM, KEY_TILE), full5),
                  pl.BlockSpec((None, None, nt, KEY_TILE, HEAD_DIM), full5),
                  pl.BlockSpec((None, None, nt, HEAD_DIM, KEY_TILE), full5)],
        out_specs=pl.BlockSpec((None, Q_BLOCK, qw), lambda i, g, q: (i, q, g)),
        scratch_shapes=[pltpu.VMEM((HEAD_DIM, lanes), BF16),
                        pltpu.VMEM((ns, Q_BLOCK), F32),
                        pltpu.VMEM((1, lanes), F32),
                        pltpu.VMEM((1, lanes), F32),
                        pltpu.VMEM((HEAD_DIM, lanes), F32),
                        pltpu.VMEM((HEAD_DIM, lanes), F32)],
        compiler_params=_cparams(("parallel", "parallel", "arbitrary"), VMEM_LIMIT_V7X),
        name="nsa_attention",
    )(z3, z3, gb, q_g.reshape(HEAD_DIM, 1), overlap_t, kc, vct, ks, vst, kw, vwt)


def _overlap_matrix_t(t):
    ncp = t // CMP_STRIDE
    ns = t // SLC_BLOCK
    c_start = np.arange(ncp) * CMP_STRIDE
    sj = np.arange(ns)
    ov = ((c_start[None, :] < (sj[:, None] + 1) * SLC_BLOCK)
          & (c_start[None, :] + CMP_BLOCK > sj[:, None] * SLC_BLOCK)
          & (c_start[None, :] + CMP_BLOCK <= t))
    return jnp.asarray(ov, dtype=BF16)


def ab_layer(h, b, t, norm_g, w_in, conv_w, conv_b, ret_g, ig_b, fg_b, m_g, w_out):
    n = b * t
    w_in_p = jnp.pad(w_in, ((0, 0), (0, AB_COLS_PAD - AB_COLS))).astype(BF16)
    z = norm_matmul(h, norm_g, w_in_p)
    z3 = z.reshape(b, t, AB_COLS_PAD)
    cos_t, sin_t = rope_tables(t)
    ret = retention(z3, cos_t, sin_t, ret_g)
    g0 = AB_COLS - 2 * M_HEADS
    gates_r = z3[:, :, g0:AB_COLS].reshape(b, t // M_CHUNK, M_CHUNK, 2 * M_HEADS).transpose(0, 1, 3, 2)
    ml = mlstm(z3, gates_r, conv_w, conv_b, ig_b, fg_b, m_g)
    w_out_b = w_out.astype(BF16)
    rw = R_HEADS * R_DV
    return proj_residual(h, [(ret.reshape(n, rw), w_out_b[:rw]), (ml.reshape(n, -1), w_out_b[rw:])])


def nsa_layer(h, b, t, norm_g, w_in, q_g, k_g, pos_k, pos_v, w1k, w2k, w1v, w2v, gate_b, w_out):
    n = b * t
    G = N_KV_GROUPS
    w_in_p = jnp.pad(w_in, ((0, 0), (0, NSA_COLS_PAD - NSA_COLS))).astype(BF16)
    z = norm_matmul(h, norm_g, w_in_p)
    kc_in, vc_in, ks, vs, kw, vw = kv_prep(z, k_g)
    nt = t // KEY_TILE
    ncb = t // CMP_STRIDE

    def cmp_rows(x):
        return x.reshape(b, t, G, HEAD_DIM).transpose(0, 2, 1, 3).reshape(b * G * ncb, CMP_STRIDE * HEAD_DIM)

    kc = compress(cmp_rows(kc_in), pos_k, w1k, w2k, k_g[0], True, ncb)
    vc = compress(cmp_rows(vc_in), pos_v, w1v, w2v, k_g[0], False, ncb)
    kc = kc.reshape(b, G, ncb, HEAD_DIM)
    vct = vc.reshape(b, G, ncb, HEAD_DIM).transpose(0, 1, 3, 2)

    def key_tiles(x):
        return x.reshape(b, nt, KEY_TILE, G, HEAD_DIM).transpose(0, 3, 1, 2, 4)

    def val_tiles_t(x):
        return x.reshape(b, nt, KEY_TILE, G, HEAD_DIM).transpose(0, 3, 1, 4, 2)

    o = nsa_attention(z.reshape(b, t, NSA_COLS_PAD), gate_b, q_g, _overlap_matrix_t(t), kc, vct,
                      key_tiles(ks), val_tiles_t(vs), key_tiles(kw), val_tiles_t(vw))
    return proj_residual(h, [(o.reshape(n, -1), w_out.astype(BF16))])


def kernel(x, p, ab_norm_g, ab_w_in, ab_conv_w, ab_conv_b, ab_ret_norm_g, ab_ig_b, ab_fg_b, ab_m_norm_g, ab_w_out, nsa_norm_g, nsa_w_in, nsa_q_norm_g, nsa_k_norm_g, nsa_cmp_pos_k, nsa_cmp_pos_v, nsa_cmp_w1k, nsa_cmp_w2k, nsa_cmp_w1v, nsa_cmp_w2v, nsa_gate_b, nsa_w_out, ffn_norm_g, ffn_w_up, ffn_conv_w, ffn_conv_b, ffn_w_down, ple_w, ple_norm_g, ple_gate_norm_g, ple_w_gate):
    b, t, d = x.shape
    n = b * t
    depth = p.shape[0]
    h = x.reshape(n, d)
    for i in range(depth):
        j = i // 2
        if i % 2 == 0:
            h = ab_layer(h, b, t, ab_norm_g[j], ab_w_in[j], ab_conv_w[j], ab_conv_b[j], ab_ret_norm_g[j],
                         ab_ig_b[j], ab_fg_b[j], ab_m_norm_g[j], ab_w_out[j])
        else:
            h = nsa_layer(h, b, t, nsa_norm_g[j], nsa_w_in[j], nsa_q_norm_g[j], nsa_k_norm_g[j],
                          nsa_cmp_pos_k[j], nsa_cmp_pos_v[j], nsa_cmp_w1k[j], nsa_cmp_w2k[j],
                          nsa_cmp_w1v[j], nsa_cmp_w2v[j], nsa_gate_b[j], nsa_w_out[j])
        h = conv_ffn(h, ffn_norm_g[i], ffn_w_up[i].astype(BF16), ffn_conv_w[i], ffn_conv_b[i],
                     ffn_w_down[i].astype(BF16), t)
        h = ple(h, p[i].reshape(n, -1), ple_w[i].astype(BF16), ple_norm_g[i], ple_gate_norm_g[i],
                ple_w_gate[i].astype(BF16))
    return h.reshape(b, t, d)
```

```python
import functools
import math

import numpy as np
import jax
import jax.numpy as jnp
from jax import lax
from jax.experimental import pallas as pl
from jax.experimental.pallas import tpu as pltpu

F32 = jnp.float32
BF16 = jnp.bfloat16

LANES_V7X = 128
VMEM_LIMIT_V7X = 56 * 1024 * 1024

D_MODEL = 1024
PLE_DIM = 256
R_HEADS, R_DK, R_DV, R_CHUNK = 4, 64, 128, 128
ROPE_BASE = 10000.0
M_HEADS, M_DK, M_DV, M_CHUNK, M_CONV = 4, 64, 128, 64, 4
AB_SIZES = (R_HEADS * R_DK, R_HEADS * R_DK, R_HEADS * R_DV, R_HEADS * R_DV,
            M_HEADS * M_DK, M_HEADS * M_DK, M_HEADS * M_DV, M_HEADS * M_DV, M_HEADS, M_HEADS)
AB_COLS = sum(AB_SIZES)
AB_COLS_PAD = 3200
N_HEADS, N_KV_GROUPS, HEAD_DIM = 16, 2, 64
HG = N_HEADS // N_KV_GROUPS
CMP_BLOCK, CMP_STRIDE, CMP_HIDDEN = 32, 16, 256
SLC_BLOCK, N_SELECT, WINDOW, Q_BLOCK = 64, 16, 512, 128
NSA_COLS = N_HEADS * HEAD_DIM + 6 * N_KV_GROUPS * HEAD_DIM + N_HEADS * 3
NSA_COLS_PAD = 1920
D_FF = 2816
FFN_CONV = 3
NEG = -1e30
BIG = 1e30
EPS = 1e-6
KEY_TILE = 128
SEL_TILE = 512


def _cparams(sem, vmem=None):
    return pltpu.CompilerParams(dimension_semantics=sem, vmem_limit_bytes=vmem)


def _rms(x, g):
    ms = jnp.mean(x * x, axis=-1, keepdims=True)
    return x * lax.rsqrt(ms + EPS) * g


def _dot(a, b):
    return jnp.dot(a, b, preferred_element_type=F32)


def _dot_nt(a, b):
    return lax.dot_general(a, b, (((1,), (1,)), ((), ())), preferred_element_type=F32)


def _dot_f32(a, b):
    return jnp.dot(a, b, preferred_element_type=F32, precision=lax.Precision.HIGHEST)


def _norm_matmul_kernel(x_ref, g_ref, w_ref, o_ref):
    xn = _rms(x_ref[...], g_ref[...]).astype(BF16)
    o_ref[...] = _dot(xn, w_ref[...]).astype(o_ref.dtype)


def norm_matmul(x, g, w, tm=512, out_dtype=F32):
    n, d = x.shape
    nc = w.shape[1]
    return pl.pallas_call(
        _norm_matmul_kernel,
        out_shape=jax.ShapeDtypeStruct((n, nc), out_dtype),
        grid=(n // tm,),
        in_specs=[pl.BlockSpec((tm, d), lambda i: (i, 0)),
                  pl.BlockSpec((1, d), lambda i: (0, 0)),
                  pl.BlockSpec((d, nc), lambda i: (0, 0))],
        out_specs=pl.BlockSpec((tm, nc), lambda i: (i, 0)),
        compiler_params=_cparams(("parallel",), VMEM_LIMIT_V7X),
        name="norm_matmul",
    )(x, g.reshape(1, d), w)


def _proj_residual_kernel(*refs, n_in):
    h_ref = refs[0]
    o_ref = refs[1 + 2 * n_in]
    acc = h_ref[...]
    for k in range(n_in):
        acc = acc + _dot(refs[1 + 2 * k][...].astype(BF16), refs[2 + 2 * k][...])
    o_ref[...] = acc


def proj_residual(h, pairs, tm=512):
    n, d = h.shape
    in_specs = [pl.BlockSpec((tm, d), lambda i: (i, 0))]
    args = [h]
    for a, w in pairs:
        in_specs.append(pl.BlockSpec((tm, a.shape[1]), lambda i: (i, 0)))
        in_specs.append(pl.BlockSpec(w.shape, lambda i: (0, 0)))
        args += [a, w]
    return pl.pallas_call(
        functools.partial(_proj_residual_kernel, n_in=len(pairs)),
        out_shape=jax.ShapeDtypeStruct((n, d), F32),
        grid=(n // tm,),
        in_specs=in_specs,
        out_specs=pl.BlockSpec((tm, d), lambda i: (i, 0)),
        compiler_params=_cparams(("parallel",), VMEM_LIMIT_V7X),
        name="proj_residual",
    )(*args)


def _rope_table_kernel(inv_ref, cos_ref, sin_ref):
    c = pl.program_id(0)
    rows, width = cos_ref.shape
    pos = (c * rows + lax.broadcasted_iota(jnp.int32, (rows, width), 0)).astype(F32)
    lane = lax.broadcasted_iota(jnp.int32, (rows, width), 1)
    ang = pos * inv_ref[...]
    cos_ref[...] = jnp.cos(ang)
    sn = jnp.sin(ang)
    sin_ref[...] = jnp.where(lane % R_DK < R_DK // 2, -sn, sn)


def rope_tables(t):
    half = R_DK // 2
    inv = ROPE_BASE ** (-jnp.arange(half, dtype=F32) / half)
    inv = jnp.tile(inv, 2 * R_HEADS).reshape(1, R_HEADS * R_DK)
    width = R_HEADS * R_DK
    shp = jax.ShapeDtypeStruct((t, width), F32)
    return pl.pallas_call(
        _rope_table_kernel,
        out_shape=(shp, shp),
        grid=(t // R_CHUNK,),
        in_specs=[pl.BlockSpec((1, width), lambda c: (0, 0))],
        out_specs=(pl.BlockSpec((R_CHUNK, width), lambda c: (c, 0)),
                   pl.BlockSpec((R_CHUNK, width), lambda c: (c, 0))),
        compiler_params=_cparams(("parallel",)),
        name="rope_tables",
    )(inv)


def _retention_kernel(cos_ref, sin_ref, q_ref, k_ref, v_ref, g_ref, gain_ref, o_ref, r_ref):
    c = pl.program_id(1)
    L = R_CHUNK

    @pl.when(c == 0)
    def _():
        r_ref[...] = jnp.zeros_like(r_ref)

    cos = cos_ref[...]
    sin = sin_ref[...]
    lane = lax.broadcasted_iota(jnp.int32, cos.shape, 1)
    first_half = lane % R_DK < R_DK // 2
    width = R_HEADS * R_DK

    def rot(x):
        swapped = jnp.where(first_half, pltpu.roll(x, width - R_DK // 2, 1), pltpu.roll(x, R_DK // 2, 1))
        return x * cos + swapped * sin

    q = rot(q_ref[...])
    k = rot(k_ref[...]) * (R_DK ** -0.5)
    v = v_ref[...]
    gate = g_ref[...]
    gain = gain_ref[...]

    ri = lax.broadcasted_iota(jnp.int32, (L, L), 0)
    ci = lax.broadcasted_iota(jnp.int32, (L, L), 1)
    diff = (ri - ci).astype(F32)
    causal = ri >= ci
    idx = lax.broadcasted_iota(jnp.int32, (L, 1), 0).astype(F32)

    for h in range(R_HEADS):
        log_g = math.log1p(-2.0 ** (-5.0 - h))
        dmask = jnp.where(causal, jnp.exp(jnp.where(causal, diff, 0.0) * log_g), 0.0)
        q_dec = jnp.exp((idx + 1.0) * log_g)
        k_dec = jnp.exp((L - 1.0 - idx) * log_g)
        c_dec = math.exp(L * log_g)
        qh = q[:, h * R_DK:(h + 1) * R_DK]
        kh = k[:, h * R_DK:(h + 1) * R_DK]
        vh = v[:, h * R_DV:(h + 1) * R_DV].astype(BF16)
        qb = qh.astype(BF16)
        s = _dot_nt(qb, kh.astype(BF16)) * dmask
        rstate = r_ref[h]
        o = _dot(s.astype(BF16), vh) + _dot(qb, rstate.astype(BF16)) * q_dec
        kd = (kh * k_dec).T.astype(BF16)
        r_ref[h] = c_dec * rstate + _dot(kd, vh)
        sl = slice(h * R_DV, (h + 1) * R_DV)
        gh = gate[:, sl]
        o_ref[:, sl] = _rms(o, gain[:, sl]) * (gh * jax.nn.sigmoid(gh))


def retention(z3, cos_t, sin_t, gain):
    b, t, _ = z3.shape
    nc = t // R_CHUNK
    qk_w = R_HEADS * R_DK
    v_w = R_HEADS * R_DV
    return pl.pallas_call(
        _retention_kernel,
        out_shape=jax.ShapeDtypeStruct((b, t, v_w), F32),
        grid=(b, nc),
        in_specs=[pl.BlockSpec((R_CHUNK, qk_w), lambda i, c: (c, 0)),
                  pl.BlockSpec((R_CHUNK, qk_w), lambda i, c: (c, 0)),
                  pl.BlockSpec((None, R_CHUNK, qk_w), lambda i, c: (i, c, 0)),
                  pl.BlockSpec((None, R_CHUNK, qk_w), lambda i, c: (i, c, 1)),
                  pl.BlockSpec((None, R_CHUNK, v_w), lambda i, c: (i, c, 1)),
                  pl.BlockSpec((None, R_CHUNK, v_w), lambda i, c: (i, c, 2)),
                  pl.BlockSpec((1, v_w), lambda i, c: (0, 0))],
        out_specs=pl.BlockSpec((None, R_CHUNK, v_w), lambda i, c: (i, c, 0)),
        scratch_shapes=[pltpu.VMEM((R_HEADS, R_DK, R_DV), F32)],
        compiler_params=_cparams(("parallel", "arbitrary")),
        name="retention",
    )(cos_t, sin_t, z3, z3, z3, z3, gain.reshape(1, v_w))


def _mlstm_kernel(q_ref, k_ref, v_ref, og_ref, gc_ref, gr_ref, cw_ref, cb_ref, bc_ref, br_ref, gain_ref,
                  o_ref, xbuf, c_ref, n_ref, m_ref):
    c = pl.program_id(1)
    L = M_CHUNK
    H = M_HEADS
    qk_w = H * M_DK
    halo = 8

    @pl.when(c == 0)
    def _():
        xbuf[0:halo, :] = jnp.zeros((halo, 2 * qk_w), F32)
        c_ref[...] = jnp.zeros_like(c_ref)
        n_ref[...] = jnp.zeros_like(n_ref)
        m_ref[...] = jnp.zeros_like(m_ref)

    xbuf[halo:halo + L, 0:qk_w] = q_ref[...]
    xbuf[halo:halo + L, qk_w:2 * qk_w] = k_ref[...]
    conv = cb_ref[...]
    for j in range(M_CONV):
        conv = conv + xbuf[pl.ds(halo - (M_CONV - 1) + j, L), :] * cw_ref[j:j + 1, :]
    tail = xbuf[L:L + halo, :]
    xbuf[0:halo, :] = tail
    act = conv * jax.nn.sigmoid(conv)
    q = act[:, 0:qk_w] * (M_DK ** -0.5)
    k = act[:, qk_w:2 * qk_w]
    v = v_ref[...]
    og = og_ref[...]
    gain = gain_ref[...]

    gc = gc_ref[...][:, 0:2 * H] + bc_ref[...]
    gr = gr_ref[...] + br_ref[...]
    ig_c = gc[:, 0:H]
    lf_c = jax.nn.log_sigmoid(gc[:, H:2 * H])
    ig_r = gr[0:H, :]
    lf_r = jax.nn.log_sigmoid(gr[H:2 * H, :])
    ri = lax.broadcasted_iota(jnp.int32, (L, L), 0)
    ci = lax.broadcasted_iota(jnp.int32, (L, L), 1)
    causal = ri >= ci
    tril = causal.astype(F32)
    triu = (ri <= ci).astype(F32)
    b_c = _dot_f32(tril, lf_c)
    b_r = _dot_f32(lf_r, triu)

    for h in range(H):
        bh = b_c[:, h:h + 1]
        dlog = jnp.where(causal, bh - b_r[h:h + 1, :] + ig_r[h:h + 1, :], NEG)
        m_prev = m_ref[h][:, 0:1]
        inter = bh + m_prev
        m_t = jnp.maximum(inter, jnp.max(dlog, axis=-1, keepdims=True))
        qh = q[:, h * M_DK:(h + 1) * M_DK]
        kh = k[:, h * M_DK:(h + 1) * M_DK]
        vh = v[:, h * M_DV:(h + 1) * M_DV].astype(BF16)
        qb = qh.astype(BF16)
        s = _dot_nt(qb, kh.astype(BF16)) * jnp.exp(dlog - m_t)
        w_inter = jnp.exp(inter - m_t)
        cstate = c_ref[h]
        nstate = n_ref[h]
        num = _dot(s.astype(BF16), vh) + w_inter * _dot(qb, cstate.astype(BF16))
        den = jnp.sum(s, axis=-1, keepdims=True) + w_inter * jnp.sum(qh * nstate, axis=-1, keepdims=True)
        hh = num / jnp.maximum(jnp.abs(den), jnp.exp(-m_t))
        b_last = bh[L - 1:L, :]
        wlog = b_last - bh + ig_c[:, h:h + 1]
        m_new = jnp.maximum(b_last + m_prev, jnp.max(wlog, axis=0, keepdims=True))
        decay = jnp.exp(b_last + m_prev - m_new)
        wk = kh * jnp.exp(wlog - m_new)
        c_ref[h] = decay * cstate + _dot(wk.T.astype(BF16), vh)
        n_ref[h] = decay * nstate + jnp.sum(wk, axis=0, keepdims=True)
        m_ref[h] = jnp.broadcast_to(m_new, (1, LANES_V7X))
        sl = slice(h * M_DV, (h + 1) * M_DV)
        o_ref[:, sl] = jax.nn.sigmoid(og[:, sl]) * _rms(hh, gain[:, sl])


def mlstm(z3, gates_r, conv_w, conv_b, ig_b, fg_b, gain):
    b, t, _ = z3.shape
    nc = t // M_CHUNK
    H = M_HEADS
    qk_w = H * M_DK
    v_w = H * M_DV
    bias = jnp.concatenate([ig_b, fg_b])
    gate_blk = AB_COLS_PAD // LANES_V7X - 1
    return pl.pallas_call(
        _mlstm_kernel,
        out_shape=jax.ShapeDtypeStruct((b, t, v_w), F32),
        grid=(b, nc),
        in_specs=[pl.BlockSpec((None, M_CHUNK, qk_w), lambda i, c: (i, c, 6)),
                  pl.BlockSpec((None, M_CHUNK, qk_w), lambda i, c: (i, c, 7)),
                  pl.BlockSpec((None, M_CHUNK, v_w), lambda i, c: (i, c, 4)),
                  pl.BlockSpec((None, M_CHUNK, v_w), lambda i, c: (i, c, 5)),
                  pl.BlockSpec((None, M_CHUNK, LANES_V7X), lambda i, c: (i, c, gate_blk)),
                  pl.BlockSpec((None, None, 2 * H, M_CHUNK), lambda i, c: (i, c, 0, 0)),
                  pl.BlockSpec((M_CONV, 2 * qk_w), lambda i, c: (0, 0)),
                  pl.BlockSpec((1, 2 * qk_w), lambda i, c: (0, 0)),
                  pl.BlockSpec((1, 2 * H), lambda i, c: (0, 0)),
                  pl.BlockSpec((2 * H, 1), lambda i, c: (0, 0)),
                  pl.BlockSpec((1, v_w), lambda i, c: (0, 0))],
        out_specs=pl.BlockSpec((None, M_CHUNK, v_w), lambda i, c: (i, c, 0)),
        scratch_shapes=[pltpu.VMEM((8 + M_CHUNK, 2 * qk_w), F32),
                        pltpu.VMEM((H, M_DK, M_DV), F32),
                        pltpu.VMEM((H, 1, M_DK), F32),
                        pltpu.VMEM((H, 1, LANES_V7X), F32)],
        compiler_params=_cparams(("parallel", "arbitrary")),
        name="mlstm",
    )(z3, z3, z3, z3, z3, gates_r, conv_w, conv_b.reshape(1, -1), bias.reshape(1, -1), bias.reshape(-1, 1),
      gain.reshape(1, v_w))


def _ffn_kernel(xp_ref, x_ref, g_ref, wa_ref, wb_ref, cw_ref, cb_ref, wd_ref, o_ref, xn_ref, a_ref,
                *, tm, seq):
    i = pl.program_id(0)
    j = pl.program_id(1)
    halo = 8

    @pl.when(j == 0)
    def _():
        g = g_ref[...]
        x = x_ref[...]
        xn_ref[halo:halo + tm, :] = _rms(x, g).astype(BF16)
        prev = _rms(xp_ref[...], g)
        keep = ((i * tm) % seq != 0).astype(F32)
        xn_ref[0:halo, :] = (prev * keep).astype(BF16)
        o_ref[...] = x

    a_ref[...] = _dot(xn_ref[...], wa_ref[...])
    conv = cb_ref[...]
    for t in range(FFN_CONV):
        conv = conv + a_ref[pl.ds(halo - (FFN_CONV - 1) + t, tm), :] * cw_ref[t:t + 1, :]
    bgate = _dot(xn_ref[halo:halo + tm, :], wb_ref[...])
    act = (jax.nn.gelu(conv) * bgate).astype(BF16)
    o_ref[...] += _dot(act, wd_ref[...])


def conv_ffn(h, g, w_up, conv_w, conv_b, w_down, seq, tm=1024, tf=256):
    n, d = h.shape
    nj = D_FF // tf
    hb = tm // 8
    return pl.pallas_call(
        functools.partial(_ffn_kernel, tm=tm, seq=seq),
        out_shape=jax.ShapeDtypeStruct((n, d), F32),
        grid=(n // tm, nj),
        in_specs=[pl.BlockSpec((8, d), lambda i, j: (jnp.maximum(i * hb - 1, 0), 0)),
                  pl.BlockSpec((tm, d), lambda i, j: (i, 0)),
                  pl.BlockSpec((1, d), lambda i, j: (0, 0)),
                  pl.BlockSpec((d, tf), lambda i, j: (0, j)),
                  pl.BlockSpec((d, tf), lambda i, j: (0, nj + j)),
                  pl.BlockSpec((FFN_CONV, tf), lambda i, j: (0, j)),
                  pl.BlockSpec((1, tf), lambda i, j: (0, j)),
                  pl.BlockSpec((tf, d), lambda i, j: (j, 0))],
        out_specs=pl.BlockSpec((tm, d), lambda i, j: (i, 0)),
        scratch_shapes=[pltpu.VMEM((8 + tm, d), BF16),
                        pltpu.VMEM((8 + tm, tf), F32)],
        compiler_params=_cparams(("parallel", "arbitrary"), VMEM_LIMIT_V7X),
        name="conv_ffn",
    )(h, h, g.reshape(1, d), w_up, w_up, conv_w, conv_b.reshape(1, -1), w_down)


def _ple_kernel(h_ref, p_ref, wp_ref, ng_ref, gg_ref, wg_ref, o_ref):
    h = h_ref[...]
    e = _rms(_dot(p_ref[...].astype(BF16), wp_ref[...]), ng_ref[...])
    gate = jax.nn.sigmoid(_dot(_rms(h, gg_ref[...]).astype(BF16), wg_ref[...]))
    o_ref[...] = h + gate * e


def ple(h, p, wp, norm_g, gate_norm_g, wg, tm=512):
    n, d = h.shape
    pd = p.shape[1]
    return pl.pallas_call(
        _ple_kernel,
        out_shape=jax.ShapeDtypeStruct((n, d), F32),
        grid=(n // tm,),
        in_specs=[pl.BlockSpec((tm, d), lambda i: (i, 0)),
                  pl.BlockSpec((tm, pd), lambda i: (i, 0)),
                  pl.BlockSpec((pd, d), lambda i: (0, 0)),
                  pl.BlockSpec((1, d), lambda i: (0, 0)),
                  pl.BlockSpec((1, d), lambda i: (0, 0)),
                  pl.BlockSpec((d, d), lambda i: (0, 0))],
        out_specs=pl.BlockSpec((tm, d), lambda i: (i, 0)),
        compiler_params=_cparams(("parallel",), VMEM_LIMIT_V7X),
        name="ple",
    )(h, p, wp, norm_g.reshape(1, d), gate_norm_g.reshape(1, d), wg)


def _group_rms(x, g):
    lane = lax.broadcasted_iota(jnp.int32, x.shape, 1)
    x2 = x * x
    ms = jnp.zeros_like(x)
    for grp in range(N_KV_GROUPS):
        in_grp = (lane >= grp * HEAD_DIM) & (lane < (grp + 1) * HEAD_DIM)
        tot = jnp.sum(jnp.where(in_grp, x2, 0.0), axis=-1, keepdims=True)
        ms = jnp.where(in_grp, tot * (1.0 / HEAD_DIM), ms)
    return x * lax.rsqrt(ms + EPS) * g


def _kv_prep_kernel(c_ref, s_ref, w_ref, gs_ref, gw_ref, kc_ref, vc_ref, ks_ref, vs_ref, kw_ref, vw_ref):
    gw = N_KV_GROUPS * HEAD_DIM
    cc = c_ref[...]
    ss = s_ref[...]
    ww = w_ref[...]
    kc_ref[...] = cc[:, 0:gw].astype(BF16)
    vc_ref[...] = cc[:, gw:2 * gw].astype(BF16)
    ks_ref[...] = _group_rms(ss[:, 0:gw], gs_ref[...]).astype(BF16)
    vs_ref[...] = ss[:, gw:2 * gw].astype(BF16)
    kw_ref[...] = _group_rms(ww[:, 0:gw], gw_ref[...]).astype(BF16)
    vw_ref[...] = ww[:, gw:2 * gw].astype(BF16)


def kv_prep(z, k_g, tm=512):
    n = z.shape[0]
    gw = N_KV_GROUPS * HEAD_DIM
    base = N_HEADS * HEAD_DIM // (2 * gw)
    shp = jax.ShapeDtypeStruct((n, gw), BF16)
    ospec = pl.BlockSpec((tm, gw), lambda i: (i, 0))
    return pl.pallas_call(
        _kv_prep_kernel,
        out_shape=(shp,) * 6,
        grid=(n // tm,),
        in_specs=[pl.BlockSpec((tm, 2 * gw), lambda i: (i, base)),
                  pl.BlockSpec((tm, 2 * gw), lambda i: (i, base + 1)),
                  pl.BlockSpec((tm, 2 * gw), lambda i: (i, base + 2)),
                  pl.BlockSpec((1, gw), lambda i: (0, 0)),
                  pl.BlockSpec((1, gw), lambda i: (0, 0))],
        out_specs=(ospec,) * 6,
        compiler_params=_cparams(("parallel",)),
        name="kv_prep",
    )(z, z, z, jnp.tile(k_g[1], N_KV_GROUPS).reshape(1, gw), jnp.tile(k_g[2], N_KV_GROUPS).reshape(1, gw))


def _compress_kernel(x_ref, pos_ref, w1_ref, w2_ref, g_ref, o_ref, *, normalize):
    half = w1_ref.shape[0] // 2
    x = x_ref[...]
    u = _dot(x, w1_ref[0:half, :])
    v = _dot(x, w1_ref[half:2 * half, :])
    rows = u.shape[0]
    posc = _dot(pos_ref[...], w1_ref[...])[0:1, :]
    hid = u + pltpu.roll(v, rows - 1, 0) + posc
    out = _dot(jax.nn.gelu(hid).astype(BF16), w2_ref[...])
    if normalize:
        out = _rms(out, g_ref[...])
    o_ref[...] = out.astype(o_ref.dtype)


def compress(x, pos, w1, w2, g, normalize, ncb):
    n, kdim = x.shape
    posf = jnp.broadcast_to(pos.reshape(1, -1), (8, 2 * kdim)).astype(BF16)
    return pl.pallas_call(
        functools.partial(_compress_kernel, normalize=normalize),
        out_shape=jax.ShapeDtypeStruct((n, HEAD_DIM), BF16),
        grid=(n // ncb,),
        in_specs=[pl.BlockSpec((ncb, kdim), lambda i: (i, 0)),
                  pl.BlockSpec((8, 2 * kdim), lambda i: (0, 0)),
                  pl.BlockSpec((2 * kdim, CMP_HIDDEN), lambda i: (0, 0)),
                  pl.BlockSpec((CMP_HIDDEN, HEAD_DIM), lambda i: (0, 0)),
                  pl.BlockSpec((1, HEAD_DIM), lambda i: (0, 0))],
        out_specs=pl.BlockSpec((ncb, HEAD_DIM), lambda i: (i, 0)),
        compiler_params=_cparams(("parallel",)),
        name="compress",
    )(x, posf, w1.astype(BF16), w2.astype(BF16), g.reshape(1, HEAD_DIM))


def _nsa_kernel(zq_ref, gt_ref, gb_ref, qg_ref, ovt_ref, kc_ref, vct_ref, ks_ref, vst_ref, kw_ref, vwt_ref,
                o_ref, q_scr, sel_scr, m_scr, l_scr, acc_scr, oc_scr):
    grp = pl.program_id(1)
    qb = pl.program_id(2)
    QB = Q_BLOCK
    ncp = kc_ref.shape[0]
    ns = ovt_ref.shape[0]
    tpos = qb * QB + lax.broadcasted_iota(jnp.int32, (1, QB), 1)

    def lane_tile(x):
        return jnp.concatenate([x] * HG, axis=1)

    zt = zq_ref[...].T
    qg = qg_ref[...] * (HEAD_DIM ** -0.5)
    heads = []
    for h in range(HG):
        xh = zt[h * HEAD_DIM:(h + 1) * HEAD_DIM, :]
        ms = jnp.mean(xh * xh, axis=0, keepdims=True)
        heads.append(xh * lax.rsqrt(ms + EPS) * qg)
    q_scr[...] = jnp.concatenate(heads, axis=1).astype(BF16)
    q = q_scr[...]

    cmp_end = lax.broadcasted_iota(jnp.int32, (ncp, 1), 0) * CMP_STRIDE + (CMP_BLOCK - 1)
    cbias = jnp.where(cmp_end <= tpos, 0.0, NEG)
    s = _dot(kc_ref[...], q) + lane_tile(cbias)
    e = jnp.exp(s - jnp.max(s, axis=0, keepdims=True))
    inv = jnp.where(lane_tile(tpos) >= CMP_BLOCK - 1, 1.0 / jnp.sum(e, axis=0, keepdims=True), 0.0)
    p = e * inv
    oc_scr[...] = _dot(vct_ref[...], p.astype(BF16))
    psum = p[:, 0:QB]
    for h in range(1, HG):
        psum = psum + p[:, h * QB:(h + 1) * QB]

    p_hi = psum.astype(BF16)
    p_lo = (psum - p_hi.astype(F32)).astype(BF16)
    ovt = ovt_ref[...]
    imp = _dot(ovt, p_hi) + _dot(ovt, p_lo)
    blk = lax.broadcasted_iota(jnp.int32, (ns, 1), 0)
    blk_f = blk.astype(F32)
    cur = jnp.right_shift(tpos, SLC_BLOCK.bit_length() - 1)
    forced = (blk == 0) | (blk == cur) | (blk == cur - 1)
    bvalid = blk <= cur
    score = jnp.where(forced, BIG, jnp.where(bvalid, imp, NEG))
    sel = jnp.zeros((ns, QB), F32)
    for _ in range(min(N_SELECT, ns)):
        mx = jnp.max(score, axis=0, keepdims=True)
        first = jnp.min(jnp.where(score == mx, blk_f, float(ns)), axis=0, keepdims=True)
        pick = blk_f == first
        sel = jnp.where(pick, 1.0, sel)
        score = jnp.where(pick, -jnp.inf, score)
    sel_scr[...] = jnp.where(bvalid, sel, 0.0)

    gates = jax.nn.sigmoid(gt_ref[...].T + gb_ref[...])
    per_grp = HG * 3
    gsel = gates[0:per_grp, :]
    for g2 in range(1, N_KV_GROUPS):
        gsel = jnp.where(grp == g2, gates[g2 * per_grp:(g2 + 1) * per_grp, :], gsel)

    def gate_row(c):
        return jnp.concatenate([gsel[3 * h + c:3 * h + c + 1, :] for h in range(HG)], axis=1)

    m_scr[...] = jnp.full(m_scr.shape, NEG, F32)
    l_scr[...] = jnp.zeros(l_scr.shape, F32)
    acc_scr[...] = jnp.zeros(acc_scr.shape, F32)
    blocks_per_tile = SEL_TILE // SLC_BLOCK
    sel_sub = lax.broadcasted_iota(jnp.int32, (SEL_TILE, 1), 0)

    def sel_body(j, carry):
        rows = [jnp.broadcast_to(sel_scr[pl.ds(j * blocks_per_tile + bi, 1), :], (SLC_BLOCK, QB))
                for bi in range(blocks_per_tile)]
        chosen = jnp.concatenate(rows, axis=0)
        kpos = j * SEL_TILE + sel_sub
        bias = jnp.where((chosen > 0.5) & (kpos <= tpos), 0.0, NEG)
        st = _dot(ks_ref[j], q_scr[...]) + lane_tile(bias)
        m_old = m_scr[...]
        m_new = jnp.maximum(m_old, jnp.max(st, axis=0, keepdims=True))
        alpha = jnp.exp(m_old - m_new)
        pt = jnp.exp(st - m_new)
        l_scr[...] = alpha * l_scr[...] + jnp.sum(pt, axis=0, keepdims=True)
        acc_scr[...] = alpha * acc_scr[...] + _dot(vst_ref[j], pt.astype(BF16))
        m_scr[...] = m_new
        return carry

    lax.fori_loop(0, qb // (SEL_TILE // QB) + 1, sel_body, 0)
    oc_scr[...] = gate_row(0) * oc_scr[...] + (gate_row(1) / l_scr[...]) * acc_scr[...]

    n_win = WINDOW // KEY_TILE + 1
    win_sub = lax.broadcasted_iota(jnp.int32, (n_win * KEY_TILE, 1), 0)
    tiles = [jnp.maximum(qb - (n_win - 1) + u, 0) for u in range(n_win)]
    kwin = jnp.concatenate([kw_ref[j] for j in tiles], axis=0)
    vwin = jnp.concatenate([vwt_ref[j] for j in tiles], axis=1)
    kpos = (qb - (n_win - 1)) * KEY_TILE + win_sub
    wbias = jnp.where((kpos <= tpos) & (kpos > tpos - WINDOW) & (kpos >= 0), 0.0, NEG)
    sw = _dot(kwin, q_scr[...]) + lane_tile(wbias)
    pw = jnp.exp(sw - jnp.max(sw, axis=0, keepdims=True))
    lw = jnp.sum(pw, axis=0, keepdims=True)
    ow = _dot(vwin, pw.astype(BF16))
    ot = oc_scr[...] + (gate_row(2) / lw) * ow
    stacked = jnp.concatenate([ot[:, h * QB:(h + 1) * QB] for h in range(HG)], axis=0)
    o_ref[...] = stacked.T


def nsa_attention(z3, gate_b, q_g, overlap_t, kc, vct, ks, vst, kw, vwt):
    b, t, _ = z3.shape
    nq = t // Q_BLOCK
    nt = t // KEY_TILE
    qw = HG * HEAD_DIM
    ns, ncp = overlap_t.shape
    gate_blk = (NSA_COLS_PAD // LANES_V7X) - 1
    gb = jnp.zeros((LANES_V7X, 1), F32).at[:N_HEADS * 3, 0].set(gate_b)
    full5 = lambda i, g, q: (i, g, 0, 0, 0)
    lanes = HG * Q_BLOCK
    return pl.pallas_call(
        _nsa_kernel,
        out_shape=jax.ShapeDtypeStruct((b, t, N_HEADS * HEAD_DIM), F32),
        grid=(b, N_KV_GROUPS, nq),
        in_specs=[pl.BlockSpec((None, Q_BLOCK, qw), lambda i, g, q: (i, q, g)),
                  pl.BlockSpec((None, Q_BLOCK, LANES_V7X), lambda i, g, q: (i, q, gate_blk)),
                  pl.BlockSpec((LANES_V7X, 1), lambda i, g, q: (0, 0)),
                  pl.BlockSpec((HEAD_DIM, 1), lambda i, g, q: (0, 0)),
                  pl.BlockSpec((ns, ncp), lambda i, g, q: (0, 0)),
                  pl.BlockSpec((None, None, ncp, HEAD_DIM), lambda i, g, q: (i, g, 0, 0)),
                  pl.BlockSpec((None, None, HEAD_DIM, ncp), lambda i, g, q: (i, g, 0, 0)),
                  pl.BlockSpec((None, None, t // SEL_TILE, SEL_TILE, HEAD_DIM), full5),
                  pl.BlockSpec((None, None, t // SEL_TILE, HEAD_DIM, SEL_TILE), full5),
                  pl.BlockSpec((None, None, nt, KEY_TILE, HEAD_DIM), full5),
                  pl.BlockSpec((None, None, nt, HEAD_DIM, KEY_TILE), full5)],
        out_specs=pl.BlockSpec((None, Q_BLOCK, qw), lambda i, g, q: (i, q, g)),
        scratch_shapes=[pltpu.VMEM((HEAD_DIM, lanes), BF16),
                        pltpu.VMEM((ns, Q_BLOCK), F32),
                        pltpu.VMEM((1, lanes), F32),
                        pltpu.VMEM((1, lanes), F32),
                        pltpu.VMEM((HEAD_DIM, lanes), F32),
                        pltpu.VMEM((HEAD_DIM, lanes), F32)],
        compiler_params=_cparams(("parallel", "parallel", "arbitrary"), VMEM_LIMIT_V7X),
        name="nsa_attention",
    )(z3, z3, gb, q_g.reshape(HEAD_DIM, 1), overlap_t, kc, vct, ks, vst, kw, vwt)


def _overlap_matrix_t(t):
    ncp = t // CMP_STRIDE
    ns = t // SLC_BLOCK
    c_start = np.arange(ncp) * CMP_STRIDE
    sj = np.arange(ns)
    ov = ((c_start[None, :] < (sj[:, None] + 1) * SLC_BLOCK)
          & (c_start[None, :] + CMP_BLOCK > sj[:, None] * SLC_BLOCK)
          & (c_start[None, :] + CMP_BLOCK <= t))
    return jnp.asarray(ov, dtype=BF16)


def ab_layer(h, b, t, norm_g, w_in, conv_w, conv_b, ret_g, ig_b, fg_b, m_g, w_out):
    n = b * t
    w_in_p = jnp.pad(w_in, ((0, 0), (0, AB_COLS_PAD - AB_COLS))).astype(BF16)
    z = norm_matmul(h, norm_g, w_in_p)
    z3 = z.reshape(b, t, AB_COLS_PAD)
    cos_t, sin_t = rope_tables(t)
    ret = retention(z3, cos_t, sin_t, ret_g)
    g0 = AB_COLS - 2 * M_HEADS
    gates_r = z3[:, :, g0:AB_COLS].reshape(b, t // M_CHUNK, M_CHUNK, 2 * M_HEADS).transpose(0, 1, 3, 2)
    ml = mlstm(z3, gates_r, conv_w, conv_b, ig_b, fg_b, m_g)
    w_out_b = w_out.astype(BF16)
    rw = R_HEADS * R_DV
    return proj_residual(h, [(ret.reshape(n, rw), w_out_b[:rw]), (ml.reshape(n, -1), w_out_b[rw:])])


def nsa_layer(h, b, t, norm_g, w_in, q_g, k_g, pos_k, pos_v, w1k, w2k, w1v, w2v, gate_b, w_out):
    n = b * t
    G = N_KV_GROUPS
    w_in_p = jnp.pad(w_in, ((0, 0), (0, NSA_COLS_PAD - NSA_COLS))).astype(BF16)
    z = norm_matmul(h, norm_g, w_in_p)
    kc_in, vc_in, ks, vs, kw, vw = kv_prep(z, k_g)
    nt = t // KEY_TILE
    ncb = t // CMP_STRIDE

    def cmp_rows(x):
        return x.reshape(b, t, G, HEAD_DIM).transpose(0, 2, 1, 3).reshape(b * G * ncb, CMP_STRIDE * HEAD_DIM)

    kc = compress(cmp_rows(kc_in), pos_k, w1k, w2k, k_g[0], True, ncb)
    vc = compress(cmp_rows(vc_in), pos_v, w1v, w2v, k_g[0], False, ncb)
    kc = kc.reshape(b, G, ncb, HEAD_DIM)
    vct = vc.reshape(b, G, ncb, HEAD_DIM).transpose(0, 1, 3, 2)

    def key_tiles(x, kt):
        return x.reshape(b, t // kt, kt, G, HEAD_DIM).transpose(0, 3, 1, 2, 4)

    def val_tiles_t(x, kt):
        return x.reshape(b, t // kt, kt, G, HEAD_DIM).transpose(0, 3, 1, 4, 2)

    o = nsa_attention(z.reshape(b, t, NSA_COLS_PAD), gate_b, q_g, _overlap_matrix_t(t), kc, vct,
                      key_tiles(ks, SEL_TILE), val_tiles_t(vs, SEL_TILE),
                      key_tiles(kw, KEY_TILE), val_tiles_t(vw, KEY_TILE))
    return proj_residual(h, [(o.reshape(n, -1), w_out.astype(BF16))])


def kernel(x, p, ab_norm_g, ab_w_in, ab_conv_w, ab_conv_b, ab_ret_norm_g, ab_ig_b, ab_fg_b, ab_m_norm_g, ab_w_out, nsa_norm_g, nsa_w_in, nsa_q_norm_g, nsa_k_norm_g, nsa_cmp_pos_k, nsa_cmp_pos_v, nsa_cmp_w1k, nsa_cmp_w2k, nsa_cmp_w1v, nsa_cmp_w2v, nsa_gate_b, nsa_w_out, ffn_norm_g, ffn_w_up, ffn_conv_w, ffn_conv_b, ffn_w_down, ple_w, ple_norm_g, ple_gate_norm_g, ple_w_gate):
    b, t, d = x.shape
    n = b * t
    depth = p.shape[0]
    h = x.reshape(n, d)
    for i in range(depth):
        j = i // 2
        if i % 2 == 0:
            h = ab_layer(h, b, t, ab_norm_g[j], ab_w_in[j], ab_conv_w[j], ab_conv_b[j], ab_ret_norm_g[j],
                         ab_ig_b[j], ab_fg_b[j], ab_m_norm_g[j], ab_w_out[j])
        else:
            h = nsa_layer(h, b, t, nsa_norm_g[j], nsa_w_in[j], nsa_q_norm_g[j], nsa_k_norm_g[j],
                          nsa_cmp_pos_k[j], nsa_cmp_pos_v[j], nsa_cmp_w1k[j], nsa_cmp_w2k[j],
                          nsa_cmp_w1v[j], nsa_cmp_w2v[j], nsa_gate_b[j], nsa_w_out[j])
        h = conv_ffn(h, ffn_norm_g[i], ffn_w_up[i].astype(BF16), ffn_conv_w[i], ffn_conv_b[i],
                     ffn_w_down[i].astype(BF16), t)
        h = ple(h, p[i].reshape(n, -1), ple_w[i].astype(BF16), ple_norm_g[i], ple_gate_norm_g[i],
                ple_w_gate[i].astype(BF16))
    return h.reshape(b, t, d)
```

```python
import functools
import math

import numpy as np
import jax
import jax.numpy as jnp
from jax import lax
from jax.experimental import pallas as pl
from jax.experimental.pallas import tpu as pltpu

F32 = jnp.float32
BF16 = jnp.bfloat16

LANES_V7X = 128
VMEM_LIMIT_V7X = 56 * 1024 * 1024

D_MODEL = 1024
PLE_DIM = 256
R_HEADS, R_DK, R_DV, R_CHUNK = 4, 64, 128, 128
ROPE_BASE = 10000.0
M_HEADS, M_DK, M_DV, M_CHUNK, M_CONV = 4, 64, 128, 64, 4
AB_SIZES = (R_HEADS * R_DK, R_HEADS * R_DK, R_HEADS * R_DV, R_HEADS * R_DV,
            M_HEADS * M_DK, M_HEADS * M_DK, M_HEADS * M_DV, M_HEADS * M_DV, M_HEADS, M_HEADS)
AB_COLS = sum(AB_SIZES)
AB_COLS_PAD = 3200
N_HEADS, N_KV_GROUPS, HEAD_DIM = 16, 2, 64
HG = N_HEADS // N_KV_GROUPS
CMP_BLOCK, CMP_STRIDE, CMP_HIDDEN = 32, 16, 256
SLC_BLOCK, N_SELECT, WINDOW, Q_BLOCK = 64, 16, 512, 128
NSA_COLS = N_HEADS * HEAD_DIM + 6 * N_KV_GROUPS * HEAD_DIM + N_HEADS * 3
NSA_COLS_PAD = 1920
D_FF = 2816
FFN_CONV = 3
NEG = -1e30
BIG = 1e30
EPS = 1e-6
KEY_TILE = 128
SEL_TILE = 512
LOG2E = math.log2(math.e)
MAX_SAFE_SCORE_BOUND = 56.0


def _cparams(sem, vmem=None):
    return pltpu.CompilerParams(dimension_semantics=sem, vmem_limit_bytes=vmem)


def _rms(x, g):
    ms = jnp.mean(x * x, axis=-1, keepdims=True)
    return x * lax.rsqrt(ms + EPS) * g


def _dot(a, b):
    return jnp.dot(a, b, preferred_element_type=F32)


def _dot_nt(a, b):
    return lax.dot_general(a, b, (((1,), (1,)), ((), ())), preferred_element_type=F32)


def _dot_f32(a, b):
    return jnp.dot(a, b, preferred_element_type=F32, precision=lax.Precision.HIGHEST)


def _norm_matmul_kernel(x_ref, g_ref, w_ref, o_ref):
    xn = _rms(x_ref[...], g_ref[...]).astype(BF16)
    o_ref[...] = _dot(xn, w_ref[...]).astype(o_ref.dtype)


def norm_matmul(x, g, w, tm=512, out_dtype=F32):
    n, d = x.shape
    nc = w.shape[1]
    return pl.pallas_call(
        _norm_matmul_kernel,
        out_shape=jax.ShapeDtypeStruct((n, nc), out_dtype),
        grid=(n // tm,),
        in_specs=[pl.BlockSpec((tm, d), lambda i: (i, 0)),
                  pl.BlockSpec((1, d), lambda i: (0, 0)),
                  pl.BlockSpec((d, nc), lambda i: (0, 0))],
        out_specs=pl.BlockSpec((tm, nc), lambda i: (i, 0)),
        compiler_params=_cparams(("parallel",), VMEM_LIMIT_V7X),
        name="norm_matmul",
    )(x, g.reshape(1, d), w)


def _proj_residual_kernel(*refs, n_in):
    h_ref = refs[0]
    o_ref = refs[1 + 2 * n_in]
    acc = h_ref[...]
    for k in range(n_in):
        acc = acc + _dot(refs[1 + 2 * k][...].astype(BF16), refs[2 + 2 * k][...])
    o_ref[...] = acc


def proj_residual(h, pairs, tm=512):
    n, d = h.shape
    in_specs = [pl.BlockSpec((tm, d), lambda i: (i, 0))]
    args = [h]
    for a, w in pairs:
        in_specs.append(pl.BlockSpec((tm, a.shape[1]), lambda i: (i, 0)))
        in_specs.append(pl.BlockSpec(w.shape, lambda i: (0, 0)))
        args += [a, w]
    return pl.pallas_call(
        functools.partial(_proj_residual_kernel, n_in=len(pairs)),
        out_shape=jax.ShapeDtypeStruct((n, d), F32),
        grid=(n // tm,),
        in_specs=in_specs,
        out_specs=pl.BlockSpec((tm, d), lambda i: (i, 0)),
        compiler_params=_cparams(("parallel",), VMEM_LIMIT_V7X),
        name="proj_residual",
    )(*args)


def _rope_table_kernel(inv_ref, cos_ref, sin_ref):
    c = pl.program_id(0)
    rows, width = cos_ref.shape
    pos = (c * rows + lax.broadcasted_iota(jnp.int32, (rows, width), 0)).astype(F32)
    lane = lax.broadcasted_iota(jnp.int32, (rows, width), 1)
    ang = pos * inv_ref[...]
    cos_ref[...] = jnp.cos(ang)
    sn = jnp.sin(ang)
    sin_ref[...] = jnp.where(lane % R_DK < R_DK // 2, -sn, sn)


def rope_tables(t):
    half = R_DK // 2
    inv = ROPE_BASE ** (-jnp.arange(half, dtype=F32) / half)
    inv = jnp.tile(inv, 2 * R_HEADS).reshape(1, R_HEADS * R_DK)
    width = R_HEADS * R_DK
    shp = jax.ShapeDtypeStruct((t, width), F32)
    return pl.pallas_call(
        _rope_table_kernel,
        out_shape=(shp, shp),
        grid=(t // R_CHUNK,),
        in_specs=[pl.BlockSpec((1, width), lambda c: (0, 0))],
        out_specs=(pl.BlockSpec((R_CHUNK, width), lambda c: (c, 0)),
                   pl.BlockSpec((R_CHUNK, width), lambda c: (c, 0))),
        compiler_params=_cparams(("parallel",)),
        name="rope_tables",
    )(inv)


def _retention_kernel(cos_ref, sin_ref, q_ref, k_ref, v_ref, g_ref, gain_ref, o_ref, r_ref):
    c = pl.program_id(1)
    L = R_CHUNK

    @pl.when(c == 0)
    def _():
        r_ref[...] = jnp.zeros_like(r_ref)

    cos = cos_ref[...]
    sin = sin_ref[...]
    lane = lax.broadcasted_iota(jnp.int32, cos.shape, 1)
    first_half = lane % R_DK < R_DK // 2
    width = R_HEADS * R_DK

    def rot(x):
        swapped = jnp.where(first_half, pltpu.roll(x, width - R_DK // 2, 1), pltpu.roll(x, R_DK // 2, 1))
        return x * cos + swapped * sin

    q = rot(q_ref[...])
    k = rot(k_ref[...]) * (R_DK ** -0.5)
    v = v_ref[...]
    gate = g_ref[...]
    gain = gain_ref[...]

    ri = lax.broadcasted_iota(jnp.int32, (L, L), 0)
    ci = lax.broadcasted_iota(jnp.int32, (L, L), 1)
    diff = (ri - ci).astype(F32)
    causal = ri >= ci
    idx = lax.broadcasted_iota(jnp.int32, (L, 1), 0).astype(F32)

    for h in range(R_HEADS):
        log_g = math.log1p(-2.0 ** (-5.0 - h))
        dmask = jnp.where(causal, jnp.exp(jnp.where(causal, diff, 0.0) * log_g), 0.0)
        q_dec = jnp.exp((idx + 1.0) * log_g)
        k_dec = jnp.exp((L - 1.0 - idx) * log_g)
        c_dec = math.exp(L * log_g)
        qh = q[:, h * R_DK:(h + 1) * R_DK]
        kh = k[:, h * R_DK:(h + 1) * R_DK]
        vh = v[:, h * R_DV:(h + 1) * R_DV].astype(BF16)
        qb = qh.astype(BF16)
        s = _dot_nt(qb, kh.astype(BF16)) * dmask
        rstate = r_ref[h]
        o = _dot(s.astype(BF16), vh) + _dot(qb, rstate.astype(BF16)) * q_dec
        kd = (kh * k_dec).T.astype(BF16)
        r_ref[h] = c_dec * rstate + _dot(kd, vh)
        sl = slice(h * R_DV, (h + 1) * R_DV)
        gh = gate[:, sl]
        o_ref[:, sl] = _rms(o, gain[:, sl]) * (gh * jax.nn.sigmoid(gh))


def retention(z3, cos_t, sin_t, gain):
    b, t, _ = z3.shape
    nc = t // R_CHUNK
    qk_w = R_HEADS * R_DK
    v_w = R_HEADS * R_DV
    return pl.pallas_call(
        _retention_kernel,
        out_shape=jax.ShapeDtypeStruct((b, t, v_w), F32),
        grid=(b, nc),
        in_specs=[pl.BlockSpec((R_CHUNK, qk_w), lambda i, c: (c, 0)),
                  pl.BlockSpec((R_CHUNK, qk_w), lambda i, c: (c, 0)),
                  pl.BlockSpec((None, R_CHUNK, qk_w), lambda i, c: (i, c, 0)),
                  pl.BlockSpec((None, R_CHUNK, qk_w), lambda i, c: (i, c, 1)),
                  pl.BlockSpec((None, R_CHUNK, v_w), lambda i, c: (i, c, 1)),
                  pl.BlockSpec((None, R_CHUNK, v_w), lambda i, c: (i, c, 2)),
                  pl.BlockSpec((1, v_w), lambda i, c: (0, 0))],
        out_specs=pl.BlockSpec((None, R_CHUNK, v_w), lambda i, c: (i, c, 0)),
        scratch_shapes=[pltpu.VMEM((R_HEADS, R_DK, R_DV), F32)],
        compiler_params=_cparams(("parallel", "arbitrary")),
        name="retention",
    )(cos_t, sin_t, z3, z3, z3, z3, gain.reshape(1, v_w))


def _mlstm_kernel(q_ref, k_ref, v_ref, og_ref, gc_ref, gr_ref, cw_ref, cb_ref, bc_ref, br_ref, gain_ref,
                  o_ref, xbuf, c_ref, n_ref, m_ref):
    c = pl.program_id(1)
    L = M_CHUNK
    H = M_HEADS
    qk_w = H * M_DK
    halo = 8

    @pl.when(c == 0)
    def _():
        xbuf[0:halo, :] = jnp.zeros((halo, 2 * qk_w), F32)
        c_ref[...] = jnp.zeros_like(c_ref)
        n_ref[...] = jnp.zeros_like(n_ref)
        m_ref[...] = jnp.zeros_like(m_ref)

    xbuf[halo:halo + L, 0:qk_w] = q_ref[...]
    xbuf[halo:halo + L, qk_w:2 * qk_w] = k_ref[...]
    conv = cb_ref[...]
    for j in range(M_CONV):
        conv = conv + xbuf[pl.ds(halo - (M_CONV - 1) + j, L), :] * cw_ref[j:j + 1, :]
    tail = xbuf[L:L + halo, :]
    xbuf[0:halo, :] = tail
    act = conv * jax.nn.sigmoid(conv)
    q = act[:, 0:qk_w] * (M_DK ** -0.5)
    k = act[:, qk_w:2 * qk_w]
    v = v_ref[...]
    og = og_ref[...]
    gain = gain_ref[...]

    gc = gc_ref[...][:, 0:2 * H] + bc_ref[...]
    gr = gr_ref[...] + br_ref[...]
    ig_c = gc[:, 0:H]
    lf_c = jax.nn.log_sigmoid(gc[:, H:2 * H])
    ig_r = gr[0:H, :]
    lf_r = jax.nn.log_sigmoid(gr[H:2 * H, :])
    ri = lax.broadcasted_iota(jnp.int32, (L, L), 0)
    ci = lax.broadcasted_iota(jnp.int32, (L, L), 1)
    causal = ri >= ci
    tril = causal.astype(F32)
    triu = (ri <= ci).astype(F32)
    b_c = _dot_f32(tril, lf_c)
    b_r = _dot_f32(lf_r, triu)

    for h in range(H):
        bh = b_c[:, h:h + 1]
        dlog = jnp.where(causal, bh - b_r[h:h + 1, :] + ig_r[h:h + 1, :], NEG)
        m_prev = m_ref[h][:, 0:1]
        inter = bh + m_prev
        m_t = jnp.maximum(inter, jnp.max(dlog, axis=-1, keepdims=True))
        qh = q[:, h * M_DK:(h + 1) * M_DK]
        kh = k[:, h * M_DK:(h + 1) * M_DK]
        vh = v[:, h * M_DV:(h + 1) * M_DV].astype(BF16)
        qb = qh.astype(BF16)
        s = _dot_nt(qb, kh.astype(BF16)) * jnp.exp(dlog - m_t)
        w_inter = jnp.exp(inter - m_t)
        cstate = c_ref[h]
        nstate = n_ref[h]
        num = _dot(s.astype(BF16), vh) + w_inter * _dot(qb, cstate.astype(BF16))
        den = jnp.sum(s, axis=-1, keepdims=True) + w_inter * jnp.sum(qh * nstate, axis=-1, keepdims=True)
        hh = num / jnp.maximum(jnp.abs(den), jnp.exp(-m_t))
        b_last = bh[L - 1:L, :]
        wlog = b_last - bh + ig_c[:, h:h + 1]
        m_new = jnp.maximum(b_last + m_prev, jnp.max(wlog, axis=0, keepdims=True))
        decay = jnp.exp(b_last + m_prev - m_new)
        wk = kh * jnp.exp(wlog - m_new)
        c_ref[h] = decay * cstate + _dot(wk.T.astype(BF16), vh)
        n_ref[h] = decay * nstate + jnp.sum(wk, axis=0, keepdims=True)
        m_ref[h] = jnp.broadcast_to(m_new, (1, LANES_V7X))
        sl = slice(h * M_DV, (h + 1) * M_DV)
        o_ref[:, sl] = jax.nn.sigmoid(og[:, sl]) * _rms(hh, gain[:, sl])


def mlstm(z3, gates_r, conv_w, conv_b, ig_b, fg_b, gain):
    b, t, _ = z3.shape
    nc = t // M_CHUNK
    H = M_HEADS
    qk_w = H * M_DK
    v_w = H * M_DV
    bias = jnp.concatenate([ig_b, fg_b])
    gate_blk = AB_COLS_PAD // LANES_V7X - 1
    return pl.pallas_call(
        _mlstm_kernel,
        out_shape=jax.ShapeDtypeStruct((b, t, v_w), F32),
        grid=(b, nc),
        in_specs=[pl.BlockSpec((None, M_CHUNK, qk_w), lambda i, c: (i, c, 6)),
                  pl.BlockSpec((None, M_CHUNK, qk_w), lambda i, c: (i, c, 7)),
                  pl.BlockSpec((None, M_CHUNK, v_w), lambda i, c: (i, c, 4)),
                  pl.BlockSpec((None, M_CHUNK, v_w), lambda i, c: (i, c, 5)),
                  pl.BlockSpec((None, M_CHUNK, LANES_V7X), lambda i, c: (i, c, gate_blk)),
                  pl.BlockSpec((None, None, 2 * H, M_CHUNK), lambda i, c: (i, c, 0, 0)),
                  pl.BlockSpec((M_CONV, 2 * qk_w), lambda i, c: (0, 0)),
                  pl.BlockSpec((1, 2 * qk_w), lambda i, c: (0, 0)),
                  pl.BlockSpec((1, 2 * H), lambda i, c: (0, 0)),
                  pl.BlockSpec((2 * H, 1), lambda i, c: (0, 0)),
                  pl.BlockSpec((1, v_w), lambda i, c: (0, 0))],
        out_specs=pl.BlockSpec((None, M_CHUNK, v_w), lambda i, c: (i, c, 0)),
        scratch_shapes=[pltpu.VMEM((8 + M_CHUNK, 2 * qk_w), F32),
                        pltpu.VMEM((H, M_DK, M_DV), F32),
                        pltpu.VMEM((H, 1, M_DK), F32),
                        pltpu.VMEM((H, 1, LANES_V7X), F32)],
        compiler_params=_cparams(("parallel", "arbitrary")),
        name="mlstm",
    )(z3, z3, z3, z3, z3, gates_r, conv_w, conv_b.reshape(1, -1), bias.reshape(1, -1), bias.reshape(-1, 1),
      gain.reshape(1, v_w))


def _ffn_kernel(xp_ref, x_ref, g_ref, wa_ref, wb_ref, cw_ref, cb_ref, wd_ref, o_ref, xn_ref, a_ref,
                *, tm, seq):
    i = pl.program_id(0)
    j = pl.program_id(1)
    halo = 8

    @pl.when(j == 0)
    def _():
        g = g_ref[...]
        x = x_ref[...]
        xn_ref[halo:halo + tm, :] = _rms(x, g).astype(BF16)
        prev = _rms(xp_ref[...], g)
        keep = ((i * tm) % seq != 0).astype(F32)
        xn_ref[0:halo, :] = (prev * keep).astype(BF16)
        o_ref[...] = x

    a_ref[...] = _dot(xn_ref[...], wa_ref[...])
    conv = cb_ref[...]
    for t in range(FFN_CONV):
        conv = conv + a_ref[pl.ds(halo - (FFN_CONV - 1) + t, tm), :] * cw_ref[t:t + 1, :]
    bgate = _dot(xn_ref[halo:halo + tm, :], wb_ref[...])
    act = (jax.nn.gelu(conv) * bgate).astype(BF16)
    o_ref[...] += _dot(act, wd_ref[...])


def conv_ffn(h, g, w_up, conv_w, conv_b, w_down, seq, tm=1024, tf=256):
    n, d = h.shape
    nj = D_FF // tf
    hb = tm // 8
    return pl.pallas_call(
        functools.partial(_ffn_kernel, tm=tm, seq=seq),
        out_shape=jax.ShapeDtypeStruct((n, d), F32),
        grid=(n // tm, nj),
        in_specs=[pl.BlockSpec((8, d), lambda i, j: (jnp.maximum(i * hb - 1, 0), 0)),
                  pl.BlockSpec((tm, d), lambda i, j: (i, 0)),
                  pl.BlockSpec((1, d), lambda i, j: (0, 0)),
                  pl.BlockSpec((d, tf), lambda i, j: (0, j)),
                  pl.BlockSpec((d, tf), lambda i, j: (0, nj + j)),
                  pl.BlockSpec((FFN_CONV, tf), lambda i, j: (0, j)),
                  pl.BlockSpec((1, tf), lambda i, j: (0, j)),
                  pl.BlockSpec((tf, d), lambda i, j: (j, 0))],
        out_specs=pl.BlockSpec((tm, d), lambda i, j: (i, 0)),
        scratch_shapes=[pltpu.VMEM((8 + tm, d), BF16),
                        pltpu.VMEM((8 + tm, tf), F32)],
        compiler_params=_cparams(("parallel", "arbitrary"), VMEM_LIMIT_V7X),
        name="conv_ffn",
    )(h, h, g.reshape(1, d), w_up, w_up, conv_w, conv_b.reshape(1, -1), w_down)


def _ple_kernel(h_ref, p_ref, wp_ref, ng_ref, gg_ref, wg_ref, o_ref):
    h = h_ref[...]
    e = _rms(_dot(p_ref[...].astype(BF16), wp_ref[...]), ng_ref[...])
    gate = jax.nn.sigmoid(_dot(_rms(h, gg_ref[...]).astype(BF16), wg_ref[...]))
    o_ref[...] = h + gate * e


def ple(h, p, wp, norm_g, gate_norm_g, wg, tm=512):
    n, d = h.shape
    pd = p.shape[1]
    return pl.pallas_call(
        _ple_kernel,
        out_shape=jax.ShapeDtypeStruct((n, d), F32),
        grid=(n // tm,),
        in_specs=[pl.BlockSpec((tm, d), lambda i: (i, 0)),
                  pl.BlockSpec((tm, pd), lambda i: (i, 0)),
                  pl.BlockSpec((pd, d), lambda i: (0, 0)),
                  pl.BlockSpec((1, d), lambda i: (0, 0)),
                  pl.BlockSpec((1, d), lambda i: (0, 0)),
                  pl.BlockSpec((d, d), lambda i: (0, 0))],
        out_specs=pl.BlockSpec((tm, d), lambda i: (i, 0)),
        compiler_params=_cparams(("parallel",), VMEM_LIMIT_V7X),
        name="ple",
    )(h, p, wp, norm_g.reshape(1, d), gate_norm_g.reshape(1, d), wg)


def _group_rms(x, g):
    lane = lax.broadcasted_iota(jnp.int32, x.shape, 1)
    x2 = x * x
    ms = jnp.zeros_like(x)
    for grp in range(N_KV_GROUPS):
        in_grp = (lane >= grp * HEAD_DIM) & (lane < (grp + 1) * HEAD_DIM)
        tot = jnp.sum(jnp.where(in_grp, x2, 0.0), axis=-1, keepdims=True)
        ms = jnp.where(in_grp, tot * (1.0 / HEAD_DIM), ms)
    return x * lax.rsqrt(ms + EPS) * g


def _kv_prep_kernel(c_ref, s_ref, w_ref, gs_ref, gw_ref, kc_ref, vc_ref, ks_ref, vs_ref, kw_ref, vw_ref):
    gw = N_KV_GROUPS * HEAD_DIM
    cc = c_ref[...]
    ss = s_ref[...]
    ww = w_ref[...]
    kc_ref[...] = cc[:, 0:gw].astype(BF16)
    vc_ref[...] = cc[:, gw:2 * gw].astype(BF16)
    ks_ref[...] = _group_rms(ss[:, 0:gw], gs_ref[...]).astype(BF16)
    vs_ref[...] = ss[:, gw:2 * gw].astype(BF16)
    kw_ref[...] = _group_rms(ww[:, 0:gw], gw_ref[...]).astype(BF16)
    vw_ref[...] = ww[:, gw:2 * gw].astype(BF16)


def kv_prep(z, k_g, tm=512):
    n = z.shape[0]
    gw = N_KV_GROUPS * HEAD_DIM
    base = N_HEADS * HEAD_DIM // (2 * gw)
    shp = jax.ShapeDtypeStruct((n, gw), BF16)
    ospec = pl.BlockSpec((tm, gw), lambda i: (i, 0))
    return pl.pallas_call(
        _kv_prep_kernel,
        out_shape=(shp,) * 6,
        grid=(n // tm,),
        in_specs=[pl.BlockSpec((tm, 2 * gw), lambda i: (i, base)),
                  pl.BlockSpec((tm, 2 * gw), lambda i: (i, base + 1)),
                  pl.BlockSpec((tm, 2 * gw), lambda i: (i, base + 2)),
                  pl.BlockSpec((1, gw), lambda i: (0, 0)),
                  pl.BlockSpec((1, gw), lambda i: (0, 0))],
        out_specs=(ospec,) * 6,
        compiler_params=_cparams(("parallel",)),
        name="kv_prep",
    )(z, z, z, jnp.tile(k_g[1], N_KV_GROUPS).reshape(1, gw), jnp.tile(k_g[2], N_KV_GROUPS).reshape(1, gw))


def _compress_kernel(x_ref, pos_ref, w1_ref, w2_ref, g_ref, o_ref, *, normalize):
    half = w1_ref.shape[0] // 2
    x = x_ref[...]
    u = _dot(x, w1_ref[0:half, :])
    v = _dot(x, w1_ref[half:2 * half, :])
    rows = u.shape[0]
    posc = _dot(pos_ref[...], w1_ref[...])[0:1, :]
    hid = u + pltpu.roll(v, rows - 1, 0) + posc
    out = _dot(jax.nn.gelu(hid).astype(BF16), w2_ref[...])
    if normalize:
        out = _rms(out, g_ref[...])
    o_ref[...] = out.astype(o_ref.dtype)


def compress(x, pos, w1, w2, g, normalize, ncb):
    n, kdim = x.shape
    posf = jnp.broadcast_to(pos.reshape(1, -1), (8, 2 * kdim)).astype(BF16)
    return pl.pallas_call(
        functools.partial(_compress_kernel, normalize=normalize),
        out_shape=jax.ShapeDtypeStruct((n, HEAD_DIM), BF16),
        grid=(n // ncb,),
        in_specs=[pl.BlockSpec((ncb, kdim), lambda i: (i, 0)),
                  pl.BlockSpec((8, 2 * kdim), lambda i: (0, 0)),
                  pl.BlockSpec((2 * kdim, CMP_HIDDEN), lambda i: (0, 0)),
                  pl.BlockSpec((CMP_HIDDEN, HEAD_DIM), lambda i: (0, 0)),
                  pl.BlockSpec((1, HEAD_DIM), lambda i: (0, 0))],
        out_specs=pl.BlockSpec((ncb, HEAD_DIM), lambda i: (i, 0)),
        compiler_params=_cparams(("parallel",)),
        name="compress",
    )(x, posf, w1.astype(BF16), w2.astype(BF16), g.reshape(1, HEAD_DIM))


def _nsa_kernel(bnd_ref, zq_ref, gt_ref, gb_ref, qg_ref, ovt_ref, kc_ref, vct_ref, ks_ref, vst_ref, kw_ref,
                vwt_ref, o_ref, q_scr, sel_scr, m_scr, l_scr, acc_scr, oc_scr, *, bounded):
    grp = pl.program_id(1)
    qb = pl.program_id(2)
    QB = Q_BLOCK
    ncp = kc_ref.shape[0]
    ns = ovt_ref.shape[0]
    tpos = qb * QB + lax.broadcasted_iota(jnp.int32, (1, QB), 1)

    def lane_tile(x):
        return jnp.concatenate([x] * HG, axis=1)

    zt = zq_ref[...].T
    qg = qg_ref[...] * (HEAD_DIM ** -0.5 * LOG2E)
    heads = []
    for h in range(HG):
        xh = zt[h * HEAD_DIM:(h + 1) * HEAD_DIM, :]
        ms = jnp.mean(xh * xh, axis=0, keepdims=True)
        heads.append(xh * lax.rsqrt(ms + EPS) * qg)
    q_scr[...] = jnp.concatenate(heads, axis=1).astype(BF16)
    q = q_scr[...]

    cmp_end = lax.broadcasted_iota(jnp.int32, (ncp, 1), 0) * CMP_STRIDE + (CMP_BLOCK - 1)
    cbias = jnp.where(cmp_end <= tpos, 0.0, NEG)
    s = _dot(kc_ref[...], q) + lane_tile(cbias)
    e = jnp.exp2(s - jnp.max(s, axis=0, keepdims=True))
    inv = jnp.where(lane_tile(tpos) >= CMP_BLOCK - 1, 1.0 / jnp.sum(e, axis=0, keepdims=True), 0.0)
    p = e * inv
    oc_scr[...] = _dot(vct_ref[...], p.astype(BF16))
    psum = p[:, 0:QB]
    for h in range(1, HG):
        psum = psum + p[:, h * QB:(h + 1) * QB]

    p_hi = psum.astype(BF16)
    p_lo = (psum - p_hi.astype(F32)).astype(BF16)
    ovt = ovt_ref[...]
    imp = _dot(ovt, p_hi) + _dot(ovt, p_lo)
    blk = lax.broadcasted_iota(jnp.int32, (ns, 1), 0)
    blk_f = blk.astype(F32)
    cur = jnp.right_shift(tpos, SLC_BLOCK.bit_length() - 1)
    forced = (blk == 0) | (blk == cur) | (blk == cur - 1)
    bvalid = blk <= cur
    score = jnp.where(forced, BIG, jnp.where(bvalid, imp, NEG))
    sel = jnp.zeros((ns, QB), F32)
    for _ in range(min(N_SELECT, ns)):
        mx = jnp.max(score, axis=0, keepdims=True)
        first = jnp.min(jnp.where(score == mx, blk_f, float(ns)), axis=0, keepdims=True)
        pick = blk_f == first
        sel = jnp.where(pick, 1.0, sel)
        score = jnp.where(pick, -jnp.inf, score)
    sel_scr[...] = jnp.where(bvalid, sel, 0.0)

    gates = jax.nn.sigmoid(gt_ref[...].T + gb_ref[...])
    per_grp = HG * 3
    gsel = gates[0:per_grp, :]
    for g2 in range(1, N_KV_GROUPS):
        gsel = jnp.where(grp == g2, gates[g2 * per_grp:(g2 + 1) * per_grp, :], gsel)

    def gate_row(c):
        return jnp.concatenate([gsel[3 * h + c:3 * h + c + 1, :] for h in range(HG)], axis=1)

    m_scr[...] = jnp.full(m_scr.shape, NEG, F32)
    l_scr[...] = jnp.zeros(l_scr.shape, F32)
    acc_scr[...] = jnp.zeros(acc_scr.shape, F32)
    blocks_per_tile = SEL_TILE // SLC_BLOCK
    sel_sub = lax.broadcasted_iota(jnp.int32, (SEL_TILE, 1), 0)
    keep = -bnd_ref[0, 0] if bounded else 0.0

    def sel_body(j, carry):
        rows = [jnp.broadcast_to(sel_scr[pl.ds(j * blocks_per_tile + bi, 1), :], (SLC_BLOCK, QB))
                for bi in range(blocks_per_tile)]
        chosen = jnp.concatenate(rows, axis=0)
        kpos = j * SEL_TILE + sel_sub
        bias = jnp.where((chosen > 0.5) & (kpos <= tpos), keep, NEG)
        st = _dot(ks_ref[j], q_scr[...]) + lane_tile(bias)
        if bounded:
            pt = jnp.exp2(st)
            l_scr[...] += jnp.sum(pt, axis=0, keepdims=True)
            acc_scr[...] += _dot(vst_ref[j], pt.astype(BF16))
        else:
            m_old = m_scr[...]
            m_new = jnp.maximum(m_old, jnp.max(st, axis=0, keepdims=True))
            alpha = jnp.exp2(m_old - m_new)
            pt = jnp.exp2(st - m_new)
            l_scr[...] = alpha * l_scr[...] + jnp.sum(pt, axis=0, keepdims=True)
            acc_scr[...] = alpha * acc_scr[...] + _dot(vst_ref[j], pt.astype(BF16))
            m_scr[...] = m_new
        return carry

    lax.fori_loop(0, qb // (SEL_TILE // QB) + 1, sel_body, 0)
    oc_scr[...] = gate_row(0) * oc_scr[...] + (gate_row(1) / l_scr[...]) * acc_scr[...]

    n_win = WINDOW // KEY_TILE + 1
    win_sub = lax.broadcasted_iota(jnp.int32, (n_win * KEY_TILE, 1), 0)
    tiles = [jnp.maximum(qb - (n_win - 1) + u, 0) for u in range(n_win)]
    kwin = jnp.concatenate([kw_ref[j] for j in tiles], axis=0)
    vwin = jnp.concatenate([vwt_ref[j] for j in tiles], axis=1)
    kpos = (qb - (n_win - 1)) * KEY_TILE + win_sub
    wbias = jnp.where((kpos <= tpos) & (kpos > tpos - WINDOW) & (kpos >= 0), 0.0, NEG)
    sw = _dot(kwin, q_scr[...]) + lane_tile(wbias)
    pw = jnp.exp2(sw - jnp.max(sw, axis=0, keepdims=True))
    lw = jnp.sum(pw, axis=0, keepdims=True)
    ow = _dot(vwin, pw.astype(BF16))
    ot = oc_scr[...] + (gate_row(2) / lw) * ow
    stacked = jnp.concatenate([ot[:, h * QB:(h + 1) * QB] for h in range(HG)], axis=0)
    o_ref[...] = stacked.T


def nsa_attention(bound, z3, gate_b, q_g, overlap_t, kc, vct, ks, vst, kw, vwt, *, bounded):
    b, t, _ = z3.shape
    nq = t // Q_BLOCK
    nt = t // KEY_TILE
    qw = HG * HEAD_DIM
    ns, ncp = overlap_t.shape
    gate_blk = (NSA_COLS_PAD // LANES_V7X) - 1
    gb = jnp.zeros((LANES_V7X, 1), F32).at[:N_HEADS * 3, 0].set(gate_b)
    full5 = lambda i, g, q: (i, g, 0, 0, 0)
    lanes = HG * Q_BLOCK
    return pl.pallas_call(
        functools.partial(_nsa_kernel, bounded=bounded),
        out_shape=jax.ShapeDtypeStruct((b, t, N_HEADS * HEAD_DIM), F32),
        grid=(b, N_KV_GROUPS, nq),
        in_specs=[pl.BlockSpec(memory_space=pltpu.SMEM),
                  pl.BlockSpec((None, Q_BLOCK, qw), lambda i, g, q: (i, q, g)),
                  pl.BlockSpec((None, Q_BLOCK, LANES_V7X), lambda i, g, q: (i, q, gate_blk)),
                  pl.BlockSpec((LANES_V7X, 1), lambda i, g, q: (0, 0)),
                  pl.BlockSpec((HEAD_DIM, 1), lambda i, g, q: (0, 0)),
                  pl.BlockSpec((ns, ncp), lambda i, g, q: (0, 0)),
                  pl.BlockSpec((None, None, ncp, HEAD_DIM), lambda i, g, q: (i, g, 0, 0)),
                  pl.BlockSpec((None, None, HEAD_DIM, ncp), lambda i, g, q: (i, g, 0, 0)),
                  pl.BlockSpec((None, None, t // SEL_TILE, SEL_TILE, HEAD_DIM), full5),
                  pl.BlockSpec((None, None, t // SEL_TILE, HEAD_DIM, SEL_TILE), full5),
                  pl.BlockSpec((None, None, nt, KEY_TILE, HEAD_DIM), full5),
                  pl.BlockSpec((None, None, nt, HEAD_DIM, KEY_TILE), full5)],
        out_specs=pl.BlockSpec((None, Q_BLOCK, qw), lambda i, g, q: (i, q, g)),
        scratch_shapes=[pltpu.VMEM((HEAD_DIM, lanes), BF16),
                        pltpu.VMEM((ns, Q_BLOCK), F32),
                        pltpu.VMEM((1, lanes), F32),
                        pltpu.VMEM((1, lanes), F32),
                        pltpu.VMEM((HEAD_DIM, lanes), F32),
                        pltpu.VMEM((HEAD_DIM, lanes), F32)],
        compiler_params=_cparams(("parallel", "parallel", "arbitrary"), VMEM_LIMIT_V7X),
        name="nsa_attention_bounded" if bounded else "nsa_attention_online",
    )(bound.reshape(1, 1), z3, z3, gb, q_g.reshape(HEAD_DIM, 1), overlap_t, kc, vct, ks, vst, kw, vwt)


def _overlap_matrix_t(t):
    ncp = t // CMP_STRIDE
    ns = t // SLC_BLOCK
    c_start = np.arange(ncp) * CMP_STRIDE
    sj = np.arange(ns)
    ov = ((c_start[None, :] < (sj[:, None] + 1) * SLC_BLOCK)
          & (c_start[None, :] + CMP_BLOCK > sj[:, None] * SLC_BLOCK)
          & (c_start[None, :] + CMP_BLOCK <= t))
    return jnp.asarray(ov, dtype=BF16)


def ab_layer(h, b, t, norm_g, w_in, conv_w, conv_b, ret_g, ig_b, fg_b, m_g, w_out):
    n = b * t
    w_in_p = jnp.pad(w_in, ((0, 0), (0, AB_COLS_PAD - AB_COLS))).astype(BF16)
    z = norm_matmul(h, norm_g, w_in_p)
    z3 = z.reshape(b, t, AB_COLS_PAD)
    cos_t, sin_t = rope_tables(t)
    ret = retention(z3, cos_t, sin_t, ret_g)
    g0 = AB_COLS - 2 * M_HEADS
    gates_r = z3[:, :, g0:AB_COLS].reshape(b, t // M_CHUNK, M_CHUNK, 2 * M_HEADS).transpose(0, 1, 3, 2)
    ml = mlstm(z3, gates_r, conv_w, conv_b, ig_b, fg_b, m_g)
    w_out_b = w_out.astype(BF16)
    rw = R_HEADS * R_DV
    return proj_residual(h, [(ret.reshape(n, rw), w_out_b[:rw]), (ml.reshape(n, -1), w_out_b[rw:])])


def nsa_layer(h, b, t, norm_g, w_in, q_g, k_g, pos_k, pos_v, w1k, w2k, w1v, w2v, gate_b, w_out):
    n = b * t
    G = N_KV_GROUPS
    w_in_p = jnp.pad(w_in, ((0, 0), (0, NSA_COLS_PAD - NSA_COLS))).astype(BF16)
    z = norm_matmul(h, norm_g, w_in_p)
    kc_in, vc_in, ks, vs, kw, vw = kv_prep(z, k_g)
    nt = t // KEY_TILE
    ncb = t // CMP_STRIDE

    def cmp_rows(x):
        return x.reshape(b, t, G, HEAD_DIM).transpose(0, 2, 1, 3).reshape(b * G * ncb, CMP_STRIDE * HEAD_DIM)

    kc = compress(cmp_rows(kc_in), pos_k, w1k, w2k, k_g[0], True, ncb)
    vc = compress(cmp_rows(vc_in), pos_v, w1v, w2v, k_g[0], False, ncb)
    kc = kc.reshape(b, G, ncb, HEAD_DIM)
    vct = vc.reshape(b, G, ncb, HEAD_DIM).transpose(0, 1, 3, 2)

    def key_tiles(x, kt):
        return x.reshape(b, t // kt, kt, G, HEAD_DIM).transpose(0, 3, 1, 2, 4)

    def val_tiles_t(x, kt):
        return x.reshape(b, t // kt, kt, G, HEAD_DIM).transpose(0, 3, 1, 4, 2)

    bound = 1.02 * LOG2E * math.sqrt(HEAD_DIM) * jnp.max(jnp.abs(q_g)) * jnp.max(jnp.abs(k_g[1]))
    args = (bound, z.reshape(b, t, NSA_COLS_PAD), gate_b, q_g, _overlap_matrix_t(t), kc, vct,
            key_tiles(ks, SEL_TILE), val_tiles_t(vs, SEL_TILE),
            key_tiles(kw, KEY_TILE), val_tiles_t(vw, KEY_TILE))
    o = lax.cond(bound <= MAX_SAFE_SCORE_BOUND,
                 functools.partial(nsa_attention, bounded=True),
                 functools.partial(nsa_attention, bounded=False), *args)
    return proj_residual(h, [(o.reshape(n, -1), w_out.astype(BF16))])


def kernel(x, p, ab_norm_g, ab_w_in, ab_conv_w, ab_conv_b, ab_ret_norm_g, ab_ig_b, ab_fg_b, ab_m_norm_g, ab_w_out, nsa_norm_g, nsa_w_in, nsa_q_norm_g, nsa_k_norm_g, nsa_cmp_pos_k, nsa_cmp_pos_v, nsa_cmp_w1k, nsa_cmp_w2k, nsa_cmp_w1v, nsa_cmp_w2v, nsa_gate_b, nsa_w_out, ffn_norm_g, ffn_w_up, ffn_conv_w, ffn_conv_b, ffn_w_down, ple_w, ple_norm_g, ple_gate_norm_g, ple_w_gate):
    b, t, d = x.shape
    n = b * t
    depth = p.shape[0]
    h = x.reshape(n, d)
    for i in range(depth):
        j = i // 2
        if i % 2 == 0:
            h = ab_layer(h, b, t, ab_norm_g[j], ab_w_in[j], ab_conv_w[j], ab_conv_b[j], ab_ret_norm_g[j],
                         ab_ig_b[j], ab_fg_b[j], ab_m_norm_g[j], ab_w_out[j])
        else:
            h = nsa_layer(h, b, t, nsa_norm_g[j], nsa_w_in[j], nsa_q_norm_g[j], nsa_k_norm_g[j],
                          nsa_cmp_pos_k[j], nsa_cmp_pos_v[j], nsa_cmp_w1k[j], nsa_cmp_w2k[j],
                          nsa_cmp_w1v[j], nsa_cmp_w2v[j], nsa_gate_b[j], nsa_w_out[j])
        h = conv_ffn(h, ffn_norm_g[i], ffn_w_up[i].astype(BF16), ffn_conv_w[i], ffn_conv_b[i],
                     ffn_w_down[i].astype(BF16), t)
        h = ple(h, p[i].reshape(n, -1), ple_w[i].astype(BF16), ple_norm_g[i], ple_gate_norm_g[i],
                ple_w_gate[i].astype(BF16))
    return h.reshape(b, t, d)
```

```python
import functools
import math

import numpy as np
import jax
import jax.numpy as jnp
from jax import lax
from jax.experimental import pallas as pl
from jax.experimental.pallas import tpu as pltpu

F32 = jnp.float32
BF16 = jnp.bfloat16

LANES_V7X = 128
VMEM_LIMIT_V7X = 56 * 1024 * 1024

D_MODEL = 1024
PLE_DIM = 256
R_HEADS, R_DK, R_DV, R_CHUNK = 4, 64, 128, 128
ROPE_BASE = 10000.0
M_HEADS, M_DK, M_DV, M_CHUNK, M_CONV = 4, 64, 128, 64, 4
AB_SIZES = (R_HEADS * R_DK, R_HEADS * R_DK, R_HEADS * R_DV, R_HEADS * R_DV,
            M_HEADS * M_DK, M_HEADS * M_DK, M_HEADS * M_DV, M_HEADS * M_DV, M_HEADS, M_HEADS)
AB_COLS = sum(AB_SIZES)
AB_COLS_PAD = 3200
N_HEADS, N_KV_GROUPS, HEAD_DIM = 16, 2, 64
HG = N_HEADS // N_KV_GROUPS
CMP_BLOCK, CMP_STRIDE, CMP_HIDDEN = 32, 16, 256
SLC_BLOCK, N_SELECT, WINDOW, Q_BLOCK = 64, 16, 512, 128
NSA_COLS = N_HEADS * HEAD_DIM + 6 * N_KV_GROUPS * HEAD_DIM + N_HEADS * 3
NSA_COLS_PAD = 1920
D_FF = 2816
FFN_CONV = 3
NEG = -1e30
BIG = 1e30
EPS = 1e-6
KEY_TILE = 128
SEL_TILE = 512
LOG2E = math.log2(math.e)
MAX_SAFE_SCORE_BOUND = 56.0


def _cparams(sem, vmem=None):
    return pltpu.CompilerParams(dimension_semantics=sem, vmem_limit_bytes=vmem)


def _rms(x, g):
    ms = jnp.mean(x * x, axis=-1, keepdims=True)
    return x * lax.rsqrt(ms + EPS) * g


def _dot(a, b):
    return jnp.dot(a, b, preferred_element_type=F32)


def _dot_nt(a, b):
    return lax.dot_general(a, b, (((1,), (1,)), ((), ())), preferred_element_type=F32)


def _dot_f32(a, b):
    return jnp.dot(a, b, preferred_element_type=F32, precision=lax.Precision.HIGHEST)


def _norm_matmul_kernel(x_ref, g_ref, w_ref, o_ref):
    xn = _rms(x_ref[...], g_ref[...]).astype(BF16)
    o_ref[...] = _dot(xn, w_ref[...]).astype(o_ref.dtype)


def norm_matmul(x, g, w, tm=512, out_dtype=F32):
    n, d = x.shape
    nc = w.shape[1]
    return pl.pallas_call(
        _norm_matmul_kernel,
        out_shape=jax.ShapeDtypeStruct((n, nc), out_dtype),
        grid=(n // tm,),
        in_specs=[pl.BlockSpec((tm, d), lambda i: (i, 0)),
                  pl.BlockSpec((1, d), lambda i: (0, 0)),
                  pl.BlockSpec((d, nc), lambda i: (0, 0))],
        out_specs=pl.BlockSpec((tm, nc), lambda i: (i, 0)),
        compiler_params=_cparams(("parallel",), VMEM_LIMIT_V7X),
        name="norm_matmul",
    )(x, g.reshape(1, d), w)


def _proj_residual_kernel(*refs, n_in):
    h_ref = refs[0]
    o_ref = refs[1 + 2 * n_in]
    acc = h_ref[...]
    for k in range(n_in):
        acc = acc + _dot(refs[1 + 2 * k][...].astype(BF16), refs[2 + 2 * k][...])
    o_ref[...] = acc


def proj_residual(h, pairs, tm=512):
    n, d = h.shape
    in_specs = [pl.BlockSpec((tm, d), lambda i: (i, 0))]
    args = [h]
    for a, w in pairs:
        in_specs.append(pl.BlockSpec((tm, a.shape[1]), lambda i: (i, 0)))
        in_specs.append(pl.BlockSpec(w.shape, lambda i: (0, 0)))
        args += [a, w]
    return pl.pallas_call(
        functools.partial(_proj_residual_kernel, n_in=len(pairs)),
        out_shape=jax.ShapeDtypeStruct((n, d), F32),
        grid=(n // tm,),
        in_specs=in_specs,
        out_specs=pl.BlockSpec((tm, d), lambda i: (i, 0)),
        compiler_params=_cparams(("parallel",), VMEM_LIMIT_V7X),
        name="proj_residual",
    )(*args)


def _rope_table_kernel(inv_ref, cos_ref, sin_ref):
    c = pl.program_id(0)
    rows, width = cos_ref.shape
    pos = (c * rows + lax.broadcasted_iota(jnp.int32, (rows, width), 0)).astype(F32)
    lane = lax.broadcasted_iota(jnp.int32, (rows, width), 1)
    ang = pos * inv_ref[...]
    cos_ref[...] = jnp.cos(ang)
    sn = jnp.sin(ang)
    sin_ref[...] = jnp.where(lane % R_DK < R_DK // 2, -sn, sn)


def rope_tables(t):
    half = R_DK // 2
    inv = ROPE_BASE ** (-jnp.arange(half, dtype=F32) / half)
    inv = jnp.tile(inv, 2 * R_HEADS).reshape(1, R_HEADS * R_DK)
    width = R_HEADS * R_DK
    shp = jax.ShapeDtypeStruct((t, width), F32)
    return pl.pallas_call(
        _rope_table_kernel,
        out_shape=(shp, shp),
        grid=(t // R_CHUNK,),
        in_specs=[pl.BlockSpec((1, width), lambda c: (0, 0))],
        out_specs=(pl.BlockSpec((R_CHUNK, width), lambda c: (c, 0)),
                   pl.BlockSpec((R_CHUNK, width), lambda c: (c, 0))),
        compiler_params=_cparams(("parallel",)),
        name="rope_tables",
    )(inv)


def _retention_kernel(cos_ref, sin_ref, q_ref, k_ref, v_ref, g_ref, gain_ref, o_ref, r_ref):
    c = pl.program_id(1)
    L = R_CHUNK

    @pl.when(c == 0)
    def _():
        r_ref[...] = jnp.zeros_like(r_ref)

    cos = cos_ref[...]
    sin = sin_ref[...]
    lane = lax.broadcasted_iota(jnp.int32, cos.shape, 1)
    first_half = lane % R_DK < R_DK // 2
    width = R_HEADS * R_DK

    def rot(x):
        swapped = jnp.where(first_half, pltpu.roll(x, width - R_DK // 2, 1), pltpu.roll(x, R_DK // 2, 1))
        return x * cos + swapped * sin

    q = rot(q_ref[...])
    k = rot(k_ref[...]) * (R_DK ** -0.5)
    v = v_ref[...]
    gate = g_ref[...]
    gain = gain_ref[...]

    ri = lax.broadcasted_iota(jnp.int32, (L, L), 0)
    ci = lax.broadcasted_iota(jnp.int32, (L, L), 1)
    diff = (ri - ci).astype(F32)
    causal = ri >= ci
    idx = lax.broadcasted_iota(jnp.int32, (L, 1), 0).astype(F32)

    for h in range(R_HEADS):
        log_g = math.log1p(-2.0 ** (-5.0 - h))
        dmask = jnp.where(causal, jnp.exp(jnp.where(causal, diff, 0.0) * log_g), 0.0)
        q_dec = jnp.exp((idx + 1.0) * log_g)
        k_dec = jnp.exp((L - 1.0 - idx) * log_g)
        c_dec = math.exp(L * log_g)
        qh = q[:, h * R_DK:(h + 1) * R_DK]
        kh = k[:, h * R_DK:(h + 1) * R_DK]
        vh = v[:, h * R_DV:(h + 1) * R_DV].astype(BF16)
        qb = qh.astype(BF16)
        s = _dot_nt(qb, kh.astype(BF16)) * dmask
        rstate = r_ref[h]
        o = _dot(s.astype(BF16), vh) + _dot(qb, rstate.astype(BF16)) * q_dec
        kd = (kh * k_dec).T.astype(BF16)
        r_ref[h] = c_dec * rstate + _dot(kd, vh)
        sl = slice(h * R_DV, (h + 1) * R_DV)
        gh = gate[:, sl]
        o_ref[:, sl] = _rms(o, gain[:, sl]) * (gh * jax.nn.sigmoid(gh))


def retention(z3, cos_t, sin_t, gain):
    b, t, _ = z3.shape
    nc = t // R_CHUNK
    qk_w = R_HEADS * R_DK
    v_w = R_HEADS * R_DV
    return pl.pallas_call(
        _retention_kernel,
        out_shape=jax.ShapeDtypeStruct((b, t, v_w), F32),
        grid=(b, nc),
        in_specs=[pl.BlockSpec((R_CHUNK, qk_w), lambda i, c: (c, 0)),
                  pl.BlockSpec((R_CHUNK, qk_w), lambda i, c: (c, 0)),
                  pl.BlockSpec((None, R_CHUNK, qk_w), lambda i, c: (i, c, 0)),
                  pl.BlockSpec((None, R_CHUNK, qk_w), lambda i, c: (i, c, 1)),
                  pl.BlockSpec((None, R_CHUNK, v_w), lambda i, c: (i, c, 1)),
                  pl.BlockSpec((None, R_CHUNK, v_w), lambda i, c: (i, c, 2)),
                  pl.BlockSpec((1, v_w), lambda i, c: (0, 0))],
        out_specs=pl.BlockSpec((None, R_CHUNK, v_w), lambda i, c: (i, c, 0)),
        scratch_shapes=[pltpu.VMEM((R_HEADS, R_DK, R_DV), F32)],
        compiler_params=_cparams(("parallel", "arbitrary")),
        name="retention",
    )(cos_t, sin_t, z3, z3, z3, z3, gain.reshape(1, v_w))


def _mlstm_kernel(q_ref, k_ref, v_ref, og_ref, gc_ref, gr_ref, cw_ref, cb_ref, bc_ref, br_ref, gain_ref,
                  o_ref, xbuf, c_ref, n_ref, m_ref):
    c = pl.program_id(1)
    L = M_CHUNK
    H = M_HEADS
    qk_w = H * M_DK
    halo = 8

    @pl.when(c == 0)
    def _():
        xbuf[0:halo, :] = jnp.zeros((halo, 2 * qk_w), F32)
        c_ref[...] = jnp.zeros_like(c_ref)
        n_ref[...] = jnp.zeros_like(n_ref)
        m_ref[...] = jnp.zeros_like(m_ref)

    xbuf[halo:halo + L, 0:qk_w] = q_ref[...]
    xbuf[halo:halo + L, qk_w:2 * qk_w] = k_ref[...]
    conv = cb_ref[...]
    for j in range(M_CONV):
        conv = conv + xbuf[pl.ds(halo - (M_CONV - 1) + j, L), :] * cw_ref[j:j + 1, :]
    tail = xbuf[L:L + halo, :]
    xbuf[0:halo, :] = tail
    act = conv * jax.nn.sigmoid(conv)
    q = act[:, 0:qk_w] * (M_DK ** -0.5)
    k = act[:, qk_w:2 * qk_w]
    v = v_ref[...]
    og = og_ref[...]
    gain = gain_ref[...]

    gc = gc_ref[...][:, 0:2 * H] + bc_ref[...]
    gr = gr_ref[...] + br_ref[...]
    ig_c = gc[:, 0:H]
    lf_c = jax.nn.log_sigmoid(gc[:, H:2 * H])
    ig_r = gr[0:H, :]
    lf_r = jax.nn.log_sigmoid(gr[H:2 * H, :])
    ri = lax.broadcasted_iota(jnp.int32, (L, L), 0)
    ci = lax.broadcasted_iota(jnp.int32, (L, L), 1)
    causal = ri >= ci
    tril = causal.astype(F32)
    triu = (ri <= ci).astype(F32)
    b_c = _dot_f32(tril, lf_c)
    b_r = _dot_f32(lf_r, triu)

    for h in range(H):
        bh = b_c[:, h:h + 1]
        dlog = jnp.where(causal, bh - b_r[h:h + 1, :] + ig_r[h:h + 1, :], NEG)
        m_prev = m_ref[h][:, 0:1]
        inter = bh + m_prev
        m_t = jnp.maximum(inter, jnp.max(dlog, axis=-1, keepdims=True))
        qh = q[:, h * M_DK:(h + 1) * M_DK]
        kh = k[:, h * M_DK:(h + 1) * M_DK]
        vh = v[:, h * M_DV:(h + 1) * M_DV].astype(BF16)
        qb = qh.astype(BF16)
        s = _dot_nt(qb, kh.astype(BF16)) * jnp.exp(dlog - m_t)
        w_inter = jnp.exp(inter - m_t)
        cstate = c_ref[h]
        nstate = n_ref[h]
        num = _dot(s.astype(BF16), vh) + w_inter * _dot(qb, cstate.astype(BF16))
        den = jnp.sum(s, axis=-1, keepdims=True) + w_inter * jnp.sum(qh * nstate, axis=-1, keepdims=True)
        hh = num / jnp.maximum(jnp.abs(den), jnp.exp(-m_t))
        b_last = bh[L - 1:L, :]
        wlog = b_last - bh + ig_c[:, h:h + 1]
        m_new = jnp.maximum(b_last + m_prev, jnp.max(wlog, axis=0, keepdims=True))
        decay = jnp.exp(b_last + m_prev - m_new)
        wk = kh * jnp.exp(wlog - m_new)
        c_ref[h] = decay * cstate + _dot(wk.T.astype(BF16), vh)
        n_ref[h] = decay * nstate + jnp.sum(wk, axis=0, keepdims=True)
        m_ref[h] = jnp.broadcast_to(m_new, (1, LANES_V7X))
        sl = slice(h * M_DV, (h + 1) * M_DV)
        o_ref[:, sl] = jax.nn.sigmoid(og[:, sl]) * _rms(hh, gain[:, sl])


def mlstm(z3, gates_r, conv_w, conv_b, ig_b, fg_b, gain):
    b, t, _ = z3.shape
    nc = t // M_CHUNK
    H = M_HEADS
    qk_w = H * M_DK
    v_w = H * M_DV
    bias = jnp.concatenate([ig_b, fg_b])
    gate_blk = AB_COLS_PAD // LANES_V7X - 1
    return pl.pallas_call(
        _mlstm_kernel,
        out_shape=jax.ShapeDtypeStruct((b, t, v_w), F32),
        grid=(b, nc),
        in_specs=[pl.BlockSpec((None, M_CHUNK, qk_w), lambda i, c: (i, c, 6)),
                  pl.BlockSpec((None, M_CHUNK, qk_w), lambda i, c: (i, c, 7)),
                  pl.BlockSpec((None, M_CHUNK, v_w), lambda i, c: (i, c, 4)),
                  pl.BlockSpec((None, M_CHUNK, v_w), lambda i, c: (i, c, 5)),
                  pl.BlockSpec((None, M_CHUNK, LANES_V7X), lambda i, c: (i, c, gate_blk)),
                  pl.BlockSpec((None, None, 2 * H, M_CHUNK), lambda i, c: (i, c, 0, 0)),
                  pl.BlockSpec((M_CONV, 2 * qk_w), lambda i, c: (0, 0)),
                  pl.BlockSpec((1, 2 * qk_w), lambda i, c: (0, 0)),
                  pl.BlockSpec((1, 2 * H), lambda i, c: (0, 0)),
                  pl.BlockSpec((2 * H, 1), lambda i, c: (0, 0)),
                  pl.BlockSpec((1, v_w), lambda i, c: (0, 0))],
        out_specs=pl.BlockSpec((None, M_CHUNK, v_w), lambda i, c: (i, c, 0)),
        scratch_shapes=[pltpu.VMEM((8 + M_CHUNK, 2 * qk_w), F32),
                        pltpu.VMEM((H, M_DK, M_DV), F32),
                        pltpu.VMEM((H, 1, M_DK), F32),
                        pltpu.VMEM((H, 1, LANES_V7X), F32)],
        compiler_params=_cparams(("parallel", "arbitrary")),
        name="mlstm",
    )(z3, z3, z3, z3, z3, gates_r, conv_w, conv_b.reshape(1, -1), bias.reshape(1, -1), bias.reshape(-1, 1),
      gain.reshape(1, v_w))


def _ffn_ple_kernel(xp_ref, x_ref, g_ref, wa_ref, wb_ref, cw_ref, cb_ref, wd_ref,
                    p_ref, wp_ref, ng_ref, gg_ref, wg_ref, o_ref, xn_ref, a_ref, *, tm, seq, tf):
    i = pl.program_id(0)
    halo = 8
    g = g_ref[...]
    x = x_ref[...]
    xn_ref[halo:halo + tm, :] = _rms(x, g).astype(BF16)
    keep = ((i * tm) % seq != 0).astype(F32)
    xn_ref[0:halo, :] = (_rms(xp_ref[...], g) * keep).astype(BF16)
    h2 = x
    for c in range(D_FF // tf):
        cs = slice(c * tf, (c + 1) * tf)
        a_slot = a_ref.at[c % 2]
        a_slot[...] = _dot(xn_ref[...], wa_ref[:, cs])
        conv = cb_ref[:, cs]
        for t in range(FFN_CONV):
            conv = conv + a_slot[pl.ds(halo - (FFN_CONV - 1) + t, tm), :] * cw_ref[t:t + 1, cs]
        bgate = _dot(xn_ref[halo:halo + tm, :], wb_ref[:, cs])
        act = (jax.nn.gelu(conv) * bgate).astype(BF16)
        h2 = h2 + _dot(act, wd_ref[cs, :])
    e = _rms(_dot(p_ref[...].astype(BF16), wp_ref[...]), ng_ref[...])
    gate = jax.nn.sigmoid(_dot(_rms(h2, gg_ref[...]).astype(BF16), wg_ref[...]))
    o_ref[...] = h2 + gate * e


def conv_ffn_ple(h, g, w_up, conv_w, conv_b, w_down, p, wp, norm_g, gate_norm_g, wg, seq, tm=512, tf=256):
    n, d = h.shape
    pd = p.shape[1]
    hb = tm // 8
    const = lambda i: (0, 0)
    resident = dict(pipeline_mode=pl.Buffered(1))
    return pl.pallas_call(
        functools.partial(_ffn_ple_kernel, tm=tm, seq=seq, tf=tf),
        out_shape=jax.ShapeDtypeStruct((n, d), F32),
        grid=(n // tm,),
        in_specs=[pl.BlockSpec((8, d), lambda i: (jnp.maximum(i * hb - 1, 0), 0)),
                  pl.BlockSpec((tm, d), lambda i: (i, 0)),
                  pl.BlockSpec((1, d), const),
                  pl.BlockSpec((d, D_FF), lambda i: (0, 0), **resident),
                  pl.BlockSpec((d, D_FF), lambda i: (0, 1), **resident),
                  pl.BlockSpec((FFN_CONV, D_FF), const),
                  pl.BlockSpec((1, D_FF), const),
                  pl.BlockSpec((D_FF, d), const, **resident),
                  pl.BlockSpec((tm, pd), lambda i: (i, 0)),
                  pl.BlockSpec((pd, d), const, **resident),
                  pl.BlockSpec((1, d), const),
                  pl.BlockSpec((1, d), const),
                  pl.BlockSpec((d, d), const, **resident)],
        out_specs=pl.BlockSpec((tm, d), lambda i: (i, 0)),
        scratch_shapes=[pltpu.VMEM((8 + tm, d), BF16),
                        pltpu.VMEM((2, 8 + tm, tf), F32)],
        compiler_params=_cparams(("parallel",), VMEM_LIMIT_V7X),
        name="conv_ffn_ple",
    )(h, h, g.reshape(1, d), w_up, w_up, conv_w, conv_b.reshape(1, -1), w_down,
      p, wp, norm_g.reshape(1, d), gate_norm_g.reshape(1, d), wg)


def _group_rms(x, g):
    lane = lax.broadcasted_iota(jnp.int32, x.shape, 1)
    x2 = x * x
    ms = jnp.zeros_like(x)
    for grp in range(N_KV_GROUPS):
        in_grp = (lane >= grp * HEAD_DIM) & (lane < (grp + 1) * HEAD_DIM)
        tot = jnp.sum(jnp.where(in_grp, x2, 0.0), axis=-1, keepdims=True)
        ms = jnp.where(in_grp, tot * (1.0 / HEAD_DIM), ms)
    return x * lax.rsqrt(ms + EPS) * g


def _kv_prep_kernel(c_ref, s_ref, w_ref, gs_ref, gw_ref, kc_ref, vc_ref, ks_ref, vs_ref, kw_ref, vw_ref):
    gw = N_KV_GROUPS * HEAD_DIM
    cc = c_ref[...]
    ss = s_ref[...]
    ww = w_ref[...]
    kc_ref[...] = cc[:, 0:gw].astype(BF16)
    vc_ref[...] = cc[:, gw:2 * gw].astype(BF16)
    ks_ref[...] = _group_rms(ss[:, 0:gw], gs_ref[...]).astype(BF16)
    vs_ref[...] = ss[:, gw:2 * gw].astype(BF16)
    kw_ref[...] = _group_rms(ww[:, 0:gw], gw_ref[...]).astype(BF16)
    vw_ref[...] = ww[:, gw:2 * gw].astype(BF16)


def kv_prep(z, k_g, tm=512):
    n = z.shape[0]
    gw = N_KV_GROUPS * HEAD_DIM
    base = N_HEADS * HEAD_DIM // (2 * gw)
    shp = jax.ShapeDtypeStruct((n, gw), BF16)
    ospec = pl.BlockSpec((tm, gw), lambda i: (i, 0))
    return pl.pallas_call(
        _kv_prep_kernel,
        out_shape=(shp,) * 6,
        grid=(n // tm,),
        in_specs=[pl.BlockSpec((tm, 2 * gw), lambda i: (i, base)),
                  pl.BlockSpec((tm, 2 * gw), lambda i: (i, base + 1)),
                  pl.BlockSpec((tm, 2 * gw), lambda i: (i, base + 2)),
                  pl.BlockSpec((1, gw), lambda i: (0, 0)),
                  pl.BlockSpec((1, gw), lambda i: (0, 0))],
        out_specs=(ospec,) * 6,
        compiler_params=_cparams(("parallel",)),
        name="kv_prep",
    )(z, z, z, jnp.tile(k_g[1], N_KV_GROUPS).reshape(1, gw), jnp.tile(k_g[2], N_KV_GROUPS).reshape(1, gw))


def _compress_kernel(x_ref, pos_ref, w1_ref, w2_ref, g_ref, o_ref, *, normalize):
    half = w1_ref.shape[0] // 2
    x = x_ref[...]
    u = _dot(x, w1_ref[0:half, :])
    v = _dot(x, w1_ref[half:2 * half, :])
    rows = u.shape[0]
    posc = _dot(pos_ref[...], w1_ref[...])[0:1, :]
    hid = u + pltpu.roll(v, rows - 1, 0) + posc
    out = _dot(jax.nn.gelu(hid).astype(BF16), w2_ref[...])
    if normalize:
        out = _rms(out, g_ref[...])
    o_ref[...] = out.astype(o_ref.dtype)


def compress(x, pos, w1, w2, g, normalize, ncb):
    n, kdim = x.shape
    posf = jnp.broadcast_to(pos.reshape(1, -1), (8, 2 * kdim)).astype(BF16)
    return pl.pallas_call(
        functools.partial(_compress_kernel, normalize=normalize),
        out_shape=jax.ShapeDtypeStruct((n, HEAD_DIM), BF16),
        grid=(n // ncb,),
        in_specs=[pl.BlockSpec((ncb, kdim), lambda i: (i, 0)),
                  pl.BlockSpec((8, 2 * kdim), lambda i: (0, 0)),
                  pl.BlockSpec((2 * kdim, CMP_HIDDEN), lambda i: (0, 0)),
                  pl.BlockSpec((CMP_HIDDEN, HEAD_DIM), lambda i: (0, 0)),
                  pl.BlockSpec((1, HEAD_DIM), lambda i: (0, 0))],
        out_specs=pl.BlockSpec((ncb, HEAD_DIM), lambda i: (i, 0)),
        compiler_params=_cparams(("parallel",)),
        name="compress",
    )(x, posf, w1.astype(BF16), w2.astype(BF16), g.reshape(1, HEAD_DIM))


def _nsa_kernel(bnd_ref, zq_ref, gt_ref, gb_ref, qg_ref, ovt_ref, kc_ref, vct_ref, ks_ref, vst_ref, kw_ref,
                vwt_ref, o_ref, q_scr, sel_scr, m_scr, l_scr, acc_scr, oc_scr, *, bounded):
    grp = pl.program_id(1)
    qb = pl.program_id(2)
    QB = Q_BLOCK
    ncp = kc_ref.shape[0]
    ns = ovt_ref.shape[0]
    tpos = qb * QB + lax.broadcasted_iota(jnp.int32, (1, QB), 1)

    def lane_tile(x):
        return jnp.concatenate([x] * HG, axis=1)

    zt = zq_ref[...].T
    qg = qg_ref[...] * (HEAD_DIM ** -0.5 * LOG2E)
    heads = []
    for h in range(HG):
        xh = zt[h * HEAD_DIM:(h + 1) * HEAD_DIM, :]
        ms = jnp.mean(xh * xh, axis=0, keepdims=True)
        heads.append(xh * lax.rsqrt(ms + EPS) * qg)
    q_scr[...] = jnp.concatenate(heads, axis=1).astype(BF16)
    q = q_scr[...]

    cmp_end = lax.broadcasted_iota(jnp.int32, (ncp, 1), 0) * CMP_STRIDE + (CMP_BLOCK - 1)
    keep = -bnd_ref[0, 0] if bounded else 0.0
    cbias = jnp.where(cmp_end <= tpos, keep, NEG)
    s = _dot(kc_ref[...], q) + lane_tile(cbias)
    e = jnp.exp2(s) if bounded else jnp.exp2(s - jnp.max(s, axis=0, keepdims=True))
    inv = jnp.where(lane_tile(tpos) >= CMP_BLOCK - 1, 1.0 / jnp.sum(e, axis=0, keepdims=True), 0.0)
    p = e * inv
    oc_scr[...] = _dot(vct_ref[...], p.astype(BF16))
    psum = p[:, 0:QB]
    for h in range(1, HG):
        psum = psum + p[:, h * QB:(h + 1) * QB]

    p_hi = psum.astype(BF16)
    p_lo = (psum - p_hi.astype(F32)).astype(BF16)
    ovt = ovt_ref[...]
    imp = _dot(ovt, p_hi) + _dot(ovt, p_lo)
    blk = lax.broadcasted_iota(jnp.int32, (ns, 1), 0)
    blk_f = blk.astype(F32)
    cur = jnp.right_shift(tpos, SLC_BLOCK.bit_length() - 1)
    forced = (blk == 0) | (blk == cur) | (blk == cur - 1)
    bvalid = blk <= cur
    score = jnp.where(forced, BIG, jnp.where(bvalid, imp, NEG))
    sel = jnp.zeros((ns, QB), F32)
    for _ in range(min(N_SELECT, ns)):
        mx = jnp.max(score, axis=0, keepdims=True)
        first = jnp.min(jnp.where(score == mx, blk_f, float(ns)), axis=0, keepdims=True)
        pick = blk_f == first
        sel = jnp.where(pick, 1.0, sel)
        score = jnp.where(pick, -jnp.inf, score)
    sel_scr[...] = jnp.where(bvalid, sel, 0.0)

    gates = jax.nn.sigmoid(gt_ref[...].T + gb_ref[...])
    per_grp = HG * 3
    gsel = gates[0:per_grp, :]
    for g2 in range(1, N_KV_GROUPS):
        gsel = jnp.where(grp == g2, gates[g2 * per_grp:(g2 + 1) * per_grp, :], gsel)

    def gate_row(c):
        return jnp.concatenate([gsel[3 * h + c:3 * h + c + 1, :] for h in range(HG)], axis=1)

    m_scr[...] = jnp.full(m_scr.shape, NEG, F32)
    l_scr[...] = jnp.zeros(l_scr.shape, F32)
    acc_scr[...] = jnp.zeros(acc_scr.shape, F32)
    blocks_per_tile = SEL_TILE // SLC_BLOCK
    sel_sub = lax.broadcasted_iota(jnp.int32, (SEL_TILE, 1), 0)

    def sel_body(j, carry):
        rows = [jnp.broadcast_to(sel_scr[pl.ds(j * blocks_per_tile + bi, 1), :], (SLC_BLOCK, QB))
                for bi in range(blocks_per_tile)]
        chosen = jnp.concatenate(rows, axis=0)
        kpos = j * SEL_TILE + sel_sub
        bias = jnp.where((chosen > 0.5) & (kpos <= tpos), keep, NEG)
        st = _dot(ks_ref[j], q_scr[...]) + lane_tile(bias)
        if bounded:
            pt = jnp.exp2(st)
            l_scr[...] += jnp.sum(pt, axis=0, keepdims=True)
            acc_scr[...] += _dot(vst_ref[j], pt.astype(BF16))
        else:
            m_old = m_scr[...]
            m_new = jnp.maximum(m_old, jnp.max(st, axis=0, keepdims=True))
            alpha = jnp.exp2(m_old - m_new)
            pt = jnp.exp2(st - m_new)
            l_scr[...] = alpha * l_scr[...] + jnp.sum(pt, axis=0, keepdims=True)
            acc_scr[...] = alpha * acc_scr[...] + _dot(vst_ref[j], pt.astype(BF16))
            m_scr[...] = m_new
        return carry

    lax.fori_loop(0, qb // (SEL_TILE // QB) + 1, sel_body, 0)
    oc_scr[...] = gate_row(0) * oc_scr[...] + (gate_row(1) / l_scr[...]) * acc_scr[...]

    n_win = WINDOW // KEY_TILE + 1
    win_sub = lax.broadcasted_iota(jnp.int32, (n_win * KEY_TILE, 1), 0)
    tiles = [jnp.maximum(qb - (n_win - 1) + u, 0) for u in range(n_win)]
    kwin = jnp.concatenate([kw_ref[j] for j in tiles], axis=0)
    vwin = jnp.concatenate([vwt_ref[j] for j in tiles], axis=1)
    kpos = (qb - (n_win - 1)) * KEY_TILE + win_sub
    wbias = jnp.where((kpos <= tpos) & (kpos > tpos - WINDOW) & (kpos >= 0), keep, NEG)
    sw = _dot(kwin, q_scr[...]) + lane_tile(wbias)
    pw = jnp.exp2(sw) if bounded else jnp.exp2(sw - jnp.max(sw, axis=0, keepdims=True))
    lw = jnp.sum(pw, axis=0, keepdims=True)
    ow = _dot(vwin, pw.astype(BF16))
    ot = oc_scr[...] + (gate_row(2) / lw) * ow
    stacked = jnp.concatenate([ot[:, h * QB:(h + 1) * QB] for h in range(HG)], axis=0)
    o_ref[...] = stacked.T


def nsa_attention(bound, z3, gate_b, q_g, overlap_t, kc, vct, ks, vst, kw, vwt, *, bounded):
    b, t, _ = z3.shape
    nq = t // Q_BLOCK
    nt = t // KEY_TILE
    qw = HG * HEAD_DIM
    ns, ncp = overlap_t.shape
    gate_blk = (NSA_COLS_PAD // LANES_V7X) - 1
    gb = jnp.zeros((LANES_V7X, 1), F32).at[:N_HEADS * 3, 0].set(gate_b)
    full5 = lambda i, g, q: (i, g, 0, 0, 0)
    lanes = HG * Q_BLOCK
    return pl.pallas_call(
        functools.partial(_nsa_kernel, bounded=bounded),
        out_shape=jax.ShapeDtypeStruct((b, t, N_HEADS * HEAD_DIM), F32),
        grid=(b, N_KV_GROUPS, nq),
        in_specs=[pl.BlockSpec(memory_space=pltpu.SMEM),
                  pl.BlockSpec((None, Q_BLOCK, qw), lambda i, g, q: (i, q, g)),
                  pl.BlockSpec((None, Q_BLOCK, LANES_V7X), lambda i, g, q: (i, q, gate_blk)),
                  pl.BlockSpec((LANES_V7X, 1), lambda i, g, q: (0, 0)),
                  pl.BlockSpec((HEAD_DIM, 1), lambda i, g, q: (0, 0)),
                  pl.BlockSpec((ns, ncp), lambda i, g, q: (0, 0)),
                  pl.BlockSpec((None, None, ncp, HEAD_DIM), lambda i, g, q: (i, g, 0, 0)),
                  pl.BlockSpec((None, None, HEAD_DIM, ncp), lambda i, g, q: (i, g, 0, 0)),
                  pl.BlockSpec((None, None, t // SEL_TILE, SEL_TILE, HEAD_DIM), full5),
                  pl.BlockSpec((None, None, t // SEL_TILE, HEAD_DIM, SEL_TILE), full5),
                  pl.BlockSpec((None, None, nt, KEY_TILE, HEAD_DIM), full5),
                  pl.BlockSpec((None, None, nt, HEAD_DIM, KEY_TILE), full5)],
        out_specs=pl.BlockSpec((None, Q_BLOCK, qw), lambda i, g, q: (i, q, g)),
        scratch_shapes=[pltpu.VMEM((HEAD_DIM, lanes), BF16),
                        pltpu.VMEM((ns, Q_BLOCK), F32),
                        pltpu.VMEM((1, lanes), F32),
                        pltpu.VMEM((1, lanes), F32),
                        pltpu.VMEM((HEAD_DIM, lanes), F32),
                        pltpu.VMEM((HEAD_DIM, lanes), F32)],
        compiler_params=_cparams(("parallel", "parallel", "arbitrary"), VMEM_LIMIT_V7X),
        name="nsa_attention_bounded" if bounded else "nsa_attention_online",
    )(bound.reshape(1, 1), z3, z3, gb, q_g.reshape(HEAD_DIM, 1), overlap_t, kc, vct, ks, vst, kw, vwt)


def _overlap_matrix_t(t):
    ncp = t // CMP_STRIDE
    ns = t // SLC_BLOCK
    c_start = np.arange(ncp) * CMP_STRIDE
    sj = np.arange(ns)
    ov = ((c_start[None, :] < (sj[:, None] + 1) * SLC_BLOCK)
          & (c_start[None, :] + CMP_BLOCK > sj[:, None] * SLC_BLOCK)
          & (c_start[None, :] + CMP_BLOCK <= t))
    return jnp.asarray(ov, dtype=BF16)


def ab_layer(h, b, t, norm_g, w_in, conv_w, conv_b, ret_g, ig_b, fg_b, m_g, w_out):
    n = b * t
    w_in_p = jnp.pad(w_in, ((0, 0), (0, AB_COLS_PAD - AB_COLS))).astype(BF16)
    z = norm_matmul(h, norm_g, w_in_p)
    z3 = z.reshape(b, t, AB_COLS_PAD)
    cos_t, sin_t = rope_tables(t)
    ret = retention(z3, cos_t, sin_t, ret_g)
    g0 = AB_COLS - 2 * M_HEADS
    gates_r = z3[:, :, g0:AB_COLS].reshape(b, t // M_CHUNK, M_CHUNK, 2 * M_HEADS).transpose(0, 1, 3, 2)
    ml = mlstm(z3, gates_r, conv_w, conv_b, ig_b, fg_b, m_g)
    w_out_b = w_out.astype(BF16)
    rw = R_HEADS * R_DV
    return proj_residual(h, [(ret.reshape(n, rw), w_out_b[:rw]), (ml.reshape(n, -1), w_out_b[rw:])])


def nsa_layer(h, b, t, norm_g, w_in, q_g, k_g, pos_k, pos_v, w1k, w2k, w1v, w2v, gate_b, w_out):
    n = b * t
    G = N_KV_GROUPS
    w_in_p = jnp.pad(w_in, ((0, 0), (0, NSA_COLS_PAD - NSA_COLS))).astype(BF16)
    z = norm_matmul(h, norm_g, w_in_p)
    kc_in, vc_in, ks, vs, kw, vw = kv_prep(z, k_g)
    nt = t // KEY_TILE
    ncb = t // CMP_STRIDE

    def cmp_rows(x):
        return x.reshape(b, t, G, HEAD_DIM).transpose(0, 2, 1, 3).reshape(b * G * ncb, CMP_STRIDE * HEAD_DIM)

    kc = compress(cmp_rows(kc_in), pos_k, w1k, w2k, k_g[0], True, ncb)
    vc = compress(cmp_rows(vc_in), pos_v, w1v, w2v, k_g[0], False, ncb)
    kc = kc.reshape(b, G, ncb, HEAD_DIM)
    vct = vc.reshape(b, G, ncb, HEAD_DIM).transpose(0, 1, 3, 2)

    def key_tiles(x, kt):
        return x.reshape(b, t // kt, kt, G, HEAD_DIM).transpose(0, 3, 1, 2, 4)

    def val_tiles_t(x, kt):
        return x.reshape(b, t // kt, kt, G, HEAD_DIM).transpose(0, 3, 1, 4, 2)

    bound = 1.02 * LOG2E * math.sqrt(HEAD_DIM) * jnp.max(jnp.abs(q_g)) * jnp.max(jnp.abs(k_g))
    args = (bound, z.reshape(b, t, NSA_COLS_PAD), gate_b, q_g, _overlap_matrix_t(t), kc, vct,
            key_tiles(ks, SEL_TILE), val_tiles_t(vs, SEL_TILE),
            key_tiles(kw, KEY_TILE), val_tiles_t(vw, KEY_TILE))
    o = lax.cond(bound <= MAX_SAFE_SCORE_BOUND,
                 functools.partial(nsa_attention, bounded=True),
                 functools.partial(nsa_attention, bounded=False), *args)
    return proj_residual(h, [(o.reshape(n, -1), w_out.astype(BF16))])


def kernel(x, p, ab_norm_g, ab_w_in, ab_conv_w, ab_conv_b, ab_ret_norm_g, ab_ig_b, ab_fg_b, ab_m_norm_g, ab_w_out, nsa_norm_g, nsa_w_in, nsa_q_norm_g, nsa_k_norm_g, nsa_cmp_pos_k, nsa_cmp_pos_v, nsa_cmp_w1k, nsa_cmp_w2k, nsa_cmp_w1v, nsa_cmp_w2v, nsa_gate_b, nsa_w_out, ffn_norm_g, ffn_w_up, ffn_conv_w, ffn_conv_b, ffn_w_down, ple_w, ple_norm_g, ple_gate_norm_g, ple_w_gate):
    b, t, d = x.shape
    n = b * t
    depth = p.shape[0]
    h = x.reshape(n, d)
    for i in range(depth):
        j = i // 2
        if i % 2 == 0:
            h = ab_layer(h, b, t, ab_norm_g[j], ab_w_in[j], ab_conv_w[j], ab_conv_b[j], ab_ret_norm_g[j],
                         ab_ig_b[j], ab_fg_b[j], ab_m_norm_g[j], ab_w_out[j])
        else:
            h = nsa_layer(h, b, t, nsa_norm_g[j], nsa_w_in[j], nsa_q_norm_g[j], nsa_k_norm_g[j],
                          nsa_cmp_pos_k[j], nsa_cmp_pos_v[j], nsa_cmp_w1k[j], nsa_cmp_w2k[j],
                          nsa_cmp_w1v[j], nsa_cmp_w2v[j], nsa_gate_b[j], nsa_w_out[j])
        h = conv_ffn_ple(h, ffn_norm_g[i], ffn_w_up[i].astype(BF16), ffn_conv_w[i], ffn_conv_b[i],
                         ffn_w_down[i].astype(BF16), p[i].reshape(n, -1), ple_w[i].astype(BF16),
                         ple_norm_g[i], ple_gate_norm_g[i], ple_w_gate[i].astype(BF16), t)
    return h.reshape(b, t, d)
```

```python
import functools
import math

import numpy as np
import jax
import jax.numpy as jnp
from jax import lax
from jax.experimental import pallas as pl
from jax.experimental.pallas import tpu as pltpu

F32 = jnp.float32
BF16 = jnp.bfloat16

LANES_V7X = 128
VMEM_LIMIT_V7X = 56 * 1024 * 1024

D_MODEL = 1024
PLE_DIM = 256
R_HEADS, R_DK, R_DV, R_CHUNK = 4, 64, 128, 128
ROPE_BASE = 10000.0
M_HEADS, M_DK, M_DV, M_CHUNK, M_CONV = 4, 64, 128, 64, 4
AB_SIZES = (R_HEADS * R_DK, R_HEADS * R_DK, R_HEADS * R_DV, R_HEADS * R_DV,
            M_HEADS * M_DK, M_HEADS * M_DK, M_HEADS * M_DV, M_HEADS * M_DV, M_HEADS, M_HEADS)
AB_COLS = sum(AB_SIZES)
AB_COLS_PAD = 3200
N_HEADS, N_KV_GROUPS, HEAD_DIM = 16, 2, 64
HG = N_HEADS // N_KV_GROUPS
CMP_BLOCK, CMP_STRIDE, CMP_HIDDEN = 32, 16, 256
SLC_BLOCK, N_SELECT, WINDOW, Q_BLOCK = 64, 16, 512, 128
NSA_COLS = N_HEADS * HEAD_DIM + 6 * N_KV_GROUPS * HEAD_DIM + N_HEADS * 3
NSA_COLS_PAD = 1920
D_FF = 2816
FFN_CONV = 3
NEG = -1e30
BIG = 1e30
EPS = 1e-6
KEY_TILE = 128
SEL_TILE = 512
LOG2E = math.log2(math.e)
MAX_SAFE_SCORE_BOUND = 56.0


def _cparams(sem, vmem=None):
    return pltpu.CompilerParams(dimension_semantics=sem, vmem_limit_bytes=vmem)


def _rms(x, g):
    ms = jnp.mean(x * x, axis=-1, keepdims=True)
    return x * lax.rsqrt(ms + EPS) * g


def _dot(a, b):
    return jnp.dot(a, b, preferred_element_type=F32)


def _dot_nt(a, b):
    return lax.dot_general(a, b, (((1,), (1,)), ((), ())), preferred_element_type=F32)


def _dot_f32(a, b):
    return jnp.dot(a, b, preferred_element_type=F32, precision=lax.Precision.HIGHEST)


def _norm_matmul_kernel(x_ref, g_ref, w_ref, o_ref):
    xn = _rms(x_ref[...], g_ref[...]).astype(BF16)
    o_ref[...] = _dot(xn, w_ref[...]).astype(o_ref.dtype)


def norm_matmul(x, g, w, tm=512, out_dtype=F32):
    n, d = x.shape
    nc = w.shape[1]
    return pl.pallas_call(
        _norm_matmul_kernel,
        out_shape=jax.ShapeDtypeStruct((n, nc), out_dtype),
        grid=(n // tm,),
        in_specs=[pl.BlockSpec((tm, d), lambda i: (i, 0)),
                  pl.BlockSpec((1, d), lambda i: (0, 0)),
                  pl.BlockSpec((d, nc), lambda i: (0, 0))],
        out_specs=pl.BlockSpec((tm, nc), lambda i: (i, 0)),
        compiler_params=_cparams(("parallel",), VMEM_LIMIT_V7X),
        name="norm_matmul",
    )(x, g.reshape(1, d), w)


def _proj_residual_kernel(*refs, n_in):
    h_ref = refs[0]
    o_ref = refs[1 + 2 * n_in]
    acc = h_ref[...]
    for k in range(n_in):
        acc = acc + _dot(refs[1 + 2 * k][...].astype(BF16), refs[2 + 2 * k][...])
    o_ref[...] = acc


def proj_residual(h, pairs, tm=512):
    n, d = h.shape
    in_specs = [pl.BlockSpec((tm, d), lambda i: (i, 0))]
    args = [h]
    for a, w in pairs:
        in_specs.append(pl.BlockSpec((tm, a.shape[1]), lambda i: (i, 0)))
        in_specs.append(pl.BlockSpec(w.shape, lambda i: (0, 0)))
        args += [a, w]
    return pl.pallas_call(
        functools.partial(_proj_residual_kernel, n_in=len(pairs)),
        out_shape=jax.ShapeDtypeStruct((n, d), F32),
        grid=(n // tm,),
        in_specs=in_specs,
        out_specs=pl.BlockSpec((tm, d), lambda i: (i, 0)),
        compiler_params=_cparams(("parallel",), VMEM_LIMIT_V7X),
        name="proj_residual",
    )(*args)


def _rope_table_kernel(inv_ref, cos_ref, sin_ref):
    c = pl.program_id(0)
    rows, width = cos_ref.shape
    pos = (c * rows + lax.broadcasted_iota(jnp.int32, (rows, width), 0)).astype(F32)
    lane = lax.broadcasted_iota(jnp.int32, (rows, width), 1)
    ang = pos * inv_ref[...]
    cos_ref[...] = jnp.cos(ang)
    sn = jnp.sin(ang)
    sin_ref[...] = jnp.where(lane % R_DK < R_DK // 2, -sn, sn)


def rope_tables(t):
    half = R_DK // 2
    inv = ROPE_BASE ** (-jnp.arange(half, dtype=F32) / half)
    inv = jnp.tile(inv, 2 * R_HEADS).reshape(1, R_HEADS * R_DK)
    width = R_HEADS * R_DK
    shp = jax.ShapeDtypeStruct((t, width), F32)
    return pl.pallas_call(
        _rope_table_kernel,
        out_shape=(shp, shp),
        grid=(t // R_CHUNK,),
        in_specs=[pl.BlockSpec((1, width), lambda c: (0, 0))],
        out_specs=(pl.BlockSpec((R_CHUNK, width), lambda c: (c, 0)),
                   pl.BlockSpec((R_CHUNK, width), lambda c: (c, 0))),
        compiler_params=_cparams(("parallel",)),
        name="rope_tables",
    )(inv)


def _retention_kernel(cos_ref, sin_ref, q_ref, k_ref, v_ref, g_ref, gain_ref, o_ref, r_ref):
    c = pl.program_id(0)
    nb = q_ref.shape[0]
    L = R_CHUNK

    @pl.when(c == 0)
    def _():
        r_ref[...] = jnp.zeros_like(r_ref)

    cos = cos_ref[...]
    sin = sin_ref[...]
    lane = lax.broadcasted_iota(jnp.int32, cos.shape, 1)
    first_half = lane % R_DK < R_DK // 2
    width = R_HEADS * R_DK

    def rot(x):
        swapped = jnp.where(first_half, pltpu.roll(x, width - R_DK // 2, 1), pltpu.roll(x, R_DK // 2, 1))
        return x * cos + swapped * sin

    gain = gain_ref[...]
    ri = lax.broadcasted_iota(jnp.int32, (L, L), 0)
    ci = lax.broadcasted_iota(jnp.int32, (L, L), 1)
    diff = (ri - ci).astype(F32)
    causal = ri >= ci
    idx = lax.broadcasted_iota(jnp.int32, (L, 1), 0).astype(F32)
    decays = []
    for h in range(R_HEADS):
        log_g = math.log1p(-2.0 ** (-5.0 - h))
        decays.append(dict(
            dmask=jnp.where(causal, jnp.exp(jnp.where(causal, diff, 0.0) * log_g), 0.0),
            q_dec=jnp.exp((idx + 1.0) * log_g), k_dec=jnp.exp((L - 1.0 - idx) * log_g),
            c_dec=math.exp(L * log_g)))

    chains = []
    for bi in range(nb):
        q = rot(q_ref[bi])
        k = rot(k_ref[bi]) * (R_DK ** -0.5)
        for h in range(R_HEADS):
            chains.append(dict(bi=bi, h=h, u=bi * R_HEADS + h,
                               qb=q[:, h * R_DK:(h + 1) * R_DK].astype(BF16),
                               kh=k[:, h * R_DK:(h + 1) * R_DK]))

    for ch in chains:
        ch['rstate'] = r_ref[ch['u']]
        ch['s'] = _dot_nt(ch['qb'], ch['kh'].astype(BF16)) * decays[ch['h']]['dmask']
        ch['qr'] = _dot(ch['qb'], ch['rstate'].astype(BF16))

    for ch in chains:
        d = decays[ch['h']]
        vh = v_ref[ch['bi'], :, ch['h'] * R_DV:(ch['h'] + 1) * R_DV].astype(BF16)
        ch['o'] = _dot(ch['s'].astype(BF16), vh) + ch['qr'] * d['q_dec']
        kd = (ch['kh'] * d['k_dec']).T.astype(BF16)
        r_ref[ch['u']] = d['c_dec'] * ch['rstate'] + _dot(kd, vh)

    for ch in chains:
        sl = slice(ch['h'] * R_DV, (ch['h'] + 1) * R_DV)
        gh = g_ref[ch['bi'], :, sl]
        o_ref[ch['bi'], :, sl] = _rms(ch['o'], gain[:, sl]) * (gh * jax.nn.sigmoid(gh))


def retention(z3, cos_t, sin_t, gain):
    b, t, _ = z3.shape
    nc = t // R_CHUNK
    qk_w = R_HEADS * R_DK
    v_w = R_HEADS * R_DV
    return pl.pallas_call(
        _retention_kernel,
        out_shape=jax.ShapeDtypeStruct((b, t, v_w), F32),
        grid=(nc,),
        in_specs=[pl.BlockSpec((R_CHUNK, qk_w), lambda c: (c, 0)),
                  pl.BlockSpec((R_CHUNK, qk_w), lambda c: (c, 0)),
                  pl.BlockSpec((b, R_CHUNK, qk_w), lambda c: (0, c, 0)),
                  pl.BlockSpec((b, R_CHUNK, qk_w), lambda c: (0, c, 1)),
                  pl.BlockSpec((b, R_CHUNK, v_w), lambda c: (0, c, 1)),
                  pl.BlockSpec((b, R_CHUNK, v_w), lambda c: (0, c, 2)),
                  pl.BlockSpec((1, v_w), lambda c: (0, 0))],
        out_specs=pl.BlockSpec((b, R_CHUNK, v_w), lambda c: (0, c, 0)),
        scratch_shapes=[pltpu.VMEM((b * R_HEADS, R_DK, R_DV), F32)],
        compiler_params=_cparams(("arbitrary",)),
        name="retention",
    )(cos_t, sin_t, z3, z3, z3, z3, gain.reshape(1, v_w))


def _mlstm_kernel(q_ref, k_ref, v_ref, og_ref, gc_ref, gr_ref, cw_ref, cb_ref, bc_ref, br_ref, gain_ref,
                  o_ref, xbuf, c_ref, n_ref, m_ref):
    c = pl.program_id(0)
    nb = q_ref.shape[0]
    L = M_CHUNK
    H = M_HEADS
    qk_w = H * M_DK
    halo = 8

    @pl.when(c == 0)
    def _():
        xbuf[:, 0:halo, :] = jnp.zeros((nb, halo, 2 * qk_w), F32)
        c_ref[...] = jnp.zeros_like(c_ref)
        n_ref[...] = jnp.zeros_like(n_ref)
        m_ref[...] = jnp.zeros_like(m_ref)

    gain = gain_ref[...]
    ri = lax.broadcasted_iota(jnp.int32, (L, L), 0)
    ci = lax.broadcasted_iota(jnp.int32, (L, L), 1)
    causal = ri >= ci
    tril = causal.astype(F32)
    triu = (ri <= ci).astype(F32)

    chains = []
    for bi in range(nb):
        xbuf[bi, halo:halo + L, 0:qk_w] = q_ref[bi]
        xbuf[bi, halo:halo + L, qk_w:2 * qk_w] = k_ref[bi]
        conv = cb_ref[...]
        for j in range(M_CONV):
            conv = conv + xbuf[bi, pl.ds(halo - (M_CONV - 1) + j, L), :] * cw_ref[j:j + 1, :]
        tail = xbuf[bi, L:L + halo, :]
        xbuf[bi, 0:halo, :] = tail
        act = conv * jax.nn.sigmoid(conv)
        q = act[:, 0:qk_w] * (M_DK ** -0.5)
        k = act[:, qk_w:2 * qk_w]
        gc = gc_ref[bi][:, 0:2 * H] + bc_ref[...]
        gr = gr_ref[bi] + br_ref[...]
        ig_c = gc[:, 0:H]
        ig_r = gr[0:H, :]
        b_c = _dot_f32(tril, jax.nn.log_sigmoid(gc[:, H:2 * H]))
        b_r = _dot_f32(jax.nn.log_sigmoid(gr[H:2 * H, :]), triu)
        for h in range(H):
            chains.append(dict(
                bi=bi, h=h, u=bi * H + h,
                qh=q[:, h * M_DK:(h + 1) * M_DK], kh=k[:, h * M_DK:(h + 1) * M_DK],
                bh=b_c[:, h:h + 1], brow=b_r[h:h + 1, :], irow=ig_r[h:h + 1, :], icol=ig_c[:, h:h + 1]))

    def stack(parts):
        return jnp.concatenate(parts, axis=0)

    def rows(x):
        return jnp.broadcast_to(x, (L, x.shape[1]))

    m_prev_u = [m_ref[ch['u']][:, 0:1] for ch in chains]
    b_last_u = [ch['bh'][L - 1:L, :] for ch in chains]
    bh = stack([ch['bh'] for ch in chains])
    icol = stack([ch['icol'] for ch in chains])
    brow = stack([rows(ch['brow']) for ch in chains])
    irow = stack([rows(ch['irow']) for ch in chains])
    m_prev = stack([rows(m) for m in m_prev_u])
    b_last = stack([rows(x) for x in b_last_u])
    causal_all = stack([causal] * len(chains))
    qs = stack([ch['qh'] for ch in chains])
    ks = stack([ch['kh'] for ch in chains])
    qb = qs.astype(BF16)
    kb = ks.astype(BF16)
    cstates = [c_ref[ch['u']] for ch in chains]
    nstates = [n_ref[ch['u']] for ch in chains]

    def chain_rows(x, i):
        return x[i * L:(i + 1) * L]

    s_raw = stack([_dot_nt(chain_rows(qb, i), chain_rows(kb, i)) for i in range(len(chains))])
    qc = stack([_dot(chain_rows(qb, i), cstates[i].astype(BF16)) for i in range(len(chains))])

    dlog = jnp.where(causal_all, bh - brow + irow, NEG)
    inter = bh + m_prev
    m_t = jnp.maximum(inter, jnp.max(dlog, axis=-1, keepdims=True))
    s = s_raw * jnp.exp(dlog - m_t)
    w_inter = jnp.exp(inter - m_t)
    wlog = b_last - bh + icol
    m_new_u = [jnp.maximum(b_last_u[i] + m_prev_u[i], jnp.max(chain_rows(wlog, i), axis=0, keepdims=True))
               for i in range(len(chains))]
    m_new = stack([rows(m) for m in m_new_u])
    wk = ks * jnp.exp(wlog - m_new)

    sb = s.astype(BF16)
    vhs = [v_ref[ch['bi'], :, ch['h'] * M_DV:(ch['h'] + 1) * M_DV].astype(BF16) for ch in chains]
    sv = stack([_dot(chain_rows(sb, i), vhs[i]) for i in range(len(chains))])
    kv = [_dot(chain_rows(wk, i).T.astype(BF16), vhs[i]) for i in range(len(chains))]

    qn = jnp.sum(qs * stack([rows(n) for n in nstates]), axis=-1, keepdims=True)
    den = jnp.sum(s, axis=-1, keepdims=True) + w_inter * qn
    hh = (sv + w_inter * qc) / jnp.maximum(jnp.abs(den), jnp.exp(-m_t))
    og = stack([og_ref[ch['bi'], :, ch['h'] * M_DV:(ch['h'] + 1) * M_DV] for ch in chains])
    gains = stack([rows(gain[:, ch['h'] * M_DV:(ch['h'] + 1) * M_DV]) for ch in chains])
    out = jax.nn.sigmoid(og) * _rms(hh, gains)

    for i, ch in enumerate(chains):
        u = ch['u']
        decay = jnp.exp(b_last_u[i] + m_prev_u[i] - m_new_u[i])
        c_ref[u] = decay * cstates[i] + kv[i]
        n_ref[u] = decay * nstates[i] + jnp.sum(chain_rows(wk, i), axis=0, keepdims=True)
        m_ref[u] = jnp.broadcast_to(m_new_u[i], (1, LANES_V7X))
        o_ref[ch['bi'], :, ch['h'] * M_DV:(ch['h'] + 1) * M_DV] = chain_rows(out, i)


def mlstm(z3, gates_r, conv_w, conv_b, ig_b, fg_b, gain):
    b, t, _ = z3.shape
    nc = t // M_CHUNK
    H = M_HEADS
    qk_w = H * M_DK
    v_w = H * M_DV
    bias = jnp.concatenate([ig_b, fg_b])
    gate_blk = AB_COLS_PAD // LANES_V7X - 1
    return pl.pallas_call(
        _mlstm_kernel,
        out_shape=jax.ShapeDtypeStruct((b, t, v_w), F32),
        grid=(nc,),
        in_specs=[pl.BlockSpec((b, M_CHUNK, qk_w), lambda c: (0, c, 6)),
                  pl.BlockSpec((b, M_CHUNK, qk_w), lambda c: (0, c, 7)),
                  pl.BlockSpec((b, M_CHUNK, v_w), lambda c: (0, c, 4)),
                  pl.BlockSpec((b, M_CHUNK, v_w), lambda c: (0, c, 5)),
                  pl.BlockSpec((b, M_CHUNK, LANES_V7X), lambda c: (0, c, gate_blk)),
                  pl.BlockSpec((b, None, 2 * H, M_CHUNK), lambda c: (0, c, 0, 0)),
                  pl.BlockSpec((M_CONV, 2 * qk_w), lambda c: (0, 0)),
                  pl.BlockSpec((1, 2 * qk_w), lambda c: (0, 0)),
                  pl.BlockSpec((1, 2 * H), lambda c: (0, 0)),
                  pl.BlockSpec((2 * H, 1), lambda c: (0, 0)),
                  pl.BlockSpec((1, v_w), lambda c: (0, 0))],
        out_specs=pl.BlockSpec((b, M_CHUNK, v_w), lambda c: (0, c, 0)),
        scratch_shapes=[pltpu.VMEM((b, 8 + M_CHUNK, 2 * qk_w), F32),
                        pltpu.VMEM((b * H, M_DK, M_DV), F32),
                        pltpu.VMEM((b * H, 1, M_DK), F32),
                        pltpu.VMEM((b * H, 1, LANES_V7X), F32)],
        compiler_params=_cparams(("arbitrary",)),
        name="mlstm",
    )(z3, z3, z3, z3, z3, gates_r, conv_w, conv_b.reshape(1, -1), bias.reshape(1, -1), bias.reshape(-1, 1),
      gain.reshape(1, v_w))


def _ffn_ple_kernel(xp_ref, x_ref, g_ref, wa_ref, wb_ref, cw_ref, cb_ref, wd_ref,
                    p_ref, wp_ref, ng_ref, gg_ref, wg_ref, o_ref, xn_ref, a_ref, *, tm, seq, tf):
    i = pl.program_id(0)
    halo = 8
    g = g_ref[...]
    x = x_ref[...]
    xn_ref[halo:halo + tm, :] = _rms(x, g).astype(BF16)
    keep = ((i * tm) % seq != 0).astype(F32)
    xn_ref[0:halo, :] = (_rms(xp_ref[...], g) * keep).astype(BF16)
    h2 = x
    for c in range(D_FF // tf):
        cs = slice(c * tf, (c + 1) * tf)
        a_slot = a_ref.at[c % 2]
        a_slot[...] = _dot(xn_ref[...], wa_ref[:, cs])
        conv = cb_ref[:, cs]
        for t in range(FFN_CONV):
            conv = conv + a_slot[pl.ds(halo - (FFN_CONV - 1) + t, tm), :] * cw_ref[t:t + 1, cs]
        bgate = _dot(xn_ref[halo:halo + tm, :], wb_ref[:, cs])
        act = (jax.nn.gelu(conv) * bgate).astype(BF16)
        h2 = h2 + _dot(act, wd_ref[cs, :])
    e = _rms(_dot(p_ref[...].astype(BF16), wp_ref[...]), ng_ref[...])
    gate = jax.nn.sigmoid(_dot(_rms(h2, gg_ref[...]).astype(BF16), wg_ref[...]))
    o_ref[...] = h2 + gate * e


def conv_ffn_ple(h, g, w_up, conv_w, conv_b, w_down, p, wp, norm_g, gate_norm_g, wg, seq, tm=512, tf=256):
    n, d = h.shape
    pd = p.shape[1]
    hb = tm // 8
    const = lambda i: (0, 0)
    resident = dict(pipeline_mode=pl.Buffered(1))
    return pl.pallas_call(
        functools.partial(_ffn_ple_kernel, tm=tm, seq=seq, tf=tf),
        out_shape=jax.ShapeDtypeStruct((n, d), F32),
        grid=(n // tm,),
        in_specs=[pl.BlockSpec((8, d), lambda i: (jnp.maximum(i * hb - 1, 0), 0)),
                  pl.BlockSpec((tm, d), lambda i: (i, 0)),
                  pl.BlockSpec((1, d), const),
                  pl.BlockSpec((d, D_FF), lambda i: (0, 0), **resident),
                  pl.BlockSpec((d, D_FF), lambda i: (0, 1), **resident),
                  pl.BlockSpec((FFN_CONV, D_FF), const),
                  pl.BlockSpec((1, D_FF), const),
                  pl.BlockSpec((D_FF, d), const, **resident),
                  pl.BlockSpec((tm, pd), lambda i: (i, 0)),
                  pl.BlockSpec((pd, d), const, **resident),
                  pl.BlockSpec((1, d), const),
                  pl.BlockSpec((1, d), const),
                  pl.BlockSpec((d, d), const, **resident)],
        out_specs=pl.BlockSpec((tm, d), lambda i: (i, 0)),
        scratch_shapes=[pltpu.VMEM((8 + tm, d), BF16),
                        pltpu.VMEM((2, 8 + tm, tf), F32)],
        compiler_params=_cparams(("parallel",), VMEM_LIMIT_V7X),
        name="conv_ffn_ple",
    )(h, h, g.reshape(1, d), w_up, w_up, conv_w, conv_b.reshape(1, -1), w_down,
      p, wp, norm_g.reshape(1, d), gate_norm_g.reshape(1, d), wg)


def _group_rms(x, g):
    lane = lax.broadcasted_iota(jnp.int32, x.shape, 1)
    x2 = x * x
    ms = jnp.zeros_like(x)
    for grp in range(N_KV_GROUPS):
        in_grp = (lane >= grp * HEAD_DIM) & (lane < (grp + 1) * HEAD_DIM)
        tot = jnp.sum(jnp.where(in_grp, x2, 0.0), axis=-1, keepdims=True)
        ms = jnp.where(in_grp, tot * (1.0 / HEAD_DIM), ms)
    return x * lax.rsqrt(ms + EPS) * g


def _kv_prep_kernel(c_ref, s_ref, w_ref, gs_ref, gw_ref, kc_ref, vc_ref, ks_ref, vs_ref, kw_ref, vw_ref):
    gw = N_KV_GROUPS * HEAD_DIM
    cc = c_ref[...]
    ss = s_ref[...]
    ww = w_ref[...]
    kc_ref[...] = cc[:, 0:gw].astype(BF16)
    vc_ref[...] = cc[:, gw:2 * gw].astype(BF16)
    ks_ref[...] = _group_rms(ss[:, 0:gw], gs_ref[...]).astype(BF16)
    vs_ref[...] = ss[:, gw:2 * gw].astype(BF16)
    kw_ref[...] = _group_rms(ww[:, 0:gw], gw_ref[...]).astype(BF16)
    vw_ref[...] = ww[:, gw:2 * gw].astype(BF16)


def kv_prep(z, k_g, tm=512):
    n = z.shape[0]
    gw = N_KV_GROUPS * HEAD_DIM
    base = N_HEADS * HEAD_DIM // (2 * gw)
    shp = jax.ShapeDtypeStruct((n, gw), BF16)
    ospec = pl.BlockSpec((tm, gw), lambda i: (i, 0))
    return pl.pallas_call(
        _kv_prep_kernel,
        out_shape=(shp,) * 6,
        grid=(n // tm,),
        in_specs=[pl.BlockSpec((tm, 2 * gw), lambda i: (i, base)),
                  pl.BlockSpec((tm, 2 * gw), lambda i: (i, base + 1)),
                  pl.BlockSpec((tm, 2 * gw), lambda i: (i, base + 2)),
                  pl.BlockSpec((1, gw), lambda i: (0, 0)),
                  pl.BlockSpec((1, gw), lambda i: (0, 0))],
        out_specs=(ospec,) * 6,
        compiler_params=_cparams(("parallel",)),
        name="kv_prep",
    )(z, z, z, jnp.tile(k_g[1], N_KV_GROUPS).reshape(1, gw), jnp.tile(k_g[2], N_KV_GROUPS).reshape(1, gw))


def _compress_kernel(x_ref, pos_ref, w1_ref, w2_ref, g_ref, o_ref, *, normalize):
    half = w1_ref.shape[0] // 2
    x = x_ref[...]
    u = _dot(x, w1_ref[0:half, :])
    v = _dot(x, w1_ref[half:2 * half, :])
    rows = u.shape[0]
    posc = _dot(pos_ref[...], w1_ref[...])[0:1, :]
    hid = u + pltpu.roll(v, rows - 1, 0) + posc
    out = _dot(jax.nn.gelu(hid).astype(BF16), w2_ref[...])
    if normalize:
        out = _rms(out, g_ref[...])
    o_ref[...] = out.astype(o_ref.dtype)


def compress(x, pos, w1, w2, g, normalize, ncb):
    n, kdim = x.shape
    posf = jnp.broadcast_to(pos.reshape(1, -1), (8, 2 * kdim)).astype(BF16)
    return pl.pallas_call(
        functools.partial(_compress_kernel, normalize=normalize),
        out_shape=jax.ShapeDtypeStruct((n, HEAD_DIM), BF16),
        grid=(n // ncb,),
        in_specs=[pl.BlockSpec((ncb, kdim), lambda i: (i, 0)),
                  pl.BlockSpec((8, 2 * kdim), lambda i: (0, 0)),
                  pl.BlockSpec((2 * kdim, CMP_HIDDEN), lambda i: (0, 0)),
                  pl.BlockSpec((CMP_HIDDEN, HEAD_DIM), lambda i: (0, 0)),
                  pl.BlockSpec((1, HEAD_DIM), lambda i: (0, 0))],
        out_specs=pl.BlockSpec((ncb, HEAD_DIM), lambda i: (i, 0)),
        compiler_params=_cparams(("parallel",)),
        name="compress",
    )(x, posf, w1.astype(BF16), w2.astype(BF16), g.reshape(1, HEAD_DIM))


def _nsa_kernel(bnd_ref, zq_ref, gt_ref, gb_ref, qg_ref, ovt_ref, kc_ref, vct_ref, ks_ref, vst_ref, kw_ref,
                vwt_ref, o_ref, q_scr, sel_scr, m_scr, l_scr, acc_scr, oc_scr, *, bounded):
    grp = pl.program_id(1)
    qb = pl.program_id(2)
    QB = Q_BLOCK
    ncp = kc_ref.shape[0]
    ns = ovt_ref.shape[0]
    tpos = qb * QB + lax.broadcasted_iota(jnp.int32, (1, QB), 1)

    def lane_tile(x):
        return jnp.concatenate([x] * HG, axis=1)

    zt = zq_ref[...].T
    qg = qg_ref[...] * (HEAD_DIM ** -0.5 * LOG2E)
    heads = []
    for h in range(HG):
        xh = zt[h * HEAD_DIM:(h + 1) * HEAD_DIM, :]
        ms = jnp.mean(xh * xh, axis=0, keepdims=True)
        heads.append(xh * lax.rsqrt(ms + EPS) * qg)
    q_scr[...] = jnp.concatenate(heads, axis=1).astype(BF16)
    q = q_scr[...]

    gates = jax.nn.sigmoid(gt_ref[...].T + gb_ref[...])
    per_grp = HG * 3
    gsel = gates[0:per_grp, :]
    for g2 in range(1, N_KV_GROUPS):
        gsel = jnp.where(grp == g2, gates[g2 * per_grp:(g2 + 1) * per_grp, :], gsel)

    def gate_row(c):
        return jnp.concatenate([gsel[3 * h + c:3 * h + c + 1, :] for h in range(HG)], axis=1)

    cmp_end = lax.broadcasted_iota(jnp.int32, (ncp, 1), 0) * CMP_STRIDE + (CMP_BLOCK - 1)
    keep = -bnd_ref[0, 0] if bounded else 0.0
    cbias = jnp.where(cmp_end <= tpos, keep, NEG)
    s = _dot(kc_ref[...], q) + lane_tile(cbias)
    e = jnp.exp2(s) if bounded else jnp.exp2(s - jnp.max(s, axis=0, keepdims=True))
    inv = jnp.where(lane_tile(tpos) >= CMP_BLOCK - 1, 1.0 / jnp.sum(e, axis=0, keepdims=True), 0.0)
    p = e * inv
    ocmp = _dot(vct_ref[...], p.astype(BF16))
    psum = p[:, 0:QB]
    for h in range(1, HG):
        psum = psum + p[:, h * QB:(h + 1) * QB]

    n_win = WINDOW // KEY_TILE + 1
    win_sub = lax.broadcasted_iota(jnp.int32, (n_win * KEY_TILE, 1), 0)
    tiles = [jnp.maximum(qb - (n_win - 1) + u, 0) for u in range(n_win)]
    kwin = jnp.concatenate([kw_ref[j] for j in tiles], axis=0)
    vwin = jnp.concatenate([vwt_ref[j] for j in tiles], axis=1)
    kpos = (qb - (n_win - 1)) * KEY_TILE + win_sub
    wbias = jnp.where((kpos <= tpos) & (kpos > tpos - WINDOW) & (kpos >= 0), keep, NEG)
    sw = _dot(kwin, q) + lane_tile(wbias)
    pw = jnp.exp2(sw) if bounded else jnp.exp2(sw - jnp.max(sw, axis=0, keepdims=True))
    lw = jnp.sum(pw, axis=0, keepdims=True)
    oc_scr[...] = gate_row(0) * ocmp + (gate_row(2) / lw) * _dot(vwin, pw.astype(BF16))

    p_hi = psum.astype(BF16)
    p_lo = (psum - p_hi.astype(F32)).astype(BF16)
    ovt = ovt_ref[...]
    imp = _dot(ovt, p_hi) + _dot(ovt, p_lo)
    blk = lax.broadcasted_iota(jnp.int32, (ns, 1), 0)
    blk_f = blk.astype(F32)
    cur = jnp.right_shift(tpos, SLC_BLOCK.bit_length() - 1)
    forced = (blk == 0) | (blk == cur) | (blk == cur - 1)
    bvalid = blk <= cur
    score = jnp.where(forced, BIG, jnp.where(bvalid, imp, NEG))
    sel = jnp.zeros((ns, QB), F32)
    for _ in range(min(N_SELECT, ns)):
        mx = jnp.max(score, axis=0, keepdims=True)
        first = jnp.min(jnp.where(score == mx, blk_f, float(ns)), axis=0, keepdims=True)
        pick = blk_f == first
        sel = jnp.where(pick, 1.0, sel)
        score = jnp.where(pick, -jnp.inf, score)
    sel_scr[...] = jnp.where(bvalid, sel, 0.0)

    m_scr[...] = jnp.full(m_scr.shape, NEG, F32)
    l_scr[...] = jnp.zeros(l_scr.shape, F32)
    acc_scr[...] = jnp.zeros(acc_scr.shape, F32)
    blocks_per_tile = SEL_TILE // SLC_BLOCK
    sel_sub = lax.broadcasted_iota(jnp.int32, (SEL_TILE, 1), 0)

    def sel_body(j, carry):
        rows = [jnp.broadcast_to(sel_scr[pl.ds(j * blocks_per_tile + bi, 1), :], (SLC_BLOCK, QB))
                for bi in range(blocks_per_tile)]
        chosen = jnp.concatenate(rows, axis=0)
        kpos = j * SEL_TILE + sel_sub
        bias = jnp.where((chosen > 0.5) & (kpos <= tpos), keep, NEG)
        st = _dot(ks_ref[j], q_scr[...]) + lane_tile(bias)
        if bounded:
            pt = jnp.exp2(st)
            l_scr[...] += jnp.sum(pt, axis=0, keepdims=True)
            acc_scr[...] += _dot(vst_ref[j], pt.astype(BF16))
        else:
            m_old = m_scr[...]
            m_new = jnp.maximum(m_old, jnp.max(st, axis=0, keepdims=True))
            alpha = jnp.exp2(m_old - m_new)
            pt = jnp.exp2(st - m_new)
            l_scr[...] = alpha * l_scr[...] + jnp.sum(pt, axis=0, keepdims=True)
            acc_scr[...] = alpha * acc_scr[...] + _dot(vst_ref[j], pt.astype(BF16))
            m_scr[...] = m_new
        return carry

    lax.fori_loop(0, qb // (SEL_TILE // QB) + 1, sel_body, 0)
    ot = oc_scr[...] + (gate_row(1) / l_scr[...]) * acc_scr[...]
    stacked = jnp.concatenate([ot[:, h * QB:(h + 1) * QB] for h in range(HG)], axis=0)
    o_ref[...] = stacked.T


def nsa_attention(bound, z3, gate_b, q_g, overlap_t, kc, vct, ks, vst, kw, vwt, *, bounded):
    b, t, _ = z3.shape
    nq = t // Q_BLOCK
    nt = t // KEY_TILE
    qw = HG * HEAD_DIM
    ns, ncp = overlap_t.shape
    gate_blk = (NSA_COLS_PAD // LANES_V7X) - 1
    gb = jnp.zeros((LANES_V7X, 1), F32).at[:N_HEADS * 3, 0].set(gate_b)
    full5 = lambda i, g, q: (i, g, 0, 0, 0)
    lanes = HG * Q_BLOCK
    return pl.pallas_call(
        functools.partial(_nsa_kernel, bounded=bounded),
        out_shape=jax.ShapeDtypeStruct((b, t, N_HEADS * HEAD_DIM), F32),
        grid=(b, N_KV_GROUPS, nq),
        in_specs=[pl.BlockSpec(memory_space=pltpu.SMEM),
                  pl.BlockSpec((None, Q_BLOCK, qw), lambda i, g, q: (i, q, g)),
                  pl.BlockSpec((None, Q_BLOCK, LANES_V7X), lambda i, g, q: (i, q, gate_blk)),
                  pl.BlockSpec((LANES_V7X, 1), lambda i, g, q: (0, 0)),
                  pl.BlockSpec((HEAD_DIM, 1), lambda i, g, q: (0, 0)),
                  pl.BlockSpec((ns, ncp), lambda i, g, q: (0, 0)),
                  pl.BlockSpec((None, None, ncp, HEAD_DIM), lambda i, g, q: (i, g, 0, 0)),
                  pl.BlockSpec((None, None, HEAD_DIM, ncp), lambda i, g, q: (i, g, 0, 0)),
                  pl.BlockSpec((None, None, t // SEL_TILE, SEL_TILE, HEAD_DIM), full5),
                  pl.BlockSpec((None, None, t // SEL_TILE, HEAD_DIM, SEL_TILE), full5),
                  pl.BlockSpec((None, None, nt, KEY_TILE, HEAD_DIM), full5),
                  pl.BlockSpec((None, None, nt, HEAD_DIM, KEY_TILE), full5)],
        out_specs=pl.BlockSpec((None, Q_BLOCK, qw), lambda i, g, q: (i, q, g)),
        scratch_shapes=[pltpu.VMEM((HEAD_DIM, lanes), BF16),
                        pltpu.VMEM((ns, Q_BLOCK), F32),
                        pltpu.VMEM((1, lanes), F32),
                        pltpu.VMEM((1, lanes), F32),
                        pltpu.VMEM((HEAD_DIM, lanes), F32),
                        pltpu.VMEM((HEAD_DIM, lanes), F32)],
        compiler_params=_cparams(("parallel", "parallel", "arbitrary"), VMEM_LIMIT_V7X),
        name="nsa_attention_bounded" if bounded else "nsa_attention_online",
    )(bound.reshape(1, 1), z3, z3, gb, q_g.reshape(HEAD_DIM, 1), overlap_t, kc, vct, ks, vst, kw, vwt)


def _overlap_matrix_t(t):
    ncp = t // CMP_STRIDE
    ns = t // SLC_BLOCK
    c_start = np.arange(ncp) * CMP_STRIDE
    sj = np.arange(ns)
    ov = ((c_start[None, :] < (sj[:, None] + 1) * SLC_BLOCK)
          & (c_start[None, :] + CMP_BLOCK > sj[:, None] * SLC_BLOCK)
          & (c_start[None, :] + CMP_BLOCK <= t))
    return jnp.asarray(ov, dtype=BF16)


def ab_layer(h, b, t, norm_g, w_in, conv_w, conv_b, ret_g, ig_b, fg_b, m_g, w_out):
    n = b * t
    w_in_p = jnp.pad(w_in, ((0, 0), (0, AB_COLS_PAD - AB_COLS))).astype(BF16)
    z = norm_matmul(h, norm_g, w_in_p)
    z3 = z.reshape(b, t, AB_COLS_PAD)
    cos_t, sin_t = rope_tables(t)
    ret = retention(z3, cos_t, sin_t, ret_g)
    g0 = AB_COLS - 2 * M_HEADS
    gates_r = z3[:, :, g0:AB_COLS].reshape(b, t // M_CHUNK, M_CHUNK, 2 * M_HEADS).transpose(0, 1, 3, 2)
    ml = mlstm(z3, gates_r, conv_w, conv_b, ig_b, fg_b, m_g)
    w_out_b = w_out.astype(BF16)
    rw = R_HEADS * R_DV
    return proj_residual(h, [(ret.reshape(n, rw), w_out_b[:rw]), (ml.reshape(n, -1), w_out_b[rw:])])


def nsa_layer(h, b, t, norm_g, w_in, q_g, k_g, pos_k, pos_v, w1k, w2k, w1v, w2v, gate_b, w_out):
    n = b * t
    G = N_KV_GROUPS
    w_in_p = jnp.pad(w_in, ((0, 0), (0, NSA_COLS_PAD - NSA_COLS))).astype(BF16)
    z = norm_matmul(h, norm_g, w_in_p)
    kc_in, vc_in, ks, vs, kw, vw = kv_prep(z, k_g)
    nt = t // KEY_TILE
    ncb = t // CMP_STRIDE

    def cmp_rows(x):
        return x.reshape(b, t, G, HEAD_DIM).transpose(0, 2, 1, 3).reshape(b * G * ncb, CMP_STRIDE * HEAD_DIM)

    kc = compress(cmp_rows(kc_in), pos_k, w1k, w2k, k_g[0], True, ncb)
    vc = compress(cmp_rows(vc_in), pos_v, w1v, w2v, k_g[0], False, ncb)
    kc = kc.reshape(b, G, ncb, HEAD_DIM)
    vct = vc.reshape(b, G, ncb, HEAD_DIM).transpose(0, 1, 3, 2)

    def key_tiles(x, kt):
        return x.reshape(b, t // kt, kt, G, HEAD_DIM).transpose(0, 3, 1, 2, 4)

    def val_tiles_t(x, kt):
        return x.reshape(b, t // kt, kt, G, HEAD_DIM).transpose(0, 3, 1, 4, 2)

    bound = 1.02 * LOG2E * math.sqrt(HEAD_DIM) * jnp.max(jnp.abs(q_g)) * jnp.max(jnp.abs(k_g))
    args = (bound, z.reshape(b, t, NSA_COLS_PAD), gate_b, q_g, _overlap_matrix_t(t), kc, vct,
            key_tiles(ks, SEL_TILE), val_tiles_t(vs, SEL_TILE),
            key_tiles(kw, KEY_TILE), val_tiles_t(vw, KEY_TILE))
    o = lax.cond(bound <= MAX_SAFE_SCORE_BOUND,
                 functools.partial(nsa_attention, bounded=True),
                 functools.partial(nsa_attention, bounded=False), *args)
    return proj_residual(h, [(o.reshape(n, -1), w_out.astype(BF16))])


def kernel(x, p, ab_norm_g, ab_w_in, ab_conv_w, ab_conv_b, ab_ret_norm_g, ab_ig_b, ab_fg_b, ab_m_norm_g, ab_w_out, nsa_norm_g, nsa_w_in, nsa_q_norm_g, nsa_k_norm_g, nsa_cmp_pos_k, nsa_cmp_pos_v, nsa_cmp_w1k, nsa_cmp_w2k, nsa_cmp_w1v, nsa_cmp_w2v, nsa_gate_b, nsa_w_out, ffn_norm_g, ffn_w_up, ffn_conv_w, ffn_conv_b, ffn_w_down, ple_w, ple_norm_g, ple_gate_norm_g, ple_w_gate):
    b, t, d = x.shape
    n = b * t
    depth = p.shape[0]
    h = x.reshape(n, d)
    for i in range(depth):
        j = i // 2
        if i % 2 == 0:
            h = ab_layer(h, b, t, ab_norm_g[j], ab_w_in[j], ab_conv_w[j], ab_conv_b[j], ab_ret_norm_g[j],
                         ab_ig_b[j], ab_fg_b[j], ab_m_norm_g[j], ab_w_out[j])
        else:
            h = nsa_layer(h, b, t, nsa_norm_g[j], nsa_w_in[j], nsa_q_norm_g[j], nsa_k_norm_g[j],
                          nsa_cmp_pos_k[j], nsa_cmp_pos_v[j], nsa_cmp_w1k[j], nsa_cmp_w2k[j],
                          nsa_cmp_w1v[j], nsa_cmp_w2v[j], nsa_gate_b[j], nsa_w_out[j])
        h = conv_ffn_ple(h, ffn_norm_g[i], ffn_w_up[i].astype(BF16), ffn_conv_w[i], ffn_conv_b[i],
                         ffn_w_down[i].astype(BF16), p[i].reshape(n, -1), ple_w[i].astype(BF16),
                         ple_norm_g[i], ple_gate_norm_g[i], ple_w_gate[i].astype(BF16), t)
    return h.reshape(b, t, d)
```

```python
import functools
import math

import numpy as np
import jax
import jax.numpy as jnp
from jax import lax
from jax.experimental import pallas as pl
from jax.experimental.pallas import tpu as pltpu

F32 = jnp.float32
BF16 = jnp.bfloat16

LANES_V7X = 128
BF16_ROWS = 16
VMEM_LIMIT_V7X = 56 * 1024 * 1024

D_MODEL = 1024
PLE_DIM = 256
R_HEADS, R_DK, R_DV, R_CHUNK = 4, 64, 128, 128
ROPE_BASE = 10000.0
M_HEADS, M_DK, M_DV, M_CHUNK, M_CONV = 4, 64, 128, 64, 4
AB_SIZES = (R_HEADS * R_DK, R_HEADS * R_DK, R_HEADS * R_DV, R_HEADS * R_DV,
            M_HEADS * M_DK, M_HEADS * M_DK, M_HEADS * M_DV, M_HEADS * M_DV, M_HEADS, M_HEADS)
AB_COLS = sum(AB_SIZES)
AB_COLS_PAD = 3200
N_HEADS, N_KV_GROUPS, HEAD_DIM = 16, 2, 64
HG = N_HEADS // N_KV_GROUPS
CMP_BLOCK, CMP_STRIDE, CMP_HIDDEN = 32, 16, 256
SLC_BLOCK, N_SELECT, WINDOW, Q_BLOCK = 64, 16, 512, 128
NSA_COLS = N_HEADS * HEAD_DIM + 6 * N_KV_GROUPS * HEAD_DIM + N_HEADS * 3
NSA_COLS_PAD = 1920
D_FF = 2816
FFN_CONV = 3
NEG = -1e30
BIG = 1e30
EPS = 1e-6
KEY_TILE = 128
SEL_TILE = 512
LOG2E = math.log2(math.e)
MAX_SAFE_SCORE_BOUND = 56.0


def _cparams(sem, vmem=None):
    return pltpu.CompilerParams(dimension_semantics=sem, vmem_limit_bytes=vmem)


def _rms(x, g):
    ms = jnp.mean(x * x, axis=-1, keepdims=True)
    return x * lax.rsqrt(ms + EPS) * g


def _dot(a, b):
    return jnp.dot(a, b, preferred_element_type=F32)


def _dot_nt(a, b):
    return lax.dot_general(a, b, (((1,), (1,)), ((), ())), preferred_element_type=F32)


def _dot_f32(a, b):
    return jnp.dot(a, b, preferred_element_type=F32, precision=lax.Precision.HIGHEST)


def _norm_matmul_kernel(x_ref, g_ref, w_ref, o_ref):
    xn = _rms(x_ref[...], g_ref[...]).astype(BF16)
    o_ref[...] = _dot(xn, w_ref[...]).astype(o_ref.dtype)


def norm_matmul(x, g, w, tm=512, out_dtype=F32):
    n, d = x.shape
    nc = w.shape[1]
    return pl.pallas_call(
        _norm_matmul_kernel,
        out_shape=jax.ShapeDtypeStruct((n, nc), out_dtype),
        grid=(n // tm,),
        in_specs=[pl.BlockSpec((tm, d), lambda i: (i, 0)),
                  pl.BlockSpec((1, d), lambda i: (0, 0)),
                  pl.BlockSpec((d, nc), lambda i: (0, 0))],
        out_specs=pl.BlockSpec((tm, nc), lambda i: (i, 0)),
        compiler_params=_cparams(("parallel",), VMEM_LIMIT_V7X),
        name="norm_matmul",
    )(x, g.reshape(1, d), w)


def _rope_table_kernel(inv_ref, cos_ref, sin_ref):
    c = pl.program_id(0)
    rows, width = cos_ref.shape
    pos = (c * rows + lax.broadcasted_iota(jnp.int32, (rows, width), 0)).astype(F32)
    lane = lax.broadcasted_iota(jnp.int32, (rows, width), 1)
    ang = pos * inv_ref[...]
    cos_ref[...] = jnp.cos(ang)
    sn = jnp.sin(ang)
    sin_ref[...] = jnp.where(lane % R_DK < R_DK // 2, -sn, sn)


def rope_tables(t):
    half = R_DK // 2
    inv = ROPE_BASE ** (-jnp.arange(half, dtype=F32) / half)
    inv = jnp.tile(inv, 2 * R_HEADS).reshape(1, R_HEADS * R_DK)
    width = R_HEADS * R_DK
    shp = jax.ShapeDtypeStruct((t, width), F32)
    return pl.pallas_call(
        _rope_table_kernel,
        out_shape=(shp, shp),
        grid=(t // R_CHUNK,),
        in_specs=[pl.BlockSpec((1, width), lambda c: (0, 0))],
        out_specs=(pl.BlockSpec((R_CHUNK, width), lambda c: (c, 0)),
                   pl.BlockSpec((R_CHUNK, width), lambda c: (c, 0))),
        compiler_params=_cparams(("parallel",)),
        name="rope_tables",
    )(inv)


def _retention_kernel(cos_ref, sin_ref, q_ref, k_ref, v_ref, g_ref, gain_ref, o_ref, r_ref):
    c = pl.program_id(0)
    nb = q_ref.shape[0]
    L = R_CHUNK

    @pl.when(c == 0)
    def _():
        r_ref[...] = jnp.zeros_like(r_ref)

    cos = cos_ref[...]
    sin = sin_ref[...]
    lane = lax.broadcasted_iota(jnp.int32, cos.shape, 1)
    first_half = lane % R_DK < R_DK // 2
    width = R_HEADS * R_DK

    def rot(x):
        swapped = jnp.where(first_half, pltpu.roll(x, width - R_DK // 2, 1), pltpu.roll(x, R_DK // 2, 1))
        return x * cos + swapped * sin

    gain = gain_ref[...]
    ri = lax.broadcasted_iota(jnp.int32, (L, L), 0)
    ci = lax.broadcasted_iota(jnp.int32, (L, L), 1)
    diff = (ri - ci).astype(F32)
    causal = ri >= ci
    idx = lax.broadcasted_iota(jnp.int32, (L, 1), 0).astype(F32)
    decays = []
    for h in range(R_HEADS):
        log_g = math.log1p(-2.0 ** (-5.0 - h))
        decays.append(dict(
            dmask=jnp.where(causal, jnp.exp(jnp.where(causal, diff, 0.0) * log_g), 0.0),
            q_dec=jnp.exp((idx + 1.0) * log_g), k_dec=jnp.exp((L - 1.0 - idx) * log_g),
            c_dec=math.exp(L * log_g)))

    chains = []
    for bi in range(nb):
        q = rot(q_ref[bi])
        k = rot(k_ref[bi]) * (R_DK ** -0.5)
        for h in range(R_HEADS):
            chains.append(dict(bi=bi, h=h, u=bi * R_HEADS + h,
                               qb=q[:, h * R_DK:(h + 1) * R_DK].astype(BF16),
                               kh=k[:, h * R_DK:(h + 1) * R_DK]))

    for ch in chains:
        ch['rstate'] = r_ref[ch['u']]
        ch['s'] = _dot_nt(ch['qb'], ch['kh'].astype(BF16)) * decays[ch['h']]['dmask']
        ch['qr'] = _dot(ch['qb'], ch['rstate'].astype(BF16))

    for ch in chains:
        d = decays[ch['h']]
        vh = v_ref[ch['bi'], :, ch['h'] * R_DV:(ch['h'] + 1) * R_DV].astype(BF16)
        ch['o'] = _dot(ch['s'].astype(BF16), vh) + ch['qr'] * d['q_dec']
        kd = (ch['kh'] * d['k_dec']).T.astype(BF16)
        r_ref[ch['u']] = d['c_dec'] * ch['rstate'] + _dot(kd, vh)

    for ch in chains:
        sl = slice(ch['h'] * R_DV, (ch['h'] + 1) * R_DV)
        gh = g_ref[ch['bi'], :, sl]
        o_ref[ch['bi'], :, sl] = (_rms(ch['o'], gain[:, sl]) * (gh * jax.nn.sigmoid(gh))).astype(o_ref.dtype)


def retention(z3, cos_t, sin_t, gain):
    b, t, _ = z3.shape
    nc = t // R_CHUNK
    qk_w = R_HEADS * R_DK
    v_w = R_HEADS * R_DV
    return pl.pallas_call(
        _retention_kernel,
        out_shape=jax.ShapeDtypeStruct((b, t, v_w), BF16),
        grid=(nc,),
        in_specs=[pl.BlockSpec((R_CHUNK, qk_w), lambda c: (c, 0)),
                  pl.BlockSpec((R_CHUNK, qk_w), lambda c: (c, 0)),
                  pl.BlockSpec((b, R_CHUNK, qk_w), lambda c: (0, c, 0)),
                  pl.BlockSpec((b, R_CHUNK, qk_w), lambda c: (0, c, 1)),
                  pl.BlockSpec((b, R_CHUNK, v_w), lambda c: (0, c, 1)),
                  pl.BlockSpec((b, R_CHUNK, v_w), lambda c: (0, c, 2)),
                  pl.BlockSpec((1, v_w), lambda c: (0, 0))],
        out_specs=pl.BlockSpec((b, R_CHUNK, v_w), lambda c: (0, c, 0)),
        scratch_shapes=[pltpu.VMEM((b * R_HEADS, R_DK, R_DV), F32)],
        compiler_params=_cparams(("arbitrary",)),
        name="retention",
    )(cos_t, sin_t, z3, z3, z3, z3, gain.reshape(1, v_w))


def _mlstm_kernel(q_ref, k_ref, v_ref, og_ref, gc_ref, gr_ref, cw_ref, cb_ref, bc_ref, br_ref, gain_ref,
                  o_ref, xbuf, c_ref, n_ref, m_ref):
    c = pl.program_id(0)
    nb = q_ref.shape[0]
    L = M_CHUNK
    H = M_HEADS
    qk_w = H * M_DK
    halo = 8

    @pl.when(c == 0)
    def _():
        xbuf[:, 0:halo, :] = jnp.zeros((nb, halo, 2 * qk_w), F32)
        c_ref[...] = jnp.zeros_like(c_ref)
        n_ref[...] = jnp.zeros_like(n_ref)
        m_ref[...] = jnp.zeros_like(m_ref)

    gain = gain_ref[...]
    ri = lax.broadcasted_iota(jnp.int32, (L, L), 0)
    ci = lax.broadcasted_iota(jnp.int32, (L, L), 1)
    causal = ri >= ci
    tril = causal.astype(F32)
    triu = (ri <= ci).astype(F32)

    chains = []
    for bi in range(nb):
        xbuf[bi, halo:halo + L, 0:qk_w] = q_ref[bi]
        xbuf[bi, halo:halo + L, qk_w:2 * qk_w] = k_ref[bi]
        conv = cb_ref[...]
        for j in range(M_CONV):
            conv = conv + xbuf[bi, pl.ds(halo - (M_CONV - 1) + j, L), :] * cw_ref[j:j + 1, :]
        tail = xbuf[bi, L:L + halo, :]
        xbuf[bi, 0:halo, :] = tail
        act = conv * jax.nn.sigmoid(conv)
        q = act[:, 0:qk_w] * (M_DK ** -0.5)
        k = act[:, qk_w:2 * qk_w]
        gc = gc_ref[bi][:, 0:2 * H] + bc_ref[...]
        gr = gr_ref[bi] + br_ref[...]
        ig_c = gc[:, 0:H]
        ig_r = gr[0:H, :]
        b_c = _dot_f32(tril, jax.nn.log_sigmoid(gc[:, H:2 * H]))
        b_r = _dot_f32(jax.nn.log_sigmoid(gr[H:2 * H, :]), triu)
        for h in range(H):
            chains.append(dict(
                bi=bi, h=h, u=bi * H + h,
                qh=q[:, h * M_DK:(h + 1) * M_DK], kh=k[:, h * M_DK:(h + 1) * M_DK],
                bh=b_c[:, h:h + 1], brow=b_r[h:h + 1, :], irow=ig_r[h:h + 1, :], icol=ig_c[:, h:h + 1]))

    def stack(parts):
        return jnp.concatenate(parts, axis=0)

    def rows(x):
        return jnp.broadcast_to(x, (L, x.shape[1]))

    m_prev_u = [m_ref[ch['u']][:, 0:1] for ch in chains]
    b_last_u = [ch['bh'][L - 1:L, :] for ch in chains]
    bh = stack([ch['bh'] for ch in chains])
    icol = stack([ch['icol'] for ch in chains])
    brow = stack([rows(ch['brow']) for ch in chains])
    irow = stack([rows(ch['irow']) for ch in chains])
    m_prev = stack([rows(m) for m in m_prev_u])
    b_last = stack([rows(x) for x in b_last_u])
    causal_all = stack([causal] * len(chains))
    qs = stack([ch['qh'] for ch in chains])
    ks = stack([ch['kh'] for ch in chains])
    qb = qs.astype(BF16)
    kb = ks.astype(BF16)
    cstates = [c_ref[ch['u']] for ch in chains]
    nstates = [n_ref[ch['u']] for ch in chains]

    def chain_rows(x, i):
        return x[i * L:(i + 1) * L]

    s_raw = stack([_dot_nt(chain_rows(qb, i), chain_rows(kb, i)) for i in range(len(chains))])
    qc = stack([_dot(chain_rows(qb, i), cstates[i].astype(BF16)) for i in range(len(chains))])

    dlog = jnp.where(causal_all, bh - brow + irow, NEG)
    inter = bh + m_prev
    m_t = jnp.maximum(inter, jnp.max(dlog, axis=-1, keepdims=True))
    s = s_raw * jnp.exp(dlog - m_t)
    w_inter = jnp.exp(inter - m_t)
    wlog = b_last - bh + icol
    m_new_u = [jnp.maximum(b_last_u[i] + m_prev_u[i], jnp.max(chain_rows(wlog, i), axis=0, keepdims=True))
               for i in range(len(chains))]
    m_new = stack([rows(m) for m in m_new_u])
    wk = ks * jnp.exp(wlog - m_new)

    sb = s.astype(BF16)
    vhs = [v_ref[ch['bi'], :, ch['h'] * M_DV:(ch['h'] + 1) * M_DV].astype(BF16) for ch in chains]
    sv = stack([_dot(chain_rows(sb, i), vhs[i]) for i in range(len(chains))])
    kv = [_dot(chain_rows(wk, i).T.astype(BF16), vhs[i]) for i in range(len(chains))]

    qn = jnp.sum(qs * stack([rows(n) for n in nstates]), axis=-1, keepdims=True)
    den = jnp.sum(s, axis=-1, keepdims=True) + w_inter * qn
    hh = (sv + w_inter * qc) / jnp.maximum(jnp.abs(den), jnp.exp(-m_t))
    og = stack([og_ref[ch['bi'], :, ch['h'] * M_DV:(ch['h'] + 1) * M_DV] for ch in chains])
    gains = stack([rows(gain[:, ch['h'] * M_DV:(ch['h'] + 1) * M_DV]) for ch in chains])
    out = jax.nn.sigmoid(og) * _rms(hh, gains)

    for i, ch in enumerate(chains):
        u = ch['u']
        decay = jnp.exp(b_last_u[i] + m_prev_u[i] - m_new_u[i])
        c_ref[u] = decay * cstates[i] + kv[i]
        n_ref[u] = decay * nstates[i] + jnp.sum(chain_rows(wk, i), axis=0, keepdims=True)
        m_ref[u] = jnp.broadcast_to(m_new_u[i], (1, LANES_V7X))
        o_ref[ch['bi'], :, ch['h'] * M_DV:(ch['h'] + 1) * M_DV] = chain_rows(out, i).astype(o_ref.dtype)


def mlstm(z3, gates_r, conv_w, conv_b, ig_b, fg_b, gain):
    b, t, _ = z3.shape
    nc = t // M_CHUNK
    H = M_HEADS
    qk_w = H * M_DK
    v_w = H * M_DV
    bias = jnp.concatenate([ig_b, fg_b])
    gate_blk = AB_COLS_PAD // LANES_V7X - 1
    return pl.pallas_call(
        _mlstm_kernel,
        out_shape=jax.ShapeDtypeStruct((b, t, v_w), BF16),
        grid=(nc,),
        in_specs=[pl.BlockSpec((b, M_CHUNK, qk_w), lambda c: (0, c, 6)),
                  pl.BlockSpec((b, M_CHUNK, qk_w), lambda c: (0, c, 7)),
                  pl.BlockSpec((b, M_CHUNK, v_w), lambda c: (0, c, 4)),
                  pl.BlockSpec((b, M_CHUNK, v_w), lambda c: (0, c, 5)),
                  pl.BlockSpec((b, M_CHUNK, LANES_V7X), lambda c: (0, c, gate_blk)),
                  pl.BlockSpec((b, None, 2 * H, M_CHUNK), lambda c: (0, c, 0, 0)),
                  pl.BlockSpec((M_CONV, 2 * qk_w), lambda c: (0, 0)),
                  pl.BlockSpec((1, 2 * qk_w), lambda c: (0, 0)),
                  pl.BlockSpec((1, 2 * H), lambda c: (0, 0)),
                  pl.BlockSpec((2 * H, 1), lambda c: (0, 0)),
                  pl.BlockSpec((1, v_w), lambda c: (0, 0))],
        out_specs=pl.BlockSpec((b, M_CHUNK, v_w), lambda c: (0, c, 0)),
        scratch_shapes=[pltpu.VMEM((b, 8 + M_CHUNK, 2 * qk_w), F32),
                        pltpu.VMEM((b * H, M_DK, M_DV), F32),
                        pltpu.VMEM((b * H, 1, M_DK), F32),
                        pltpu.VMEM((b * H, 1, LANES_V7X), F32)],
        compiler_params=_cparams(("arbitrary",)),
        name="mlstm",
    )(z3, z3, z3, z3, z3, gates_r, conv_w, conv_b.reshape(1, -1), bias.reshape(1, -1), bias.reshape(-1, 1),
      gain.reshape(1, v_w))


def _mix_ffn_ple_kernel(*refs, n_mix, tm, seq, tf):
    hp_ref, h_ref = refs[0], refs[1]
    mix_refs = refs[2:2 + 2 * n_mix]
    (wo_ref, g_ref, wa_ref, wb_ref, cw_ref, cb_ref, wd_ref, p_ref, wp_ref, ng_ref, gg_ref, wg_ref,
     o_ref, xn_ref, a_ref) = refs[2 + 2 * n_mix:]
    i = pl.program_id(0)
    halo = 8
    g = g_ref[...]
    m_prev = jnp.concatenate([mix_refs[2 * k][...] for k in range(n_mix)], axis=1)
    m_tile = jnp.concatenate([mix_refs[2 * k + 1][...] for k in range(n_mix)], axis=1)
    x = h_ref[...] + _dot(m_tile, wo_ref[...])
    xp = hp_ref[...] + _dot(m_prev, wo_ref[...])[BF16_ROWS - halo:, :]
    xn_ref[halo:halo + tm, :] = _rms(x, g).astype(BF16)
    keep = ((i * tm) % seq != 0).astype(F32)
    xn_ref[0:halo, :] = (_rms(xp, g) * keep).astype(BF16)
    n_chunks = D_FF // tf

    def up_proj(c):
        cs = slice(c * tf, (c + 1) * tf)
        a_ref[c % 2] = _dot(xn_ref[...], wa_ref[:, cs])
        return _dot(xn_ref[halo:halo + tm, :], wb_ref[:, cs])

    h2 = x
    bgate = up_proj(0)
    for c in range(n_chunks):
        cs = slice(c * tf, (c + 1) * tf)
        bgate_next = up_proj(c + 1) if c + 1 < n_chunks else None
        conv = cb_ref[:, cs]
        for t in range(FFN_CONV):
            conv = conv + a_ref[c % 2, pl.ds(halo - (FFN_CONV - 1) + t, tm), :] * cw_ref[t:t + 1, cs]
        act = (jax.nn.gelu(conv) * bgate).astype(BF16)
        h2 = h2 + _dot(act, wd_ref[cs, :])
        bgate = bgate_next
    e = _rms(_dot(p_ref[...].astype(BF16), wp_ref[...]), ng_ref[...])
    gate = jax.nn.sigmoid(_dot(_rms(h2, gg_ref[...]).astype(BF16), wg_ref[...]))
    o_ref[...] = h2 + gate * e


def mix_ffn_ple(h, mix, g, w_up, conv_w, conv_b, w_down, p, wp, norm_g, gate_norm_g, wg, seq, tm=512, tf=256):
    mix_outs, w_out = mix
    n, d = h.shape
    pd = p.shape[1]
    hb = tm // 8
    const = lambda i: (0, 0)
    prev8 = lambda i: (jnp.maximum(i * hb - 1, 0), 0)
    resident = dict(pipeline_mode=pl.Buffered(1))
    prev16 = lambda i: (jnp.maximum(i * (tm // BF16_ROWS) - 1, 0), 0)
    mix_specs, mix_args = [], []
    for m in mix_outs:
        mix_specs += [pl.BlockSpec((BF16_ROWS, m.shape[1]), prev16),
                      pl.BlockSpec((tm, m.shape[1]), lambda i: (i, 0))]
        mix_args += [m, m]
    mix_specs.append(pl.BlockSpec(w_out.shape, const, **resident))
    mix_args.append(w_out)
    return pl.pallas_call(
        functools.partial(_mix_ffn_ple_kernel, n_mix=len(mix_outs), tm=tm, seq=seq, tf=tf),
        out_shape=jax.ShapeDtypeStruct((n, d), F32),
        grid=(n // tm,),
        in_specs=[pl.BlockSpec((8, d), prev8),
                  pl.BlockSpec((tm, d), lambda i: (i, 0))] + mix_specs + [
                  pl.BlockSpec((1, d), const),
                  pl.BlockSpec((d, D_FF), lambda i: (0, 0), **resident),
                  pl.BlockSpec((d, D_FF), lambda i: (0, 1), **resident),
                  pl.BlockSpec((FFN_CONV, D_FF), const),
                  pl.BlockSpec((1, D_FF), const),
                  pl.BlockSpec((D_FF, d), const, **resident),
                  pl.BlockSpec((tm, pd), lambda i: (i, 0)),
                  pl.BlockSpec((pd, d), const, **resident),
                  pl.BlockSpec((1, d), const),
                  pl.BlockSpec((1, d), const),
                  pl.BlockSpec((d, d), const, **resident)],
        out_specs=pl.BlockSpec((tm, d), lambda i: (i, 0)),
        scratch_shapes=[pltpu.VMEM((8 + tm, d), BF16),
                        pltpu.VMEM((2, 8 + tm, tf), F32)],
        compiler_params=_cparams(("parallel",), VMEM_LIMIT_V7X),
        name="mix_ffn_ple",
    )(h, h, *mix_args, g.reshape(1, d), w_up, w_up, conv_w, conv_b.reshape(1, -1), w_down,
      p, wp, norm_g.reshape(1, d), gate_norm_g.reshape(1, d), wg)


def _group_rms(x, g):
    lane = lax.broadcasted_iota(jnp.int32, x.shape, 1)
    x2 = x * x
    ms = jnp.zeros_like(x)
    for grp in range(N_KV_GROUPS):
        in_grp = (lane >= grp * HEAD_DIM) & (lane < (grp + 1) * HEAD_DIM)
        tot = jnp.sum(jnp.where(in_grp, x2, 0.0), axis=-1, keepdims=True)
        ms = jnp.where(in_grp, tot * (1.0 / HEAD_DIM), ms)
    return x * lax.rsqrt(ms + EPS) * g


def _kv_prep_kernel(c_ref, s_ref, w_ref, gs_ref, gw_ref, kc_ref, vc_ref, ks_ref, vs_ref, kw_ref, vw_ref):
    gw = N_KV_GROUPS * HEAD_DIM
    cc = c_ref[...]
    ss = s_ref[...]
    ww = w_ref[...]
    kc_ref[...] = cc[:, 0:gw].astype(BF16)
    vc_ref[...] = cc[:, gw:2 * gw].astype(BF16)
    ks_ref[...] = _group_rms(ss[:, 0:gw], gs_ref[...]).astype(BF16)
    vs_ref[...] = ss[:, gw:2 * gw].astype(BF16)
    kw_ref[...] = _group_rms(ww[:, 0:gw], gw_ref[...]).astype(BF16)
    vw_ref[...] = ww[:, gw:2 * gw].astype(BF16)


def kv_prep(z, k_g, tm=512):
    n = z.shape[0]
    gw = N_KV_GROUPS * HEAD_DIM
    base = N_HEADS * HEAD_DIM // (2 * gw)
    shp = jax.ShapeDtypeStruct((n, gw), BF16)
    ospec = pl.BlockSpec((tm, gw), lambda i: (i, 0))
    return pl.pallas_call(
        _kv_prep_kernel,
        out_shape=(shp,) * 6,
        grid=(n // tm,),
        in_specs=[pl.BlockSpec((tm, 2 * gw), lambda i: (i, base)),
                  pl.BlockSpec((tm, 2 * gw), lambda i: (i, base + 1)),
                  pl.BlockSpec((tm, 2 * gw), lambda i: (i, base + 2)),
                  pl.BlockSpec((1, gw), lambda i: (0, 0)),
                  pl.BlockSpec((1, gw), lambda i: (0, 0))],
        out_specs=(ospec,) * 6,
        compiler_params=_cparams(("parallel",)),
        name="kv_prep",
    )(z, z, z, jnp.tile(k_g[1], N_KV_GROUPS).reshape(1, gw), jnp.tile(k_g[2], N_KV_GROUPS).reshape(1, gw))


def _compress_kernel(x_ref, pos_ref, w1_ref, w2_ref, g_ref, o_ref, *, normalize):
    half = w1_ref.shape[0] // 2
    x = x_ref[...]
    u = _dot(x, w1_ref[0:half, :])
    v = _dot(x, w1_ref[half:2 * half, :])
    rows = u.shape[0]
    posc = _dot(pos_ref[...], w1_ref[...])[0:1, :]
    hid = u + pltpu.roll(v, rows - 1, 0) + posc
    out = _dot(jax.nn.gelu(hid).astype(BF16), w2_ref[...])
    if normalize:
        out = _rms(out, g_ref[...])
    o_ref[...] = out.astype(o_ref.dtype)


def compress(x, pos, w1, w2, g, normalize, ncb):
    n, kdim = x.shape
    posf = jnp.broadcast_to(pos.reshape(1, -1), (8, 2 * kdim)).astype(BF16)
    return pl.pallas_call(
        functools.partial(_compress_kernel, normalize=normalize),
        out_shape=jax.ShapeDtypeStruct((n, HEAD_DIM), BF16),
        grid=(n // ncb,),
        in_specs=[pl.BlockSpec((ncb, kdim), lambda i: (i, 0)),
                  pl.BlockSpec((8, 2 * kdim), lambda i: (0, 0)),
                  pl.BlockSpec((2 * kdim, CMP_HIDDEN), lambda i: (0, 0)),
                  pl.BlockSpec((CMP_HIDDEN, HEAD_DIM), lambda i: (0, 0)),
                  pl.BlockSpec((1, HEAD_DIM), lambda i: (0, 0))],
        out_specs=pl.BlockSpec((ncb, HEAD_DIM), lambda i: (i, 0)),
        compiler_params=_cparams(("parallel",)),
        name="compress",
    )(x, posf, w1.astype(BF16), w2.astype(BF16), g.reshape(1, HEAD_DIM))


def _nsa_kernel(bnd_ref, *refs):
    safe = bnd_ref[0, 0] <= MAX_SAFE_SCORE_BOUND

    @pl.when(safe)
    def _():
        _nsa_body(bnd_ref, *refs, bounded=True)

    @pl.when(jnp.logical_not(safe))
    def _():
        _nsa_body(bnd_ref, *refs, bounded=False)


def _nsa_body(bnd_ref, zq_ref, gt_ref, gb_ref, qg_ref, ovt_ref, kc_ref, vct_ref, ks_ref, vst_ref, kw_ref,
              vwt_ref, o_ref, q_scr, sel_scr, m_scr, l_scr, acc_scr, oc_scr, *, bounded):
    grp = pl.program_id(1)
    qb = pl.program_id(2)
    QB = Q_BLOCK
    ncp = kc_ref.shape[0]
    ns = ovt_ref.shape[0]
    tpos = qb * QB + lax.broadcasted_iota(jnp.int32, (1, QB), 1)

    def lane_tile(x):
        return jnp.concatenate([x] * HG, axis=1)

    zt = zq_ref[...].T
    qg = qg_ref[...] * (HEAD_DIM ** -0.5 * LOG2E)
    heads = []
    for h in range(HG):
        xh = zt[h * HEAD_DIM:(h + 1) * HEAD_DIM, :]
        ms = jnp.mean(xh * xh, axis=0, keepdims=True)
        heads.append(xh * lax.rsqrt(ms + EPS) * qg)
    q_scr[...] = jnp.concatenate(heads, axis=1).astype(BF16)
    q = q_scr[...]

    gates = jax.nn.sigmoid(gt_ref[...].T + gb_ref[...])
    per_grp = HG * 3
    gsel = gates[0:per_grp, :]
    for g2 in range(1, N_KV_GROUPS):
        gsel = jnp.where(grp == g2, gates[g2 * per_grp:(g2 + 1) * per_grp, :], gsel)

    def gate_row(c):
        return jnp.concatenate([gsel[3 * h + c:3 * h + c + 1, :] for h in range(HG)], axis=1)

    cmp_end = lax.broadcasted_iota(jnp.int32, (ncp, 1), 0) * CMP_STRIDE + (CMP_BLOCK - 1)
    keep = -bnd_ref[0, 0] if bounded else 0.0
    cbias = jnp.where(cmp_end <= tpos, keep, NEG)
    s = _dot(kc_ref[...], q) + lane_tile(cbias)
    e = jnp.exp2(s) if bounded else jnp.exp2(s - jnp.max(s, axis=0, keepdims=True))
    inv = jnp.where(lane_tile(tpos) >= CMP_BLOCK - 1, 1.0 / jnp.sum(e, axis=0, keepdims=True), 0.0)
    p = e * inv
    ocmp = _dot(vct_ref[...], p.astype(BF16))
    psum = p[:, 0:QB]
    for h in range(1, HG):
        psum = psum + p[:, h * QB:(h + 1) * QB]

    n_win = WINDOW // KEY_TILE + 1
    win_sub = lax.broadcasted_iota(jnp.int32, (n_win * KEY_TILE, 1), 0)
    tiles = [jnp.maximum(qb - (n_win - 1) + u, 0) for u in range(n_win)]
    kwin = jnp.concatenate([kw_ref[j] for j in tiles], axis=0)
    vwin = jnp.concatenate([vwt_ref[j] for j in tiles], axis=1)
    kpos = (qb - (n_win - 1)) * KEY_TILE + win_sub
    wbias = jnp.where((kpos <= tpos) & (kpos > tpos - WINDOW) & (kpos >= 0), keep, NEG)
    sw = _dot(kwin, q) + lane_tile(wbias)
    pw = jnp.exp2(sw) if bounded else jnp.exp2(sw - jnp.max(sw, axis=0, keepdims=True))
    lw = jnp.sum(pw, axis=0, keepdims=True)
    oc_scr[...] = gate_row(0) * ocmp + (gate_row(2) / lw) * _dot(vwin, pw.astype(BF16))

    p_hi = psum.astype(BF16)
    p_lo = (psum - p_hi.astype(F32)).astype(BF16)
    ovt = ovt_ref[...]
    imp = _dot(ovt, p_hi) + _dot(ovt, p_lo)
    blk = lax.broadcasted_iota(jnp.int32, (ns, 1), 0)
    blk_f = blk.astype(F32)
    cur = jnp.right_shift(tpos, SLC_BLOCK.bit_length() - 1)
    forced = (blk == 0) | (blk == cur) | (blk == cur - 1)
    bvalid = blk <= cur
    score = jnp.where(forced, BIG, jnp.where(bvalid, imp, NEG))
    sel = jnp.zeros((ns, QB), F32)
    for _ in range(min(N_SELECT, ns)):
        mx = jnp.max(score, axis=0, keepdims=True)
        first = jnp.min(jnp.where(score == mx, blk_f, float(ns)), axis=0, keepdims=True)
        pick = blk_f == first
        sel = jnp.where(pick, 1.0, sel)
        score = jnp.where(pick, -jnp.inf, score)
    sel_scr[...] = jnp.where(bvalid, sel, 0.0)

    m_scr[...] = jnp.full(m_scr.shape, NEG, F32)
    l_scr[...] = jnp.zeros(l_scr.shape, F32)
    acc_scr[...] = jnp.zeros(acc_scr.shape, F32)
    blocks_per_tile = SEL_TILE // SLC_BLOCK
    sel_sub = lax.broadcasted_iota(jnp.int32, (SEL_TILE, 1), 0)

    def sel_body(j, carry):
        rows = [jnp.broadcast_to(sel_scr[pl.ds(j * blocks_per_tile + bi, 1), :], (SLC_BLOCK, QB))
                for bi in range(blocks_per_tile)]
        chosen = jnp.concatenate(rows, axis=0)
        kpos = j * SEL_TILE + sel_sub
        bias = jnp.where((chosen > 0.5) & (kpos <= tpos), keep, NEG)
        st = _dot(ks_ref[j], q_scr[...]) + lane_tile(bias)
        if bounded:
            pt = jnp.exp2(st)
            l_scr[...] += jnp.sum(pt, axis=0, keepdims=True)
            acc_scr[...] += _dot(vst_ref[j], pt.astype(BF16))
        else:
            m_old = m_scr[...]
            m_new = jnp.maximum(m_old, jnp.max(st, axis=0, keepdims=True))
            alpha = jnp.exp2(m_old - m_new)
            pt = jnp.exp2(st - m_new)
            l_scr[...] = alpha * l_scr[...] + jnp.sum(pt, axis=0, keepdims=True)
            acc_scr[...] = alpha * acc_scr[...] + _dot(vst_ref[j], pt.astype(BF16))
            m_scr[...] = m_new
        return carry

    lax.fori_loop(0, qb // (SEL_TILE // QB) + 1, sel_body, 0)
    ot = oc_scr[...] + (gate_row(1) / l_scr[...]) * acc_scr[...]
    stacked = jnp.concatenate([ot[:, h * QB:(h + 1) * QB] for h in range(HG)], axis=0)
    o_ref[...] = stacked.T.astype(o_ref.dtype)


def nsa_attention(bound, z3, gate_b, q_g, overlap_t, kc, vct, ks, vst, kw, vwt):
    b, t, _ = z3.shape
    nq = t // Q_BLOCK
    nt = t // KEY_TILE
    qw = HG * HEAD_DIM
    ns, ncp = overlap_t.shape
    gate_blk = (NSA_COLS_PAD // LANES_V7X) - 1
    gb = jnp.zeros((LANES_V7X, 1), F32).at[:N_HEADS * 3, 0].set(gate_b)
    full5 = lambda i, g, q: (i, g, 0, 0, 0)
    lanes = HG * Q_BLOCK
    return pl.pallas_call(
        _nsa_kernel,
        out_shape=jax.ShapeDtypeStruct((b, t, N_HEADS * HEAD_DIM), BF16),
        grid=(b, N_KV_GROUPS, nq),
        in_specs=[pl.BlockSpec(memory_space=pltpu.SMEM),
                  pl.BlockSpec((None, Q_BLOCK, qw), lambda i, g, q: (i, q, g)),
                  pl.BlockSpec((None, Q_BLOCK, LANES_V7X), lambda i, g, q: (i, q, gate_blk)),
                  pl.BlockSpec((LANES_V7X, 1), lambda i, g, q: (0, 0)),
                  pl.BlockSpec((HEAD_DIM, 1), lambda i, g, q: (0, 0)),
                  pl.BlockSpec((ns, ncp), lambda i, g, q: (0, 0)),
                  pl.BlockSpec((None, None, ncp, HEAD_DIM), lambda i, g, q: (i, g, 0, 0)),
                  pl.BlockSpec((None, None, HEAD_DIM, ncp), lambda i, g, q: (i, g, 0, 0)),
                  pl.BlockSpec((None, None, t // SEL_TILE, SEL_TILE, HEAD_DIM), full5),
                  pl.BlockSpec((None, None, t // SEL_TILE, HEAD_DIM, SEL_TILE), full5),
                  pl.BlockSpec((None, None, nt, KEY_TILE, HEAD_DIM), full5),
                  pl.BlockSpec((None, None, nt, HEAD_DIM, KEY_TILE), full5)],
        out_specs=pl.BlockSpec((None, Q_BLOCK, qw), lambda i, g, q: (i, q, g)),
        scratch_shapes=[pltpu.VMEM((HEAD_DIM, lanes), BF16),
                        pltpu.VMEM((ns, Q_BLOCK), F32),
                        pltpu.VMEM((1, lanes), F32),
                        pltpu.VMEM((1, lanes), F32),
                        pltpu.VMEM((HEAD_DIM, lanes), F32),
                        pltpu.VMEM((HEAD_DIM, lanes), F32)],
        compiler_params=_cparams(("parallel", "parallel", "arbitrary"), VMEM_LIMIT_V7X),
        name="nsa_attention",
    )(bound.reshape(1, 1), z3, z3, gb, q_g.reshape(HEAD_DIM, 1), overlap_t, kc, vct, ks, vst, kw, vwt)


def _overlap_matrix_t(t):
    ncp = t // CMP_STRIDE
    ns = t // SLC_BLOCK
    c_start = np.arange(ncp) * CMP_STRIDE
    sj = np.arange(ns)
    ov = ((c_start[None, :] < (sj[:, None] + 1) * SLC_BLOCK)
          & (c_start[None, :] + CMP_BLOCK > sj[:, None] * SLC_BLOCK)
          & (c_start[None, :] + CMP_BLOCK <= t))
    return jnp.asarray(ov, dtype=BF16)


def ab_layer(h, b, t, norm_g, w_in, conv_w, conv_b, ret_g, ig_b, fg_b, m_g, w_out):
    n = b * t
    w_in_p = jnp.pad(w_in, ((0, 0), (0, AB_COLS_PAD - AB_COLS))).astype(BF16)
    z = norm_matmul(h, norm_g, w_in_p)
    z3 = z.reshape(b, t, AB_COLS_PAD)
    cos_t, sin_t = rope_tables(t)
    ret = retention(z3, cos_t, sin_t, ret_g)
    g0 = AB_COLS - 2 * M_HEADS
    gates_r = z3[:, :, g0:AB_COLS].reshape(b, t // M_CHUNK, M_CHUNK, 2 * M_HEADS).transpose(0, 1, 3, 2)
    ml = mlstm(z3, gates_r, conv_w, conv_b, ig_b, fg_b, m_g)
    w_out_b = w_out.astype(BF16)
    rw = R_HEADS * R_DV
    return [ret.reshape(n, rw), ml.reshape(n, -1)], w_out_b


def nsa_layer(h, b, t, norm_g, w_in, q_g, k_g, pos_k, pos_v, w1k, w2k, w1v, w2v, gate_b, w_out):
    n = b * t
    G = N_KV_GROUPS
    w_in_p = jnp.pad(w_in, ((0, 0), (0, NSA_COLS_PAD - NSA_COLS))).astype(BF16)
    z = norm_matmul(h, norm_g, w_in_p)
    kc_in, vc_in, ks, vs, kw, vw = kv_prep(z, k_g)
    nt = t // KEY_TILE
    ncb = t // CMP_STRIDE

    def cmp_rows(x):
        return x.reshape(b, t, G, HEAD_DIM).transpose(0, 2, 1, 3).reshape(b * G * ncb, CMP_STRIDE * HEAD_DIM)

    kc = compress(cmp_rows(kc_in), pos_k, w1k, w2k, k_g[0], True, ncb)
    vc = compress(cmp_rows(vc_in), pos_v, w1v, w2v, k_g[0], False, ncb)
    kc = kc.reshape(b, G, ncb, HEAD_DIM)
    vct = vc.reshape(b, G, ncb, HEAD_DIM).transpose(0, 1, 3, 2)

    def key_tiles(x, kt):
        return x.reshape(b, t // kt, kt, G, HEAD_DIM).transpose(0, 3, 1, 2, 4)

    def val_tiles_t(x, kt):
        return x.reshape(b, t // kt, kt, G, HEAD_DIM).transpose(0, 3, 1, 4, 2)

    bound = 1.02 * LOG2E * math.sqrt(HEAD_DIM) * jnp.max(jnp.abs(q_g)) * jnp.max(jnp.abs(k_g))
    args = (bound, z.reshape(b, t, NSA_COLS_PAD), gate_b, q_g, _overlap_matrix_t(t), kc, vct,
            key_tiles(ks, SEL_TILE), val_tiles_t(vs, SEL_TILE),
            key_tiles(kw, KEY_TILE), val_tiles_t(vw, KEY_TILE))
    o = nsa_attention(*args)
    return [o.reshape(n, -1)], w_out.astype(BF16)


def kernel(x, p, ab_norm_g, ab_w_in, ab_conv_w, ab_conv_b, ab_ret_norm_g, ab_ig_b, ab_fg_b, ab_m_norm_g, ab_w_out, nsa_norm_g, nsa_w_in, nsa_q_norm_g, nsa_k_norm_g, nsa_cmp_pos_k, nsa_cmp_pos_v, nsa_cmp_w1k, nsa_cmp_w2k, nsa_cmp_w1v, nsa_cmp_w2v, nsa_gate_b, nsa_w_out, ffn_norm_g, ffn_w_up, ffn_conv_w, ffn_conv_b, ffn_w_down, ple_w, ple_norm_g, ple_gate_norm_g, ple_w_gate):
    b, t, d = x.shape
    n = b * t
    depth = p.shape[0]
    h = x.reshape(n, d)
    for i in range(depth):
        j = i // 2
        if i % 2 == 0:
            mix = ab_layer(h, b, t, ab_norm_g[j], ab_w_in[j], ab_conv_w[j], ab_conv_b[j], ab_ret_norm_g[j],
                           ab_ig_b[j], ab_fg_b[j], ab_m_norm_g[j], ab_w_out[j])
        else:
            mix = nsa_layer(h, b, t, nsa_norm_g[j], nsa_w_in[j], nsa_q_norm_g[j], nsa_k_norm_g[j],
                            nsa_cmp_pos_k[j], nsa_cmp_pos_v[j], nsa_cmp_w1k[j], nsa_cmp_w2k[j],
                            nsa_cmp_w1v[j], nsa_cmp_w2v[j], nsa_gate_b[j], nsa_w_out[j])
        h = mix_ffn_ple(h, mix, ffn_norm_g[i], ffn_w_up[i].astype(BF16), ffn_conv_w[i], ffn_conv_b[i],
                        ffn_w_down[i].astype(BF16), p[i].reshape(n, -1), ple_w[i].astype(BF16),
                        ple_norm_g[i], ple_gate_norm_g[i], ple_w_gate[i].astype(BF16), t)
    return h.reshape(b, t, d)
```

```python
import functools
import math

import numpy as np
import jax
import jax.numpy as jnp
from jax import lax
from jax.experimental import pallas as pl
from jax.experimental.pallas import tpu as pltpu

F32 = jnp.float32
BF16 = jnp.bfloat16

LANES_V7X = 128
BF16_ROWS = 16
VMEM_LIMIT_V7X = 56 * 1024 * 1024

D_MODEL = 1024
PLE_DIM = 256
R_HEADS, R_DK, R_DV, R_CHUNK = 4, 64, 128, 128
ROPE_BASE = 10000.0
M_HEADS, M_DK, M_DV, M_CHUNK, M_CONV = 4, 64, 128, 64, 4
AB_SIZES = (R_HEADS * R_DK, R_HEADS * R_DK, R_HEADS * R_DV, R_HEADS * R_DV,
            M_HEADS * M_DK, M_HEADS * M_DK, M_HEADS * M_DV, M_HEADS * M_DV, M_HEADS, M_HEADS)
AB_COLS = sum(AB_SIZES)
AB_COLS_PAD = 3200
N_HEADS, N_KV_GROUPS, HEAD_DIM = 16, 2, 64
HG = N_HEADS // N_KV_GROUPS
CMP_BLOCK, CMP_STRIDE, CMP_HIDDEN = 32, 16, 256
SLC_BLOCK, N_SELECT, WINDOW, Q_BLOCK = 64, 16, 512, 128
NSA_COLS = N_HEADS * HEAD_DIM + 6 * N_KV_GROUPS * HEAD_DIM + N_HEADS * 3
NSA_COLS_PAD = 1920
D_FF = 2816
FFN_CONV = 3
NEG = -1e30
BIG = 1e30
EPS = 1e-6
KEY_TILE = 128
SEL_TILE = 512
LOG2E = math.log2(math.e)
MAX_SAFE_SCORE_BOUND = 56.0


def _cparams(sem, vmem=None):
    return pltpu.CompilerParams(dimension_semantics=sem, vmem_limit_bytes=vmem)


def _rms(x, g):
    ms = jnp.mean(x * x, axis=-1, keepdims=True)
    return x * lax.rsqrt(ms + EPS) * g


def _dot(a, b):
    return jnp.dot(a, b, preferred_element_type=F32)


def _dot_nt(a, b):
    return lax.dot_general(a, b, (((1,), (1,)), ((), ())), preferred_element_type=F32)


def _dot_f32(a, b):
    return jnp.dot(a, b, preferred_element_type=F32, precision=lax.Precision.HIGHEST)


def _gelu_tanh(x):
    k1 = -2.0 * math.sqrt(2.0 / math.pi) * LOG2E
    return x / (1.0 + jnp.exp2(x * (k1 + (k1 * 0.044715) * (x * x))))


def _norm_matmul_kernel(x_ref, g_ref, w_ref, o_ref):
    xn = _rms(x_ref[...], g_ref[...]).astype(BF16)
    o_ref[...] = _dot(xn, w_ref[...]).astype(o_ref.dtype)


def norm_matmul(x, g, w, tm=512, out_dtype=F32):
    n, d = x.shape
    nc = w.shape[1]
    return pl.pallas_call(
        _norm_matmul_kernel,
        out_shape=jax.ShapeDtypeStruct((n, nc), out_dtype),
        grid=(n // tm,),
        in_specs=[pl.BlockSpec((tm, d), lambda i: (i, 0)),
                  pl.BlockSpec((1, d), lambda i: (0, 0)),
                  pl.BlockSpec((d, nc), lambda i: (0, 0))],
        out_specs=pl.BlockSpec((tm, nc), lambda i: (i, 0)),
        compiler_params=_cparams(("parallel",), VMEM_LIMIT_V7X),
        name="norm_matmul",
    )(x, g.reshape(1, d), w)


def _rope_table_kernel(inv_ref, cos_ref, sin_ref):
    c = pl.program_id(0)
    rows, width = cos_ref.shape
    pos = (c * rows + lax.broadcasted_iota(jnp.int32, (rows, width), 0)).astype(F32)
    lane = lax.broadcasted_iota(jnp.int32, (rows, width), 1)
    ang = pos * inv_ref[...]
    cos_ref[...] = jnp.cos(ang)
    sn = jnp.sin(ang)
    sin_ref[...] = jnp.where(lane % R_DK < R_DK // 2, -sn, sn)


def rope_tables(t):
    half = R_DK // 2
    inv = ROPE_BASE ** (-jnp.arange(half, dtype=F32) / half)
    inv = jnp.tile(inv, 2 * R_HEADS).reshape(1, R_HEADS * R_DK)
    width = R_HEADS * R_DK
    shp = jax.ShapeDtypeStruct((t, width), F32)
    return pl.pallas_call(
        _rope_table_kernel,
        out_shape=(shp, shp),
        grid=(t // R_CHUNK,),
        in_specs=[pl.BlockSpec((1, width), lambda c: (0, 0))],
        out_specs=(pl.BlockSpec((R_CHUNK, width), lambda c: (c, 0)),
                   pl.BlockSpec((R_CHUNK, width), lambda c: (c, 0))),
        compiler_params=_cparams(("parallel",)),
        name="rope_tables",
    )(inv)


def _retention_kernel(cos_ref, sin_ref, q_ref, k_ref, v_ref, g_ref, gain_ref, o_ref, r_ref):
    c = pl.program_id(0)
    nb = q_ref.shape[0]
    L = R_CHUNK

    @pl.when(c == 0)
    def _():
        r_ref[...] = jnp.zeros_like(r_ref)

    cos = cos_ref[...]
    sin = sin_ref[...]
    lane = lax.broadcasted_iota(jnp.int32, cos.shape, 1)
    first_half = lane % R_DK < R_DK // 2
    width = R_HEADS * R_DK

    def rot(x):
        swapped = jnp.where(first_half, pltpu.roll(x, width - R_DK // 2, 1), pltpu.roll(x, R_DK // 2, 1))
        return x * cos + swapped * sin

    gain = gain_ref[...]
    ri = lax.broadcasted_iota(jnp.int32, (L, L), 0)
    ci = lax.broadcasted_iota(jnp.int32, (L, L), 1)
    diff = (ri - ci).astype(F32)
    causal = ri >= ci
    idx = lax.broadcasted_iota(jnp.int32, (L, 1), 0).astype(F32)
    decays = []
    for h in range(R_HEADS):
        log_g = math.log1p(-2.0 ** (-5.0 - h))
        decays.append(dict(
            dmask=jnp.where(causal, jnp.exp(jnp.where(causal, diff, 0.0) * log_g), 0.0),
            q_dec=jnp.exp((idx + 1.0) * log_g), k_dec=jnp.exp((L - 1.0 - idx) * log_g),
            c_dec=math.exp(L * log_g)))

    chains = []
    for bi in range(nb):
        q = rot(q_ref[bi])
        k = rot(k_ref[bi]) * (R_DK ** -0.5)
        for h in range(R_HEADS):
            chains.append(dict(bi=bi, h=h, u=bi * R_HEADS + h,
                               qb=q[:, h * R_DK:(h + 1) * R_DK].astype(BF16),
                               kh=k[:, h * R_DK:(h + 1) * R_DK]))

    for ch in chains:
        ch['rstate'] = r_ref[ch['u']]
        ch['s'] = _dot_nt(ch['qb'], ch['kh'].astype(BF16)) * decays[ch['h']]['dmask']
        ch['qr'] = _dot(ch['qb'], ch['rstate'].astype(BF16))

    for ch in chains:
        d = decays[ch['h']]
        vh = v_ref[ch['bi'], :, ch['h'] * R_DV:(ch['h'] + 1) * R_DV].astype(BF16)
        ch['o'] = _dot(ch['s'].astype(BF16), vh) + ch['qr'] * d['q_dec']
        kd = (ch['kh'] * d['k_dec']).T.astype(BF16)
        r_ref[ch['u']] = d['c_dec'] * ch['rstate'] + _dot(kd, vh)

    for ch in chains:
        sl = slice(ch['h'] * R_DV, (ch['h'] + 1) * R_DV)
        gh = g_ref[ch['bi'], :, sl]
        o_ref[ch['bi'], :, sl] = (_rms(ch['o'], gain[:, sl]) * (gh * jax.nn.sigmoid(gh))).astype(o_ref.dtype)


def retention(z3, cos_t, sin_t, gain):
    b, t, _ = z3.shape
    nc = t // R_CHUNK
    qk_w = R_HEADS * R_DK
    v_w = R_HEADS * R_DV
    return pl.pallas_call(
        _retention_kernel,
        out_shape=jax.ShapeDtypeStruct((b, t, v_w), BF16),
        grid=(nc,),
        in_specs=[pl.BlockSpec((R_CHUNK, qk_w), lambda c: (c, 0)),
                  pl.BlockSpec((R_CHUNK, qk_w), lambda c: (c, 0)),
                  pl.BlockSpec((b, R_CHUNK, qk_w), lambda c: (0, c, 0)),
                  pl.BlockSpec((b, R_CHUNK, qk_w), lambda c: (0, c, 1)),
                  pl.BlockSpec((b, R_CHUNK, v_w), lambda c: (0, c, 1)),
                  pl.BlockSpec((b, R_CHUNK, v_w), lambda c: (0, c, 2)),
                  pl.BlockSpec((1, v_w), lambda c: (0, 0))],
        out_specs=pl.BlockSpec((b, R_CHUNK, v_w), lambda c: (0, c, 0)),
        scratch_shapes=[pltpu.VMEM((b * R_HEADS, R_DK, R_DV), F32)],
        compiler_params=_cparams(("arbitrary",)),
        name="retention",
    )(cos_t, sin_t, z3, z3, z3, z3, gain.reshape(1, v_w))


def _mlstm_kernel(q_ref, k_ref, v_ref, og_ref, gc_ref, gr_ref, cw_ref, cb_ref, bc_ref, br_ref, gain_ref,
                  o_ref, xbuf, c_ref, n_ref, m_ref):
    c = pl.program_id(0)
    nb = q_ref.shape[0]
    L = M_CHUNK
    H = M_HEADS
    qk_w = H * M_DK
    halo = 8

    @pl.when(c == 0)
    def _():
        xbuf[:, 0:halo, :] = jnp.zeros((nb, halo, 2 * qk_w), F32)
        c_ref[...] = jnp.zeros_like(c_ref)
        n_ref[...] = jnp.zeros_like(n_ref)
        m_ref[...] = jnp.zeros_like(m_ref)

    gain = gain_ref[...]
    ri = lax.broadcasted_iota(jnp.int32, (L, L), 0)
    ci = lax.broadcasted_iota(jnp.int32, (L, L), 1)
    causal = ri >= ci
    tril = causal.astype(F32)
    triu = (ri <= ci).astype(F32)

    chains = []
    for bi in range(nb):
        xbuf[bi, halo:halo + L, 0:qk_w] = q_ref[bi]
        xbuf[bi, halo:halo + L, qk_w:2 * qk_w] = k_ref[bi]
        conv = cb_ref[...]
        for j in range(M_CONV):
            conv = conv + xbuf[bi, pl.ds(halo - (M_CONV - 1) + j, L), :] * cw_ref[j:j + 1, :]
        tail = xbuf[bi, L:L + halo, :]
        xbuf[bi, 0:halo, :] = tail
        act = conv * jax.nn.sigmoid(conv)
        q = act[:, 0:qk_w] * (M_DK ** -0.5)
        k = act[:, qk_w:2 * qk_w]
        gc = gc_ref[bi][:, 0:2 * H] + bc_ref[...]
        gr = gr_ref[bi] + br_ref[...]
        ig_c = gc[:, 0:H]
        ig_r = gr[0:H, :]
        b_c = _dot_f32(tril, jax.nn.log_sigmoid(gc[:, H:2 * H]))
        b_r = _dot_f32(jax.nn.log_sigmoid(gr[H:2 * H, :]), triu)
        for h in range(H):
            chains.append(dict(
                bi=bi, h=h, u=bi * H + h,
                qh=q[:, h * M_DK:(h + 1) * M_DK], kh=k[:, h * M_DK:(h + 1) * M_DK],
                bh=b_c[:, h:h + 1], brow=b_r[h:h + 1, :], irow=ig_r[h:h + 1, :], icol=ig_c[:, h:h + 1]))

    def stack(parts):
        return jnp.concatenate(parts, axis=0)

    def rows(x):
        return jnp.broadcast_to(x, (L, x.shape[1]))

    m_prev_u = [m_ref[ch['u']][:, 0:1] for ch in chains]
    b_last_u = [ch['bh'][L - 1:L, :] for ch in chains]
    bh = stack([ch['bh'] for ch in chains])
    icol = stack([ch['icol'] for ch in chains])
    brow = stack([rows(ch['brow']) for ch in chains])
    irow = stack([rows(ch['irow']) for ch in chains])
    m_prev = stack([rows(m) for m in m_prev_u])
    b_last = stack([rows(x) for x in b_last_u])
    causal_all = stack([causal] * len(chains))
    qs = stack([ch['qh'] for ch in chains])
    ks = stack([ch['kh'] for ch in chains])
    qb = qs.astype(BF16)
    kb = ks.astype(BF16)
    cstates = [c_ref[ch['u']] for ch in chains]
    nstates = [n_ref[ch['u']] for ch in chains]

    def chain_rows(x, i):
        return x[i * L:(i + 1) * L]

    s_raw = stack([_dot_nt(chain_rows(qb, i), chain_rows(kb, i)) for i in range(len(chains))])
    qc = stack([_dot(chain_rows(qb, i), cstates[i].astype(BF16)) for i in range(len(chains))])

    dlog = jnp.where(causal_all, bh - brow + irow, NEG)
    inter = bh + m_prev
    m_t = jnp.maximum(inter, jnp.max(dlog, axis=-1, keepdims=True))
    s = s_raw * jnp.exp(dlog - m_t)
    w_inter = jnp.exp(inter - m_t)
    wlog = b_last - bh + icol
    m_new_u = [jnp.maximum(b_last_u[i] + m_prev_u[i], jnp.max(chain_rows(wlog, i), axis=0, keepdims=True))
               for i in range(len(chains))]
    m_new = stack([rows(m) for m in m_new_u])
    wk = ks * jnp.exp(wlog - m_new)

    sb = s.astype(BF16)
    vhs = [v_ref[ch['bi'], :, ch['h'] * M_DV:(ch['h'] + 1) * M_DV].astype(BF16) for ch in chains]
    sv = stack([_dot(chain_rows(sb, i), vhs[i]) for i in range(len(chains))])
    kv = [_dot(chain_rows(wk, i).T.astype(BF16), vhs[i]) for i in range(len(chains))]

    qn = jnp.sum(qs * stack([rows(n) for n in nstates]), axis=-1, keepdims=True)
    den = jnp.sum(s, axis=-1, keepdims=True) + w_inter * qn
    hh = (sv + w_inter * qc) / jnp.maximum(jnp.abs(den), jnp.exp(-m_t))
    og = stack([og_ref[ch['bi'], :, ch['h'] * M_DV:(ch['h'] + 1) * M_DV] for ch in chains])
    gains = stack([rows(gain[:, ch['h'] * M_DV:(ch['h'] + 1) * M_DV]) for ch in chains])
    out = jax.nn.sigmoid(og) * _rms(hh, gains)

    for i, ch in enumerate(chains):
        u = ch['u']
        decay = jnp.exp(b_last_u[i] + m_prev_u[i] - m_new_u[i])
        c_ref[u] = decay * cstates[i] + kv[i]
        n_ref[u] = decay * nstates[i] + jnp.sum(chain_rows(wk, i), axis=0, keepdims=True)
        m_ref[u] = jnp.broadcast_to(m_new_u[i], (1, LANES_V7X))
        o_ref[ch['bi'], :, ch['h'] * M_DV:(ch['h'] + 1) * M_DV] = chain_rows(out, i).astype(o_ref.dtype)


def mlstm(z3, gates_r, conv_w, conv_b, ig_b, fg_b, gain):
    b, t, _ = z3.shape
    nc = t // M_CHUNK
    H = M_HEADS
    qk_w = H * M_DK
    v_w = H * M_DV
    bias = jnp.concatenate([ig_b, fg_b])
    gate_blk = AB_COLS_PAD // LANES_V7X - 1
    return pl.pallas_call(
        _mlstm_kernel,
        out_shape=jax.ShapeDtypeStruct((b, t, v_w), BF16),
        grid=(nc,),
        in_specs=[pl.BlockSpec((b, M_CHUNK, qk_w), lambda c: (0, c, 6)),
                  pl.BlockSpec((b, M_CHUNK, qk_w), lambda c: (0, c, 7)),
                  pl.BlockSpec((b, M_CHUNK, v_w), lambda c: (0, c, 4)),
                  pl.BlockSpec((b, M_CHUNK, v_w), lambda c: (0, c, 5)),
                  pl.BlockSpec((b, M_CHUNK, LANES_V7X), lambda c: (0, c, gate_blk)),
                  pl.BlockSpec((b, None, 2 * H, M_CHUNK), lambda c: (0, c, 0, 0)),
                  pl.BlockSpec((M_CONV, 2 * qk_w), lambda c: (0, 0)),
                  pl.BlockSpec((1, 2 * qk_w), lambda c: (0, 0)),
                  pl.BlockSpec((1, 2 * H), lambda c: (0, 0)),
                  pl.BlockSpec((2 * H, 1), lambda c: (0, 0)),
                  pl.BlockSpec((1, v_w), lambda c: (0, 0))],
        out_specs=pl.BlockSpec((b, M_CHUNK, v_w), lambda c: (0, c, 0)),
        scratch_shapes=[pltpu.VMEM((b, 8 + M_CHUNK, 2 * qk_w), F32),
                        pltpu.VMEM((b * H, M_DK, M_DV), F32),
                        pltpu.VMEM((b * H, 1, M_DK), F32),
                        pltpu.VMEM((b * H, 1, LANES_V7X), F32)],
        compiler_params=_cparams(("arbitrary",)),
        name="mlstm",
    )(z3, z3, z3, z3, z3, gates_r, conv_w, conv_b.reshape(1, -1), bias.reshape(1, -1), bias.reshape(-1, 1),
      gain.reshape(1, v_w))


def _mix_ffn_ple_kernel(*refs, n_mix, tm, seq, tf):
    hp_ref, h_ref = refs[0], refs[1]
    mix_refs = refs[2:2 + 2 * n_mix]
    (wo_ref, g_ref, wa_ref, wb_ref, cw_ref, cb_ref, wd_ref, p_ref, wp_ref, ng_ref, gg_ref, wg_ref,
     o_ref, xn_ref, a_ref) = refs[2 + 2 * n_mix:]
    i = pl.program_id(0)
    halo = 8
    g = g_ref[...]
    m_prev = jnp.concatenate([mix_refs[2 * k][...] for k in range(n_mix)], axis=1)
    m_tile = jnp.concatenate([mix_refs[2 * k + 1][...] for k in range(n_mix)], axis=1)
    x = h_ref[...] + _dot(m_tile, wo_ref[...])
    xp = hp_ref[...] + _dot(m_prev, wo_ref[...])[BF16_ROWS - halo:, :]
    xn_ref[halo:halo + tm, :] = _rms(x, g).astype(BF16)
    keep = ((i * tm) % seq != 0).astype(F32)
    xn_ref[0:halo, :] = (_rms(xp, g) * keep).astype(BF16)
    n_chunks = D_FF // tf

    def up_proj(c):
        cs = slice(c * tf, (c + 1) * tf)
        a_ref[c % 2] = _dot(xn_ref[...], wa_ref[:, cs])
        return _dot(xn_ref[halo:halo + tm, :], wb_ref[:, cs])

    h2 = x
    bgate = up_proj(0)
    for c in range(n_chunks):
        cs = slice(c * tf, (c + 1) * tf)
        bgate_next = up_proj(c + 1) if c + 1 < n_chunks else None
        conv = cb_ref[:, cs]
        for t in range(FFN_CONV):
            conv = conv + a_ref[c % 2, pl.ds(halo - (FFN_CONV - 1) + t, tm), :] * cw_ref[t:t + 1, cs]
        act = (_gelu_tanh(conv) * bgate).astype(BF16)
        h2 = h2 + _dot(act, wd_ref[cs, :])
        bgate = bgate_next
    e = _rms(_dot(p_ref[...].astype(BF16), wp_ref[...]), ng_ref[...])
    gate = jax.nn.sigmoid(_dot(_rms(h2, gg_ref[...]).astype(BF16), wg_ref[...]))
    o_ref[...] = h2 + gate * e


def mix_ffn_ple(h, mix, li, g, w_up, conv_w, conv_b, w_down, p, wp, norm_g, gate_norm_g, wg, seq, tm=512, tf=256):
    mix_outs, w_out = mix
    n, d = h.shape
    pd = p.shape[-1]
    hb = tm // 8
    const = lambda i: (0, 0)
    layer = lambda i: (li, 0, 0)
    prev8 = lambda i: (jnp.maximum(i * hb - 1, 0), 0)
    resident = dict(pipeline_mode=pl.Buffered(1))
    prev16 = lambda i: (jnp.maximum(i * (tm // BF16_ROWS) - 1, 0), 0)
    mix_specs, mix_args = [], []
    for m in mix_outs:
        mix_specs += [pl.BlockSpec((BF16_ROWS, m.shape[1]), prev16),
                      pl.BlockSpec((tm, m.shape[1]), lambda i: (i, 0))]
        mix_args += [m, m]
    mix_specs.append(pl.BlockSpec(w_out.shape, const, **resident))
    mix_args.append(w_out)
    return pl.pallas_call(
        functools.partial(_mix_ffn_ple_kernel, n_mix=len(mix_outs), tm=tm, seq=seq, tf=tf),
        out_shape=jax.ShapeDtypeStruct((n, d), F32),
        grid=(n // tm,),
        in_specs=[pl.BlockSpec((8, d), prev8),
                  pl.BlockSpec((tm, d), lambda i: (i, 0))] + mix_specs + [
                  pl.BlockSpec((None, 1, d), layer),
                  pl.BlockSpec((None, d, D_FF), lambda i: (li, 0, 0), **resident),
                  pl.BlockSpec((None, d, D_FF), lambda i: (li, 0, 1), **resident),
                  pl.BlockSpec((None, FFN_CONV, D_FF), layer),
                  pl.BlockSpec((None, 1, D_FF), layer),
                  pl.BlockSpec((None, D_FF, d), layer, **resident),
                  pl.BlockSpec((None, tm, pd), lambda i: (li, i, 0)),
                  pl.BlockSpec((None, pd, d), layer, **resident),
                  pl.BlockSpec((None, 1, d), layer),
                  pl.BlockSpec((None, 1, d), layer),
                  pl.BlockSpec((None, d, d), layer, **resident)],
        out_specs=pl.BlockSpec((tm, d), lambda i: (i, 0)),
        scratch_shapes=[pltpu.VMEM((8 + tm, d), BF16),
                        pltpu.VMEM((2, 8 + tm, tf), F32)],
        compiler_params=_cparams(("parallel",), VMEM_LIMIT_V7X),
        name="mix_ffn_ple",
    )(h, h, *mix_args, g[:, None, :], w_up, w_up, conv_w, conv_b[:, None, :], w_down,
      p, wp, norm_g[:, None, :], gate_norm_g[:, None, :], wg)


def _group_rms(x, g):
    lane = lax.broadcasted_iota(jnp.int32, x.shape, 1)
    x2 = x * x
    ms = jnp.zeros_like(x)
    for grp in range(N_KV_GROUPS):
        in_grp = (lane >= grp * HEAD_DIM) & (lane < (grp + 1) * HEAD_DIM)
        tot = jnp.sum(jnp.where(in_grp, x2, 0.0), axis=-1, keepdims=True)
        ms = jnp.where(in_grp, tot * (1.0 / HEAD_DIM), ms)
    return x * lax.rsqrt(ms + EPS) * g


def _kv_prep_kernel(c_ref, s_ref, w_ref, gs_ref, gw_ref, kc_ref, vc_ref, ks_ref, vs_ref, kw_ref, vw_ref):
    gw = N_KV_GROUPS * HEAD_DIM
    cc = c_ref[...]
    ss = s_ref[...]
    ww = w_ref[...]
    kc_ref[...] = cc[:, 0:gw].astype(BF16)
    vc_ref[...] = cc[:, gw:2 * gw].astype(BF16)
    ks = _group_rms(ss[:, 0:gw], gs_ref[...]).astype(BF16)
    kw = _group_rms(ww[:, 0:gw], gw_ref[...]).astype(BF16)
    vst = ss[:, gw:2 * gw].T
    vwt = ww[:, gw:2 * gw].T
    for g in range(N_KV_GROUPS):
        lanes = slice(g * HEAD_DIM, (g + 1) * HEAD_DIM)
        ks_ref[g] = ks[:, lanes]
        kw_ref[g] = kw[:, lanes]
        vs_ref[g] = vst[lanes, :].astype(BF16)
        for u in range(SEL_TILE // KEY_TILE):
            vw_ref[g, u] = vwt[lanes, u * KEY_TILE:(u + 1) * KEY_TILE].astype(BF16)


def kv_prep(z, k_g, b, t):
    n = z.shape[0]
    G = N_KV_GROUPS
    gw = G * HEAD_DIM
    base = N_HEADS * HEAD_DIM // (2 * gw)
    tm = SEL_TILE
    nst = t // tm
    sub = SEL_TILE // KEY_TILE
    row = lambda i, j: i * nst + j
    flat = jax.ShapeDtypeStruct((n, gw), BF16)
    keys = jax.ShapeDtypeStruct((b, G, t, HEAD_DIM), BF16)
    flat_spec = pl.BlockSpec((tm, gw), lambda i, j: (row(i, j), 0))
    key_spec = pl.BlockSpec((None, G, tm, HEAD_DIM), lambda i, j: (i, 0, j, 0))
    return pl.pallas_call(
        _kv_prep_kernel,
        out_shape=(flat, flat, keys, jax.ShapeDtypeStruct((b, G, nst, HEAD_DIM, tm), BF16),
                   keys, jax.ShapeDtypeStruct((b, G, nst * sub, HEAD_DIM, KEY_TILE), BF16)),
        grid=(b, nst),
        in_specs=[pl.BlockSpec((tm, 2 * gw), lambda i, j: (row(i, j), base)),
                  pl.BlockSpec((tm, 2 * gw), lambda i, j: (row(i, j), base + 1)),
                  pl.BlockSpec((tm, 2 * gw), lambda i, j: (row(i, j), base + 2)),
                  pl.BlockSpec((1, gw), lambda i, j: (0, 0)),
                  pl.BlockSpec((1, gw), lambda i, j: (0, 0))],
        out_specs=(flat_spec, flat_spec, key_spec,
                   pl.BlockSpec((None, G, None, HEAD_DIM, tm), lambda i, j: (i, 0, j, 0, 0)),
                   key_spec,
                   pl.BlockSpec((None, G, sub, HEAD_DIM, KEY_TILE), lambda i, j: (i, 0, j, 0, 0))),
        compiler_params=_cparams(("parallel", "parallel")),
        name="kv_prep",
    )(z, z, z, jnp.tile(k_g[1], G).reshape(1, gw), jnp.tile(k_g[2], G).reshape(1, gw))


def _compress_kernel(x_ref, pos_ref, w1_ref, w2_ref, g_ref, o_ref, *, normalize):
    half = w1_ref.shape[0] // 2
    x = x_ref[...]
    u = _dot(x, w1_ref[0:half, :])
    v = _dot(x, w1_ref[half:2 * half, :])
    rows = u.shape[0]
    posc = _dot(pos_ref[...], w1_ref[...])[0:1, :]
    hid = u + pltpu.roll(v, rows - 1, 0) + posc
    out = _dot(jax.nn.gelu(hid).astype(BF16), w2_ref[...])
    if normalize:
        out = _rms(out, g_ref[...])
    o_ref[...] = out.astype(o_ref.dtype)


def compress(x, pos, w1, w2, g, normalize, ncb):
    n, kdim = x.shape
    posf = jnp.broadcast_to(pos.reshape(1, -1), (8, 2 * kdim)).astype(BF16)
    return pl.pallas_call(
        functools.partial(_compress_kernel, normalize=normalize),
        out_shape=jax.ShapeDtypeStruct((n, HEAD_DIM), BF16),
        grid=(n // ncb,),
        in_specs=[pl.BlockSpec((ncb, kdim), lambda i: (i, 0)),
                  pl.BlockSpec((8, 2 * kdim), lambda i: (0, 0)),
                  pl.BlockSpec((2 * kdim, CMP_HIDDEN), lambda i: (0, 0)),
                  pl.BlockSpec((CMP_HIDDEN, HEAD_DIM), lambda i: (0, 0)),
                  pl.BlockSpec((1, HEAD_DIM), lambda i: (0, 0))],
        out_specs=pl.BlockSpec((ncb, HEAD_DIM), lambda i: (i, 0)),
        compiler_params=_cparams(("parallel",)),
        name="compress",
    )(x, posf, w1.astype(BF16), w2.astype(BF16), g.reshape(1, HEAD_DIM))


def _nsa_kernel(bnd_ref, *refs):
    safe = bnd_ref[0, 0] <= MAX_SAFE_SCORE_BOUND

    @pl.when(safe)
    def _():
        _nsa_body(bnd_ref, *refs, bounded=True)

    @pl.when(jnp.logical_not(safe))
    def _():
        _nsa_body(bnd_ref, *refs, bounded=False)


def _nsa_body(bnd_ref, zq_ref, gt_ref, gb_ref, qg_ref, ovt_ref, kc_ref, vct_ref, ks_ref, vst_ref, kw_ref,
              vwt_ref, o_ref, q_scr, sel_scr, m_scr, l_scr, acc_scr, oc_scr, *, bounded):
    grp = pl.program_id(1)
    qb = pl.program_id(2)
    QB = Q_BLOCK
    ncp = kc_ref.shape[0]
    ns = ovt_ref.shape[0]
    tpos = qb * QB + lax.broadcasted_iota(jnp.int32, (1, QB), 1)

    def lane_tile(x):
        return jnp.concatenate([x] * HG, axis=1)

    zt = zq_ref[...].T
    qg = qg_ref[...] * (HEAD_DIM ** -0.5 * LOG2E)
    heads = []
    for h in range(HG):
        xh = zt[h * HEAD_DIM:(h + 1) * HEAD_DIM, :]
        ms = jnp.mean(xh * xh, axis=0, keepdims=True)
        heads.append(xh * lax.rsqrt(ms + EPS) * qg)
    q_scr[...] = jnp.concatenate(heads, axis=1).astype(BF16)
    q = q_scr[...]

    gates = jax.nn.sigmoid(gt_ref[...].T + gb_ref[...])
    per_grp = HG * 3
    gsel = gates[0:per_grp, :]
    for g2 in range(1, N_KV_GROUPS):
        gsel = jnp.where(grp == g2, gates[g2 * per_grp:(g2 + 1) * per_grp, :], gsel)

    def gate_row(c):
        return jnp.concatenate([gsel[3 * h + c:3 * h + c + 1, :] for h in range(HG)], axis=1)

    cmp_end = lax.broadcasted_iota(jnp.int32, (ncp, 1), 0) * CMP_STRIDE + (CMP_BLOCK - 1)
    keep = -bnd_ref[0, 0] if bounded else 0.0
    cbias = jnp.where(cmp_end <= tpos, keep, NEG)
    s = _dot(kc_ref[...], q) + lane_tile(cbias)
    e = jnp.exp2(s) if bounded else jnp.exp2(s - jnp.max(s, axis=0, keepdims=True))
    inv = jnp.where(lane_tile(tpos) >= CMP_BLOCK - 1, 1.0 / jnp.sum(e, axis=0, keepdims=True), 0.0)
    p = e * inv
    ocmp = _dot(vct_ref[...], p.astype(BF16))
    psum = p[:, 0:QB]
    for h in range(1, HG):
        psum = psum + p[:, h * QB:(h + 1) * QB]

    n_win = WINDOW // KEY_TILE + 1
    win_sub = lax.broadcasted_iota(jnp.int32, (n_win * KEY_TILE, 1), 0)
    tiles = [jnp.maximum(qb - (n_win - 1) + u, 0) for u in range(n_win)]
    kwin = jnp.concatenate([kw_ref[j] for j in tiles], axis=0)
    vwin = jnp.concatenate([vwt_ref[j] for j in tiles], axis=1)
    kpos = (qb - (n_win - 1)) * KEY_TILE + win_sub
    wbias = jnp.where((kpos <= tpos) & (kpos > tpos - WINDOW) & (kpos >= 0), keep, NEG)
    sw = _dot(kwin, q) + lane_tile(wbias)
    pw = jnp.exp2(sw) if bounded else jnp.exp2(sw - jnp.max(sw, axis=0, keepdims=True))
    lw = jnp.sum(pw, axis=0, keepdims=True)
    oc_scr[...] = gate_row(0) * ocmp + (gate_row(2) / lw) * _dot(vwin, pw.astype(BF16))

    p_hi = psum.astype(BF16)
    p_lo = (psum - p_hi.astype(F32)).astype(BF16)
    ovt = ovt_ref[...]
    imp = _dot(ovt, p_hi) + _dot(ovt, p_lo)
    blk = lax.broadcasted_iota(jnp.int32, (ns, 1), 0)
    blk_f = blk.astype(F32)
    cur = jnp.right_shift(tpos, SLC_BLOCK.bit_length() - 1)
    forced = (blk == 0) | (blk == cur) | (blk == cur - 1)
    bvalid = blk <= cur
    score = jnp.where(forced, BIG, jnp.where(bvalid, imp, NEG))
    sel = jnp.zeros((ns, QB), F32)
    for _ in range(min(N_SELECT, ns)):
        mx = jnp.max(score, axis=0, keepdims=True)
        first = jnp.min(jnp.where(score == mx, blk_f, float(ns)), axis=0, keepdims=True)
        pick = blk_f == first
        sel = jnp.where(pick, 1.0, sel)
        score = jnp.where(pick, -jnp.inf, score)
    sel_scr[...] = jnp.where(bvalid, sel, 0.0)

    m_scr[...] = jnp.full(m_scr.shape, NEG, F32)
    l_scr[...] = jnp.zeros(l_scr.shape, F32)
    acc_scr[...] = jnp.zeros(acc_scr.shape, F32)
    blocks_per_tile = SEL_TILE // SLC_BLOCK
    sel_sub = lax.broadcasted_iota(jnp.int32, (SEL_TILE, 1), 0)

    def sel_body(j, carry):
        rows = [jnp.broadcast_to(sel_scr[pl.ds(j * blocks_per_tile + bi, 1), :], (SLC_BLOCK, QB))
                for bi in range(blocks_per_tile)]
        chosen = jnp.concatenate(rows, axis=0)
        kpos = j * SEL_TILE + sel_sub
        bias = jnp.where((chosen > 0.5) & (kpos <= tpos), keep, NEG)
        st = _dot(ks_ref[j], q_scr[...]) + lane_tile(bias)
        if bounded:
            pt = jnp.exp2(st)
            l_scr[...] += jnp.sum(pt, axis=0, keepdims=True)
            acc_scr[...] += _dot(vst_ref[j], pt.astype(BF16))
        else:
            m_old = m_scr[...]
            m_new = jnp.maximum(m_old, jnp.max(st, axis=0, keepdims=True))
            alpha = jnp.exp2(m_old - m_new)
            pt = jnp.exp2(st - m_new)
            l_scr[...] = alpha * l_scr[...] + jnp.sum(pt, axis=0, keepdims=True)
            acc_scr[...] = alpha * acc_scr[...] + _dot(vst_ref[j], pt.astype(BF16))
            m_scr[...] = m_new
        return carry

    lax.fori_loop(0, qb // (SEL_TILE // QB) + 1, sel_body, 0)
    ot = oc_scr[...] + (gate_row(1) / l_scr[...]) * acc_scr[...]
    stacked = jnp.concatenate([ot[:, h * QB:(h + 1) * QB] for h in range(HG)], axis=0)
    o_ref[...] = stacked.T.astype(o_ref.dtype)


def nsa_attention(bound, z3, gate_b, q_g, overlap_t, kc, vct, ks, vst, kw, vwt):
    b, t, _ = z3.shape
    nq = t // Q_BLOCK
    nt = t // KEY_TILE
    qw = HG * HEAD_DIM
    ns, ncp = overlap_t.shape
    gate_blk = (NSA_COLS_PAD // LANES_V7X) - 1
    gb = jnp.zeros((LANES_V7X, 1), F32).at[:N_HEADS * 3, 0].set(gate_b)
    full5 = lambda i, g, q: (i, g, 0, 0, 0)
    lanes = HG * Q_BLOCK
    return pl.pallas_call(
        _nsa_kernel,
        out_shape=jax.ShapeDtypeStruct((b, t, N_HEADS * HEAD_DIM), BF16),
        grid=(b, N_KV_GROUPS, nq),
        in_specs=[pl.BlockSpec(memory_space=pltpu.SMEM),
                  pl.BlockSpec((None, Q_BLOCK, qw), lambda i, g, q: (i, q, g)),
                  pl.BlockSpec((None, Q_BLOCK, LANES_V7X), lambda i, g, q: (i, q, gate_blk)),
                  pl.BlockSpec((LANES_V7X, 1), lambda i, g, q: (0, 0)),
                  pl.BlockSpec((HEAD_DIM, 1), lambda i, g, q: (0, 0)),
                  pl.BlockSpec((ns, ncp), lambda i, g, q: (0, 0)),
                  pl.BlockSpec((None, None, ncp, HEAD_DIM), lambda i, g, q: (i, g, 0, 0)),
                  pl.BlockSpec((None, None, HEAD_DIM, ncp), lambda i, g, q: (i, g, 0, 0)),
                  pl.BlockSpec((None, None, t // SEL_TILE, SEL_TILE, HEAD_DIM), full5),
                  pl.BlockSpec((None, None, t // SEL_TILE, HEAD_DIM, SEL_TILE), full5),
                  pl.BlockSpec((None, None, nt, KEY_TILE, HEAD_DIM), full5),
                  pl.BlockSpec((None, None, nt, HEAD_DIM, KEY_TILE), full5)],
        out_specs=pl.BlockSpec((None, Q_BLOCK, qw), lambda i, g, q: (i, q, g)),
        scratch_shapes=[pltpu.VMEM((HEAD_DIM, lanes), BF16),
                        pltpu.VMEM((ns, Q_BLOCK), F32),
                        pltpu.VMEM((1, lanes), F32),
                        pltpu.VMEM((1, lanes), F32),
                        pltpu.VMEM((HEAD_DIM, lanes), F32),
                        pltpu.VMEM((HEAD_DIM, lanes), F32)],
        compiler_params=_cparams(("parallel", "parallel", "arbitrary"), VMEM_LIMIT_V7X),
        name="nsa_attention",
    )(bound.reshape(1, 1), z3, z3, gb, q_g.reshape(HEAD_DIM, 1), overlap_t, kc, vct, ks, vst, kw, vwt)


def _overlap_matrix_t(t):
    ncp = t // CMP_STRIDE
    ns = t // SLC_BLOCK
    c_start = np.arange(ncp) * CMP_STRIDE
    sj = np.arange(ns)
    ov = ((c_start[None, :] < (sj[:, None] + 1) * SLC_BLOCK)
          & (c_start[None, :] + CMP_BLOCK > sj[:, None] * SLC_BLOCK)
          & (c_start[None, :] + CMP_BLOCK <= t))
    return jnp.asarray(ov, dtype=BF16)


def ab_layer(h, b, t, norm_g, w_in, conv_w, conv_b, ret_g, ig_b, fg_b, m_g, w_out):
    n = b * t
    w_in_p = jnp.pad(w_in, ((0, 0), (0, AB_COLS_PAD - AB_COLS))).astype(BF16)
    z = norm_matmul(h, norm_g, w_in_p)
    z3 = z.reshape(b, t, AB_COLS_PAD)
    cos_t, sin_t = rope_tables(t)
    ret = retention(z3, cos_t, sin_t, ret_g)
    g0 = AB_COLS - 2 * M_HEADS
    gates_r = z3[:, :, g0:AB_COLS].reshape(b, t // M_CHUNK, M_CHUNK, 2 * M_HEADS).transpose(0, 1, 3, 2)
    ml = mlstm(z3, gates_r, conv_w, conv_b, ig_b, fg_b, m_g)
    w_out_b = w_out.astype(BF16)
    rw = R_HEADS * R_DV
    return [ret.reshape(n, rw), ml.reshape(n, -1)], w_out_b


def nsa_layer(h, b, t, norm_g, w_in, q_g, k_g, pos_k, pos_v, w1k, w2k, w1v, w2v, gate_b, w_out):
    n = b * t
    G = N_KV_GROUPS
    w_in_p = jnp.pad(w_in, ((0, 0), (0, NSA_COLS_PAD - NSA_COLS))).astype(BF16)
    z = norm_matmul(h, norm_g, w_in_p)
    kc_in, vc_in, ks, vst, kw, vwt = kv_prep(z, k_g, b, t)
    ncb = t // CMP_STRIDE

    def cmp_rows(x):
        return x.reshape(b, t, G, HEAD_DIM).transpose(0, 2, 1, 3).reshape(b * G * ncb, CMP_STRIDE * HEAD_DIM)

    kc = compress(cmp_rows(kc_in), pos_k, w1k, w2k, k_g[0], True, ncb)
    vc = compress(cmp_rows(vc_in), pos_v, w1v, w2v, k_g[0], False, ncb)
    kc = kc.reshape(b, G, ncb, HEAD_DIM)
    vct = vc.reshape(b, G, ncb, HEAD_DIM).transpose(0, 1, 3, 2)

    def key_tiles(x, kt):
        return x.reshape(b, G, t // kt, kt, HEAD_DIM)

    bound = 1.02 * LOG2E * math.sqrt(HEAD_DIM) * jnp.max(jnp.abs(q_g)) * jnp.max(jnp.abs(k_g))
    args = (bound, z.reshape(b, t, NSA_COLS_PAD), gate_b, q_g, _overlap_matrix_t(t), kc, vct,
            key_tiles(ks, SEL_TILE), vst, key_tiles(kw, KEY_TILE), vwt)
    o = nsa_attention(*args)
    return [o.reshape(n, -1)], w_out.astype(BF16)


def kernel(x, p, ab_norm_g, ab_w_in, ab_conv_w, ab_conv_b, ab_ret_norm_g, ab_ig_b, ab_fg_b, ab_m_norm_g, ab_w_out, nsa_norm_g, nsa_w_in, nsa_q_norm_g, nsa_k_norm_g, nsa_cmp_pos_k, nsa_cmp_pos_v, nsa_cmp_w1k, nsa_cmp_w2k, nsa_cmp_w1v, nsa_cmp_w2v, nsa_gate_b, nsa_w_out, ffn_norm_g, ffn_w_up, ffn_conv_w, ffn_conv_b, ffn_w_down, ple_w, ple_norm_g, ple_gate_norm_g, ple_w_gate):
    b, t, d = x.shape
    n = b * t
    depth = p.shape[0]
    h = x.reshape(n, d)
    p2 = p.reshape(depth, n, -1)
    ffn_w_up_b, ffn_w_down_b = ffn_w_up.astype(BF16), ffn_w_down.astype(BF16)
    ple_w_b, ple_w_gate_b = ple_w.astype(BF16), ple_w_gate.astype(BF16)
    for i in range(depth):
        j = i // 2
        if i % 2 == 0:
            mix = ab_layer(h, b, t, ab_norm_g[j], ab_w_in[j], ab_conv_w[j], ab_conv_b[j], ab_ret_norm_g[j],
                           ab_ig_b[j], ab_fg_b[j], ab_m_norm_g[j], ab_w_out[j])
        else:
            mix = nsa_layer(h, b, t, nsa_norm_g[j], nsa_w_in[j], nsa_q_norm_g[j], nsa_k_norm_g[j],
                            nsa_cmp_pos_k[j], nsa_cmp_pos_v[j], nsa_cmp_w1k[j], nsa_cmp_w2k[j],
                            nsa_cmp_w1v[j], nsa_cmp_w2v[j], nsa_gate_b[j], nsa_w_out[j])
        h = mix_ffn_ple(h, mix, i, ffn_norm_g, ffn_w_up_b, ffn_conv_w, ffn_conv_b, ffn_w_down_b, p2, ple_w_b,
                        ple_norm_g, ple_gate_norm_g, ple_w_gate_b, t)
    return h.reshape(b, t, d)
```

```python
import functools
import math

import numpy as np
import jax
import jax.numpy as jnp
from jax import lax
from jax.experimental import pallas as pl
from jax.experimental.pallas import tpu as pltpu

F32 = jnp.float32
BF16 = jnp.bfloat16

LANES_V7X = 128
BF16_ROWS = 16
VMEM_LIMIT_V7X = 56 * 1024 * 1024

D_MODEL = 1024
PLE_DIM = 256
R_HEADS, R_DK, R_DV, R_CHUNK = 4, 64, 128, 128
ROPE_BASE = 10000.0
M_HEADS, M_DK, M_DV, M_CHUNK, M_CONV = 4, 64, 128, 64, 4
AB_SIZES = (R_HEADS * R_DK, R_HEADS * R_DK, R_HEADS * R_DV, R_HEADS * R_DV,
            M_HEADS * M_DK, M_HEADS * M_DK, M_HEADS * M_DV, M_HEADS * M_DV, M_HEADS, M_HEADS)
AB_COLS = sum(AB_SIZES)
AB_COLS_PAD = 3200
N_HEADS, N_KV_GROUPS, HEAD_DIM = 16, 2, 64
HG = N_HEADS // N_KV_GROUPS
CMP_BLOCK, CMP_STRIDE, CMP_HIDDEN = 32, 16, 256
SLC_BLOCK, N_SELECT, WINDOW = 64, 16, 512
Q_BLOCK = 256
NSA_COLS = N_HEADS * HEAD_DIM + 6 * N_KV_GROUPS * HEAD_DIM + N_HEADS * 3
NSA_COLS_PAD = 1920
D_FF = 2816
FFN_CONV = 3
NEG = -1e30
BIG = 1e30
EPS = 1e-6
KEY_TILE = 128
SEL_TILE = 512
LOG2E = math.log2(math.e)
MAX_SAFE_SCORE_BOUND = 56.0


def _cparams(sem, vmem=None):
    return pltpu.CompilerParams(dimension_semantics=sem, vmem_limit_bytes=vmem)


def _rms(x, g):
    ms = jnp.mean(x * x, axis=-1, keepdims=True)
    return x * lax.rsqrt(ms + EPS) * g


def _dot(a, b):
    return jnp.dot(a, b, preferred_element_type=F32)


def _dot_nt(a, b):
    return lax.dot_general(a, b, (((1,), (1,)), ((), ())), preferred_element_type=F32)


def _dot_f32(a, b):
    return jnp.dot(a, b, preferred_element_type=F32, precision=lax.Precision.HIGHEST)


def _gelu_tanh(x):
    k1 = -2.0 * math.sqrt(2.0 / math.pi) * LOG2E
    return x / (1.0 + jnp.exp2(x * (k1 + (k1 * 0.044715) * (x * x))))


def _norm_matmul_kernel(x_ref, g_ref, w_ref, o_ref):
    xn = _rms(x_ref[...], g_ref[...]).astype(BF16)
    o_ref[...] = _dot(xn, w_ref[...]).astype(o_ref.dtype)


def norm_matmul(x, g, w, tm=512, out_dtype=F32):
    n, d = x.shape
    nc = w.shape[1]
    return pl.pallas_call(
        _norm_matmul_kernel,
        out_shape=jax.ShapeDtypeStruct((n, nc), out_dtype),
        grid=(n // tm,),
        in_specs=[pl.BlockSpec((tm, d), lambda i: (i, 0)),
                  pl.BlockSpec((1, d), lambda i: (0, 0)),
                  pl.BlockSpec((d, nc), lambda i: (0, 0))],
        out_specs=pl.BlockSpec((tm, nc), lambda i: (i, 0)),
        compiler_params=_cparams(("parallel",), VMEM_LIMIT_V7X),
        name="norm_matmul",
    )(x, g.reshape(1, d), w)


def _rope_table_kernel(inv_ref, cos_ref, sin_ref):
    c = pl.program_id(0)
    rows, width = cos_ref.shape
    pos = (c * rows + lax.broadcasted_iota(jnp.int32, (rows, width), 0)).astype(F32)
    lane = lax.broadcasted_iota(jnp.int32, (rows, width), 1)
    ang = pos * inv_ref[...]
    cos_ref[...] = jnp.cos(ang)
    sn = jnp.sin(ang)
    sin_ref[...] = jnp.where(lane % R_DK < R_DK // 2, -sn, sn)


def rope_tables(t):
    half = R_DK // 2
    inv = ROPE_BASE ** (-jnp.arange(half, dtype=F32) / half)
    inv = jnp.tile(inv, 2 * R_HEADS).reshape(1, R_HEADS * R_DK)
    width = R_HEADS * R_DK
    shp = jax.ShapeDtypeStruct((t, width), F32)
    return pl.pallas_call(
        _rope_table_kernel,
        out_shape=(shp, shp),
        grid=(t // R_CHUNK,),
        in_specs=[pl.BlockSpec((1, width), lambda c: (0, 0))],
        out_specs=(pl.BlockSpec((R_CHUNK, width), lambda c: (c, 0)),
                   pl.BlockSpec((R_CHUNK, width), lambda c: (c, 0))),
        compiler_params=_cparams(("parallel",)),
        name="rope_tables",
    )(inv)


def _retention_kernel(cos_ref, sin_ref, q_ref, k_ref, v_ref, g_ref, gain_ref, o_ref, r_ref):
    c = pl.program_id(0)
    nb = q_ref.shape[0]
    L = R_CHUNK

    @pl.when(c == 0)
    def _():
        r_ref[...] = jnp.zeros_like(r_ref)

    cos = cos_ref[...]
    sin = sin_ref[...]
    lane = lax.broadcasted_iota(jnp.int32, cos.shape, 1)
    first_half = lane % R_DK < R_DK // 2
    width = R_HEADS * R_DK

    def rot(x):
        swapped = jnp.where(first_half, pltpu.roll(x, width - R_DK // 2, 1), pltpu.roll(x, R_DK // 2, 1))
        return x * cos + swapped * sin

    gain = gain_ref[...]
    ri = lax.broadcasted_iota(jnp.int32, (L, L), 0)
    ci = lax.broadcasted_iota(jnp.int32, (L, L), 1)
    diff = (ri - ci).astype(F32)
    causal = ri >= ci
    idx = lax.broadcasted_iota(jnp.int32, (L, 1), 0).astype(F32)
    decays = []
    for h in range(R_HEADS):
        log_g = math.log1p(-2.0 ** (-5.0 - h))
        decays.append(dict(
            dmask=jnp.where(causal, jnp.exp(jnp.where(causal, diff, 0.0) * log_g), 0.0),
            q_dec=jnp.exp((idx + 1.0) * log_g), k_dec=jnp.exp((L - 1.0 - idx) * log_g),
            c_dec=math.exp(L * log_g)))

    chains = []
    for bi in range(nb):
        q = rot(q_ref[bi])
        k = rot(k_ref[bi]) * (R_DK ** -0.5)
        for h in range(R_HEADS):
            chains.append(dict(bi=bi, h=h, u=bi * R_HEADS + h,
                               qb=q[:, h * R_DK:(h + 1) * R_DK].astype(BF16),
                               kh=k[:, h * R_DK:(h + 1) * R_DK]))

    for ch in chains:
        ch['rstate'] = r_ref[ch['u']]
        ch['s'] = _dot_nt(ch['qb'], ch['kh'].astype(BF16)) * decays[ch['h']]['dmask']
        ch['qr'] = _dot(ch['qb'], ch['rstate'].astype(BF16))

    for ch in chains:
        d = decays[ch['h']]
        vh = v_ref[ch['bi'], :, ch['h'] * R_DV:(ch['h'] + 1) * R_DV].astype(BF16)
        ch['o'] = _dot(ch['s'].astype(BF16), vh) + ch['qr'] * d['q_dec']
        kd = (ch['kh'] * d['k_dec']).T.astype(BF16)
        r_ref[ch['u']] = d['c_dec'] * ch['rstate'] + _dot(kd, vh)

    for ch in chains:
        sl = slice(ch['h'] * R_DV, (ch['h'] + 1) * R_DV)
        gh = g_ref[ch['bi'], :, sl]
        o_ref[ch['bi'], :, sl] = (_rms(ch['o'], gain[:, sl]) * (gh * jax.nn.sigmoid(gh))).astype(o_ref.dtype)


def retention(z3, cos_t, sin_t, gain):
    b, t, _ = z3.shape
    nc = t // R_CHUNK
    qk_w = R_HEADS * R_DK
    v_w = R_HEADS * R_DV
    return pl.pallas_call(
        _retention_kernel,
        out_shape=jax.ShapeDtypeStruct((b, t, v_w), BF16),
        grid=(nc,),
        in_specs=[pl.BlockSpec((R_CHUNK, qk_w), lambda c: (c, 0)),
                  pl.BlockSpec((R_CHUNK, qk_w), lambda c: (c, 0)),
                  pl.BlockSpec((b, R_CHUNK, qk_w), lambda c: (0, c, 0)),
                  pl.BlockSpec((b, R_CHUNK, qk_w), lambda c: (0, c, 1)),
                  pl.BlockSpec((b, R_CHUNK, v_w), lambda c: (0, c, 1)),
                  pl.BlockSpec((b, R_CHUNK, v_w), lambda c: (0, c, 2)),
                  pl.BlockSpec((1, v_w), lambda c: (0, 0))],
        out_specs=pl.BlockSpec((b, R_CHUNK, v_w), lambda c: (0, c, 0)),
        scratch_shapes=[pltpu.VMEM((b * R_HEADS, R_DK, R_DV), F32)],
        compiler_params=_cparams(("arbitrary",)),
        name="retention",
    )(cos_t, sin_t, z3, z3, z3, z3, gain.reshape(1, v_w))


def _mlstm_kernel(q_ref, k_ref, v_ref, og_ref, gc_ref, gr_ref, cw_ref, cb_ref, bc_ref, br_ref, gain_ref,
                  o_ref, xbuf, c_ref, n_ref, m_ref):
    c = pl.program_id(0)
    nb = q_ref.shape[0]
    L = M_CHUNK
    H = M_HEADS
    qk_w = H * M_DK
    halo = 8

    @pl.when(c == 0)
    def _():
        xbuf[:, 0:halo, :] = jnp.zeros((nb, halo, 2 * qk_w), F32)
        c_ref[...] = jnp.zeros_like(c_ref)
        n_ref[...] = jnp.zeros_like(n_ref)
        m_ref[...] = jnp.zeros_like(m_ref)

    gain = gain_ref[...]
    ri = lax.broadcasted_iota(jnp.int32, (L, L), 0)
    ci = lax.broadcasted_iota(jnp.int32, (L, L), 1)
    causal = ri >= ci
    tril = causal.astype(F32)
    triu = (ri <= ci).astype(F32)

    chains = []
    for bi in range(nb):
        xbuf[bi, halo:halo + L, 0:qk_w] = q_ref[bi]
        xbuf[bi, halo:halo + L, qk_w:2 * qk_w] = k_ref[bi]
        conv = cb_ref[...]
        for j in range(M_CONV):
            conv = conv + xbuf[bi, pl.ds(halo - (M_CONV - 1) + j, L), :] * cw_ref[j:j + 1, :]
        tail = xbuf[bi, L:L + halo, :]
        xbuf[bi, 0:halo, :] = tail
        act = conv * jax.nn.sigmoid(conv)
        q = act[:, 0:qk_w] * (M_DK ** -0.5)
        k = act[:, qk_w:2 * qk_w]
        gc = gc_ref[bi][:, 0:2 * H] + bc_ref[...]
        gr = gr_ref[bi] + br_ref[...]
        ig_c = gc[:, 0:H]
        ig_r = gr[0:H, :]
        b_c = _dot_f32(tril, jax.nn.log_sigmoid(gc[:, H:2 * H]))
        b_r = _dot_f32(jax.nn.log_sigmoid(gr[H:2 * H, :]), triu)
        for h in range(H):
            chains.append(dict(
                bi=bi, h=h, u=bi * H + h,
                qh=q[:, h * M_DK:(h + 1) * M_DK], kh=k[:, h * M_DK:(h + 1) * M_DK],
                bh=b_c[:, h:h + 1], brow=b_r[h:h + 1, :], irow=ig_r[h:h + 1, :], icol=ig_c[:, h:h + 1]))

    def stack(parts):
        return jnp.concatenate(parts, axis=0)

    def rows(x):
        return jnp.broadcast_to(x, (L, x.shape[1]))

    m_prev_u = [m_ref[ch['u']][:, 0:1] for ch in chains]
    b_last_u = [ch['bh'][L - 1:L, :] for ch in chains]
    bh = stack([ch['bh'] for ch in chains])
    icol = stack([ch['icol'] for ch in chains])
    brow = stack([rows(ch['brow']) for ch in chains])
    irow = stack([rows(ch['irow']) for ch in chains])
    m_prev = stack([rows(m) for m in m_prev_u])
    b_last = stack([rows(x) for x in b_last_u])
    causal_all = stack([causal] * len(chains))
    qs = stack([ch['qh'] for ch in chains])
    ks = stack([ch['kh'] for ch in chains])
    qb = qs.astype(BF16)
    kb = ks.astype(BF16)
    cstates = [c_ref[ch['u']] for ch in chains]
    nstates = [n_ref[ch['u']] for ch in chains]

    def chain_rows(x, i):
        return x[i * L:(i + 1) * L]

    s_raw = stack([_dot_nt(chain_rows(qb, i), chain_rows(kb, i)) for i in range(len(chains))])
    qc = stack([_dot(chain_rows(qb, i), cstates[i].astype(BF16)) for i in range(len(chains))])

    dlog = jnp.where(causal_all, bh - brow + irow, NEG)
    inter = bh + m_prev
    m_t = jnp.maximum(inter, jnp.max(dlog, axis=-1, keepdims=True))
    s = s_raw * jnp.exp(dlog - m_t)
    w_inter = jnp.exp(inter - m_t)
    wlog = b_last - bh + icol
    m_new_u = [jnp.maximum(b_last_u[i] + m_prev_u[i], jnp.max(chain_rows(wlog, i), axis=0, keepdims=True))
               for i in range(len(chains))]
    m_new = stack([rows(m) for m in m_new_u])
    wk = ks * jnp.exp(wlog - m_new)

    sb = s.astype(BF16)
    vhs = [v_ref[ch['bi'], :, ch['h'] * M_DV:(ch['h'] + 1) * M_DV].astype(BF16) for ch in chains]
    sv = stack([_dot(chain_rows(sb, i), vhs[i]) for i in range(len(chains))])
    kv = [_dot(chain_rows(wk, i).T.astype(BF16), vhs[i]) for i in range(len(chains))]

    qn = jnp.sum(qs * stack([rows(n) for n in nstates]), axis=-1, keepdims=True)
    den = jnp.sum(s, axis=-1, keepdims=True) + w_inter * qn
    hh = (sv + w_inter * qc) / jnp.maximum(jnp.abs(den), jnp.exp(-m_t))
    og = stack([og_ref[ch['bi'], :, ch['h'] * M_DV:(ch['h'] + 1) * M_DV] for ch in chains])
    gains = stack([rows(gain[:, ch['h'] * M_DV:(ch['h'] + 1) * M_DV]) for ch in chains])
    out = jax.nn.sigmoid(og) * _rms(hh, gains)

    for i, ch in enumerate(chains):
        u = ch['u']
        decay = jnp.exp(b_last_u[i] + m_prev_u[i] - m_new_u[i])
        c_ref[u] = decay * cstates[i] + kv[i]
        n_ref[u] = decay * nstates[i] + jnp.sum(chain_rows(wk, i), axis=0, keepdims=True)
        m_ref[u] = jnp.broadcast_to(m_new_u[i], (1, LANES_V7X))
        o_ref[ch['bi'], :, ch['h'] * M_DV:(ch['h'] + 1) * M_DV] = chain_rows(out, i).astype(o_ref.dtype)


def mlstm(z3, gates_r, conv_w, conv_b, ig_b, fg_b, gain):
    b, t, _ = z3.shape
    nc = t // M_CHUNK
    H = M_HEADS
    qk_w = H * M_DK
    v_w = H * M_DV
    bias = jnp.concatenate([ig_b, fg_b])
    gate_blk = AB_COLS_PAD // LANES_V7X - 1
    return pl.pallas_call(
        _mlstm_kernel,
        out_shape=jax.ShapeDtypeStruct((b, t, v_w), BF16),
        grid=(nc,),
        in_specs=[pl.BlockSpec((b, M_CHUNK, qk_w), lambda c: (0, c, 6)),
                  pl.BlockSpec((b, M_CHUNK, qk_w), lambda c: (0, c, 7)),
                  pl.BlockSpec((b, M_CHUNK, v_w), lambda c: (0, c, 4)),
                  pl.BlockSpec((b, M_CHUNK, v_w), lambda c: (0, c, 5)),
                  pl.BlockSpec((b, M_CHUNK, LANES_V7X), lambda c: (0, c, gate_blk)),
                  pl.BlockSpec((b, None, 2 * H, M_CHUNK), lambda c: (0, c, 0, 0)),
                  pl.BlockSpec((M_CONV, 2 * qk_w), lambda c: (0, 0)),
                  pl.BlockSpec((1, 2 * qk_w), lambda c: (0, 0)),
                  pl.BlockSpec((1, 2 * H), lambda c: (0, 0)),
                  pl.BlockSpec((2 * H, 1), lambda c: (0, 0)),
                  pl.BlockSpec((1, v_w), lambda c: (0, 0))],
        out_specs=pl.BlockSpec((b, M_CHUNK, v_w), lambda c: (0, c, 0)),
        scratch_shapes=[pltpu.VMEM((b, 8 + M_CHUNK, 2 * qk_w), F32),
                        pltpu.VMEM((b * H, M_DK, M_DV), F32),
                        pltpu.VMEM((b * H, 1, M_DK), F32),
                        pltpu.VMEM((b * H, 1, LANES_V7X), F32)],
        compiler_params=_cparams(("arbitrary",)),
        name="mlstm",
    )(z3, z3, z3, z3, z3, gates_r, conv_w, conv_b.reshape(1, -1), bias.reshape(1, -1), bias.reshape(-1, 1),
      gain.reshape(1, v_w))


def _mix_ffn_ple_kernel(*refs, n_mix, tm, seq, tf):
    hp_ref, h_ref = refs[0], refs[1]
    mix_refs = refs[2:2 + 2 * n_mix]
    (wo_ref, g_ref, wa_ref, wb_ref, cw_ref, cb_ref, wd_ref, p_ref, wp_ref, ng_ref, gg_ref, wg_ref,
     o_ref, xn_ref, a_ref) = refs[2 + 2 * n_mix:]
    i = pl.program_id(0)
    halo = 8
    g = g_ref[...]
    m_prev = jnp.concatenate([mix_refs[2 * k][...] for k in range(n_mix)], axis=1)
    m_tile = jnp.concatenate([mix_refs[2 * k + 1][...] for k in range(n_mix)], axis=1)
    x = h_ref[...] + _dot(m_tile, wo_ref[...])
    xp = hp_ref[...] + _dot(m_prev, wo_ref[...])[BF16_ROWS - halo:, :]
    xn_ref[halo:halo + tm, :] = _rms(x, g).astype(BF16)
    keep = ((i * tm) % seq != 0).astype(F32)
    xn_ref[0:halo, :] = (_rms(xp, g) * keep).astype(BF16)
    n_chunks = D_FF // tf

    def up_proj(c):
        cs = slice(c * tf, (c + 1) * tf)
        a_ref[c % 2] = _dot(xn_ref[...], wa_ref[:, cs])
        return _dot(xn_ref[halo:halo + tm, :], wb_ref[:, cs])

    h2 = x
    bgate = up_proj(0)
    for c in range(n_chunks):
        cs = slice(c * tf, (c + 1) * tf)
        bgate_next = up_proj(c + 1) if c + 1 < n_chunks else None
        conv = cb_ref[:, cs]
        for t in range(FFN_CONV):
            conv = conv + a_ref[c % 2, pl.ds(halo - (FFN_CONV - 1) + t, tm), :] * cw_ref[t:t + 1, cs]
        act = (_gelu_tanh(conv) * bgate).astype(BF16)
        h2 = h2 + _dot(act, wd_ref[cs, :])
        bgate = bgate_next
    e = _rms(_dot(p_ref[...].astype(BF16), wp_ref[...]), ng_ref[...])
    gate = jax.nn.sigmoid(_dot(_rms(h2, gg_ref[...]).astype(BF16), wg_ref[...]))
    o_ref[...] = h2 + gate * e


def mix_ffn_ple(h, mix, li, g, w_up, conv_w, conv_b, w_down, p, wp, norm_g, gate_norm_g, wg, seq, tm=512, tf=256):
    mix_outs, w_out = mix
    n, d = h.shape
    pd = p.shape[-1]
    hb = tm // 8
    const = lambda i: (0, 0)
    layer = lambda i: (li, 0, 0)
    prev8 = lambda i: (jnp.maximum(i * hb - 1, 0), 0)
    resident = dict(pipeline_mode=pl.Buffered(1))
    prev16 = lambda i: (jnp.maximum(i * (tm // BF16_ROWS) - 1, 0), 0)
    mix_specs, mix_args = [], []
    for m in mix_outs:
        mix_specs += [pl.BlockSpec((BF16_ROWS, m.shape[1]), prev16),
                      pl.BlockSpec((tm, m.shape[1]), lambda i: (i, 0))]
        mix_args += [m, m]
    mix_specs.append(pl.BlockSpec(w_out.shape, const, **resident))
    mix_args.append(w_out)
    return pl.pallas_call(
        functools.partial(_mix_ffn_ple_kernel, n_mix=len(mix_outs), tm=tm, seq=seq, tf=tf),
        out_shape=jax.ShapeDtypeStruct((n, d), F32),
        grid=(n // tm,),
        in_specs=[pl.BlockSpec((8, d), prev8),
                  pl.BlockSpec((tm, d), lambda i: (i, 0))] + mix_specs + [
                  pl.BlockSpec((None, 1, d), layer),
                  pl.BlockSpec((None, d, D_FF), lambda i: (li, 0, 0), **resident),
                  pl.BlockSpec((None, d, D_FF), lambda i: (li, 0, 1), **resident),
                  pl.BlockSpec((None, FFN_CONV, D_FF), layer),
                  pl.BlockSpec((None, 1, D_FF), layer),
                  pl.BlockSpec((None, D_FF, d), layer, **resident),
                  pl.BlockSpec((None, tm, pd), lambda i: (li, i, 0)),
                  pl.BlockSpec((None, pd, d), layer, **resident),
                  pl.BlockSpec((None, 1, d), layer),
                  pl.BlockSpec((None, 1, d), layer),
                  pl.BlockSpec((None, d, d), layer, **resident)],
        out_specs=pl.BlockSpec((tm, d), lambda i: (i, 0)),
        scratch_shapes=[pltpu.VMEM((8 + tm, d), BF16),
                        pltpu.VMEM((2, 8 + tm, tf), F32)],
        compiler_params=_cparams(("parallel",), VMEM_LIMIT_V7X),
        name="mix_ffn_ple",
    )(h, h, *mix_args, g[:, None, :], w_up, w_up, conv_w, conv_b[:, None, :], w_down,
      p, wp, norm_g[:, None, :], gate_norm_g[:, None, :], wg)


def _group_rms(x, g):
    lane = lax.broadcasted_iota(jnp.int32, x.shape, 1)
    x2 = x * x
    ms = jnp.zeros_like(x)
    for grp in range(N_KV_GROUPS):
        in_grp = (lane >= grp * HEAD_DIM) & (lane < (grp + 1) * HEAD_DIM)
        tot = jnp.sum(jnp.where(in_grp, x2, 0.0), axis=-1, keepdims=True)
        ms = jnp.where(in_grp, tot * (1.0 / HEAD_DIM), ms)
    return x * lax.rsqrt(ms + EPS) * g


def _kv_prep_kernel(c_ref, s_ref, w_ref, gs_ref, gw_ref, kc_ref, vc_ref, ks_ref, vs_ref, kw_ref, vw_ref):
    gw = N_KV_GROUPS * HEAD_DIM
    cc = c_ref[...]
    ss = s_ref[...]
    ww = w_ref[...]
    kc_ref[...] = cc[:, 0:gw].astype(BF16)
    vc_ref[...] = cc[:, gw:2 * gw].astype(BF16)
    ks = _group_rms(ss[:, 0:gw], gs_ref[...]).astype(BF16)
    kw = _group_rms(ww[:, 0:gw], gw_ref[...]).astype(BF16)
    vst = ss[:, gw:2 * gw].T
    vwt = ww[:, gw:2 * gw].T
    for g in range(N_KV_GROUPS):
        lanes = slice(g * HEAD_DIM, (g + 1) * HEAD_DIM)
        ks_ref[g] = ks[:, lanes]
        kw_ref[g] = kw[:, lanes]
        vs_ref[g] = vst[lanes, :].astype(BF16)
        for u in range(SEL_TILE // KEY_TILE):
            vw_ref[g, u] = vwt[lanes, u * KEY_TILE:(u + 1) * KEY_TILE].astype(BF16)


def kv_prep(z, k_g, b, t):
    n = z.shape[0]
    G = N_KV_GROUPS
    gw = G * HEAD_DIM
    base = N_HEADS * HEAD_DIM // (2 * gw)
    tm = SEL_TILE
    nst = t // tm
    sub = SEL_TILE // KEY_TILE
    row = lambda i, j: i * nst + j
    flat = jax.ShapeDtypeStruct((n, gw), BF16)
    keys = jax.ShapeDtypeStruct((b, G, t, HEAD_DIM), BF16)
    flat_spec = pl.BlockSpec((tm, gw), lambda i, j: (row(i, j), 0))
    key_spec = pl.BlockSpec((None, G, tm, HEAD_DIM), lambda i, j: (i, 0, j, 0))
    return pl.pallas_call(
        _kv_prep_kernel,
        out_shape=(flat, flat, keys, jax.ShapeDtypeStruct((b, G, nst, HEAD_DIM, tm), BF16),
                   keys, jax.ShapeDtypeStruct((b, G, nst * sub, HEAD_DIM, KEY_TILE), BF16)),
        grid=(b, nst),
        in_specs=[pl.BlockSpec((tm, 2 * gw), lambda i, j: (row(i, j), base)),
                  pl.BlockSpec((tm, 2 * gw), lambda i, j: (row(i, j), base + 1)),
                  pl.BlockSpec((tm, 2 * gw), lambda i, j: (row(i, j), base + 2)),
                  pl.BlockSpec((1, gw), lambda i, j: (0, 0)),
                  pl.BlockSpec((1, gw), lambda i, j: (0, 0))],
        out_specs=(flat_spec, flat_spec, key_spec,
                   pl.BlockSpec((None, G, None, HEAD_DIM, tm), lambda i, j: (i, 0, j, 0, 0)),
                   key_spec,
                   pl.BlockSpec((None, G, sub, HEAD_DIM, KEY_TILE), lambda i, j: (i, 0, j, 0, 0))),
        compiler_params=_cparams(("parallel", "parallel")),
        name="kv_prep",
    )(z, z, z, jnp.tile(k_g[1], G).reshape(1, gw), jnp.tile(k_g[2], G).reshape(1, gw))


def _compress_kernel(x_ref, pos_ref, w1_ref, w2_ref, g_ref, o_ref, *, normalize):
    half = w1_ref.shape[0] // 2
    x = x_ref[...]
    u = _dot(x, w1_ref[0:half, :])
    v = _dot(x, w1_ref[half:2 * half, :])
    rows = u.shape[0]
    posc = _dot(pos_ref[...], w1_ref[...])[0:1, :]
    hid = u + pltpu.roll(v, rows - 1, 0) + posc
    out = _dot(jax.nn.gelu(hid).astype(BF16), w2_ref[...])
    if normalize:
        out = _rms(out, g_ref[...])
    o_ref[...] = out.astype(o_ref.dtype)


def compress(x, pos, w1, w2, g, normalize, ncb):
    n, kdim = x.shape
    posf = jnp.broadcast_to(pos.reshape(1, -1), (8, 2 * kdim)).astype(BF16)
    return pl.pallas_call(
        functools.partial(_compress_kernel, normalize=normalize),
        out_shape=jax.ShapeDtypeStruct((n, HEAD_DIM), BF16),
        grid=(n // ncb,),
        in_specs=[pl.BlockSpec((ncb, kdim), lambda i: (i, 0)),
                  pl.BlockSpec((8, 2 * kdim), lambda i: (0, 0)),
                  pl.BlockSpec((2 * kdim, CMP_HIDDEN), lambda i: (0, 0)),
                  pl.BlockSpec((CMP_HIDDEN, HEAD_DIM), lambda i: (0, 0)),
                  pl.BlockSpec((1, HEAD_DIM), lambda i: (0, 0))],
        out_specs=pl.BlockSpec((ncb, HEAD_DIM), lambda i: (i, 0)),
        compiler_params=_cparams(("parallel",)),
        name="compress",
    )(x, posf, w1.astype(BF16), w2.astype(BF16), g.reshape(1, HEAD_DIM))


def _nsa_kernel(bnd_ref, *refs):
    safe = bnd_ref[0, 0] <= MAX_SAFE_SCORE_BOUND

    @pl.when(safe)
    def _():
        _nsa_body(bnd_ref, *refs, bounded=True)

    @pl.when(jnp.logical_not(safe))
    def _():
        _nsa_body(bnd_ref, *refs, bounded=False)


def _nsa_body(bnd_ref, zq_ref, gt_ref, gb_ref, qg_ref, ovt_ref, kc_ref, vct_ref, ks_ref, vst_ref, kw_ref,
              vwt_ref, o_ref, q_scr, sel_scr, m_scr, l_scr, acc_scr, oc_scr, *, bounded):
    grp = pl.program_id(1)
    qb = pl.program_id(2)
    QB = Q_BLOCK
    ncp = kc_ref.shape[0]
    ns = ovt_ref.shape[0]
    tpos = qb * QB + lax.broadcasted_iota(jnp.int32, (1, QB), 1)

    def lane_tile(x):
        return jnp.concatenate([x] * HG, axis=1)

    zt = zq_ref[...].T
    qg = qg_ref[...] * (HEAD_DIM ** -0.5 * LOG2E)
    heads = []
    for h in range(HG):
        xh = zt[h * HEAD_DIM:(h + 1) * HEAD_DIM, :]
        ms = jnp.mean(xh * xh, axis=0, keepdims=True)
        heads.append(xh * lax.rsqrt(ms + EPS) * qg)
    q_scr[...] = jnp.concatenate(heads, axis=1).astype(BF16)
    q = q_scr[...]

    gates = jax.nn.sigmoid(gt_ref[...].T + gb_ref[...])
    per_grp = HG * 3
    gsel = gates[0:per_grp, :]
    for g2 in range(1, N_KV_GROUPS):
        gsel = jnp.where(grp == g2, gates[g2 * per_grp:(g2 + 1) * per_grp, :], gsel)

    def gate_row(c):
        return jnp.concatenate([gsel[3 * h + c:3 * h + c + 1, :] for h in range(HG)], axis=1)

    cmp_end = lax.broadcasted_iota(jnp.int32, (ncp, 1), 0) * CMP_STRIDE + (CMP_BLOCK - 1)
    keep = -bnd_ref[0, 0] if bounded else 0.0
    cbias = jnp.where(cmp_end <= tpos, keep, NEG)
    s = _dot(kc_ref[...], q) + lane_tile(cbias)
    e = jnp.exp2(s) if bounded else jnp.exp2(s - jnp.max(s, axis=0, keepdims=True))
    inv = jnp.where(lane_tile(tpos) >= CMP_BLOCK - 1, 1.0 / jnp.sum(e, axis=0, keepdims=True), 0.0)
    p = e * inv
    ocmp = _dot(vct_ref[...], p.astype(BF16))
    psum = p[:, 0:QB]
    for h in range(1, HG):
        psum = psum + p[:, h * QB:(h + 1) * QB]

    n_win = (WINDOW + QB) // KEY_TILE
    first_tile = qb * (QB // KEY_TILE) - WINDOW // KEY_TILE
    win_sub = lax.broadcasted_iota(jnp.int32, (n_win * KEY_TILE, 1), 0)
    tiles = [jnp.maximum(first_tile + u, 0) for u in range(n_win)]
    kwin = jnp.concatenate([kw_ref[j] for j in tiles], axis=0)
    vwin = jnp.concatenate([vwt_ref[j] for j in tiles], axis=1)
    kpos = first_tile * KEY_TILE + win_sub
    wbias = jnp.where((kpos <= tpos) & (kpos > tpos - WINDOW) & (kpos >= 0), keep, NEG)
    sw = _dot(kwin, q) + lane_tile(wbias)
    pw = jnp.exp2(sw) if bounded else jnp.exp2(sw - jnp.max(sw, axis=0, keepdims=True))
    lw = jnp.sum(pw, axis=0, keepdims=True)
    oc_scr[...] = gate_row(0) * ocmp + (gate_row(2) / lw) * _dot(vwin, pw.astype(BF16))

    p_hi = psum.astype(BF16)
    p_lo = (psum - p_hi.astype(F32)).astype(BF16)
    ovt = ovt_ref[...]
    imp = _dot(ovt, p_hi) + _dot(ovt, p_lo)
    blk = lax.broadcasted_iota(jnp.int32, (ns, 1), 0)
    blk_f = blk.astype(F32)
    cur = jnp.right_shift(tpos, SLC_BLOCK.bit_length() - 1)
    forced = (blk == 0) | (blk == cur) | (blk == cur - 1)
    bvalid = blk <= cur
    score = jnp.where(forced, BIG, jnp.where(bvalid, imp, NEG))
    sel = jnp.zeros((ns, QB), F32)
    for _ in range(min(N_SELECT, ns)):
        mx = jnp.max(score, axis=0, keepdims=True)
        first = jnp.min(jnp.where(score == mx, blk_f, float(ns)), axis=0, keepdims=True)
        pick = blk_f == first
        sel = jnp.where(pick, 1.0, sel)
        score = jnp.where(pick, -jnp.inf, score)
    sel_scr[...] = jnp.where(bvalid, sel, 0.0)

    m_scr[...] = jnp.full(m_scr.shape, NEG, F32)
    l_scr[...] = jnp.zeros(l_scr.shape, F32)
    acc_scr[...] = jnp.zeros(acc_scr.shape, F32)
    blocks_per_tile = SEL_TILE // SLC_BLOCK
    sel_sub = lax.broadcasted_iota(jnp.int32, (SEL_TILE, 1), 0)

    def sel_body(j, carry):
        rows = [jnp.broadcast_to(sel_scr[pl.ds(j * blocks_per_tile + bi, 1), :], (SLC_BLOCK, QB))
                for bi in range(blocks_per_tile)]
        chosen = jnp.concatenate(rows, axis=0)
        kpos = j * SEL_TILE + sel_sub
        bias = jnp.where((chosen > 0.5) & (kpos <= tpos), keep, NEG)
        st = _dot(ks_ref[j], q_scr[...]) + lane_tile(bias)
        if bounded:
            pt = jnp.exp2(st)
            l_scr[...] += jnp.sum(pt, axis=0, keepdims=True)
            acc_scr[...] += _dot(vst_ref[j], pt.astype(BF16))
        else:
            m_old = m_scr[...]
            m_new = jnp.maximum(m_old, jnp.max(st, axis=0, keepdims=True))
            alpha = jnp.exp2(m_old - m_new)
            pt = jnp.exp2(st - m_new)
            l_scr[...] = alpha * l_scr[...] + jnp.sum(pt, axis=0, keepdims=True)
            acc_scr[...] = alpha * acc_scr[...] + _dot(vst_ref[j], pt.astype(BF16))
            m_scr[...] = m_new
        return carry

    lax.fori_loop(0, qb // (SEL_TILE // QB) + 1, sel_body, 0)
    ot = oc_scr[...] + (gate_row(1) / l_scr[...]) * acc_scr[...]
    stacked = jnp.concatenate([ot[:, h * QB:(h + 1) * QB] for h in range(HG)], axis=0)
    o_ref[...] = stacked.T.astype(o_ref.dtype)


def nsa_attention(bound, z3, gate_b, q_g, overlap_t, kc, vct, ks, vst, kw, vwt):
    b, t, _ = z3.shape
    nq = t // Q_BLOCK
    nt = t // KEY_TILE
    qw = HG * HEAD_DIM
    ns, ncp = overlap_t.shape
    gate_blk = (NSA_COLS_PAD // LANES_V7X) - 1
    gb = jnp.zeros((LANES_V7X, 1), F32).at[:N_HEADS * 3, 0].set(gate_b)
    full5 = lambda i, g, q: (i, g, 0, 0, 0)
    lanes = HG * Q_BLOCK
    return pl.pallas_call(
        _nsa_kernel,
        out_shape=jax.ShapeDtypeStruct((b, t, N_HEADS * HEAD_DIM), BF16),
        grid=(b, N_KV_GROUPS, nq),
        in_specs=[pl.BlockSpec(memory_space=pltpu.SMEM),
                  pl.BlockSpec((None, Q_BLOCK, qw), lambda i, g, q: (i, q, g)),
                  pl.BlockSpec((None, Q_BLOCK, LANES_V7X), lambda i, g, q: (i, q, gate_blk)),
                  pl.BlockSpec((LANES_V7X, 1), lambda i, g, q: (0, 0)),
                  pl.BlockSpec((HEAD_DIM, 1), lambda i, g, q: (0, 0)),
                  pl.BlockSpec((ns, ncp), lambda i, g, q: (0, 0)),
                  pl.BlockSpec((None, None, ncp, HEAD_DIM), lambda i, g, q: (i, g, 0, 0)),
                  pl.BlockSpec((None, None, HEAD_DIM, ncp), lambda i, g, q: (i, g, 0, 0)),
                  pl.BlockSpec((None, None, t // SEL_TILE, SEL_TILE, HEAD_DIM), full5),
                  pl.BlockSpec((None, None, t // SEL_TILE, HEAD_DIM, SEL_TILE), full5),
                  pl.BlockSpec((None, None, nt, KEY_TILE, HEAD_DIM), full5),
                  pl.BlockSpec((None, None, nt, HEAD_DIM, KEY_TILE), full5)],
        out_specs=pl.BlockSpec((None, Q_BLOCK, qw), lambda i, g, q: (i, q, g)),
        scratch_shapes=[pltpu.VMEM((HEAD_DIM, lanes), BF16),
                        pltpu.VMEM((ns, Q_BLOCK), F32),
                        pltpu.VMEM((1, lanes), F32),
                        pltpu.VMEM((1, lanes), F32),
                        pltpu.VMEM((HEAD_DIM, lanes), F32),
                        pltpu.VMEM((HEAD_DIM, lanes), F32)],
        compiler_params=_cparams(("parallel", "parallel", "arbitrary"), VMEM_LIMIT_V7X),
        name="nsa_attention",
    )(bound.reshape(1, 1), z3, z3, gb, q_g.reshape(HEAD_DIM, 1), overlap_t, kc, vct, ks, vst, kw, vwt)


def _overlap_matrix_t(t):
    ncp = t // CMP_STRIDE
    ns = t // SLC_BLOCK
    c_start = np.arange(ncp) * CMP_STRIDE
    sj = np.arange(ns)
    ov = ((c_start[None, :] < (sj[:, None] + 1) * SLC_BLOCK)
          & (c_start[None, :] + CMP_BLOCK > sj[:, None] * SLC_BLOCK)
          & (c_start[None, :] + CMP_BLOCK <= t))
    return jnp.asarray(ov, dtype=BF16)


def ab_layer(h, b, t, norm_g, w_in, conv_w, conv_b, ret_g, ig_b, fg_b, m_g, w_out):
    n = b * t
    w_in_p = jnp.pad(w_in, ((0, 0), (0, AB_COLS_PAD - AB_COLS))).astype(BF16)
    z = norm_matmul(h, norm_g, w_in_p)
    z3 = z.reshape(b, t, AB_COLS_PAD)
    cos_t, sin_t = rope_tables(t)
    ret = retention(z3, cos_t, sin_t, ret_g)
    g0 = AB_COLS - 2 * M_HEADS
    gates_r = z3[:, :, g0:AB_COLS].reshape(b, t // M_CHUNK, M_CHUNK, 2 * M_HEADS).transpose(0, 1, 3, 2)
    ml = mlstm(z3, gates_r, conv_w, conv_b, ig_b, fg_b, m_g)
    w_out_b = w_out.astype(BF16)
    rw = R_HEADS * R_DV
    return [ret.reshape(n, rw), ml.reshape(n, -1)], w_out_b


def nsa_layer(h, b, t, norm_g, w_in, q_g, k_g, pos_k, pos_v, w1k, w2k, w1v, w2v, gate_b, w_out):
    n = b * t
    G = N_KV_GROUPS
    w_in_p = jnp.pad(w_in, ((0, 0), (0, NSA_COLS_PAD - NSA_COLS))).astype(BF16)
    z = norm_matmul(h, norm_g, w_in_p)
    kc_in, vc_in, ks, vst, kw, vwt = kv_prep(z, k_g, b, t)
    ncb = t // CMP_STRIDE

    def cmp_rows(x):
        return x.reshape(b, t, G, HEAD_DIM).transpose(0, 2, 1, 3).reshape(b * G * ncb, CMP_STRIDE * HEAD_DIM)

    kc = compress(cmp_rows(kc_in), pos_k, w1k, w2k, k_g[0], True, ncb)
    vc = compress(cmp_rows(vc_in), pos_v, w1v, w2v, k_g[0], False, ncb)
    kc = kc.reshape(b, G, ncb, HEAD_DIM)
    vct = vc.reshape(b, G, ncb, HEAD_DIM).transpose(0, 1, 3, 2)

    def key_tiles(x, kt):
        return x.reshape(b, G, t // kt, kt, HEAD_DIM)

    bound = 1.02 * LOG2E * math.sqrt(HEAD_DIM) * jnp.max(jnp.abs(q_g)) * jnp.max(jnp.abs(k_g))
    args = (bound, z.reshape(b, t, NSA_COLS_PAD), gate_b, q_g, _overlap_matrix_t(t), kc, vct,
            key_tiles(ks, SEL_TILE), vst, key_tiles(kw, KEY_TILE), vwt)
    o = nsa_attention(*args)
    return [o.reshape(n, -1)], w_out.astype(BF16)


def kernel(x, p, ab_norm_g, ab_w_in, ab_conv_w, ab_conv_b, ab_ret_norm_g, ab_ig_b, ab_fg_b, ab_m_norm_g, ab_w_out, nsa_norm_g, nsa_w_in, nsa_q_norm_g, nsa_k_norm_g, nsa_cmp_pos_k, nsa_cmp_pos_v, nsa_cmp_w1k, nsa_cmp_w2k, nsa_cmp_w1v, nsa_cmp_w2v, nsa_gate_b, nsa_w_out, ffn_norm_g, ffn_w_up, ffn_conv_w, ffn_conv_b, ffn_w_down, ple_w, ple_norm_g, ple_gate_norm_g, ple_w_gate):
    b, t, d = x.shape
    n = b * t
    depth = p.shape[0]
    h = x.reshape(n, d)
    p2 = p.reshape(depth, n, -1)
    ffn_w_up_b, ffn_w_down_b = ffn_w_up.astype(BF16), ffn_w_down.astype(BF16)
    ple_w_b, ple_w_gate_b = ple_w.astype(BF16), ple_w_gate.astype(BF16)
    for i in range(depth):
        j = i // 2
        if i % 2 == 0:
            mix = ab_layer(h, b, t, ab_norm_g[j], ab_w_in[j], ab_conv_w[j], ab_conv_b[j], ab_ret_norm_g[j],
                           ab_ig_b[j], ab_fg_b[j], ab_m_norm_g[j], ab_w_out[j])
        else:
            mix = nsa_layer(h, b, t, nsa_norm_g[j], nsa_w_in[j], nsa_q_norm_g[j], nsa_k_norm_g[j],
                            nsa_cmp_pos_k[j], nsa_cmp_pos_v[j], nsa_cmp_w1k[j], nsa_cmp_w2k[j],
                            nsa_cmp_w1v[j], nsa_cmp_w2v[j], nsa_gate_b[j], nsa_w_out[j])
        h = mix_ffn_ple(h, mix, i, ffn_norm_g, ffn_w_up_b, ffn_conv_w, ffn_conv_b, ffn_w_down_b, p2, ple_w_b,
                        ple_norm_g, ple_gate_norm_g, ple_w_gate_b, t)
    return h.reshape(b, t, d)
```

```python
import functools
import math

import numpy as np
import jax
import jax.numpy as jnp
from jax import lax
from jax.experimental import pallas as pl
from jax.experimental.pallas import tpu as pltpu

F32 = jnp.float32
BF16 = jnp.bfloat16

LANES_V7X = 128
BF16_ROWS = 16
VMEM_LIMIT_V7X = 56 * 1024 * 1024

D_MODEL = 1024
PLE_DIM = 256
R_HEADS, R_DK, R_DV, R_CHUNK = 4, 64, 128, 128
ROPE_BASE = 10000.0
M_HEADS, M_DK, M_DV, M_CHUNK, M_CONV = 4, 64, 128, 64, 4
AB_SIZES = (R_HEADS * R_DK, R_HEADS * R_DK, R_HEADS * R_DV, R_HEADS * R_DV,
            M_HEADS * M_DK, M_HEADS * M_DK, M_HEADS * M_DV, M_HEADS * M_DV, M_HEADS, M_HEADS)
AB_COLS = sum(AB_SIZES)
AB_COLS_PAD = 3200
N_HEADS, N_KV_GROUPS, HEAD_DIM = 16, 2, 64
HG = N_HEADS // N_KV_GROUPS
CMP_BLOCK, CMP_STRIDE, CMP_HIDDEN = 32, 16, 256
SLC_BLOCK, N_SELECT, WINDOW = 64, 16, 512
Q_BLOCK = 256
NSA_COLS = N_HEADS * HEAD_DIM + 6 * N_KV_GROUPS * HEAD_DIM + N_HEADS * 3
NSA_COLS_PAD = 1920
D_FF = 2816
FFN_CONV = 3
NEG = -1e30
BIG = 1e30
EPS = 1e-6
KEY_TILE = 128
SEL_TILE = 512
LOG2E = math.log2(math.e)
MAX_SAFE_SCORE_BOUND = 56.0


def _cparams(sem, vmem=None):
    return pltpu.CompilerParams(dimension_semantics=sem, vmem_limit_bytes=vmem)


def _rms(x, g):
    ms = jnp.mean(x * x, axis=-1, keepdims=True)
    return x * lax.rsqrt(ms + EPS) * g


def _dot(a, b):
    return jnp.dot(a, b, preferred_element_type=F32)


def _dot_nt(a, b):
    return lax.dot_general(a, b, (((1,), (1,)), ((), ())), preferred_element_type=F32)


def _dot_f32(a, b):
    return jnp.dot(a, b, preferred_element_type=F32, precision=lax.Precision.HIGHEST)


def _gelu_tanh(x):
    k1 = -2.0 * math.sqrt(2.0 / math.pi) * LOG2E
    return x / (1.0 + jnp.exp2(x * (k1 + (k1 * 0.044715) * (x * x))))


def _norm_matmul_kernel(x_ref, g_ref, w_ref, o_ref):
    xn = _rms(x_ref[...], g_ref[...]).astype(BF16)
    o_ref[...] = _dot(xn, w_ref[...]).astype(o_ref.dtype)


def norm_matmul(x, g, w, tm=512, out_dtype=F32):
    n, d = x.shape
    nc = w.shape[1]
    return pl.pallas_call(
        _norm_matmul_kernel,
        out_shape=jax.ShapeDtypeStruct((n, nc), out_dtype),
        grid=(n // tm,),
        in_specs=[pl.BlockSpec((tm, d), lambda i: (i, 0)),
                  pl.BlockSpec((1, d), lambda i: (0, 0)),
                  pl.BlockSpec((d, nc), lambda i: (0, 0))],
        out_specs=pl.BlockSpec((tm, nc), lambda i: (i, 0)),
        compiler_params=_cparams(("parallel",), VMEM_LIMIT_V7X),
        name="norm_matmul",
    )(x, g.reshape(1, d), w)


def _rope_table_kernel(inv_ref, cos_ref, sin_ref):
    c = pl.program_id(0)
    rows, width = cos_ref.shape
    pos = (c * rows + lax.broadcasted_iota(jnp.int32, (rows, width), 0)).astype(F32)
    lane = lax.broadcasted_iota(jnp.int32, (rows, width), 1)
    ang = pos * inv_ref[...]
    cos_ref[...] = jnp.cos(ang)
    sn = jnp.sin(ang)
    sin_ref[...] = jnp.where(lane % R_DK < R_DK // 2, -sn, sn)


def rope_tables(t):
    half = R_DK // 2
    inv = ROPE_BASE ** (-jnp.arange(half, dtype=F32) / half)
    inv = jnp.tile(inv, 2 * R_HEADS).reshape(1, R_HEADS * R_DK)
    width = R_HEADS * R_DK
    shp = jax.ShapeDtypeStruct((t, width), F32)
    return pl.pallas_call(
        _rope_table_kernel,
        out_shape=(shp, shp),
        grid=(t // R_CHUNK,),
        in_specs=[pl.BlockSpec((1, width), lambda c: (0, 0))],
        out_specs=(pl.BlockSpec((R_CHUNK, width), lambda c: (c, 0)),
                   pl.BlockSpec((R_CHUNK, width), lambda c: (c, 0))),
        compiler_params=_cparams(("parallel",)),
        name="rope_tables",
    )(inv)


def _retention_kernel(cos_ref, sin_ref, q_ref, k_ref, v_ref, g_ref, gain_ref, o_ref, r_ref):
    c = pl.program_id(0)
    nb = q_ref.shape[0]
    L = R_CHUNK

    @pl.when(c == 0)
    def _():
        r_ref[...] = jnp.zeros_like(r_ref)

    cos = cos_ref[...]
    sin = sin_ref[...]
    lane = lax.broadcasted_iota(jnp.int32, cos.shape, 1)
    first_half = lane % R_DK < R_DK // 2
    width = R_HEADS * R_DK

    def rot(x):
        swapped = jnp.where(first_half, pltpu.roll(x, width - R_DK // 2, 1), pltpu.roll(x, R_DK // 2, 1))
        return x * cos + swapped * sin

    gain = gain_ref[...]
    ri = lax.broadcasted_iota(jnp.int32, (L, L), 0)
    ci = lax.broadcasted_iota(jnp.int32, (L, L), 1)
    diff = (ri - ci).astype(F32)
    causal = ri >= ci
    idx = lax.broadcasted_iota(jnp.int32, (L, 1), 0).astype(F32)
    decays = []
    for h in range(R_HEADS):
        log_g = math.log1p(-2.0 ** (-5.0 - h))
        decays.append(dict(
            dmask=jnp.where(causal, jnp.exp(jnp.where(causal, diff, 0.0) * log_g), 0.0),
            q_dec=jnp.exp((idx + 1.0) * log_g), k_dec=jnp.exp((L - 1.0 - idx) * log_g),
            c_dec=math.exp(L * log_g)))

    chains = []
    for bi in range(nb):
        q = rot(q_ref[bi])
        k = rot(k_ref[bi]) * (R_DK ** -0.5)
        for h in range(R_HEADS):
            chains.append(dict(bi=bi, h=h, u=bi * R_HEADS + h,
                               qb=q[:, h * R_DK:(h + 1) * R_DK].astype(BF16),
                               kh=k[:, h * R_DK:(h + 1) * R_DK]))

    for ch in chains:
        ch['rstate'] = r_ref[ch['u']]
        ch['s'] = _dot_nt(ch['qb'], ch['kh'].astype(BF16)) * decays[ch['h']]['dmask']
        ch['qr'] = _dot(ch['qb'], ch['rstate'].astype(BF16))

    for ch in chains:
        d = decays[ch['h']]
        vh = v_ref[ch['bi'], :, ch['h'] * R_DV:(ch['h'] + 1) * R_DV].astype(BF16)
        ch['o'] = _dot(ch['s'].astype(BF16), vh) + ch['qr'] * d['q_dec']
        kd = (ch['kh'] * d['k_dec']).T.astype(BF16)
        r_ref[ch['u']] = d['c_dec'] * ch['rstate'] + _dot(kd, vh)

    for ch in chains:
        sl = slice(ch['h'] * R_DV, (ch['h'] + 1) * R_DV)
        gh = g_ref[ch['bi'], :, sl]
        o_ref[ch['bi'], :, sl] = (_rms(ch['o'], gain[:, sl]) * (gh * jax.nn.sigmoid(gh))).astype(o_ref.dtype)


def retention(z3, cos_t, sin_t, gain):
    b, t, _ = z3.shape
    nc = t // R_CHUNK
    qk_w = R_HEADS * R_DK
    v_w = R_HEADS * R_DV
    return pl.pallas_call(
        _retention_kernel,
        out_shape=jax.ShapeDtypeStruct((b, t, v_w), BF16),
        grid=(nc,),
        in_specs=[pl.BlockSpec((R_CHUNK, qk_w), lambda c: (c, 0)),
                  pl.BlockSpec((R_CHUNK, qk_w), lambda c: (c, 0)),
                  pl.BlockSpec((b, R_CHUNK, qk_w), lambda c: (0, c, 0)),
                  pl.BlockSpec((b, R_CHUNK, qk_w), lambda c: (0, c, 1)),
                  pl.BlockSpec((b, R_CHUNK, v_w), lambda c: (0, c, 1)),
                  pl.BlockSpec((b, R_CHUNK, v_w), lambda c: (0, c, 2)),
                  pl.BlockSpec((1, v_w), lambda c: (0, 0))],
        out_specs=pl.BlockSpec((b, R_CHUNK, v_w), lambda c: (0, c, 0)),
        scratch_shapes=[pltpu.VMEM((b * R_HEADS, R_DK, R_DV), F32)],
        compiler_params=_cparams(("arbitrary",)),
        name="retention",
    )(cos_t, sin_t, z3, z3, z3, z3, gain.reshape(1, v_w))


def _mlstm_kernel(q_ref, k_ref, v_ref, og_ref, gc_ref, gr_ref, cw_ref, cb_ref, bc_ref, br_ref, gain_ref,
                  o_ref, xbuf, c_ref, n_ref, m_ref):
    c = pl.program_id(0)
    nb = q_ref.shape[0]
    L = M_CHUNK
    H = M_HEADS
    qk_w = H * M_DK
    halo = 8

    @pl.when(c == 0)
    def _():
        xbuf[:, 0:halo, :] = jnp.zeros((nb, halo, 2 * qk_w), F32)
        c_ref[...] = jnp.zeros_like(c_ref)
        n_ref[...] = jnp.zeros_like(n_ref)
        m_ref[...] = jnp.zeros_like(m_ref)

    gain = gain_ref[...]
    ri = lax.broadcasted_iota(jnp.int32, (L, L), 0)
    ci = lax.broadcasted_iota(jnp.int32, (L, L), 1)
    causal = ri >= ci
    tril = causal.astype(F32)
    triu = (ri <= ci).astype(F32)

    chains = []
    for bi in range(nb):
        xbuf[bi, halo:halo + L, 0:qk_w] = q_ref[bi]
        xbuf[bi, halo:halo + L, qk_w:2 * qk_w] = k_ref[bi]
        conv = cb_ref[...]
        for j in range(M_CONV):
            conv = conv + xbuf[bi, pl.ds(halo - (M_CONV - 1) + j, L), :] * cw_ref[j:j + 1, :]
        tail = xbuf[bi, L:L + halo, :]
        xbuf[bi, 0:halo, :] = tail
        act = conv * jax.nn.sigmoid(conv)
        q = act[:, 0:qk_w] * (M_DK ** -0.5)
        k = act[:, qk_w:2 * qk_w]
        gc = gc_ref[bi][:, 0:2 * H] + bc_ref[...]
        gr = gr_ref[bi] + br_ref[...]
        ig_c = gc[:, 0:H]
        ig_r = gr[0:H, :]
        b_c = _dot_f32(tril, jax.nn.log_sigmoid(gc[:, H:2 * H]))
        b_r = _dot_f32(jax.nn.log_sigmoid(gr[H:2 * H, :]), triu)
        for h in range(H):
            chains.append(dict(
                bi=bi, h=h, u=bi * H + h,
                qh=q[:, h * M_DK:(h + 1) * M_DK], kh=k[:, h * M_DK:(h + 1) * M_DK],
                bh=b_c[:, h:h + 1], brow=b_r[h:h + 1, :], irow=ig_r[h:h + 1, :], icol=ig_c[:, h:h + 1]))

    def stack(parts):
        return jnp.concatenate(parts, axis=0)

    def rows(x):
        return jnp.broadcast_to(x, (L, x.shape[1]))

    m_prev_u = [m_ref[ch['u']][:, 0:1] for ch in chains]
    b_last_u = [ch['bh'][L - 1:L, :] for ch in chains]
    bh = stack([ch['bh'] for ch in chains])
    icol = stack([ch['icol'] for ch in chains])
    brow = stack([rows(ch['brow']) for ch in chains])
    irow = stack([rows(ch['irow']) for ch in chains])
    m_prev = stack([rows(m) for m in m_prev_u])
    b_last = stack([rows(x) for x in b_last_u])
    causal_all = stack([causal] * len(chains))
    qs = stack([ch['qh'] for ch in chains])
    ks = stack([ch['kh'] for ch in chains])
    qb = qs.astype(BF16)
    kb = ks.astype(BF16)
    cstates = [c_ref[ch['u']] for ch in chains]
    nstates = [n_ref[ch['u']] for ch in chains]

    def chain_rows(x, i):
        return x[i * L:(i + 1) * L]

    s_raw = stack([_dot_nt(chain_rows(qb, i), chain_rows(kb, i)) for i in range(len(chains))])
    qc = stack([_dot(chain_rows(qb, i), cstates[i].astype(BF16)) for i in range(len(chains))])

    dlog = jnp.where(causal_all, bh - brow + irow, NEG)
    inter = bh + m_prev
    m_t = jnp.maximum(inter, jnp.max(dlog, axis=-1, keepdims=True))
    s = s_raw * jnp.exp(dlog - m_t)
    w_inter = jnp.exp(inter - m_t)
    wlog = b_last - bh + icol
    m_new_u = [jnp.maximum(b_last_u[i] + m_prev_u[i], jnp.max(chain_rows(wlog, i), axis=0, keepdims=True))
               for i in range(len(chains))]
    m_new = stack([rows(m) for m in m_new_u])
    wk = ks * jnp.exp(wlog - m_new)

    sb = s.astype(BF16)
    vhs = [v_ref[ch['bi'], :, ch['h'] * M_DV:(ch['h'] + 1) * M_DV].astype(BF16) for ch in chains]
    sv = stack([_dot(chain_rows(sb, i), vhs[i]) for i in range(len(chains))])
    kv = [_dot(chain_rows(wk, i).T.astype(BF16), vhs[i]) for i in range(len(chains))]

    qn = jnp.sum(qs * stack([rows(n) for n in nstates]), axis=-1, keepdims=True)
    den = jnp.sum(s, axis=-1, keepdims=True) + w_inter * qn
    hh = (sv + w_inter * qc) / jnp.maximum(jnp.abs(den), jnp.exp(-m_t))
    og = stack([og_ref[ch['bi'], :, ch['h'] * M_DV:(ch['h'] + 1) * M_DV] for ch in chains])
    gains = stack([rows(gain[:, ch['h'] * M_DV:(ch['h'] + 1) * M_DV]) for ch in chains])
    out = jax.nn.sigmoid(og) * _rms(hh, gains)

    for i, ch in enumerate(chains):
        u = ch['u']
        decay = jnp.exp(b_last_u[i] + m_prev_u[i] - m_new_u[i])
        c_ref[u] = decay * cstates[i] + kv[i]
        n_ref[u] = decay * nstates[i] + jnp.sum(chain_rows(wk, i), axis=0, keepdims=True)
        m_ref[u] = jnp.broadcast_to(m_new_u[i], (1, LANES_V7X))
        o_ref[ch['bi'], :, ch['h'] * M_DV:(ch['h'] + 1) * M_DV] = chain_rows(out, i).astype(o_ref.dtype)


def mlstm(z3, gates_r, conv_w, conv_b, ig_b, fg_b, gain):
    b, t, _ = z3.shape
    nc = t // M_CHUNK
    H = M_HEADS
    qk_w = H * M_DK
    v_w = H * M_DV
    bias = jnp.concatenate([ig_b, fg_b])
    gate_blk = AB_COLS_PAD // LANES_V7X - 1
    return pl.pallas_call(
        _mlstm_kernel,
        out_shape=jax.ShapeDtypeStruct((b, t, v_w), BF16),
        grid=(nc,),
        in_specs=[pl.BlockSpec((b, M_CHUNK, qk_w), lambda c: (0, c, 6)),
                  pl.BlockSpec((b, M_CHUNK, qk_w), lambda c: (0, c, 7)),
                  pl.BlockSpec((b, M_CHUNK, v_w), lambda c: (0, c, 4)),
                  pl.BlockSpec((b, M_CHUNK, v_w), lambda c: (0, c, 5)),
                  pl.BlockSpec((b, M_CHUNK, LANES_V7X), lambda c: (0, c, gate_blk)),
                  pl.BlockSpec((b, None, 2 * H, M_CHUNK), lambda c: (0, c, 0, 0)),
                  pl.BlockSpec((M_CONV, 2 * qk_w), lambda c: (0, 0)),
                  pl.BlockSpec((1, 2 * qk_w), lambda c: (0, 0)),
                  pl.BlockSpec((1, 2 * H), lambda c: (0, 0)),
                  pl.BlockSpec((2 * H, 1), lambda c: (0, 0)),
                  pl.BlockSpec((1, v_w), lambda c: (0, 0))],
        out_specs=pl.BlockSpec((b, M_CHUNK, v_w), lambda c: (0, c, 0)),
        scratch_shapes=[pltpu.VMEM((b, 8 + M_CHUNK, 2 * qk_w), F32),
                        pltpu.VMEM((b * H, M_DK, M_DV), F32),
                        pltpu.VMEM((b * H, 1, M_DK), F32),
                        pltpu.VMEM((b * H, 1, LANES_V7X), F32)],
        compiler_params=_cparams(("arbitrary",)),
        name="mlstm",
    )(z3, z3, z3, z3, z3, gates_r, conv_w, conv_b.reshape(1, -1), bias.reshape(1, -1), bias.reshape(-1, 1),
      gain.reshape(1, v_w))


def _mix_ffn_ple_kernel(*refs, n_mix, tm, seq, tf):
    hp_ref, h_ref = refs[0], refs[1]
    mix_refs = refs[2:2 + 2 * n_mix]
    (wo_ref, g_ref, wa_ref, wb_ref, cw_ref, cb_ref, wd_ref, p_ref, wp_ref, ng_ref, gg_ref, wg_ref,
     o_ref, xn_ref, a_ref) = refs[2 + 2 * n_mix:]
    i = pl.program_id(0)
    halo = 8
    g = g_ref[...]
    m_prev = jnp.concatenate([mix_refs[2 * k][...] for k in range(n_mix)], axis=1)
    m_tile = jnp.concatenate([mix_refs[2 * k + 1][...] for k in range(n_mix)], axis=1)
    x = h_ref[...] + _dot(m_tile, wo_ref[...])
    xp = hp_ref[...] + _dot(m_prev, wo_ref[...])[BF16_ROWS - halo:, :]
    xn_ref[halo:halo + tm, :] = _rms(x, g).astype(BF16)
    keep = ((i * tm) % seq != 0).astype(F32)
    xn_ref[0:halo, :] = (_rms(xp, g) * keep).astype(BF16)
    n_chunks = D_FF // tf

    def up_proj(c):
        cs = slice(c * tf, (c + 1) * tf)
        a_ref[c % 2] = _dot(xn_ref[...], wa_ref[:, cs])
        return _dot(xn_ref[halo:halo + tm, :], wb_ref[:, cs])

    h2 = x
    bgate = up_proj(0)
    for c in range(n_chunks):
        cs = slice(c * tf, (c + 1) * tf)
        bgate_next = up_proj(c + 1) if c + 1 < n_chunks else None
        conv = cb_ref[:, cs]
        for t in range(FFN_CONV):
            conv = conv + a_ref[c % 2, pl.ds(halo - (FFN_CONV - 1) + t, tm), :] * cw_ref[t:t + 1, cs]
        act = (_gelu_tanh(conv) * bgate).astype(BF16)
        h2 = h2 + _dot(act, wd_ref[cs, :])
        bgate = bgate_next
    e = _rms(_dot(p_ref[...].astype(BF16), wp_ref[...]), ng_ref[...])
    gate = jax.nn.sigmoid(_dot(_rms(h2, gg_ref[...]).astype(BF16), wg_ref[...]))
    o_ref[...] = h2 + gate * e


def mix_ffn_ple(h, mix, li, g, w_up, conv_w, conv_b, w_down, p, wp, norm_g, gate_norm_g, wg, seq, tm=512, tf=2816):
    mix_outs, w_out = mix
    n, d = h.shape
    pd = p.shape[-1]
    hb = tm // 8
    const = lambda i: (0, 0)
    layer = lambda i: (li, 0, 0)
    prev8 = lambda i: (jnp.maximum(i * hb - 1, 0), 0)
    resident = dict(pipeline_mode=pl.Buffered(1))
    prev16 = lambda i: (jnp.maximum(i * (tm // BF16_ROWS) - 1, 0), 0)
    mix_specs, mix_args = [], []
    for m in mix_outs:
        mix_specs += [pl.BlockSpec((BF16_ROWS, m.shape[1]), prev16),
                      pl.BlockSpec((tm, m.shape[1]), lambda i: (i, 0))]
        mix_args += [m, m]
    mix_specs.append(pl.BlockSpec(w_out.shape, const, **resident))
    mix_args.append(w_out)
    return pl.pallas_call(
        functools.partial(_mix_ffn_ple_kernel, n_mix=len(mix_outs), tm=tm, seq=seq, tf=tf),
        out_shape=jax.ShapeDtypeStruct((n, d), F32),
        grid=(n // tm,),
        in_specs=[pl.BlockSpec((8, d), prev8),
                  pl.BlockSpec((tm, d), lambda i: (i, 0))] + mix_specs + [
                  pl.BlockSpec((None, 1, d), layer),
                  pl.BlockSpec((None, d, D_FF), lambda i: (li, 0, 0), **resident),
                  pl.BlockSpec((None, d, D_FF), lambda i: (li, 0, 1), **resident),
                  pl.BlockSpec((None, FFN_CONV, D_FF), layer),
                  pl.BlockSpec((None, 1, D_FF), layer),
                  pl.BlockSpec((None, D_FF, d), layer, **resident),
                  pl.BlockSpec((None, tm, pd), lambda i: (li, i, 0)),
                  pl.BlockSpec((None, pd, d), layer, **resident),
                  pl.BlockSpec((None, 1, d), layer),
                  pl.BlockSpec((None, 1, d), layer),
                  pl.BlockSpec((None, d, d), layer, **resident)],
        out_specs=pl.BlockSpec((tm, d), lambda i: (i, 0)),
        scratch_shapes=[pltpu.VMEM((8 + tm, d), BF16),
                        pltpu.VMEM((min(2, D_FF // tf), 8 + tm, tf), F32)],
        compiler_params=_cparams(("parallel",), VMEM_LIMIT_V7X),
        name="mix_ffn_ple",
    )(h, h, *mix_args, g[:, None, :], w_up, w_up, conv_w, conv_b[:, None, :], w_down,
      p, wp, norm_g[:, None, :], gate_norm_g[:, None, :], wg)


def _group_rms(x, g):
    lane = lax.broadcasted_iota(jnp.int32, x.shape, 1)
    x2 = x * x
    ms = jnp.zeros_like(x)
    for grp in range(N_KV_GROUPS):
        in_grp = (lane >= grp * HEAD_DIM) & (lane < (grp + 1) * HEAD_DIM)
        tot = jnp.sum(jnp.where(in_grp, x2, 0.0), axis=-1, keepdims=True)
        ms = jnp.where(in_grp, tot * (1.0 / HEAD_DIM), ms)
    return x * lax.rsqrt(ms + EPS) * g


def _kv_prep_kernel(c_ref, s_ref, w_ref, gs_ref, gw_ref, kc_ref, vc_ref, ks_ref, vs_ref, kw_ref, vw_ref):
    gw = N_KV_GROUPS * HEAD_DIM
    cc = c_ref[...]
    ss = s_ref[...]
    ww = w_ref[...]
    kc_ref[...] = cc[:, 0:gw].astype(BF16)
    vc_ref[...] = cc[:, gw:2 * gw].astype(BF16)
    ks = _group_rms(ss[:, 0:gw], gs_ref[...]).astype(BF16)
    kw = _group_rms(ww[:, 0:gw], gw_ref[...]).astype(BF16)
    vst = ss[:, gw:2 * gw].T
    vwt = ww[:, gw:2 * gw].T
    for g in range(N_KV_GROUPS):
        lanes = slice(g * HEAD_DIM, (g + 1) * HEAD_DIM)
        ks_ref[g] = ks[:, lanes]
        kw_ref[g] = kw[:, lanes]
        vs_ref[g] = vst[lanes, :].astype(BF16)
        for u in range(SEL_TILE // KEY_TILE):
            vw_ref[g, u] = vwt[lanes, u * KEY_TILE:(u + 1) * KEY_TILE].astype(BF16)


def kv_prep(z, k_g, b, t):
    n = z.shape[0]
    G = N_KV_GROUPS
    gw = G * HEAD_DIM
    base = N_HEADS * HEAD_DIM // (2 * gw)
    tm = SEL_TILE
    nst = t // tm
    sub = SEL_TILE // KEY_TILE
    row = lambda i, j: i * nst + j
    flat = jax.ShapeDtypeStruct((n, gw), BF16)
    keys = jax.ShapeDtypeStruct((b, G, t, HEAD_DIM), BF16)
    flat_spec = pl.BlockSpec((tm, gw), lambda i, j: (row(i, j), 0))
    key_spec = pl.BlockSpec((None, G, tm, HEAD_DIM), lambda i, j: (i, 0, j, 0))
    return pl.pallas_call(
        _kv_prep_kernel,
        out_shape=(flat, flat, keys, jax.ShapeDtypeStruct((b, G, nst, HEAD_DIM, tm), BF16),
                   keys, jax.ShapeDtypeStruct((b, G, nst * sub, HEAD_DIM, KEY_TILE), BF16)),
        grid=(b, nst),
        in_specs=[pl.BlockSpec((tm, 2 * gw), lambda i, j: (row(i, j), base)),
                  pl.BlockSpec((tm, 2 * gw), lambda i, j: (row(i, j), base + 1)),
                  pl.BlockSpec((tm, 2 * gw), lambda i, j: (row(i, j), base + 2)),
                  pl.BlockSpec((1, gw), lambda i, j: (0, 0)),
                  pl.BlockSpec((1, gw), lambda i, j: (0, 0))],
        out_specs=(flat_spec, flat_spec, key_spec,
                   pl.BlockSpec((None, G, None, HEAD_DIM, tm), lambda i, j: (i, 0, j, 0, 0)),
                   key_spec,
                   pl.BlockSpec((None, G, sub, HEAD_DIM, KEY_TILE), lambda i, j: (i, 0, j, 0, 0))),
        compiler_params=_cparams(("parallel", "parallel")),
        name="kv_prep",
    )(z, z, z, jnp.tile(k_g[1], G).reshape(1, gw), jnp.tile(k_g[2], G).reshape(1, gw))


def _compress_kernel(x_ref, pos_ref, w1_ref, w2_ref, g_ref, o_ref, *, normalize):
    half = w1_ref.shape[0] // 2
    x = x_ref[...]
    u = _dot(x, w1_ref[0:half, :])
    v = _dot(x, w1_ref[half:2 * half, :])
    rows = u.shape[0]
    posc = _dot(pos_ref[...], w1_ref[...])[0:1, :]
    hid = u + pltpu.roll(v, rows - 1, 0) + posc
    out = _dot(jax.nn.gelu(hid).astype(BF16), w2_ref[...])
    if normalize:
        out = _rms(out, g_ref[...])
    o_ref[...] = out.astype(o_ref.dtype)


def compress(x, pos, w1, w2, g, normalize, ncb):
    n, kdim = x.shape
    posf = jnp.broadcast_to(pos.reshape(1, -1), (8, 2 * kdim)).astype(BF16)
    return pl.pallas_call(
        functools.partial(_compress_kernel, normalize=normalize),
        out_shape=jax.ShapeDtypeStruct((n, HEAD_DIM), BF16),
        grid=(n // ncb,),
        in_specs=[pl.BlockSpec((ncb, kdim), lambda i: (i, 0)),
                  pl.BlockSpec((8, 2 * kdim), lambda i: (0, 0)),
                  pl.BlockSpec((2 * kdim, CMP_HIDDEN), lambda i: (0, 0)),
                  pl.BlockSpec((CMP_HIDDEN, HEAD_DIM), lambda i: (0, 0)),
                  pl.BlockSpec((1, HEAD_DIM), lambda i: (0, 0))],
        out_specs=pl.BlockSpec((ncb, HEAD_DIM), lambda i: (i, 0)),
        compiler_params=_cparams(("parallel",)),
        name="compress",
    )(x, posf, w1.astype(BF16), w2.astype(BF16), g.reshape(1, HEAD_DIM))


def _nsa_kernel(bnd_ref, *refs):
    safe = bnd_ref[0, 0] <= MAX_SAFE_SCORE_BOUND

    @pl.when(safe)
    def _():
        _nsa_body(bnd_ref, *refs, bounded=True)

    @pl.when(jnp.logical_not(safe))
    def _():
        _nsa_body(bnd_ref, *refs, bounded=False)


def _nsa_body(bnd_ref, zq_ref, gt_ref, gb_ref, qg_ref, ovt_ref, kc_ref, vct_ref, ks_ref, vst_ref, kw_ref,
              vwt_ref, o_ref, q_scr, sel_scr, m_scr, l_scr, acc_scr, oc_scr, *, bounded):
    grp = pl.program_id(1)
    qb = pl.program_id(2)
    QB = Q_BLOCK
    ncp = kc_ref.shape[0]
    ns = ovt_ref.shape[0]
    tpos = qb * QB + lax.broadcasted_iota(jnp.int32, (1, QB), 1)

    def lane_tile(x):
        return jnp.concatenate([x] * HG, axis=1)

    zt = zq_ref[...].T
    qg = qg_ref[...] * (HEAD_DIM ** -0.5 * LOG2E)
    heads = []
    for h in range(HG):
        xh = zt[h * HEAD_DIM:(h + 1) * HEAD_DIM, :]
        ms = jnp.mean(xh * xh, axis=0, keepdims=True)
        heads.append(xh * lax.rsqrt(ms + EPS) * qg)
    q_scr[...] = jnp.concatenate(heads, axis=1).astype(BF16)
    q = q_scr[...]

    gates = jax.nn.sigmoid(gt_ref[...].T + gb_ref[...])
    per_grp = HG * 3
    gsel = gates[0:per_grp, :]
    for g2 in range(1, N_KV_GROUPS):
        gsel = jnp.where(grp == g2, gates[g2 * per_grp:(g2 + 1) * per_grp, :], gsel)

    def gate_row(c):
        return jnp.concatenate([gsel[3 * h + c:3 * h + c + 1, :] for h in range(HG)], axis=1)

    cmp_end = lax.broadcasted_iota(jnp.int32, (ncp, 1), 0) * CMP_STRIDE + (CMP_BLOCK - 1)
    keep = -bnd_ref[0, 0] if bounded else 0.0
    cbias = jnp.where(cmp_end <= tpos, keep, NEG)
    s = _dot(kc_ref[...], q) + lane_tile(cbias)
    e = jnp.exp2(s) if bounded else jnp.exp2(s - jnp.max(s, axis=0, keepdims=True))
    inv = jnp.where(lane_tile(tpos) >= CMP_BLOCK - 1, 1.0 / jnp.sum(e, axis=0, keepdims=True), 0.0)
    p = e * inv
    ocmp = _dot(vct_ref[...], p.astype(BF16))
    psum = p[:, 0:QB]
    for h in range(1, HG):
        psum = psum + p[:, h * QB:(h + 1) * QB]

    n_win = (WINDOW + QB) // KEY_TILE
    first_tile = qb * (QB // KEY_TILE) - WINDOW // KEY_TILE
    win_sub = lax.broadcasted_iota(jnp.int32, (n_win * KEY_TILE, 1), 0)
    tiles = [jnp.maximum(first_tile + u, 0) for u in range(n_win)]
    kwin = jnp.concatenate([kw_ref[j] for j in tiles], axis=0)
    vwin = jnp.concatenate([vwt_ref[j] for j in tiles], axis=1)
    kpos = first_tile * KEY_TILE + win_sub
    wbias = jnp.where((kpos <= tpos) & (kpos > tpos - WINDOW) & (kpos >= 0), keep, NEG)
    sw = _dot(kwin, q) + lane_tile(wbias)
    pw = jnp.exp2(sw) if bounded else jnp.exp2(sw - jnp.max(sw, axis=0, keepdims=True))
    lw = jnp.sum(pw, axis=0, keepdims=True)
    oc_scr[...] = gate_row(0) * ocmp + (gate_row(2) / lw) * _dot(vwin, pw.astype(BF16))

    p_hi = psum.astype(BF16)
    p_lo = (psum - p_hi.astype(F32)).astype(BF16)
    ovt = ovt_ref[...]
    imp = _dot(ovt, p_hi) + _dot(ovt, p_lo)
    blk = lax.broadcasted_iota(jnp.int32, (ns, 1), 0)
    blk_f = blk.astype(F32)
    cur = jnp.right_shift(tpos, SLC_BLOCK.bit_length() - 1)
    forced = (blk == 0) | (blk == cur) | (blk == cur - 1)
    bvalid = blk <= cur
    score = jnp.where(forced, BIG, jnp.where(bvalid, imp, NEG))
    sel = jnp.zeros((ns, QB), F32)
    for _ in range(min(N_SELECT, ns)):
        mx = jnp.max(score, axis=0, keepdims=True)
        first = jnp.min(jnp.where(score == mx, blk_f, float(ns)), axis=0, keepdims=True)
        pick = blk_f == first
        sel = jnp.where(pick, 1.0, sel)
        score = jnp.where(pick, -jnp.inf, score)
    sel_scr[...] = jnp.where(bvalid, sel, 0.0)

    m_scr[...] = jnp.full(m_scr.shape, NEG, F32)
    l_scr[...] = jnp.zeros(l_scr.shape, F32)
    acc_scr[...] = jnp.zeros(acc_scr.shape, F32)
    blocks_per_tile = SEL_TILE // SLC_BLOCK
    sel_sub = lax.broadcasted_iota(jnp.int32, (SEL_TILE, 1), 0)

    def sel_body(j, carry):
        rows = [jnp.broadcast_to(sel_scr[pl.ds(j * blocks_per_tile + bi, 1), :], (SLC_BLOCK, QB))
                for bi in range(blocks_per_tile)]
        chosen = jnp.concatenate(rows, axis=0)
        kpos = j * SEL_TILE + sel_sub
        bias = jnp.where((chosen > 0.5) & (kpos <= tpos), keep, NEG)
        st = _dot(ks_ref[j], q_scr[...]) + lane_tile(bias)
        if bounded:
            pt = jnp.exp2(st)
            l_scr[...] += jnp.sum(pt, axis=0, keepdims=True)
            acc_scr[...] += _dot(vst_ref[j], pt.astype(BF16))
        else:
            m_old = m_scr[...]
            m_new = jnp.maximum(m_old, jnp.max(st, axis=0, keepdims=True))
            alpha = jnp.exp2(m_old - m_new)
            pt = jnp.exp2(st - m_new)
            l_scr[...] = alpha * l_scr[...] + jnp.sum(pt, axis=0, keepdims=True)
            acc_scr[...] = alpha * acc_scr[...] + _dot(vst_ref[j], pt.astype(BF16))
            m_scr[...] = m_new
        return carry

    lax.fori_loop(0, qb // (SEL_TILE // QB) + 1, sel_body, 0)
    ot = oc_scr[...] + (gate_row(1) / l_scr[...]) * acc_scr[...]
    stacked = jnp.concatenate([ot[:, h * QB:(h + 1) * QB] for h in range(HG)], axis=0)
    o_ref[...] = stacked.T.astype(o_ref.dtype)


def nsa_attention(bound, z3, gate_b, q_g, overlap_t, kc, vct, ks, vst, kw, vwt):
    b, t, _ = z3.shape
    nq = t // Q_BLOCK
    nt = t // KEY_TILE
    qw = HG * HEAD_DIM
    ns, ncp = overlap_t.shape
    gate_blk = (NSA_COLS_PAD // LANES_V7X) - 1
    gb = jnp.zeros((LANES_V7X, 1), F32).at[:N_HEADS * 3, 0].set(gate_b)
    full5 = lambda i, g, q: (i, g, 0, 0, 0)
    lanes = HG * Q_BLOCK
    return pl.pallas_call(
        _nsa_kernel,
        out_shape=jax.ShapeDtypeStruct((b, t, N_HEADS * HEAD_DIM), BF16),
        grid=(b, N_KV_GROUPS, nq),
        in_specs=[pl.BlockSpec(memory_space=pltpu.SMEM),
                  pl.BlockSpec((None, Q_BLOCK, qw), lambda i, g, q: (i, q, g)),
                  pl.BlockSpec((None, Q_BLOCK, LANES_V7X), lambda i, g, q: (i, q, gate_blk)),
                  pl.BlockSpec((LANES_V7X, 1), lambda i, g, q: (0, 0)),
                  pl.BlockSpec((HEAD_DIM, 1), lambda i, g, q: (0, 0)),
                  pl.BlockSpec((ns, ncp), lambda i, g, q: (0, 0)),
                  pl.BlockSpec((None, None, ncp, HEAD_DIM), lambda i, g, q: (i, g, 0, 0)),
                  pl.BlockSpec((None, None, HEAD_DIM, ncp), lambda i, g, q: (i, g, 0, 0)),
                  pl.BlockSpec((None, None, t // SEL_TILE, SEL_TILE, HEAD_DIM), full5),
                  pl.BlockSpec((None, None, t // SEL_TILE, HEAD_DIM, SEL_TILE), full5),
                  pl.BlockSpec((None, None, nt, KEY_TILE, HEAD_DIM), full5),
                  pl.BlockSpec((None, None, nt, HEAD_DIM, KEY_TILE), full5)],
        out_specs=pl.BlockSpec((None, Q_BLOCK, qw), lambda i, g, q: (i, q, g)),
        scratch_shapes=[pltpu.VMEM((HEAD_DIM, lanes), BF16),
                        pltpu.VMEM((ns, Q_BLOCK), F32),
                        pltpu.VMEM((1, lanes), F32),
                        pltpu.VMEM((1, lanes), F32),
                        pltpu.VMEM((HEAD_DIM, lanes), F32),
                        pltpu.VMEM((HEAD_DIM, lanes), F32)],
        compiler_params=_cparams(("parallel", "parallel", "arbitrary"), VMEM_LIMIT_V7X),
        name="nsa_attention",
    )(bound.reshape(1, 1), z3, z3, gb, q_g.reshape(HEAD_DIM, 1), overlap_t, kc, vct, ks, vst, kw, vwt)


def _overlap_matrix_t(t):
    ncp = t // CMP_STRIDE
    ns = t // SLC_BLOCK
    c_start = np.arange(ncp) * CMP_STRIDE
    sj = np.arange(ns)
    ov = ((c_start[None, :] < (sj[:, None] + 1) * SLC_BLOCK)
          & (c_start[None, :] + CMP_BLOCK > sj[:, None] * SLC_BLOCK)
          & (c_start[None, :] + CMP_BLOCK <= t))
    return jnp.asarray(ov, dtype=BF16)


def ab_layer(h, b, t, norm_g, w_in, conv_w, conv_b, ret_g, ig_b, fg_b, m_g, w_out):
    n = b * t
    w_in_p = jnp.pad(w_in, ((0, 0), (0, AB_COLS_PAD - AB_COLS))).astype(BF16)
    z = norm_matmul(h, norm_g, w_in_p)
    z3 = z.reshape(b, t, AB_COLS_PAD)
    cos_t, sin_t = rope_tables(t)
    ret = retention(z3, cos_t, sin_t, ret_g)
    g0 = AB_COLS - 2 * M_HEADS
    gates_r = z3[:, :, g0:AB_COLS].reshape(b, t // M_CHUNK, M_CHUNK, 2 * M_HEADS).transpose(0, 1, 3, 2)
    ml = mlstm(z3, gates_r, conv_w, conv_b, ig_b, fg_b, m_g)
    w_out_b = w_out.astype(BF16)
    rw = R_HEADS * R_DV
    return [ret.reshape(n, rw), ml.reshape(n, -1)], w_out_b


def nsa_layer(h, b, t, norm_g, w_in, q_g, k_g, pos_k, pos_v, w1k, w2k, w1v, w2v, gate_b, w_out):
    n = b * t
    G = N_KV_GROUPS
    w_in_p = jnp.pad(w_in, ((0, 0), (0, NSA_COLS_PAD - NSA_COLS))).astype(BF16)
    z = norm_matmul(h, norm_g, w_in_p)
    kc_in, vc_in, ks, vst, kw, vwt = kv_prep(z, k_g, b, t)
    ncb = t // CMP_STRIDE

    def cmp_rows(x):
        return x.reshape(b, t, G, HEAD_DIM).transpose(0, 2, 1, 3).reshape(b * G * ncb, CMP_STRIDE * HEAD_DIM)

    kc = compress(cmp_rows(kc_in), pos_k, w1k, w2k, k_g[0], True, ncb)
    vc = compress(cmp_rows(vc_in), pos_v, w1v, w2v, k_g[0], False, ncb)
    kc = kc.reshape(b, G, ncb, HEAD_DIM)
    vct = vc.reshape(b, G, ncb, HEAD_DIM).transpose(0, 1, 3, 2)

    def key_tiles(x, kt):
        return x.reshape(b, G, t // kt, kt, HEAD_DIM)

    bound = 1.02 * LOG2E * math.sqrt(HEAD_DIM) * jnp.max(jnp.abs(q_g)) * jnp.max(jnp.abs(k_g))
    args = (bound, z.reshape(b, t, NSA_COLS_PAD), gate_b, q_g, _overlap_matrix_t(t), kc, vct,
            key_tiles(ks, SEL_TILE), vst, key_tiles(kw, KEY_TILE), vwt)
    o = nsa_attention(*args)
    return [o.reshape(n, -1)], w_out.astype(BF16)


def kernel(x, p, ab_norm_g, ab_w_in, ab_conv_w, ab_conv_b, ab_ret_norm_g, ab_ig_b, ab_fg_b, ab_m_norm_g, ab_w_out, nsa_norm_g, nsa_w_in, nsa_q_norm_g, nsa_k_norm_g, nsa_cmp_pos_k, nsa_cmp_pos_v, nsa_cmp_w1k, nsa_cmp_w2k, nsa_cmp_w1v, nsa_cmp_w2v, nsa_gate_b, nsa_w_out, ffn_norm_g, ffn_w_up, ffn_conv_w, ffn_conv_b, ffn_w_down, ple_w, ple_norm_g, ple_gate_norm_g, ple_w_gate):
    b, t, d = x.shape
    n = b * t
    depth = p.shape[0]
    h = x.reshape(n, d)
    p2 = p.reshape(depth, n, -1)
    ffn_w_up_b, ffn_w_down_b = ffn_w_up.astype(BF16), ffn_w_down.astype(BF16)
    ple_w_b, ple_w_gate_b = ple_w.astype(BF16), ple_w_gate.astype(BF16)
    for i in range(depth):
        j = i // 2
        if i % 2 == 0:
            mix = ab_layer(h, b, t, ab_norm_g[j], ab_w_in[j], ab_conv_w[j], ab_conv_b[j], ab_ret_norm_g[j],
                           ab_ig_b[j], ab_fg_b[j], ab_m_norm_g[j], ab_w_out[j])
        else:
            mix = nsa_layer(h, b, t, nsa_norm_g[j], nsa_w_in[j], nsa_q_norm_g[j], nsa_k_norm_g[j],
                            nsa_cmp_pos_k[j], nsa_cmp_pos_v[j], nsa_cmp_w1k[j], nsa_cmp_w2k[j],
                            nsa_cmp_w1v[j], nsa_cmp_w2v[j], nsa_gate_b[j], nsa_w_out[j])
        h = mix_ffn_ple(h, mix, i, ffn_norm_g, ffn_w_up_b, ffn_conv_w, ffn_conv_b, ffn_w_down_b, p2, ple_w_b,
                        ple_norm_g, ple_gate_norm_g, ple_w_gate_b, t)
    return h.reshape(b, t, d)
```

```python
import functools
import math

import numpy as np
import jax
import jax.numpy as jnp
from jax import lax
from jax.experimental import pallas as pl
from jax.experimental.pallas import tpu as pltpu

F32 = jnp.float32
BF16 = jnp.bfloat16

LANES_V7X = 128
BF16_ROWS = 16
VMEM_LIMIT_V7X = 56 * 1024 * 1024

D_MODEL = 1024
PLE_DIM = 256
R_HEADS, R_DK, R_DV, R_CHUNK = 4, 64, 128, 128
ROPE_BASE = 10000.0
M_HEADS, M_DK, M_DV, M_CHUNK, M_CONV = 4, 64, 128, 64, 4
AB_SIZES = (R_HEADS * R_DK, R_HEADS * R_DK, R_HEADS * R_DV, R_HEADS * R_DV,
            M_HEADS * M_DK, M_HEADS * M_DK, M_HEADS * M_DV, M_HEADS * M_DV, M_HEADS, M_HEADS)
AB_COLS = sum(AB_SIZES)
AB_COLS_PAD = 3200
N_HEADS, N_KV_GROUPS, HEAD_DIM = 16, 2, 64
HG = N_HEADS // N_KV_GROUPS
CMP_BLOCK, CMP_STRIDE, CMP_HIDDEN = 32, 16, 256
SLC_BLOCK, N_SELECT, WINDOW = 64, 16, 512
Q_BLOCK = 256
NSA_COLS = N_HEADS * HEAD_DIM + 6 * N_KV_GROUPS * HEAD_DIM + N_HEADS * 3
NSA_COLS_PAD = 1920
D_FF = 2816
FFN_CONV = 3
NEG = -1e30
BIG = 1e30
EPS = 1e-6
KEY_TILE = 128
SEL_TILE = 512
V_ROWS = HEAD_DIM + BF16_ROWS
LOG2E = math.log2(math.e)
MAX_SAFE_SCORE_BOUND = 56.0


def _cparams(sem, vmem=None):
    return pltpu.CompilerParams(dimension_semantics=sem, vmem_limit_bytes=vmem)


def _rms(x, g):
    ms = jnp.mean(x * x, axis=-1, keepdims=True)
    return x * lax.rsqrt(ms + EPS) * g


def _dot(a, b):
    return jnp.dot(a, b, preferred_element_type=F32)


def _dot_nt(a, b):
    return lax.dot_general(a, b, (((1,), (1,)), ((), ())), preferred_element_type=F32)


def _dot_f32(a, b):
    return jnp.dot(a, b, preferred_element_type=F32, precision=lax.Precision.HIGHEST)


def _gelu_tanh(x):
    k1 = -2.0 * math.sqrt(2.0 / math.pi) * LOG2E
    return x / (1.0 + jnp.exp2(x * (k1 + (k1 * 0.044715) * (x * x))))


def _norm_matmul_kernel(x_ref, g_ref, w_ref, o_ref):
    xn = _rms(x_ref[...], g_ref[...]).astype(BF16)
    o_ref[...] = _dot(xn, w_ref[...]).astype(o_ref.dtype)


def norm_matmul(x, g, w, tm=1024, out_dtype=F32):
    n, d = x.shape
    nc = w.shape[1]
    return pl.pallas_call(
        _norm_matmul_kernel,
        out_shape=jax.ShapeDtypeStruct((n, nc), out_dtype),
        grid=(n // tm,),
        in_specs=[pl.BlockSpec((tm, d), lambda i: (i, 0)),
                  pl.BlockSpec((1, d), lambda i: (0, 0)),
                  pl.BlockSpec((d, nc), lambda i: (0, 0))],
        out_specs=pl.BlockSpec((tm, nc), lambda i: (i, 0)),
        compiler_params=_cparams(("parallel",), VMEM_LIMIT_V7X),
        name="norm_matmul",
    )(x, g.reshape(1, d), w)


def _rope_table_kernel(inv_ref, cos_ref, sin_ref):
    c = pl.program_id(0)
    rows, width = cos_ref.shape
    pos = (c * rows + lax.broadcasted_iota(jnp.int32, (rows, LANES_V7X), 0)).astype(F32)
    lane = lax.broadcasted_iota(jnp.int32, (rows, LANES_V7X), 1)
    ang = pos * inv_ref[:, 0:LANES_V7X]
    cs = jnp.cos(ang)
    sn = jnp.sin(ang)
    sn = jnp.where(lane % R_DK < R_DK // 2, -sn, sn)
    reps = width // LANES_V7X
    cos_ref[...] = jnp.concatenate([cs] * reps, axis=1)
    sin_ref[...] = jnp.concatenate([sn] * reps, axis=1)


def rope_tables(t):
    half = R_DK // 2
    inv = ROPE_BASE ** (-jnp.arange(half, dtype=F32) / half)
    inv = jnp.tile(inv, 2 * R_HEADS).reshape(1, R_HEADS * R_DK)
    width = R_HEADS * R_DK
    shp = jax.ShapeDtypeStruct((t, width), F32)
    return pl.pallas_call(
        _rope_table_kernel,
        out_shape=(shp, shp),
        grid=(t // R_CHUNK,),
        in_specs=[pl.BlockSpec((1, width), lambda c: (0, 0))],
        out_specs=(pl.BlockSpec((R_CHUNK, width), lambda c: (c, 0)),
                   pl.BlockSpec((R_CHUNK, width), lambda c: (c, 0))),
        compiler_params=_cparams(("parallel",)),
        name="rope_tables",
    )(inv)


def _retention_kernel(cos_ref, sin_ref, q_ref, k_ref, v_ref, g_ref, gain_ref, o_ref, r_ref):
    c = pl.program_id(0)
    nb = q_ref.shape[0]
    L = R_CHUNK

    @pl.when(c == 0)
    def _():
        r_ref[...] = jnp.zeros_like(r_ref)

    cos = cos_ref[...]
    sin = sin_ref[...]
    lane = lax.broadcasted_iota(jnp.int32, cos.shape, 1)
    first_half = lane % R_DK < R_DK // 2
    width = R_HEADS * R_DK

    def rot(x):
        swapped = jnp.where(first_half, pltpu.roll(x, width - R_DK // 2, 1), pltpu.roll(x, R_DK // 2, 1))
        return x * cos + swapped * sin

    gain = gain_ref[...]
    ri = lax.broadcasted_iota(jnp.int32, (L, L), 0)
    ci = lax.broadcasted_iota(jnp.int32, (L, L), 1)
    diff = (ri - ci).astype(F32)
    causal = ri >= ci
    idx = lax.broadcasted_iota(jnp.int32, (L, 1), 0).astype(F32)
    decays = []
    for h in range(R_HEADS):
        log_g = math.log1p(-2.0 ** (-5.0 - h))
        decays.append(dict(
            dmask=jnp.where(causal, jnp.exp(jnp.where(causal, diff, 0.0) * log_g), 0.0),
            q_dec=jnp.exp((idx + 1.0) * log_g), k_dec=jnp.exp((L - 1.0 - idx) * log_g),
            c_dec=math.exp(L * log_g)))

    chains = []
    for bi in range(nb):
        q = rot(q_ref[bi])
        k = rot(k_ref[bi]) * (R_DK ** -0.5)
        for h in range(R_HEADS):
            chains.append(dict(bi=bi, h=h, u=bi * R_HEADS + h,
                               qb=q[:, h * R_DK:(h + 1) * R_DK].astype(BF16),
                               kh=k[:, h * R_DK:(h + 1) * R_DK]))

    for ch in chains:
        ch['rstate'] = r_ref[ch['u']]
        ch['s'] = _dot_nt(ch['qb'], ch['kh'].astype(BF16)) * decays[ch['h']]['dmask']
        ch['qr'] = _dot(ch['qb'], ch['rstate'].astype(BF16))

    for ch in chains:
        d = decays[ch['h']]
        vh = v_ref[ch['bi'], :, ch['h'] * R_DV:(ch['h'] + 1) * R_DV].astype(BF16)
        ch['o'] = _dot(ch['s'].astype(BF16), vh) + ch['qr'] * d['q_dec']
        kd = (ch['kh'] * d['k_dec']).T.astype(BF16)
        r_ref[ch['u']] = d['c_dec'] * ch['rstate'] + _dot(kd, vh)

    for ch in chains:
        sl = slice(ch['h'] * R_DV, (ch['h'] + 1) * R_DV)
        gh = g_ref[ch['bi'], :, sl]
        o_ref[ch['bi'], :, sl] = (_rms(ch['o'], gain[:, sl]) * (gh * jax.nn.sigmoid(gh))).astype(o_ref.dtype)


def retention(z3, cos_t, sin_t, gain):
    b, t, _ = z3.shape
    nc = t // R_CHUNK
    qk_w = R_HEADS * R_DK
    v_w = R_HEADS * R_DV
    return pl.pallas_call(
        _retention_kernel,
        out_shape=jax.ShapeDtypeStruct((b, t, v_w), BF16),
        grid=(nc,),
        in_specs=[pl.BlockSpec((R_CHUNK, qk_w), lambda c: (c, 0)),
                  pl.BlockSpec((R_CHUNK, qk_w), lambda c: (c, 0)),
                  pl.BlockSpec((b, R_CHUNK, qk_w), lambda c: (0, c, 0)),
                  pl.BlockSpec((b, R_CHUNK, qk_w), lambda c: (0, c, 1)),
                  pl.BlockSpec((b, R_CHUNK, v_w), lambda c: (0, c, 1)),
                  pl.BlockSpec((b, R_CHUNK, v_w), lambda c: (0, c, 2)),
                  pl.BlockSpec((1, v_w), lambda c: (0, 0))],
        out_specs=pl.BlockSpec((b, R_CHUNK, v_w), lambda c: (0, c, 0)),
        scratch_shapes=[pltpu.VMEM((b * R_HEADS, R_DK, R_DV), F32)],
        compiler_params=_cparams(("arbitrary",)),
        name="retention",
    )(cos_t, sin_t, z3, z3, z3, z3, gain.reshape(1, v_w))


def _mlstm_kernel(q_ref, k_ref, v_ref, og_ref, gc_ref, gr_ref, cw_ref, cb_ref, bc_ref, br_ref, gain_ref,
                  o_ref, xbuf, c_ref, n_ref, m_ref):
    c = pl.program_id(0)
    nb = q_ref.shape[0]
    L = M_CHUNK
    H = M_HEADS
    qk_w = H * M_DK
    halo = 8

    @pl.when(c == 0)
    def _():
        xbuf[:, 0:halo, :] = jnp.zeros((nb, halo, 2 * qk_w), F32)
        c_ref[...] = jnp.zeros_like(c_ref)
        n_ref[...] = jnp.zeros_like(n_ref)
        m_ref[...] = jnp.zeros_like(m_ref)

    gain = gain_ref[...]
    ri = lax.broadcasted_iota(jnp.int32, (L, L), 0)
    ci = lax.broadcasted_iota(jnp.int32, (L, L), 1)
    causal = ri >= ci
    tril = causal.astype(F32)
    triu = (ri <= ci).astype(F32)

    chains = []
    for bi in range(nb):
        xbuf[bi, halo:halo + L, 0:qk_w] = q_ref[bi]
        xbuf[bi, halo:halo + L, qk_w:2 * qk_w] = k_ref[bi]
        conv = cb_ref[...]
        for j in range(M_CONV):
            conv = conv + xbuf[bi, pl.ds(halo - (M_CONV - 1) + j, L), :] * cw_ref[j:j + 1, :]
        tail = xbuf[bi, L:L + halo, :]
        xbuf[bi, 0:halo, :] = tail
        act = conv * jax.nn.sigmoid(conv)
        q = act[:, 0:qk_w] * (M_DK ** -0.5)
        k = act[:, qk_w:2 * qk_w]
        gc = gc_ref[bi][:, 0:2 * H] + bc_ref[...]
        gr = gr_ref[bi] + br_ref[...]
        ig_c = gc[:, 0:H]
        ig_r = gr[0:H, :]
        b_c = _dot_f32(tril, jax.nn.log_sigmoid(gc[:, H:2 * H]))
        b_r = _dot_f32(jax.nn.log_sigmoid(gr[H:2 * H, :]), triu)
        for h in range(H):
            chains.append(dict(
                bi=bi, h=h, u=bi * H + h,
                qh=q[:, h * M_DK:(h + 1) * M_DK], kh=k[:, h * M_DK:(h + 1) * M_DK],
                bh=b_c[:, h:h + 1], brow=b_r[h:h + 1, :], irow=ig_r[h:h + 1, :], icol=ig_c[:, h:h + 1]))

    def stack(parts):
        return jnp.concatenate(parts, axis=0)

    def rows(x):
        return jnp.broadcast_to(x, (L, x.shape[1]))

    m_prev_u = [m_ref[ch['u']][:, 0:1] for ch in chains]
    b_last_u = [ch['bh'][L - 1:L, :] for ch in chains]
    bh = stack([ch['bh'] for ch in chains])
    icol = stack([ch['icol'] for ch in chains])
    brow = stack([rows(ch['brow']) for ch in chains])
    irow = stack([rows(ch['irow']) for ch in chains])
    m_prev = stack([rows(m) for m in m_prev_u])
    b_last = stack([rows(x) for x in b_last_u])
    causal_all = stack([causal] * len(chains))
    qs = stack([ch['qh'] for ch in chains])
    ks = stack([ch['kh'] for ch in chains])
    qb = qs.astype(BF16)
    kb = ks.astype(BF16)
    cstates = [c_ref[ch['u']] for ch in chains]
    nstates = [n_ref[ch['u']] for ch in chains]

    def chain_rows(x, i):
        return x[i * L:(i + 1) * L]

    s_raw = stack([_dot_nt(chain_rows(qb, i), chain_rows(kb, i)) for i in range(len(chains))])
    qc = stack([_dot(chain_rows(qb, i), cstates[i].astype(BF16)) for i in range(len(chains))])

    dlog = jnp.where(causal_all, bh - brow + irow, NEG)
    inter = bh + m_prev
    m_t = jnp.maximum(inter, jnp.max(dlog, axis=-1, keepdims=True))
    s = s_raw * jnp.exp(dlog - m_t)
    w_inter = jnp.exp(inter - m_t)
    wlog = b_last - bh + icol
    m_new_u = [jnp.maximum(b_last_u[i] + m_prev_u[i], jnp.max(chain_rows(wlog, i), axis=0, keepdims=True))
               for i in range(len(chains))]
    m_new = stack([rows(m) for m in m_new_u])
    wk = ks * jnp.exp(wlog - m_new)

    sb = s.astype(BF16)
    vhs = [v_ref[ch['bi'], :, ch['h'] * M_DV:(ch['h'] + 1) * M_DV].astype(BF16) for ch in chains]
    sv = stack([_dot(chain_rows(sb, i), vhs[i]) for i in range(len(chains))])
    kv = [_dot(chain_rows(wk, i).T.astype(BF16), vhs[i]) for i in range(len(chains))]

    qn = jnp.sum(qs * stack([rows(n) for n in nstates]), axis=-1, keepdims=True)
    den = jnp.sum(s, axis=-1, keepdims=True) + w_inter * qn
    hh = (sv + w_inter * qc) / jnp.maximum(jnp.abs(den), jnp.exp(-m_t))
    og = stack([og_ref[ch['bi'], :, ch['h'] * M_DV:(ch['h'] + 1) * M_DV] for ch in chains])
    gains = stack([rows(gain[:, ch['h'] * M_DV:(ch['h'] + 1) * M_DV]) for ch in chains])
    out = jax.nn.sigmoid(og) * _rms(hh, gains)

    for i, ch in enumerate(chains):
        u = ch['u']
        decay = jnp.exp(b_last_u[i] + m_prev_u[i] - m_new_u[i])
        c_ref[u] = decay * cstates[i] + kv[i]
        n_ref[u] = decay * nstates[i] + jnp.sum(chain_rows(wk, i), axis=0, keepdims=True)
        m_ref[u] = jnp.broadcast_to(m_new_u[i], (1, LANES_V7X))
        o_ref[ch['bi'], :, ch['h'] * M_DV:(ch['h'] + 1) * M_DV] = chain_rows(out, i).astype(o_ref.dtype)


def mlstm(z3, gates_r, conv_w, conv_b, ig_b, fg_b, gain):
    b, t, _ = z3.shape
    nc = t // M_CHUNK
    H = M_HEADS
    qk_w = H * M_DK
    v_w = H * M_DV
    bias = jnp.concatenate([ig_b, fg_b])
    gate_blk = AB_COLS_PAD // LANES_V7X - 1
    return pl.pallas_call(
        _mlstm_kernel,
        out_shape=jax.ShapeDtypeStruct((b, t, v_w), BF16),
        grid=(nc,),
        in_specs=[pl.BlockSpec((b, M_CHUNK, qk_w), lambda c: (0, c, 6)),
                  pl.BlockSpec((b, M_CHUNK, qk_w), lambda c: (0, c, 7)),
                  pl.BlockSpec((b, M_CHUNK, v_w), lambda c: (0, c, 4)),
                  pl.BlockSpec((b, M_CHUNK, v_w), lambda c: (0, c, 5)),
                  pl.BlockSpec((b, M_CHUNK, LANES_V7X), lambda c: (0, c, gate_blk)),
                  pl.BlockSpec((b, None, 2 * H, M_CHUNK), lambda c: (0, c, 0, 0)),
                  pl.BlockSpec((M_CONV, 2 * qk_w), lambda c: (0, 0)),
                  pl.BlockSpec((1, 2 * qk_w), lambda c: (0, 0)),
                  pl.BlockSpec((1, 2 * H), lambda c: (0, 0)),
                  pl.BlockSpec((2 * H, 1), lambda c: (0, 0)),
                  pl.BlockSpec((1, v_w), lambda c: (0, 0))],
        out_specs=pl.BlockSpec((b, M_CHUNK, v_w), lambda c: (0, c, 0)),
        scratch_shapes=[pltpu.VMEM((b, 8 + M_CHUNK, 2 * qk_w), F32),
                        pltpu.VMEM((b * H, M_DK, M_DV), F32),
                        pltpu.VMEM((b * H, 1, M_DK), F32),
                        pltpu.VMEM((b * H, 1, LANES_V7X), F32)],
        compiler_params=_cparams(("arbitrary",)),
        name="mlstm",
    )(z3, z3, z3, z3, z3, gates_r, conv_w, conv_b.reshape(1, -1), bias.reshape(1, -1), bias.reshape(-1, 1),
      gain.reshape(1, v_w))


def _mix_ffn_ple_kernel(*refs, n_mix, tm, seq, tf):
    hp_ref, h_ref = refs[0], refs[1]
    mix_refs = refs[2:2 + 2 * n_mix]
    (wo_ref, g_ref, wa_ref, wb_ref, cw_ref, cb_ref, wd_ref, p_ref, wp_ref, ng_ref, gg_ref, wg_ref,
     o_ref, xn_ref, a_ref) = refs[2 + 2 * n_mix:]
    i = pl.program_id(0)
    halo = 8
    g = g_ref[...]
    m_prev = jnp.concatenate([mix_refs[2 * k][...] for k in range(n_mix)], axis=1)
    m_tile = jnp.concatenate([mix_refs[2 * k + 1][...] for k in range(n_mix)], axis=1)
    x = h_ref[...] + _dot(m_tile, wo_ref[...])
    xp = hp_ref[...] + _dot(m_prev, wo_ref[...])[BF16_ROWS - halo:, :]
    xn_ref[halo:halo + tm, :] = _rms(x, g).astype(BF16)
    keep = ((i * tm) % seq != 0).astype(F32)
    xn_ref[0:halo, :] = (_rms(xp, g) * keep).astype(BF16)
    n_chunks = D_FF // tf

    def up_proj(c):
        cs = slice(c * tf, (c + 1) * tf)
        a_ref[c % 2] = _dot(xn_ref[...], wa_ref[:, cs])
        return _dot(xn_ref[halo:halo + tm, :], wb_ref[:, cs])

    h2 = x
    bgate = up_proj(0)
    for c in range(n_chunks):
        cs = slice(c * tf, (c + 1) * tf)
        bgate_next = up_proj(c + 1) if c + 1 < n_chunks else None
        conv = cb_ref[:, cs]
        for t in range(FFN_CONV):
            conv = conv + a_ref[c % 2, pl.ds(halo - (FFN_CONV - 1) + t, tm), :] * cw_ref[t:t + 1, cs]
        act = (_gelu_tanh(conv) * bgate).astype(BF16)
        h2 = h2 + _dot(act, wd_ref[cs, :])
        bgate = bgate_next
    e = _rms(_dot(p_ref[...].astype(BF16), wp_ref[...]), ng_ref[...])
    gate = jax.nn.sigmoid(_dot(_rms(h2, gg_ref[...]).astype(BF16), wg_ref[...]))
    o_ref[...] = h2 + gate * e


def mix_ffn_ple(h, mix, li, g, w_up, conv_w, conv_b, w_down, p, wp, norm_g, gate_norm_g, wg, seq, tm=512, tf=2816):
    mix_outs, w_out = mix
    n, d = h.shape
    pd = p.shape[-1]
    hb = tm // 8
    const = lambda i: (0, 0)
    layer = lambda i: (li, 0, 0)
    prev8 = lambda i: (jnp.maximum(i * hb - 1, 0), 0)
    resident = dict(pipeline_mode=pl.Buffered(1))
    prev16 = lambda i: (jnp.maximum(i * (tm // BF16_ROWS) - 1, 0), 0)
    mix_specs, mix_args = [], []
    for m in mix_outs:
        mix_specs += [pl.BlockSpec((BF16_ROWS, m.shape[1]), prev16),
                      pl.BlockSpec((tm, m.shape[1]), lambda i: (i, 0))]
        mix_args += [m, m]
    mix_specs.append(pl.BlockSpec(w_out.shape, const, **resident))
    mix_args.append(w_out)
    return pl.pallas_call(
        functools.partial(_mix_ffn_ple_kernel, n_mix=len(mix_outs), tm=tm, seq=seq, tf=tf),
        out_shape=jax.ShapeDtypeStruct((n, d), F32),
        grid=(n // tm,),
        in_specs=[pl.BlockSpec((8, d), prev8),
                  pl.BlockSpec((tm, d), lambda i: (i, 0))] + mix_specs + [
                  pl.BlockSpec((None, 1, d), layer),
                  pl.BlockSpec((None, d, D_FF), lambda i: (li, 0, 0), **resident),
                  pl.BlockSpec((None, d, D_FF), lambda i: (li, 0, 1), **resident),
                  pl.BlockSpec((None, FFN_CONV, D_FF), layer),
                  pl.BlockSpec((None, 1, D_FF), layer),
                  pl.BlockSpec((None, D_FF, d), layer, **resident),
                  pl.BlockSpec((None, tm, pd), lambda i: (li, i, 0)),
                  pl.BlockSpec((None, pd, d), layer, **resident),
                  pl.BlockSpec((None, 1, d), layer),
                  pl.BlockSpec((None, 1, d), layer),
                  pl.BlockSpec((None, d, d), layer, **resident)],
        out_specs=pl.BlockSpec((tm, d), lambda i: (i, 0)),
        scratch_shapes=[pltpu.VMEM((8 + tm, d), BF16),
                        pltpu.VMEM((min(2, D_FF // tf), 8 + tm, tf), F32)],
        compiler_params=_cparams(("parallel",), VMEM_LIMIT_V7X),
        name="mix_ffn_ple",
    )(h, h, *mix_args, g[:, None, :], w_up, w_up, conv_w, conv_b[:, None, :], w_down,
      p, wp, norm_g[:, None, :], gate_norm_g[:, None, :], wg)


def _group_rms(x, g):
    lane = lax.broadcasted_iota(jnp.int32, x.shape, 1)
    x2 = x * x
    ms = jnp.zeros_like(x)
    for grp in range(N_KV_GROUPS):
        in_grp = (lane >= grp * HEAD_DIM) & (lane < (grp + 1) * HEAD_DIM)
        tot = jnp.sum(jnp.where(in_grp, x2, 0.0), axis=-1, keepdims=True)
        ms = jnp.where(in_grp, tot * (1.0 / HEAD_DIM), ms)
    return x * lax.rsqrt(ms + EPS) * g


def _ones_rows(width):
    return (lax.broadcasted_iota(jnp.int32, (BF16_ROWS, width), 0) == 0).astype(BF16)


def _kv_prep_kernel(c_ref, s_ref, w_ref, gs_ref, gw_ref, kc_ref, vc_ref, ks_ref, vs_ref, kw_ref, vw_ref):
    gw = N_KV_GROUPS * HEAD_DIM
    cc = c_ref[...]
    ss = s_ref[...]
    ww = w_ref[...]
    kc_ref[...] = cc[:, 0:gw].astype(BF16)
    vc_ref[...] = cc[:, gw:2 * gw].astype(BF16)
    ks = _group_rms(ss[:, 0:gw], gs_ref[...]).astype(BF16)
    kw = _group_rms(ww[:, 0:gw], gw_ref[...]).astype(BF16)
    vst = ss[:, gw:2 * gw].T
    vwt = ww[:, gw:2 * gw].T
    for g in range(N_KV_GROUPS):
        lanes = slice(g * HEAD_DIM, (g + 1) * HEAD_DIM)
        ks_ref[g] = ks[:, lanes]
        kw_ref[g] = kw[:, lanes]
        vs_ref[g] = jnp.concatenate([vst[lanes, :].astype(BF16), _ones_rows(SEL_TILE)], axis=0)
        for u in range(SEL_TILE // KEY_TILE):
            vw_ref[g, u] = jnp.concatenate([vwt[lanes, u * KEY_TILE:(u + 1) * KEY_TILE].astype(BF16),
                                            _ones_rows(KEY_TILE)], axis=0)


def kv_prep(z, k_g, b, t):
    n = z.shape[0]
    G = N_KV_GROUPS
    gw = G * HEAD_DIM
    base = N_HEADS * HEAD_DIM // (2 * gw)
    tm = SEL_TILE
    nst = t // tm
    sub = SEL_TILE // KEY_TILE
    row = lambda i, j: i * nst + j
    flat = jax.ShapeDtypeStruct((n, gw), BF16)
    keys = jax.ShapeDtypeStruct((b, G, t, HEAD_DIM), BF16)
    flat_spec = pl.BlockSpec((tm, gw), lambda i, j: (row(i, j), 0))
    key_spec = pl.BlockSpec((None, G, tm, HEAD_DIM), lambda i, j: (i, 0, j, 0))
    return pl.pallas_call(
        _kv_prep_kernel,
        out_shape=(flat, flat, keys, jax.ShapeDtypeStruct((b, G, nst, V_ROWS, tm), BF16),
                   keys, jax.ShapeDtypeStruct((b, G, nst * sub, V_ROWS, KEY_TILE), BF16)),
        grid=(b, nst),
        in_specs=[pl.BlockSpec((tm, 2 * gw), lambda i, j: (row(i, j), base)),
                  pl.BlockSpec((tm, 2 * gw), lambda i, j: (row(i, j), base + 1)),
                  pl.BlockSpec((tm, 2 * gw), lambda i, j: (row(i, j), base + 2)),
                  pl.BlockSpec((1, gw), lambda i, j: (0, 0)),
                  pl.BlockSpec((1, gw), lambda i, j: (0, 0))],
        out_specs=(flat_spec, flat_spec, key_spec,
                   pl.BlockSpec((None, G, None, V_ROWS, tm), lambda i, j: (i, 0, j, 0, 0)),
                   key_spec,
                   pl.BlockSpec((None, G, sub, V_ROWS, KEY_TILE), lambda i, j: (i, 0, j, 0, 0))),
        compiler_params=_cparams(("parallel", "parallel")),
        name="kv_prep",
    )(z, z, z, jnp.tile(k_g[1], G).reshape(1, gw), jnp.tile(k_g[2], G).reshape(1, gw))


def _compress_kernel(x_ref, pos_ref, w1_ref, w2_ref, g_ref, o_ref, *, normalize):
    half = w1_ref.shape[0] // 2
    x = x_ref[...]
    u = _dot(x, w1_ref[0:half, :])
    v = _dot(x, w1_ref[half:2 * half, :])
    rows = u.shape[0]
    posc = _dot(pos_ref[...], w1_ref[...])[0:1, :]
    hid = u + pltpu.roll(v, rows - 1, 0) + posc
    out = _dot(jax.nn.gelu(hid).astype(BF16), w2_ref[...])
    if normalize:
        out = _rms(out, g_ref[...])
    o_ref[...] = out.astype(o_ref.dtype)


def compress(x, pos, w1, w2, g, normalize, ncb):
    n, kdim = x.shape
    posf = jnp.broadcast_to(pos.reshape(1, -1), (8, 2 * kdim)).astype(BF16)
    return pl.pallas_call(
        functools.partial(_compress_kernel, normalize=normalize),
        out_shape=jax.ShapeDtypeStruct((n, HEAD_DIM), BF16),
        grid=(n // ncb,),
        in_specs=[pl.BlockSpec((ncb, kdim), lambda i: (i, 0)),
                  pl.BlockSpec((8, 2 * kdim), lambda i: (0, 0)),
                  pl.BlockSpec((2 * kdim, CMP_HIDDEN), lambda i: (0, 0)),
                  pl.BlockSpec((CMP_HIDDEN, HEAD_DIM), lambda i: (0, 0)),
                  pl.BlockSpec((1, HEAD_DIM), lambda i: (0, 0))],
        out_specs=pl.BlockSpec((ncb, HEAD_DIM), lambda i: (i, 0)),
        compiler_params=_cparams(("parallel",)),
        name="compress",
    )(x, posf, w1.astype(BF16), w2.astype(BF16), g.reshape(1, HEAD_DIM))


def _nsa_kernel(bnd_ref, *refs):
    safe = bnd_ref[0, 0] <= MAX_SAFE_SCORE_BOUND

    @pl.when(safe)
    def _():
        _nsa_body(bnd_ref, *refs, bounded=True)

    @pl.when(jnp.logical_not(safe))
    def _():
        _nsa_body(bnd_ref, *refs, bounded=False)


def _nsa_body(bnd_ref, zq_ref, gt_ref, gb_ref, qg_ref, ovt_ref, kc_ref, vct_ref, ks_ref, vst_ref, kw_ref,
              vwt_ref, o_ref, q_scr, sel_scr, m_scr, acc_scr, oc_scr, *, bounded):
    grp = pl.program_id(1)
    qb = pl.program_id(2)
    QB = Q_BLOCK
    ncp = kc_ref.shape[0]
    ns = ovt_ref.shape[0]
    tpos = qb * QB + lax.broadcasted_iota(jnp.int32, (1, QB), 1)

    def lane_tile(x):
        return jnp.concatenate([x] * HG, axis=1)

    zt = zq_ref[...].T
    qg = qg_ref[...] * (HEAD_DIM ** -0.5 * LOG2E)
    heads = []
    for h in range(HG):
        xh = zt[h * HEAD_DIM:(h + 1) * HEAD_DIM, :]
        ms = jnp.mean(xh * xh, axis=0, keepdims=True)
        heads.append(xh * lax.rsqrt(ms + EPS) * qg)
    q_scr[...] = jnp.concatenate(heads, axis=1).astype(BF16)
    q = q_scr[...]

    gates = jax.nn.sigmoid(gt_ref[...].T + gb_ref[...])
    per_grp = HG * 3
    gsel = gates[0:per_grp, :]
    for g2 in range(1, N_KV_GROUPS):
        gsel = jnp.where(grp == g2, gates[g2 * per_grp:(g2 + 1) * per_grp, :], gsel)

    def gate_row(c):
        return jnp.concatenate([gsel[3 * h + c:3 * h + c + 1, :] for h in range(HG)], axis=1)

    cmp_end = lax.broadcasted_iota(jnp.int32, (ncp, 1), 0) * CMP_STRIDE + (CMP_BLOCK - 1)
    keep = -bnd_ref[0, 0] if bounded else 0.0
    cbias = jnp.where(cmp_end <= tpos, keep, NEG)
    s = _dot(kc_ref[...], q) + lane_tile(cbias)
    e = jnp.exp2(s) if bounded else jnp.exp2(s - jnp.max(s, axis=0, keepdims=True))
    inv = jnp.where(lane_tile(tpos) >= CMP_BLOCK - 1, 1.0 / jnp.sum(e, axis=0, keepdims=True), 0.0)
    p = e * inv
    ocmp = _dot(vct_ref[...], p.astype(BF16))
    psum = p[:, 0:QB]
    for h in range(1, HG):
        psum = psum + p[:, h * QB:(h + 1) * QB]

    n_win = (WINDOW + QB) // KEY_TILE
    first_tile = qb * (QB // KEY_TILE) - WINDOW // KEY_TILE
    win_sub = lax.broadcasted_iota(jnp.int32, (n_win * KEY_TILE, 1), 0)
    tiles = [jnp.maximum(first_tile + u, 0) for u in range(n_win)]
    kwin = jnp.concatenate([kw_ref[j] for j in tiles], axis=0)
    vwin = jnp.concatenate([vwt_ref[j] for j in tiles], axis=1)
    kpos = first_tile * KEY_TILE + win_sub
    wbias = jnp.where((kpos <= tpos) & (kpos > tpos - WINDOW) & (kpos >= 0), keep, NEG)
    sw = _dot(kwin, q) + lane_tile(wbias)
    pw = jnp.exp2(sw) if bounded else jnp.exp2(sw - jnp.max(sw, axis=0, keepdims=True))
    ow = _dot(vwin, pw.astype(BF16))
    oc_scr[...] = gate_row(0) * ocmp + (gate_row(2) / ow[HEAD_DIM:HEAD_DIM + 1, :]) * ow[0:HEAD_DIM, :]

    p_hi = psum.astype(BF16)
    p_lo = (psum - p_hi.astype(F32)).astype(BF16)
    ovt = ovt_ref[...]
    imp = _dot(ovt, p_hi) + _dot(ovt, p_lo)
    blk = lax.broadcasted_iota(jnp.int32, (ns, 1), 0)
    blk_f = blk.astype(F32)
    cur = jnp.right_shift(tpos, SLC_BLOCK.bit_length() - 1)
    forced = (blk == 0) | (blk == cur) | (blk == cur - 1)
    bvalid = blk <= cur
    score = jnp.where(forced, BIG, jnp.where(bvalid, imp, NEG))
    sel = jnp.zeros((ns, QB), F32)
    for _ in range(min(N_SELECT, ns)):
        mx = jnp.max(score, axis=0, keepdims=True)
        first = jnp.min(jnp.where(score == mx, blk_f, float(ns)), axis=0, keepdims=True)
        pick = blk_f == first
        sel = jnp.where(pick, 1.0, sel)
        score = jnp.where(pick, -jnp.inf, score)
    sel_scr[...] = jnp.where(bvalid, sel, 0.0)

    m_scr[...] = jnp.full(m_scr.shape, NEG, F32)
    acc_scr[...] = jnp.zeros(acc_scr.shape, F32)
    blocks_per_tile = SEL_TILE // SLC_BLOCK
    sel_sub = lax.broadcasted_iota(jnp.int32, (SEL_TILE, 1), 0)

    def sel_body(j, carry):
        rows = [jnp.broadcast_to(sel_scr[pl.ds(j * blocks_per_tile + bi, 1), :], (SLC_BLOCK, QB))
                for bi in range(blocks_per_tile)]
        chosen = jnp.concatenate(rows, axis=0)
        kpos = j * SEL_TILE + sel_sub
        bias = jnp.where((chosen > 0.5) & (kpos <= tpos), keep, NEG)
        st = _dot(ks_ref[j], q_scr[...]) + lane_tile(bias)
        if bounded:
            pt = jnp.exp2(st)
            acc_scr[...] += _dot(vst_ref[j], pt.astype(BF16))
        else:
            m_old = m_scr[...]
            m_new = jnp.maximum(m_old, jnp.max(st, axis=0, keepdims=True))
            alpha = jnp.exp2(m_old - m_new)
            pt = jnp.exp2(st - m_new)
            acc_scr[...] = alpha * acc_scr[...] + _dot(vst_ref[j], pt.astype(BF16))
            m_scr[...] = m_new
        return carry

    lax.fori_loop(0, qb // (SEL_TILE // QB) + 1, sel_body, 0)
    ot = (oc_scr[...]
          + (gate_row(1) / acc_scr[HEAD_DIM:HEAD_DIM + 1, :]) * acc_scr[0:HEAD_DIM, :])
    stacked = jnp.concatenate([ot[:, h * QB:(h + 1) * QB] for h in range(HG)], axis=0)
    o_ref[...] = stacked.T.astype(o_ref.dtype)


def nsa_attention(bound, z3, gate_b, q_g, overlap_t, kc, vct, ks, vst, kw, vwt):
    b, t, _ = z3.shape
    nq = t // Q_BLOCK
    nt = t // KEY_TILE
    qw = HG * HEAD_DIM
    ns, ncp = overlap_t.shape
    gate_blk = (NSA_COLS_PAD // LANES_V7X) - 1
    gb = jnp.zeros((LANES_V7X, 1), F32).at[:N_HEADS * 3, 0].set(gate_b)
    full5 = lambda i, g, q: (i, g, 0, 0, 0)
    lanes = HG * Q_BLOCK
    return pl.pallas_call(
        _nsa_kernel,
        out_shape=jax.ShapeDtypeStruct((b, t, N_HEADS * HEAD_DIM), BF16),
        grid=(b, N_KV_GROUPS, nq),
        in_specs=[pl.BlockSpec(memory_space=pltpu.SMEM),
                  pl.BlockSpec((None, Q_BLOCK, qw), lambda i, g, q: (i, q, g)),
                  pl.BlockSpec((None, Q_BLOCK, LANES_V7X), lambda i, g, q: (i, q, gate_blk)),
                  pl.BlockSpec((LANES_V7X, 1), lambda i, g, q: (0, 0)),
                  pl.BlockSpec((HEAD_DIM, 1), lambda i, g, q: (0, 0)),
                  pl.BlockSpec((ns, ncp), lambda i, g, q: (0, 0)),
                  pl.BlockSpec((None, None, ncp, HEAD_DIM), lambda i, g, q: (i, g, 0, 0)),
                  pl.BlockSpec((None, None, HEAD_DIM, ncp), lambda i, g, q: (i, g, 0, 0)),
                  pl.BlockSpec((None, None, t // SEL_TILE, SEL_TILE, HEAD_DIM), full5),
                  pl.BlockSpec((None, None, t // SEL_TILE, V_ROWS, SEL_TILE), full5),
                  pl.BlockSpec((None, None, nt, KEY_TILE, HEAD_DIM), full5),
                  pl.BlockSpec((None, None, nt, V_ROWS, KEY_TILE), full5)],
        out_specs=pl.BlockSpec((None, Q_BLOCK, qw), lambda i, g, q: (i, q, g)),
        scratch_shapes=[pltpu.VMEM((HEAD_DIM, lanes), BF16),
                        pltpu.VMEM((ns, Q_BLOCK), F32),
                        pltpu.VMEM((1, lanes), F32),
                        pltpu.VMEM((V_ROWS, lanes), F32),
                        pltpu.VMEM((HEAD_DIM, lanes), F32)],
        compiler_params=_cparams(("parallel", "parallel", "arbitrary"), VMEM_LIMIT_V7X),
        name="nsa_attention",
    )(bound.reshape(1, 1), z3, z3, gb, q_g.reshape(HEAD_DIM, 1), overlap_t, kc, vct, ks, vst, kw, vwt)


def _overlap_matrix_t(t):
    ncp = t // CMP_STRIDE
    ns = t // SLC_BLOCK
    c_start = np.arange(ncp) * CMP_STRIDE
    sj = np.arange(ns)
    ov = ((c_start[None, :] < (sj[:, None] + 1) * SLC_BLOCK)
          & (c_start[None, :] + CMP_BLOCK > sj[:, None] * SLC_BLOCK)
          & (c_start[None, :] + CMP_BLOCK <= t))
    return jnp.asarray(ov, dtype=BF16)


def ab_layer(h, b, t, norm_g, w_in, conv_w, conv_b, ret_g, ig_b, fg_b, m_g, w_out):
    n = b * t
    w_in_p = jnp.pad(w_in, ((0, 0), (0, AB_COLS_PAD - AB_COLS))).astype(BF16)
    z = norm_matmul(h, norm_g, w_in_p)
    z3 = z.reshape(b, t, AB_COLS_PAD)
    cos_t, sin_t = rope_tables(t)
    ret = retention(z3, cos_t, sin_t, ret_g)
    g0 = AB_COLS - 2 * M_HEADS
    gates_r = z3[:, :, g0:AB_COLS].reshape(b, t // M_CHUNK, M_CHUNK, 2 * M_HEADS).transpose(0, 1, 3, 2)
    ml = mlstm(z3, gates_r, conv_w, conv_b, ig_b, fg_b, m_g)
    w_out_b = w_out.astype(BF16)
    rw = R_HEADS * R_DV
    return [ret.reshape(n, rw), ml.reshape(n, -1)], w_out_b


def nsa_layer(h, b, t, norm_g, w_in, q_g, k_g, pos_k, pos_v, w1k, w2k, w1v, w2v, gate_b, w_out):
    n = b * t
    G = N_KV_GROUPS
    w_in_p = jnp.pad(w_in, ((0, 0), (0, NSA_COLS_PAD - NSA_COLS))).astype(BF16)
    z = norm_matmul(h, norm_g, w_in_p)
    kc_in, vc_in, ks, vst, kw, vwt = kv_prep(z, k_g, b, t)
    ncb = t // CMP_STRIDE

    def cmp_rows(x):
        return x.reshape(b, t, G, HEAD_DIM).transpose(0, 2, 1, 3).reshape(b * G * ncb, CMP_STRIDE * HEAD_DIM)

    kc = compress(cmp_rows(kc_in), pos_k, w1k, w2k, k_g[0], True, ncb)
    vc = compress(cmp_rows(vc_in), pos_v, w1v, w2v, k_g[0], False, ncb)
    kc = kc.reshape(b, G, ncb, HEAD_DIM)
    vct = vc.reshape(b, G, ncb, HEAD_DIM).transpose(0, 1, 3, 2)

    def key_tiles(x, kt):
        return x.reshape(b, G, t // kt, kt, HEAD_DIM)

    bound = 1.02 * LOG2E * math.sqrt(HEAD_DIM) * jnp.max(jnp.abs(q_g)) * jnp.max(jnp.abs(k_g))
    args = (bound, z.reshape(b, t, NSA_COLS_PAD), gate_b, q_g, _overlap_matrix_t(t), kc, vct,
            key_tiles(ks, SEL_TILE), vst, key_tiles(kw, KEY_TILE), vwt)
    o = nsa_attention(*args)
    return [o.reshape(n, -1)], w_out.astype(BF16)


def kernel(x, p, ab_norm_g, ab_w_in, ab_conv_w, ab_conv_b, ab_ret_norm_g, ab_ig_b, ab_fg_b, ab_m_norm_g, ab_w_out, nsa_norm_g, nsa_w_in, nsa_q_norm_g, nsa_k_norm_g, nsa_cmp_pos_k, nsa_cmp_pos_v, nsa_cmp_w1k, nsa_cmp_w2k, nsa_cmp_w1v, nsa_cmp_w2v, nsa_gate_b, nsa_w_out, ffn_norm_g, ffn_w_up, ffn_conv_w, ffn_conv_b, ffn_w_down, ple_w, ple_norm_g, ple_gate_norm_g, ple_w_gate):
    b, t, d = x.shape
    n = b * t
    depth = p.shape[0]
    h = x.reshape(n, d)
    p2 = p.reshape(depth, n, -1)
    ffn_w_up_b, ffn_w_down_b = ffn_w_up.astype(BF16), ffn_w_down.astype(BF16)
    ple_w_b, ple_w_gate_b = ple_w.astype(BF16), ple_w_gate.astype(BF16)
    for i in range(depth):
        j = i // 2
        if i % 2 == 0:
            mix = ab_layer(h, b, t, ab_norm_g[j], ab_w_in[j], ab_conv_w[j], ab_conv_b[j], ab_ret_norm_g[j],
                           ab_ig_b[j], ab_fg_b[j], ab_m_norm_g[j], ab_w_out[j])
        else:
            mix = nsa_layer(h, b, t, nsa_norm_g[j], nsa_w_in[j], nsa_q_norm_g[j], nsa_k_norm_g[j],
                            nsa_cmp_pos_k[j], nsa_cmp_pos_v[j], nsa_cmp_w1k[j], nsa_cmp_w2k[j],
                            nsa_cmp_w1v[j], nsa_cmp_w2v[j], nsa_gate_b[j], nsa_w_out[j])
        h = mix_ffn_ple(h, mix, i, ffn_norm_g, ffn_w_up_b, ffn_conv_w, ffn_conv_b, ffn_w_down_b, p2, ple_w_b,
                        ple_norm_g, ple_gate_norm_g, ple_w_gate_b, t)
    return h.reshape(b, t, d)
```

```python
import functools
import math

import numpy as np
import jax
import jax.numpy as jnp
from jax import lax
from jax.experimental import pallas as pl
from jax.experimental.pallas import tpu as pltpu

F32 = jnp.float32
BF16 = jnp.bfloat16

LANES_V7X = 128
BF16_ROWS = 16
VMEM_LIMIT_V7X = 56 * 1024 * 1024

D_MODEL = 1024
PLE_DIM = 256
R_HEADS, R_DK, R_DV, R_CHUNK = 4, 64, 128, 128
ROPE_BASE = 10000.0
M_HEADS, M_DK, M_DV, M_CHUNK, M_CONV = 4, 64, 128, 64, 4
AB_SIZES = (R_HEADS * R_DK, R_HEADS * R_DK, R_HEADS * R_DV, R_HEADS * R_DV,
            M_HEADS * M_DK, M_HEADS * M_DK, M_HEADS * M_DV, M_HEADS * M_DV, M_HEADS, M_HEADS)
AB_COLS = sum(AB_SIZES)
AB_COLS_PAD = 3200
N_HEADS, N_KV_GROUPS, HEAD_DIM = 16, 2, 64
HG = N_HEADS // N_KV_GROUPS
CMP_BLOCK, CMP_STRIDE, CMP_HIDDEN = 32, 16, 256
SLC_BLOCK, N_SELECT, WINDOW = 64, 16, 512
Q_BLOCK = 256
NSA_COLS = N_HEADS * HEAD_DIM + 6 * N_KV_GROUPS * HEAD_DIM + N_HEADS * 3
NSA_COLS_PAD = 1920
D_FF = 2816
FFN_CONV = 3
NEG = -1e30
BIG = 1e30
EPS = 1e-6
KEY_TILE = 128
SEL_TILE = 512
V_ROWS = HEAD_DIM + BF16_ROWS
LOG2E = math.log2(math.e)
MAX_SAFE_SCORE_BOUND = 56.0


def _cparams(sem, vmem=None):
    return pltpu.CompilerParams(dimension_semantics=sem, vmem_limit_bytes=vmem)


def _rms(x, g):
    ms = jnp.mean(x * x, axis=-1, keepdims=True)
    return x * lax.rsqrt(ms + EPS) * g


def _dot(a, b):
    return jnp.dot(a, b, preferred_element_type=F32)


def _dot_nt(a, b):
    return lax.dot_general(a, b, (((1,), (1,)), ((), ())), preferred_element_type=F32)


def _dot_f32(a, b):
    return jnp.dot(a, b, preferred_element_type=F32, precision=lax.Precision.HIGHEST)


def _gelu_tanh(x):
    k1 = -2.0 * math.sqrt(2.0 / math.pi) * LOG2E
    return x / (1.0 + jnp.exp2(x * (k1 + (k1 * 0.044715) * (x * x))))


def _norm_matmul_kernel(x_ref, g_ref, w_ref, o_ref):
    xn = _rms(x_ref[...], g_ref[...]).astype(BF16)
    o_ref[...] = _dot(xn, w_ref[...]).astype(o_ref.dtype)


def norm_matmul(x, g, w, tm=1024, out_dtype=F32):
    n, d = x.shape
    nc = w.shape[1]
    return pl.pallas_call(
        _norm_matmul_kernel,
        out_shape=jax.ShapeDtypeStruct((n, nc), out_dtype),
        grid=(n // tm,),
        in_specs=[pl.BlockSpec((tm, d), lambda i: (i, 0)),
                  pl.BlockSpec((1, d), lambda i: (0, 0)),
                  pl.BlockSpec((d, nc), lambda i: (0, 0))],
        out_specs=pl.BlockSpec((tm, nc), lambda i: (i, 0)),
        compiler_params=_cparams(("parallel",), VMEM_LIMIT_V7X),
        name="norm_matmul",
    )(x, g.reshape(1, d), w)


def _rope_table_kernel(inv_ref, cos_ref, sin_ref):
    c = pl.program_id(0)
    rows, width = cos_ref.shape
    pos = (c * rows + lax.broadcasted_iota(jnp.int32, (rows, LANES_V7X), 0)).astype(F32)
    lane = lax.broadcasted_iota(jnp.int32, (rows, LANES_V7X), 1)
    ang = pos * inv_ref[:, 0:LANES_V7X]
    cs = jnp.cos(ang)
    sn = jnp.sin(ang)
    sn = jnp.where(lane % R_DK < R_DK // 2, -sn, sn)
    reps = width // LANES_V7X
    cos_ref[...] = jnp.concatenate([cs] * reps, axis=1)
    sin_ref[...] = jnp.concatenate([sn] * reps, axis=1)


def rope_tables(t):
    half = R_DK // 2
    inv = ROPE_BASE ** (-jnp.arange(half, dtype=F32) / half)
    inv = jnp.tile(inv, 2 * R_HEADS).reshape(1, R_HEADS * R_DK)
    width = R_HEADS * R_DK
    shp = jax.ShapeDtypeStruct((t, width), F32)
    return pl.pallas_call(
        _rope_table_kernel,
        out_shape=(shp, shp),
        grid=(t // R_CHUNK,),
        in_specs=[pl.BlockSpec((1, width), lambda c: (0, 0))],
        out_specs=(pl.BlockSpec((R_CHUNK, width), lambda c: (c, 0)),
                   pl.BlockSpec((R_CHUNK, width), lambda c: (c, 0))),
        compiler_params=_cparams(("parallel",)),
        name="rope_tables",
    )(inv)


def _retention_kernel(cos_ref, sin_ref, q_ref, k_ref, v_ref, g_ref, gain_ref, o_ref, r_ref):
    c = pl.program_id(0)
    nb = q_ref.shape[0]
    L = R_CHUNK

    @pl.when(c == 0)
    def _():
        r_ref[...] = jnp.zeros_like(r_ref)

    cos = cos_ref[...]
    sin = sin_ref[...]
    lane = lax.broadcasted_iota(jnp.int32, cos.shape, 1)
    first_half = lane % R_DK < R_DK // 2
    width = R_HEADS * R_DK

    def rot(x):
        swapped = jnp.where(first_half, pltpu.roll(x, width - R_DK // 2, 1), pltpu.roll(x, R_DK // 2, 1))
        return x * cos + swapped * sin

    gain = gain_ref[...]
    ri = lax.broadcasted_iota(jnp.int32, (L, L), 0)
    ci = lax.broadcasted_iota(jnp.int32, (L, L), 1)
    diff = (ri - ci).astype(F32)
    causal = ri >= ci
    idx = lax.broadcasted_iota(jnp.int32, (L, 1), 0).astype(F32)
    decays = []
    for h in range(R_HEADS):
        log_g = math.log1p(-2.0 ** (-5.0 - h))
        decays.append(dict(
            dmask=jnp.where(causal, jnp.exp(jnp.where(causal, diff, 0.0) * log_g), 0.0),
            q_dec=jnp.exp((idx + 1.0) * log_g), k_dec=jnp.exp((L - 1.0 - idx) * log_g),
            c_dec=math.exp(L * log_g)))

    chains = []
    for bi in range(nb):
        q = rot(q_ref[bi])
        k = rot(k_ref[bi]) * (R_DK ** -0.5)
        for h in range(R_HEADS):
            chains.append(dict(bi=bi, h=h, u=bi * R_HEADS + h,
                               qb=q[:, h * R_DK:(h + 1) * R_DK].astype(BF16),
                               kh=k[:, h * R_DK:(h + 1) * R_DK]))

    for ch in chains:
        ch['rstate'] = r_ref[ch['u']]
        ch['s'] = _dot_nt(ch['qb'], ch['kh'].astype(BF16)) * decays[ch['h']]['dmask']
        ch['qr'] = _dot(ch['qb'], ch['rstate'].astype(BF16))

    for ch in chains:
        d = decays[ch['h']]
        vh = v_ref[ch['bi'], :, ch['h'] * R_DV:(ch['h'] + 1) * R_DV].astype(BF16)
        ch['o'] = _dot(ch['s'].astype(BF16), vh) + ch['qr'] * d['q_dec']
        kd = (ch['kh'] * d['k_dec']).T.astype(BF16)
        r_ref[ch['u']] = d['c_dec'] * ch['rstate'] + _dot(kd, vh)

    for ch in chains:
        sl = slice(ch['h'] * R_DV, (ch['h'] + 1) * R_DV)
        gh = g_ref[ch['bi'], :, sl]
        o_ref[ch['bi'], :, sl] = (_rms(ch['o'], gain[:, sl]) * (gh * jax.nn.sigmoid(gh))).astype(o_ref.dtype)


def retention(z3, cos_t, sin_t, gain):
    b, t, _ = z3.shape
    nc = t // R_CHUNK
    qk_w = R_HEADS * R_DK
    v_w = R_HEADS * R_DV
    return pl.pallas_call(
        _retention_kernel,
        out_shape=jax.ShapeDtypeStruct((b, t, v_w), BF16),
        grid=(nc,),
        in_specs=[pl.BlockSpec((R_CHUNK, qk_w), lambda c: (c, 0)),
                  pl.BlockSpec((R_CHUNK, qk_w), lambda c: (c, 0)),
                  pl.BlockSpec((b, R_CHUNK, qk_w), lambda c: (0, c, 0)),
                  pl.BlockSpec((b, R_CHUNK, qk_w), lambda c: (0, c, 1)),
                  pl.BlockSpec((b, R_CHUNK, v_w), lambda c: (0, c, 1)),
                  pl.BlockSpec((b, R_CHUNK, v_w), lambda c: (0, c, 2)),
                  pl.BlockSpec((1, v_w), lambda c: (0, 0))],
        out_specs=pl.BlockSpec((b, R_CHUNK, v_w), lambda c: (0, c, 0)),
        scratch_shapes=[pltpu.VMEM((b * R_HEADS, R_DK, R_DV), F32)],
        compiler_params=_cparams(("arbitrary",)),
        name="retention",
    )(cos_t, sin_t, z3, z3, z3, z3, gain.reshape(1, v_w))


def _mlstm_kernel(q_ref, k_ref, v_ref, og_ref, gc_ref, gr_ref, cw_ref, cb_ref, bc_ref, br_ref, gain_ref,
                  o_ref, xbuf, c_ref, n_ref, m_ref):
    c = pl.program_id(0)
    nb = q_ref.shape[0]
    L = M_CHUNK
    H = M_HEADS
    qk_w = H * M_DK
    halo = 8

    @pl.when(c == 0)
    def _():
        xbuf[:, 0:halo, :] = jnp.zeros((nb, halo, 2 * qk_w), F32)
        c_ref[...] = jnp.zeros_like(c_ref)
        n_ref[...] = jnp.zeros_like(n_ref)
        m_ref[...] = jnp.zeros_like(m_ref)

    gain = gain_ref[...]
    ri = lax.broadcasted_iota(jnp.int32, (L, L), 0)
    ci = lax.broadcasted_iota(jnp.int32, (L, L), 1)
    causal = ri >= ci
    tril = causal.astype(F32)
    triu = (ri <= ci).astype(F32)

    chains = []
    for bi in range(nb):
        xbuf[bi, halo:halo + L, 0:qk_w] = q_ref[bi]
        xbuf[bi, halo:halo + L, qk_w:2 * qk_w] = k_ref[bi]
        conv = cb_ref[...]
        for j in range(M_CONV):
            conv = conv + xbuf[bi, pl.ds(halo - (M_CONV - 1) + j, L), :] * cw_ref[j:j + 1, :]
        tail = xbuf[bi, L:L + halo, :]
        xbuf[bi, 0:halo, :] = tail
        act = conv * jax.nn.sigmoid(conv)
        q = act[:, 0:qk_w] * (M_DK ** -0.5)
        k = act[:, qk_w:2 * qk_w]
        gc = gc_ref[bi][:, 0:2 * H] + bc_ref[...]
        gr = gr_ref[bi] + br_ref[...]
        ig_c = gc[:, 0:H]
        ig_r = gr[0:H, :]
        b_c = _dot_f32(tril, jax.nn.log_sigmoid(gc[:, H:2 * H]))
        b_r = _dot_f32(jax.nn.log_sigmoid(gr[H:2 * H, :]), triu)
        for h in range(H):
            chains.append(dict(
                bi=bi, h=h, u=bi * H + h,
                qh=q[:, h * M_DK:(h + 1) * M_DK], kh=k[:, h * M_DK:(h + 1) * M_DK],
                bh=b_c[:, h:h + 1], brow=b_r[h:h + 1, :], irow=ig_r[h:h + 1, :], icol=ig_c[:, h:h + 1]))

    def stack(parts):
        return jnp.concatenate(parts, axis=0)

    def rows(x):
        return jnp.broadcast_to(x, (L, x.shape[1]))

    m_prev_u = [m_ref[ch['u']][:, 0:1] for ch in chains]
    b_last_u = [ch['bh'][L - 1:L, :] for ch in chains]
    bh = stack([ch['bh'] for ch in chains])
    icol = stack([ch['icol'] for ch in chains])
    brow = stack([rows(ch['brow']) for ch in chains])
    irow = stack([rows(ch['irow']) for ch in chains])
    m_prev = stack([rows(m) for m in m_prev_u])
    b_last = stack([rows(x) for x in b_last_u])
    causal_all = stack([causal] * len(chains))
    qs = stack([ch['qh'] for ch in chains])
    ks = stack([ch['kh'] for ch in chains])
    qb = qs.astype(BF16)
    kb = ks.astype(BF16)
    cstates = [c_ref[ch['u']] for ch in chains]
    nstates = [n_ref[ch['u']] for ch in chains]

    def chain_rows(x, i):
        return x[i * L:(i + 1) * L]

    s_raw = stack([_dot_nt(chain_rows(qb, i), chain_rows(kb, i)) for i in range(len(chains))])
    qc = stack([_dot(chain_rows(qb, i), cstates[i].astype(BF16)) for i in range(len(chains))])

    dlog = jnp.where(causal_all, bh - brow + irow, NEG)
    inter = bh + m_prev
    m_t = jnp.maximum(inter, jnp.max(dlog, axis=-1, keepdims=True))
    s = s_raw * jnp.exp(dlog - m_t)
    w_inter = jnp.exp(inter - m_t)
    wlog = b_last - bh + icol
    m_new_u = [jnp.maximum(b_last_u[i] + m_prev_u[i], jnp.max(chain_rows(wlog, i), axis=0, keepdims=True))
               for i in range(len(chains))]
    m_new = stack([rows(m) for m in m_new_u])
    wk = ks * jnp.exp(wlog - m_new)

    sb = s.astype(BF16)
    vhs = [v_ref[ch['bi'], :, ch['h'] * M_DV:(ch['h'] + 1) * M_DV].astype(BF16) for ch in chains]
    sv = stack([_dot(chain_rows(sb, i), vhs[i]) for i in range(len(chains))])
    kv = [_dot(chain_rows(wk, i).T.astype(BF16), vhs[i]) for i in range(len(chains))]

    qn = jnp.sum(qs * stack([rows(n) for n in nstates]), axis=-1, keepdims=True)
    den = jnp.sum(s, axis=-1, keepdims=True) + w_inter * qn
    hh = (sv + w_inter * qc) / jnp.maximum(jnp.abs(den), jnp.exp(-m_t))
    og = stack([og_ref[ch['bi'], :, ch['h'] * M_DV:(ch['h'] + 1) * M_DV] for ch in chains])
    gains = stack([rows(gain[:, ch['h'] * M_DV:(ch['h'] + 1) * M_DV]) for ch in chains])
    out = jax.nn.sigmoid(og) * _rms(hh, gains)

    for i, ch in enumerate(chains):
        u = ch['u']
        decay = jnp.exp(b_last_u[i] + m_prev_u[i] - m_new_u[i])
        c_ref[u] = decay * cstates[i] + kv[i]
        n_ref[u] = decay * nstates[i] + jnp.sum(chain_rows(wk, i), axis=0, keepdims=True)
        m_ref[u] = jnp.broadcast_to(m_new_u[i], (1, LANES_V7X))
        o_ref[ch['bi'], :, ch['h'] * M_DV:(ch['h'] + 1) * M_DV] = chain_rows(out, i).astype(o_ref.dtype)


def mlstm(z3, gates_r, conv_w, conv_b, ig_b, fg_b, gain):
    b, t, _ = z3.shape
    nc = t // M_CHUNK
    H = M_HEADS
    qk_w = H * M_DK
    v_w = H * M_DV
    bias = jnp.concatenate([ig_b, fg_b])
    gate_blk = AB_COLS_PAD // LANES_V7X - 1
    return pl.pallas_call(
        _mlstm_kernel,
        out_shape=jax.ShapeDtypeStruct((b, t, v_w), BF16),
        grid=(nc,),
        in_specs=[pl.BlockSpec((b, M_CHUNK, qk_w), lambda c: (0, c, 6)),
                  pl.BlockSpec((b, M_CHUNK, qk_w), lambda c: (0, c, 7)),
                  pl.BlockSpec((b, M_CHUNK, v_w), lambda c: (0, c, 4)),
                  pl.BlockSpec((b, M_CHUNK, v_w), lambda c: (0, c, 5)),
                  pl.BlockSpec((b, M_CHUNK, LANES_V7X), lambda c: (0, c, gate_blk)),
                  pl.BlockSpec((b, None, 2 * H, M_CHUNK), lambda c: (0, c, 0, 0)),
                  pl.BlockSpec((M_CONV, 2 * qk_w), lambda c: (0, 0)),
                  pl.BlockSpec((1, 2 * qk_w), lambda c: (0, 0)),
                  pl.BlockSpec((1, 2 * H), lambda c: (0, 0)),
                  pl.BlockSpec((2 * H, 1), lambda c: (0, 0)),
                  pl.BlockSpec((1, v_w), lambda c: (0, 0))],
        out_specs=pl.BlockSpec((b, M_CHUNK, v_w), lambda c: (0, c, 0)),
        scratch_shapes=[pltpu.VMEM((b, 8 + M_CHUNK, 2 * qk_w), F32),
                        pltpu.VMEM((b * H, M_DK, M_DV), F32),
                        pltpu.VMEM((b * H, 1, M_DK), F32),
                        pltpu.VMEM((b * H, 1, LANES_V7X), F32)],
        compiler_params=_cparams(("arbitrary",)),
        name="mlstm",
    )(z3, z3, z3, z3, z3, gates_r, conv_w, conv_b.reshape(1, -1), bias.reshape(1, -1), bias.reshape(-1, 1),
      gain.reshape(1, v_w))


def _mix_ffn_ple_kernel(*refs, n_mix, tm, seq, tf):
    hp_ref, h_ref = refs[0], refs[1]
    mix_refs = refs[2:2 + 2 * n_mix]
    (wo_ref, g_ref, wa_ref, wb_ref, cw_ref, cb_ref, wd_ref, p_ref, wp_ref, ng_ref, gg_ref, wg_ref,
     o_ref, xn_ref, a_ref) = refs[2 + 2 * n_mix:]
    i = pl.program_id(0)
    halo = 8
    g = g_ref[...]
    m_prev = jnp.concatenate([mix_refs[2 * k][...] for k in range(n_mix)], axis=1)
    m_tile = jnp.concatenate([mix_refs[2 * k + 1][...] for k in range(n_mix)], axis=1)
    x = h_ref[...] + _dot(m_tile, wo_ref[...])
    xp = hp_ref[...] + _dot(m_prev, wo_ref[...])[BF16_ROWS - halo:, :]
    xn_ref[halo:halo + tm, :] = _rms(x, g).astype(BF16)
    keep = ((i * tm) % seq != 0).astype(F32)
    xn_ref[0:halo, :] = (_rms(xp, g) * keep).astype(BF16)
    n_chunks = D_FF // tf

    def up_proj(c):
        cs = slice(c * tf, (c + 1) * tf)
        a_ref[c % 2] = _dot(xn_ref[...], wa_ref[:, cs])
        return _dot(xn_ref[halo:halo + tm, :], wb_ref[:, cs])

    h2 = x
    bgate = up_proj(0)
    for c in range(n_chunks):
        cs = slice(c * tf, (c + 1) * tf)
        bgate_next = up_proj(c + 1) if c + 1 < n_chunks else None
        conv = cb_ref[:, cs]
        for t in range(FFN_CONV):
            conv = conv + a_ref[c % 2, pl.ds(halo - (FFN_CONV - 1) + t, tm), :] * cw_ref[t:t + 1, cs]
        act = (_gelu_tanh(conv) * bgate).astype(BF16)
        h2 = h2 + _dot(act, wd_ref[cs, :])
        bgate = bgate_next
    e = _rms(_dot(p_ref[...].astype(BF16), wp_ref[...]), ng_ref[...])
    gate = jax.nn.sigmoid(_dot(_rms(h2, gg_ref[...]).astype(BF16), wg_ref[...]))
    o_ref[...] = h2 + gate * e


def mix_ffn_ple(h, mix, li, g, w_up, conv_w, conv_b, w_down, p, wp, norm_g, gate_norm_g, wg, seq, tm=512, tf=2816):
    mix_outs, w_out = mix
    n, d = h.shape
    pd = p.shape[-1]
    hb = tm // 8
    const = lambda i: (0, 0)
    layer = lambda i: (li, 0, 0)
    prev8 = lambda i: (jnp.maximum(i * hb - 1, 0), 0)
    resident = dict(pipeline_mode=pl.Buffered(1))
    prev16 = lambda i: (jnp.maximum(i * (tm // BF16_ROWS) - 1, 0), 0)
    mix_specs, mix_args = [], []
    for m in mix_outs:
        mix_specs += [pl.BlockSpec((BF16_ROWS, m.shape[1]), prev16),
                      pl.BlockSpec((tm, m.shape[1]), lambda i: (i, 0))]
        mix_args += [m, m]
    mix_specs.append(pl.BlockSpec(w_out.shape, const, **resident))
    mix_args.append(w_out)
    return pl.pallas_call(
        functools.partial(_mix_ffn_ple_kernel, n_mix=len(mix_outs), tm=tm, seq=seq, tf=tf),
        out_shape=jax.ShapeDtypeStruct((n, d), F32),
        grid=(n // tm,),
        in_specs=[pl.BlockSpec((8, d), prev8),
                  pl.BlockSpec((tm, d), lambda i: (i, 0))] + mix_specs + [
                  pl.BlockSpec((None, 1, d), layer),
                  pl.BlockSpec((None, d, D_FF), lambda i: (li, 0, 0), **resident),
                  pl.BlockSpec((None, d, D_FF), lambda i: (li, 0, 1), **resident),
                  pl.BlockSpec((None, FFN_CONV, D_FF), layer),
                  pl.BlockSpec((None, 1, D_FF), layer),
                  pl.BlockSpec((None, D_FF, d), layer, **resident),
                  pl.BlockSpec((None, tm, pd), lambda i: (li, i, 0)),
                  pl.BlockSpec((None, pd, d), layer, **resident),
                  pl.BlockSpec((None, 1, d), layer),
                  pl.BlockSpec((None, 1, d), layer),
                  pl.BlockSpec((None, d, d), layer, **resident)],
        out_specs=pl.BlockSpec((tm, d), lambda i: (i, 0)),
        scratch_shapes=[pltpu.VMEM((8 + tm, d), BF16),
                        pltpu.VMEM((min(2, D_FF // tf), 8 + tm, tf), F32)],
        compiler_params=_cparams(("parallel",), VMEM_LIMIT_V7X),
        name="mix_ffn_ple",
    )(h, h, *mix_args, g[:, None, :], w_up, w_up, conv_w, conv_b[:, None, :], w_down,
      p, wp, norm_g[:, None, :], gate_norm_g[:, None, :], wg)


def _group_rms(x, g):
    lane = lax.broadcasted_iota(jnp.int32, x.shape, 1)
    x2 = x * x
    ms = jnp.zeros_like(x)
    for grp in range(N_KV_GROUPS):
        in_grp = (lane >= grp * HEAD_DIM) & (lane < (grp + 1) * HEAD_DIM)
        tot = jnp.sum(jnp.where(in_grp, x2, 0.0), axis=-1, keepdims=True)
        ms = jnp.where(in_grp, tot * (1.0 / HEAD_DIM), ms)
    return x * lax.rsqrt(ms + EPS) * g


def _ones_rows(width):
    return (lax.broadcasted_iota(jnp.int32, (BF16_ROWS, width), 0) == 0).astype(BF16)


def _kv_prep_kernel(c_ref, s_ref, w_ref, gs_ref, gw_ref, kc_ref, vc_ref, ks_ref, vs_ref, kw_ref, vw_ref):
    gw = N_KV_GROUPS * HEAD_DIM
    cc = c_ref[...]
    ss = s_ref[...]
    ww = w_ref[...]
    kc_ref[...] = cc[:, 0:gw]
    vc_ref[...] = cc[:, gw:2 * gw]
    ks = _group_rms(ss[:, 0:gw], gs_ref[...]).astype(BF16)
    kw = _group_rms(ww[:, 0:gw], gw_ref[...]).astype(BF16)
    vst = ss[:, gw:2 * gw].T
    vwt = ww[:, gw:2 * gw].T
    for g in range(N_KV_GROUPS):
        lanes = slice(g * HEAD_DIM, (g + 1) * HEAD_DIM)
        ks_ref[g] = ks[:, lanes]
        kw_ref[g] = kw[:, lanes]
        vs_ref[g] = jnp.concatenate([vst[lanes, :].astype(BF16), _ones_rows(SEL_TILE)], axis=0)
        for u in range(SEL_TILE // KEY_TILE):
            vw_ref[g, u] = jnp.concatenate([vwt[lanes, u * KEY_TILE:(u + 1) * KEY_TILE].astype(BF16),
                                            _ones_rows(KEY_TILE)], axis=0)


def kv_prep(z, k_g, b, t):
    n = z.shape[0]
    G = N_KV_GROUPS
    gw = G * HEAD_DIM
    base = N_HEADS * HEAD_DIM // (2 * gw)
    tm = SEL_TILE
    nst = t // tm
    sub = SEL_TILE // KEY_TILE
    row = lambda i, j: i * nst + j
    flat = jax.ShapeDtypeStruct((n, gw), F32)
    keys = jax.ShapeDtypeStruct((b, G, t, HEAD_DIM), BF16)
    flat_spec = pl.BlockSpec((tm, gw), lambda i, j: (row(i, j), 0))
    key_spec = pl.BlockSpec((None, G, tm, HEAD_DIM), lambda i, j: (i, 0, j, 0))
    return pl.pallas_call(
        _kv_prep_kernel,
        out_shape=(flat, flat, keys, jax.ShapeDtypeStruct((b, G, nst, V_ROWS, tm), BF16),
                   keys, jax.ShapeDtypeStruct((b, G, nst * sub, V_ROWS, KEY_TILE), BF16)),
        grid=(b, nst),
        in_specs=[pl.BlockSpec((tm, 2 * gw), lambda i, j: (row(i, j), base)),
                  pl.BlockSpec((tm, 2 * gw), lambda i, j: (row(i, j), base + 1)),
                  pl.BlockSpec((tm, 2 * gw), lambda i, j: (row(i, j), base + 2)),
                  pl.BlockSpec((1, gw), lambda i, j: (0, 0)),
                  pl.BlockSpec((1, gw), lambda i, j: (0, 0))],
        out_specs=(flat_spec, flat_spec, key_spec,
                   pl.BlockSpec((None, G, None, V_ROWS, tm), lambda i, j: (i, 0, j, 0, 0)),
                   key_spec,
                   pl.BlockSpec((None, G, sub, V_ROWS, KEY_TILE), lambda i, j: (i, 0, j, 0, 0))),
        compiler_params=_cparams(("parallel", "parallel")),
        name="kv_prep",
    )(z, z, z, jnp.tile(k_g[1], G).reshape(1, gw), jnp.tile(k_g[2], G).reshape(1, gw))


def _compress_kernel(x_ref, pos_ref, w1_ref, wbd_ref, w2_ref, g_ref, o_ref, *, normalize, transposed):
    G = N_KV_GROUPS
    ncb = x_ref.shape[0] // CMP_STRIDE
    u = jnp.zeros((ncb, G * CMP_HIDDEN), F32)
    v = jnp.zeros((ncb, G * CMP_HIDDEN), F32)
    for r in range(CMP_STRIDE):
        xr = x_ref[pl.ds(r, ncb, stride=CMP_STRIDE), :].astype(BF16)
        u = u + _dot(xr, wbd_ref[0, r])
        v = v + _dot(xr, wbd_ref[1, r])
    posc = _dot(pos_ref[...], w1_ref[...])[0:1, :]
    hid = u + pltpu.roll(v, ncb - 1, 0) + jnp.concatenate([posc] * G, axis=1)
    outs = []
    for grp in range(G):
        out = _dot(jax.nn.gelu(hid[:, grp * CMP_HIDDEN:(grp + 1) * CMP_HIDDEN]).astype(BF16), w2_ref[...])
        outs.append(_rms(out, g_ref[...]) if normalize else out)
    if transposed:
        both = jnp.concatenate(outs, axis=1).T
        for grp in range(G):
            o_ref[grp] = both[grp * HEAD_DIM:(grp + 1) * HEAD_DIM, :].astype(o_ref.dtype)
    else:
        for grp in range(G):
            o_ref[grp] = outs[grp].astype(o_ref.dtype)


def compress(x3, pos, w1, w2, g, normalize, transposed):
    b, t, gw = x3.shape
    G = N_KV_GROUPS
    ncb = t // CMP_STRIDE
    kdim = CMP_STRIDE * HEAD_DIM
    posf = jnp.broadcast_to(pos.reshape(1, -1), (8, 2 * kdim)).astype(BF16)
    w1b = w1.astype(BF16)
    w1r = w1b.reshape(2, CMP_STRIDE, HEAD_DIM, CMP_HIDDEN)
    eye = jnp.eye(G, dtype=BF16)
    wbd = jnp.einsum('gh,srdc->srgdhc', eye, w1r).reshape(2, CMP_STRIDE, G * HEAD_DIM, G * CMP_HIDDEN)
    out_sds = (jax.ShapeDtypeStruct((b, G, HEAD_DIM, ncb), BF16) if transposed
               else jax.ShapeDtypeStruct((b, G, ncb, HEAD_DIM), BF16))
    out_block = (None, G, HEAD_DIM, ncb) if transposed else (None, G, ncb, HEAD_DIM)
    return pl.pallas_call(
        functools.partial(_compress_kernel, normalize=normalize, transposed=transposed),
        out_shape=out_sds,
        grid=(b,),
        in_specs=[pl.BlockSpec((None, t, gw), lambda i: (i, 0, 0)),
                  pl.BlockSpec((8, 2 * kdim), lambda i: (0, 0)),
                  pl.BlockSpec((2 * kdim, CMP_HIDDEN), lambda i: (0, 0)),
                  pl.BlockSpec((2, CMP_STRIDE, G * HEAD_DIM, G * CMP_HIDDEN), lambda i: (0, 0, 0, 0)),
                  pl.BlockSpec((CMP_HIDDEN, HEAD_DIM), lambda i: (0, 0)),
                  pl.BlockSpec((1, HEAD_DIM), lambda i: (0, 0))],
        out_specs=pl.BlockSpec(out_block, lambda i: (i, 0, 0, 0)),
        compiler_params=_cparams(("parallel",)),
        name="compress",
    )(x3, posf, w1b, wbd, w2.astype(BF16), g.reshape(1, HEAD_DIM))


def _nsa_kernel(bnd_ref, *refs):
    safe = bnd_ref[0, 0] <= MAX_SAFE_SCORE_BOUND

    @pl.when(safe)
    def _():
        _nsa_body(bnd_ref, *refs, bounded=True)

    @pl.when(jnp.logical_not(safe))
    def _():
        _nsa_body(bnd_ref, *refs, bounded=False)


def _nsa_body(bnd_ref, zq_ref, gt_ref, gb_ref, qg_ref, ovt_ref, kc_ref, vct_ref, ks_ref, vst_ref, kw_ref,
              vwt_ref, o_ref, q_scr, sel_scr, m_scr, acc_scr, oc_scr, *, bounded):
    grp = pl.program_id(1)
    qb = pl.program_id(2)
    QB = Q_BLOCK
    ncp = kc_ref.shape[0]
    ns = ovt_ref.shape[0]
    tpos = qb * QB + lax.broadcasted_iota(jnp.int32, (1, QB), 1)

    def lane_tile(x):
        return jnp.concatenate([x] * HG, axis=1)

    zt = zq_ref[...].T
    qg = qg_ref[...] * (HEAD_DIM ** -0.5 * LOG2E)
    heads = []
    for h in range(HG):
        xh = zt[h * HEAD_DIM:(h + 1) * HEAD_DIM, :]
        ms = jnp.mean(xh * xh, axis=0, keepdims=True)
        heads.append(xh * lax.rsqrt(ms + EPS) * qg)
    q_scr[...] = jnp.concatenate(heads, axis=1).astype(BF16)
    q = q_scr[...]

    gates = jax.nn.sigmoid(gt_ref[...].T + gb_ref[...])
    per_grp = HG * 3
    gsel = gates[0:per_grp, :]
    for g2 in range(1, N_KV_GROUPS):
        gsel = jnp.where(grp == g2, gates[g2 * per_grp:(g2 + 1) * per_grp, :], gsel)

    def gate_row(c):
        return jnp.concatenate([gsel[3 * h + c:3 * h + c + 1, :] for h in range(HG)], axis=1)

    cmp_end = lax.broadcasted_iota(jnp.int32, (ncp, 1), 0) * CMP_STRIDE + (CMP_BLOCK - 1)
    keep = -bnd_ref[0, 0] if bounded else 0.0
    cbias = jnp.where(cmp_end <= tpos, keep, NEG)
    s = _dot(kc_ref[...], q) + lane_tile(cbias)
    e = jnp.exp2(s) if bounded else jnp.exp2(s - jnp.max(s, axis=0, keepdims=True))
    inv = jnp.where(lane_tile(tpos) >= CMP_BLOCK - 1, 1.0 / jnp.sum(e, axis=0, keepdims=True), 0.0)
    p = e * inv
    ocmp = _dot(vct_ref[...], p.astype(BF16))
    psum = p[:, 0:QB]
    for h in range(1, HG):
        psum = psum + p[:, h * QB:(h + 1) * QB]

    n_win = (WINDOW + QB) // KEY_TILE
    first_tile = qb * (QB // KEY_TILE) - WINDOW // KEY_TILE
    win_sub = lax.broadcasted_iota(jnp.int32, (n_win * KEY_TILE, 1), 0)
    tiles = [jnp.maximum(first_tile + u, 0) for u in range(n_win)]
    kwin = jnp.concatenate([kw_ref[j] for j in tiles], axis=0)
    vwin = jnp.concatenate([vwt_ref[j] for j in tiles], axis=1)
    kpos = first_tile * KEY_TILE + win_sub
    wbias = jnp.where((kpos <= tpos) & (kpos > tpos - WINDOW) & (kpos >= 0), keep, NEG)
    sw = _dot(kwin, q) + lane_tile(wbias)
    pw = jnp.exp2(sw) if bounded else jnp.exp2(sw - jnp.max(sw, axis=0, keepdims=True))
    ow = _dot(vwin, pw.astype(BF16))
    oc_scr[...] = gate_row(0) * ocmp + (gate_row(2) / ow[HEAD_DIM:HEAD_DIM + 1, :]) * ow[0:HEAD_DIM, :]

    p_hi = psum.astype(BF16)
    p_lo = (psum - p_hi.astype(F32)).astype(BF16)
    ovt = ovt_ref[...]
    imp = _dot(ovt, p_hi) + _dot(ovt, p_lo)
    blk = lax.broadcasted_iota(jnp.int32, (ns, 1), 0)
    blk_f = blk.astype(F32)
    cur = jnp.right_shift(tpos, SLC_BLOCK.bit_length() - 1)
    forced = (blk == 0) | (blk == cur) | (blk == cur - 1)
    bvalid = blk <= cur
    score = jnp.where(forced, BIG, jnp.where(bvalid, imp, NEG))
    sel = jnp.zeros((ns, QB), F32)
    for _ in range(min(N_SELECT, ns)):
        mx = jnp.max(score, axis=0, keepdims=True)
        first = jnp.min(jnp.where(score == mx, blk_f, float(ns)), axis=0, keepdims=True)
        pick = blk_f == first
        sel = jnp.where(pick, 1.0, sel)
        score = jnp.where(pick, -jnp.inf, score)
    sel_scr[...] = jnp.where(bvalid, sel, 0.0)

    m_scr[...] = jnp.full(m_scr.shape, NEG, F32)
    acc_scr[...] = jnp.zeros(acc_scr.shape, F32)
    blocks_per_tile = SEL_TILE // SLC_BLOCK
    sel_sub = lax.broadcasted_iota(jnp.int32, (SEL_TILE, 1), 0)

    def sel_body(j, carry):
        rows = [jnp.broadcast_to(sel_scr[pl.ds(j * blocks_per_tile + bi, 1), :], (SLC_BLOCK, QB))
                for bi in range(blocks_per_tile)]
        chosen = jnp.concatenate(rows, axis=0)
        kpos = j * SEL_TILE + sel_sub
        bias = jnp.where((chosen > 0.5) & (kpos <= tpos), keep, NEG)
        st = _dot(ks_ref[j], q_scr[...]) + lane_tile(bias)
        if bounded:
            pt = jnp.exp2(st)
            acc_scr[...] += _dot(vst_ref[j], pt.astype(BF16))
        else:
            m_old = m_scr[...]
            m_new = jnp.maximum(m_old, jnp.max(st, axis=0, keepdims=True))
            alpha = jnp.exp2(m_old - m_new)
            pt = jnp.exp2(st - m_new)
            acc_scr[...] = alpha * acc_scr[...] + _dot(vst_ref[j], pt.astype(BF16))
            m_scr[...] = m_new
        return carry

    lax.fori_loop(0, qb // (SEL_TILE // QB) + 1, sel_body, 0)
    ot = (oc_scr[...]
          + (gate_row(1) / acc_scr[HEAD_DIM:HEAD_DIM + 1, :]) * acc_scr[0:HEAD_DIM, :])
    stacked = jnp.concatenate([ot[:, h * QB:(h + 1) * QB] for h in range(HG)], axis=0)
    o_ref[...] = stacked.T.astype(o_ref.dtype)


def nsa_attention(bound, z3, gate_b, q_g, overlap_t, kc, vct, ks, vst, kw, vwt):
    b, t, _ = z3.shape
    nq = t // Q_BLOCK
    nt = t // KEY_TILE
    qw = HG * HEAD_DIM
    ns, ncp = overlap_t.shape
    gate_blk = (NSA_COLS_PAD // LANES_V7X) - 1
    gb = jnp.zeros((LANES_V7X, 1), F32).at[:N_HEADS * 3, 0].set(gate_b)
    full5 = lambda i, g, q: (i, g, 0, 0, 0)
    lanes = HG * Q_BLOCK
    return pl.pallas_call(
        _nsa_kernel,
        out_shape=jax.ShapeDtypeStruct((b, t, N_HEADS * HEAD_DIM), BF16),
        grid=(b, N_KV_GROUPS, nq),
        in_specs=[pl.BlockSpec(memory_space=pltpu.SMEM),
                  pl.BlockSpec((None, Q_BLOCK, qw), lambda i, g, q: (i, q, g)),
                  pl.BlockSpec((None, Q_BLOCK, LANES_V7X), lambda i, g, q: (i, q, gate_blk)),
                  pl.BlockSpec((LANES_V7X, 1), lambda i, g, q: (0, 0)),
                  pl.BlockSpec((HEAD_DIM, 1), lambda i, g, q: (0, 0)),
                  pl.BlockSpec((ns, ncp), lambda i, g, q: (0, 0)),
                  pl.BlockSpec((None, None, ncp, HEAD_DIM), lambda i, g, q: (i, g, 0, 0)),
                  pl.BlockSpec((None, None, HEAD_DIM, ncp), lambda i, g, q: (i, g, 0, 0)),
                  pl.BlockSpec((None, None, t // SEL_TILE, SEL_TILE, HEAD_DIM), full5),
                  pl.BlockSpec((None, None, t // SEL_TILE, V_ROWS, SEL_TILE), full5),
                  pl.BlockSpec((None, None, nt, KEY_TILE, HEAD_DIM), full5),
                  pl.BlockSpec((None, None, nt, V_ROWS, KEY_TILE), full5)],
        out_specs=pl.BlockSpec((None, Q_BLOCK, qw), lambda i, g, q: (i, q, g)),
        scratch_shapes=[pltpu.VMEM((HEAD_DIM, lanes), BF16),
                        pltpu.VMEM((ns, Q_BLOCK), F32),
                        pltpu.VMEM((1, lanes), F32),
                        pltpu.VMEM((V_ROWS, lanes), F32),
                        pltpu.VMEM((HEAD_DIM, lanes), F32)],
        compiler_params=_cparams(("parallel", "parallel", "arbitrary"), VMEM_LIMIT_V7X),
        name="nsa_attention",
    )(bound.reshape(1, 1), z3, z3, gb, q_g.reshape(HEAD_DIM, 1), overlap_t, kc, vct, ks, vst, kw, vwt)


def _overlap_matrix_t(t):
    ncp = t // CMP_STRIDE
    ns = t // SLC_BLOCK
    c_start = np.arange(ncp) * CMP_STRIDE
    sj = np.arange(ns)
    ov = ((c_start[None, :] < (sj[:, None] + 1) * SLC_BLOCK)
          & (c_start[None, :] + CMP_BLOCK > sj[:, None] * SLC_BLOCK)
          & (c_start[None, :] + CMP_BLOCK <= t))
    return jnp.asarray(ov, dtype=BF16)


def ab_layer(h, b, t, norm_g, w_in, conv_w, conv_b, ret_g, ig_b, fg_b, m_g, w_out):
    n = b * t
    w_in_p = jnp.pad(w_in, ((0, 0), (0, AB_COLS_PAD - AB_COLS))).astype(BF16)
    z = norm_matmul(h, norm_g, w_in_p)
    z3 = z.reshape(b, t, AB_COLS_PAD)
    cos_t, sin_t = rope_tables(t)
    ret = retention(z3, cos_t, sin_t, ret_g)
    g0 = AB_COLS - 2 * M_HEADS
    gates_r = z3[:, :, g0:AB_COLS].reshape(b, t // M_CHUNK, M_CHUNK, 2 * M_HEADS).transpose(0, 1, 3, 2)
    ml = mlstm(z3, gates_r, conv_w, conv_b, ig_b, fg_b, m_g)
    w_out_b = w_out.astype(BF16)
    rw = R_HEADS * R_DV
    return [ret.reshape(n, rw), ml.reshape(n, -1)], w_out_b


def nsa_layer(h, b, t, norm_g, w_in, q_g, k_g, pos_k, pos_v, w1k, w2k, w1v, w2v, gate_b, w_out):
    n = b * t
    G = N_KV_GROUPS
    w_in_p = jnp.pad(w_in, ((0, 0), (0, NSA_COLS_PAD - NSA_COLS))).astype(BF16)
    z = norm_matmul(h, norm_g, w_in_p)
    kc_in, vc_in, ks, vst, kw, vwt = kv_prep(z, k_g, b, t)
    gw = G * HEAD_DIM
    kc = compress(kc_in.reshape(b, t, gw), pos_k, w1k, w2k, k_g[0], True, False)
    vct = compress(vc_in.reshape(b, t, gw), pos_v, w1v, w2v, k_g[0], False, True)

    def key_tiles(x, kt):
        return x.reshape(b, G, t // kt, kt, HEAD_DIM)

    bound = 1.02 * LOG2E * math.sqrt(HEAD_DIM) * jnp.max(jnp.abs(q_g)) * jnp.max(jnp.abs(k_g))
    args = (bound, z.reshape(b, t, NSA_COLS_PAD), gate_b, q_g, _overlap_matrix_t(t), kc, vct,
            key_tiles(ks, SEL_TILE), vst, key_tiles(kw, KEY_TILE), vwt)
    o = nsa_attention(*args)
    return [o.reshape(n, -1)], w_out.astype(BF16)


def kernel(x, p, ab_norm_g, ab_w_in, ab_conv_w, ab_conv_b, ab_ret_norm_g, ab_ig_b, ab_fg_b, ab_m_norm_g, ab_w_out, nsa_norm_g, nsa_w_in, nsa_q_norm_g, nsa_k_norm_g, nsa_cmp_pos_k, nsa_cmp_pos_v, nsa_cmp_w1k, nsa_cmp_w2k, nsa_cmp_w1v, nsa_cmp_w2v, nsa_gate_b, nsa_w_out, ffn_norm_g, ffn_w_up, ffn_conv_w, ffn_conv_b, ffn_w_down, ple_w, ple_norm_g, ple_gate_norm_g, ple_w_gate):
    b, t, d = x.shape
    n = b * t
    depth = p.shape[0]
    h = x.reshape(n, d)
    p2 = p.reshape(depth, n, -1)
    ffn_w_up_b, ffn_w_down_b = ffn_w_up.astype(BF16), ffn_w_down.astype(BF16)
    ple_w_b, ple_w_gate_b = ple_w.astype(BF16), ple_w_gate.astype(BF16)
    for i in range(depth):
        j = i // 2
        if i % 2 == 0:
            mix = ab_layer(h, b, t, ab_norm_g[j], ab_w_in[j], ab_conv_w[j], ab_conv_b[j], ab_ret_norm_g[j],
                           ab_ig_b[j], ab_fg_b[j], ab_m_norm_g[j], ab_w_out[j])
        else:
            mix = nsa_layer(h, b, t, nsa_norm_g[j], nsa_w_in[j], nsa_q_norm_g[j], nsa_k_norm_g[j],
                            nsa_cmp_pos_k[j], nsa_cmp_pos_v[j], nsa_cmp_w1k[j], nsa_cmp_w2k[j],
                            nsa_cmp_w1v[j], nsa_cmp_w2v[j], nsa_gate_b[j], nsa_w_out[j])
        h = mix_ffn_ple(h, mix, i, ffn_norm_g, ffn_w_up_b, ffn_conv_w, ffn_conv_b, ffn_w_down_b, p2, ple_w_b,
                        ple_norm_g, ple_gate_norm_g, ple_w_gate_b, t)
    return h.reshape(b, t, d)
```

```python
import functools
import math

import numpy as np
import jax
import jax.numpy as jnp
from jax import lax
from jax.experimental import pallas as pl
from jax.experimental.pallas import tpu as pltpu

F32 = jnp.float32
BF16 = jnp.bfloat16

LANES_V7X = 128
BF16_ROWS = 16
VMEM_LIMIT_V7X = 56 * 1024 * 1024

D_MODEL = 1024
PLE_DIM = 256
R_HEADS, R_DK, R_DV, R_CHUNK = 4, 64, 128, 128
ROPE_BASE = 10000.0
M_HEADS, M_DK, M_DV, M_CHUNK, M_CONV = 4, 64, 128, 64, 4
AB_SIZES = (R_HEADS * R_DK, R_HEADS * R_DK, R_HEADS * R_DV, R_HEADS * R_DV,
            M_HEADS * M_DK, M_HEADS * M_DK, M_HEADS * M_DV, M_HEADS * M_DV, M_HEADS, M_HEADS)
AB_COLS = sum(AB_SIZES)
AB_COLS_PAD = 3200
N_HEADS, N_KV_GROUPS, HEAD_DIM = 16, 2, 64
HG = N_HEADS // N_KV_GROUPS
CMP_BLOCK, CMP_STRIDE, CMP_HIDDEN = 32, 16, 256
SLC_BLOCK, N_SELECT, WINDOW = 64, 16, 512
Q_BLOCK = 256
NSA_COLS = N_HEADS * HEAD_DIM + 6 * N_KV_GROUPS * HEAD_DIM + N_HEADS * 3
NSA_COLS_PAD = 1920
D_FF = 2816
FFN_CONV = 3
NEG = -1e30
EPS = 1e-6
KEY_TILE = 128
SEL_TILE = 512
V_ROWS = HEAD_DIM + BF16_ROWS
LOG2E = math.log2(math.e)
MAX_SAFE_SCORE_BOUND = 56.0


def _cparams(sem, vmem=None):
    return pltpu.CompilerParams(dimension_semantics=sem, vmem_limit_bytes=vmem)


def _rms(x, g):
    ms = jnp.mean(x * x, axis=-1, keepdims=True)
    return x * lax.rsqrt(ms + EPS) * g


def _dot(a, b):
    return jnp.dot(a, b, preferred_element_type=F32)


def _dot_nt(a, b):
    return lax.dot_general(a, b, (((1,), (1,)), ((), ())), preferred_element_type=F32)


def _dot_f32(a, b):
    return jnp.dot(a, b, preferred_element_type=F32, precision=lax.Precision.HIGHEST)


def _gelu_tanh(x):
    k1 = -2.0 * math.sqrt(2.0 / math.pi) * LOG2E
    return x / (1.0 + jnp.exp2(x * (k1 + (k1 * 0.044715) * (x * x))))


def _norm_matmul_kernel(x_ref, g_ref, w_ref, o_ref):
    xn = _rms(x_ref[...], g_ref[...]).astype(BF16)
    o_ref[...] = _dot(xn, w_ref[...]).astype(o_ref.dtype)


def norm_matmul(x, g, w, tm=1024, out_dtype=F32):
    n, d = x.shape
    nc = w.shape[1]
    return pl.pallas_call(
        _norm_matmul_kernel,
        out_shape=jax.ShapeDtypeStruct((n, nc), out_dtype),
        grid=(n // tm,),
        in_specs=[pl.BlockSpec((tm, d), lambda i: (i, 0)),
                  pl.BlockSpec((1, d), lambda i: (0, 0)),
                  pl.BlockSpec((d, nc), lambda i: (0, 0))],
        out_specs=pl.BlockSpec((tm, nc), lambda i: (i, 0)),
        compiler_params=_cparams(("parallel",), VMEM_LIMIT_V7X),
        name="norm_matmul",
    )(x, g.reshape(1, d), w)


def _rope_table_kernel(inv_ref, cos_ref, sin_ref):
    c = pl.program_id(0)
    rows, width = cos_ref.shape
    pos = (c * rows + lax.broadcasted_iota(jnp.int32, (rows, LANES_V7X), 0)).astype(F32)
    lane = lax.broadcasted_iota(jnp.int32, (rows, LANES_V7X), 1)
    ang = pos * inv_ref[:, 0:LANES_V7X]
    cs = jnp.cos(ang)
    sn = jnp.sin(ang)
    sn = jnp.where(lane % R_DK < R_DK // 2, -sn, sn)
    reps = width // LANES_V7X
    cos_ref[...] = jnp.concatenate([cs] * reps, axis=1)
    sin_ref[...] = jnp.concatenate([sn] * reps, axis=1)


def rope_tables(t):
    half = R_DK // 2
    inv = ROPE_BASE ** (-jnp.arange(half, dtype=F32) / half)
    inv = jnp.tile(inv, 2 * R_HEADS).reshape(1, R_HEADS * R_DK)
    width = R_HEADS * R_DK
    shp = jax.ShapeDtypeStruct((t, width), F32)
    return pl.pallas_call(
        _rope_table_kernel,
        out_shape=(shp, shp),
        grid=(t // R_CHUNK,),
        in_specs=[pl.BlockSpec((1, width), lambda c: (0, 0))],
        out_specs=(pl.BlockSpec((R_CHUNK, width), lambda c: (c, 0)),
                   pl.BlockSpec((R_CHUNK, width), lambda c: (c, 0))),
        compiler_params=_cparams(("parallel",)),
        name="rope_tables",
    )(inv)


def _retention_kernel(cos_ref, sin_ref, q_ref, k_ref, v_ref, g_ref, gain_ref, o_ref, r_ref):
    c = pl.program_id(0)
    nb = q_ref.shape[0]
    L = R_CHUNK

    @pl.when(c == 0)
    def _():
        r_ref[...] = jnp.zeros_like(r_ref)

    cos = cos_ref[...]
    sin = sin_ref[...]
    lane = lax.broadcasted_iota(jnp.int32, cos.shape, 1)
    first_half = lane % R_DK < R_DK // 2
    width = R_HEADS * R_DK

    def rot(x):
        swapped = jnp.where(first_half, pltpu.roll(x, width - R_DK // 2, 1), pltpu.roll(x, R_DK // 2, 1))
        return x * cos + swapped * sin

    gain = gain_ref[...]
    ri = lax.broadcasted_iota(jnp.int32, (L, L), 0)
    ci = lax.broadcasted_iota(jnp.int32, (L, L), 1)
    diff = (ri - ci).astype(F32)
    causal = ri >= ci
    idx = lax.broadcasted_iota(jnp.int32, (L, 1), 0).astype(F32)
    decays = []
    for h in range(R_HEADS):
        log_g = math.log1p(-2.0 ** (-5.0 - h))
        decays.append(dict(
            dmask=jnp.where(causal, jnp.exp(jnp.where(causal, diff, 0.0) * log_g), 0.0),
            q_dec=jnp.exp((idx + 1.0) * log_g), k_dec=jnp.exp((L - 1.0 - idx) * log_g),
            c_dec=math.exp(L * log_g)))

    chains = []
    for bi in range(nb):
        q = rot(q_ref[bi])
        k = rot(k_ref[bi]) * (R_DK ** -0.5)
        for h in range(R_HEADS):
            chains.append(dict(bi=bi, h=h, u=bi * R_HEADS + h,
                               qb=q[:, h * R_DK:(h + 1) * R_DK].astype(BF16),
                               kh=k[:, h * R_DK:(h + 1) * R_DK]))

    for ch in chains:
        ch['rstate'] = r_ref[ch['u']]
        ch['s'] = _dot_nt(ch['qb'], ch['kh'].astype(BF16)) * decays[ch['h']]['dmask']
        ch['qr'] = _dot(ch['qb'], ch['rstate'].astype(BF16))

    for ch in chains:
        d = decays[ch['h']]
        vh = v_ref[ch['bi'], :, ch['h'] * R_DV:(ch['h'] + 1) * R_DV].astype(BF16)
        ch['o'] = _dot(ch['s'].astype(BF16), vh) + ch['qr'] * d['q_dec']
        kd = (ch['kh'] * d['k_dec']).T.astype(BF16)
        r_ref[ch['u']] = d['c_dec'] * ch['rstate'] + _dot(kd, vh)

    for ch in chains:
        sl = slice(ch['h'] * R_DV, (ch['h'] + 1) * R_DV)
        gh = g_ref[ch['bi'], :, sl]
        o_ref[ch['bi'], :, sl] = (_rms(ch['o'], gain[:, sl]) * (gh * jax.nn.sigmoid(gh))).astype(o_ref.dtype)


def retention(z3, cos_t, sin_t, gain):
    b, t, _ = z3.shape
    nc = t // R_CHUNK
    qk_w = R_HEADS * R_DK
    v_w = R_HEADS * R_DV
    return pl.pallas_call(
        _retention_kernel,
        out_shape=jax.ShapeDtypeStruct((b, t, v_w), BF16),
        grid=(nc,),
        in_specs=[pl.BlockSpec((R_CHUNK, qk_w), lambda c: (c, 0)),
                  pl.BlockSpec((R_CHUNK, qk_w), lambda c: (c, 0)),
                  pl.BlockSpec((b, R_CHUNK, qk_w), lambda c: (0, c, 0)),
                  pl.BlockSpec((b, R_CHUNK, qk_w), lambda c: (0, c, 1)),
                  pl.BlockSpec((b, R_CHUNK, v_w), lambda c: (0, c, 1)),
                  pl.BlockSpec((b, R_CHUNK, v_w), lambda c: (0, c, 2)),
                  pl.BlockSpec((1, v_w), lambda c: (0, 0))],
        out_specs=pl.BlockSpec((b, R_CHUNK, v_w), lambda c: (0, c, 0)),
        scratch_shapes=[pltpu.VMEM((b * R_HEADS, R_DK, R_DV), F32)],
        compiler_params=_cparams(("arbitrary",)),
        name="retention",
    )(cos_t, sin_t, z3, z3, z3, z3, gain.reshape(1, v_w))


def _mlstm_kernel(q_ref, k_ref, v_ref, og_ref, gc_ref, gr_ref, cw_ref, cb_ref, bc_ref, br_ref, gain_ref,
                  o_ref, xbuf, c_ref, n_ref, m_ref):
    c = pl.program_id(0)
    nb = q_ref.shape[0]
    L = M_CHUNK
    H = M_HEADS
    qk_w = H * M_DK
    halo = 8

    @pl.when(c == 0)
    def _():
        xbuf[:, 0:halo, :] = jnp.zeros((nb, halo, 2 * qk_w), F32)
        c_ref[...] = jnp.zeros_like(c_ref)
        n_ref[...] = jnp.zeros_like(n_ref)
        m_ref[...] = jnp.zeros_like(m_ref)

    gain = gain_ref[...]
    ri = lax.broadcasted_iota(jnp.int32, (L, L), 0)
    ci = lax.broadcasted_iota(jnp.int32, (L, L), 1)
    causal = ri >= ci
    tril = causal.astype(F32)
    triu = (ri <= ci).astype(F32)

    chains = []
    for bi in range(nb):
        xbuf[bi, halo:halo + L, 0:qk_w] = q_ref[bi]
        xbuf[bi, halo:halo + L, qk_w:2 * qk_w] = k_ref[bi]
        conv = cb_ref[...]
        for j in range(M_CONV):
            conv = conv + xbuf[bi, pl.ds(halo - (M_CONV - 1) + j, L), :] * cw_ref[j:j + 1, :]
        tail = xbuf[bi, L:L + halo, :]
        xbuf[bi, 0:halo, :] = tail
        act = conv * jax.nn.sigmoid(conv)
        q = act[:, 0:qk_w] * (M_DK ** -0.5)
        k = act[:, qk_w:2 * qk_w]
        gc = gc_ref[bi][:, 0:2 * H] + bc_ref[...]
        gr = gr_ref[bi] + br_ref[...]
        ig_c = gc[:, 0:H]
        ig_r = gr[0:H, :]
        b_c = _dot_f32(tril, jax.nn.log_sigmoid(gc[:, H:2 * H]))
        b_r = _dot_f32(jax.nn.log_sigmoid(gr[H:2 * H, :]), triu)
        for h in range(H):
            chains.append(dict(
                bi=bi, h=h, u=bi * H + h,
                qh=q[:, h * M_DK:(h + 1) * M_DK], kh=k[:, h * M_DK:(h + 1) * M_DK],
                bh=b_c[:, h:h + 1], brow=b_r[h:h + 1, :], irow=ig_r[h:h + 1, :], icol=ig_c[:, h:h + 1]))

    def stack(parts):
        return jnp.concatenate(parts, axis=0)

    def rows(x):
        return jnp.broadcast_to(x, (L, x.shape[1]))

    m_prev_u = [m_ref[ch['u']][:, 0:1] for ch in chains]
    b_last_u = [ch['bh'][L - 1:L, :] for ch in chains]
    bh = stack([ch['bh'] for ch in chains])
    icol = stack([ch['icol'] for ch in chains])
    brow = stack([rows(ch['brow']) for ch in chains])
    irow = stack([rows(ch['irow']) for ch in chains])
    m_prev = stack([rows(m) for m in m_prev_u])
    b_last = stack([rows(x) for x in b_last_u])
    causal_all = stack([causal] * len(chains))
    qs = stack([ch['qh'] for ch in chains])
    ks = stack([ch['kh'] for ch in chains])
    qb = qs.astype(BF16)
    kb = ks.astype(BF16)
    cstates = [c_ref[ch['u']] for ch in chains]
    nstates = [n_ref[ch['u']] for ch in chains]

    def chain_rows(x, i):
        return x[i * L:(i + 1) * L]

    s_raw = stack([_dot_nt(chain_rows(qb, i), chain_rows(kb, i)) for i in range(len(chains))])
    qc = stack([_dot(chain_rows(qb, i), cstates[i].astype(BF16)) for i in range(len(chains))])

    dlog = jnp.where(causal_all, bh - brow + irow, NEG)
    inter = bh + m_prev
    m_t = jnp.maximum(inter, jnp.max(dlog, axis=-1, keepdims=True))
    s = s_raw * jnp.exp(dlog - m_t)
    w_inter = jnp.exp(inter - m_t)
    wlog = b_last - bh + icol
    m_new_u = [jnp.maximum(b_last_u[i] + m_prev_u[i], jnp.max(chain_rows(wlog, i), axis=0, keepdims=True))
               for i in range(len(chains))]
    m_new = stack([rows(m) for m in m_new_u])
    wk = ks * jnp.exp(wlog - m_new)

    sb = s.astype(BF16)
    vhs = [v_ref[ch['bi'], :, ch['h'] * M_DV:(ch['h'] + 1) * M_DV].astype(BF16) for ch in chains]
    sv = stack([_dot(chain_rows(sb, i), vhs[i]) for i in range(len(chains))])
    kv = [_dot(chain_rows(wk, i).T.astype(BF16), vhs[i]) for i in range(len(chains))]

    qn = jnp.sum(qs * stack([rows(n) for n in nstates]), axis=-1, keepdims=True)
    den = jnp.sum(s, axis=-1, keepdims=True) + w_inter * qn
    hh = (sv + w_inter * qc) / jnp.maximum(jnp.abs(den), jnp.exp(-m_t))
    og = stack([og_ref[ch['bi'], :, ch['h'] * M_DV:(ch['h'] + 1) * M_DV] for ch in chains])
    gains = stack([rows(gain[:, ch['h'] * M_DV:(ch['h'] + 1) * M_DV]) for ch in chains])
    out = jax.nn.sigmoid(og) * _rms(hh, gains)

    for i, ch in enumerate(chains):
        u = ch['u']
        decay = jnp.exp(b_last_u[i] + m_prev_u[i] - m_new_u[i])
        c_ref[u] = decay * cstates[i] + kv[i]
        n_ref[u] = decay * nstates[i] + jnp.sum(chain_rows(wk, i), axis=0, keepdims=True)
        m_ref[u] = jnp.broadcast_to(m_new_u[i], (1, LANES_V7X))
        o_ref[ch['bi'], :, ch['h'] * M_DV:(ch['h'] + 1) * M_DV] = chain_rows(out, i).astype(o_ref.dtype)


def mlstm(z3, gates_r, conv_w, conv_b, ig_b, fg_b, gain):
    b, t, _ = z3.shape
    nc = t // M_CHUNK
    H = M_HEADS
    qk_w = H * M_DK
    v_w = H * M_DV
    bias = jnp.concatenate([ig_b, fg_b])
    gate_blk = AB_COLS_PAD // LANES_V7X - 1
    return pl.pallas_call(
        _mlstm_kernel,
        out_shape=jax.ShapeDtypeStruct((b, t, v_w), BF16),
        grid=(nc,),
        in_specs=[pl.BlockSpec((b, M_CHUNK, qk_w), lambda c: (0, c, 6)),
                  pl.BlockSpec((b, M_CHUNK, qk_w), lambda c: (0, c, 7)),
                  pl.BlockSpec((b, M_CHUNK, v_w), lambda c: (0, c, 4)),
                  pl.BlockSpec((b, M_CHUNK, v_w), lambda c: (0, c, 5)),
                  pl.BlockSpec((b, M_CHUNK, LANES_V7X), lambda c: (0, c, gate_blk)),
                  pl.BlockSpec((b, None, 2 * H, M_CHUNK), lambda c: (0, c, 0, 0)),
                  pl.BlockSpec((M_CONV, 2 * qk_w), lambda c: (0, 0)),
                  pl.BlockSpec((1, 2 * qk_w), lambda c: (0, 0)),
                  pl.BlockSpec((1, 2 * H), lambda c: (0, 0)),
                  pl.BlockSpec((2 * H, 1), lambda c: (0, 0)),
                  pl.BlockSpec((1, v_w), lambda c: (0, 0))],
        out_specs=pl.BlockSpec((b, M_CHUNK, v_w), lambda c: (0, c, 0)),
        scratch_shapes=[pltpu.VMEM((b, 8 + M_CHUNK, 2 * qk_w), F32),
                        pltpu.VMEM((b * H, M_DK, M_DV), F32),
                        pltpu.VMEM((b * H, 1, M_DK), F32),
                        pltpu.VMEM((b * H, 1, LANES_V7X), F32)],
        compiler_params=_cparams(("arbitrary",)),
        name="mlstm",
    )(z3, z3, z3, z3, z3, gates_r, conv_w, conv_b.reshape(1, -1), bias.reshape(1, -1), bias.reshape(-1, 1),
      gain.reshape(1, v_w))


def _mix_ffn_ple_kernel(*refs, n_mix, tm, seq, tf):
    hp_ref, h_ref = refs[0], refs[1]
    mix_refs = refs[2:2 + 2 * n_mix]
    (wo_ref, g_ref, wa_ref, wb_ref, cw_ref, cb_ref, wd_ref, p_ref, wp_ref, ng_ref, gg_ref, wg_ref,
     o_ref, xn_ref, a_ref) = refs[2 + 2 * n_mix:]
    i = pl.program_id(0)
    halo = 8
    g = g_ref[...]
    m_prev = jnp.concatenate([mix_refs[2 * k][...] for k in range(n_mix)], axis=1)
    m_tile = jnp.concatenate([mix_refs[2 * k + 1][...] for k in range(n_mix)], axis=1)
    x = h_ref[...] + _dot(m_tile, wo_ref[...])
    xp = hp_ref[...] + _dot(m_prev, wo_ref[...])[BF16_ROWS - halo:, :]
    xn_ref[halo:halo + tm, :] = _rms(x, g).astype(BF16)
    keep = ((i * tm) % seq != 0).astype(F32)
    xn_ref[0:halo, :] = (_rms(xp, g) * keep).astype(BF16)
    n_chunks = D_FF // tf

    def up_proj(c):
        cs = slice(c * tf, (c + 1) * tf)
        a_ref[c % 2] = _dot(xn_ref[...], wa_ref[:, cs])
        return _dot(xn_ref[halo:halo + tm, :], wb_ref[:, cs])

    h2 = x
    bgate = up_proj(0)
    for c in range(n_chunks):
        cs = slice(c * tf, (c + 1) * tf)
        bgate_next = up_proj(c + 1) if c + 1 < n_chunks else None
        conv = cb_ref[:, cs]
        for t in range(FFN_CONV):
            conv = conv + a_ref[c % 2, pl.ds(halo - (FFN_CONV - 1) + t, tm), :] * cw_ref[t:t + 1, cs]
        act = (_gelu_tanh(conv) * bgate).astype(BF16)
        h2 = h2 + _dot(act, wd_ref[cs, :])
        bgate = bgate_next
    e = _rms(_dot(p_ref[...].astype(BF16), wp_ref[...]), ng_ref[...])
    gate = jax.nn.sigmoid(_dot(_rms(h2, gg_ref[...]).astype(BF16), wg_ref[...]))
    o_ref[...] = h2 + gate * e


def mix_ffn_ple(h, mix, li, g, w_up, conv_w, conv_b, w_down, p, wp, norm_g, gate_norm_g, wg, seq, tm=512, tf=2816):
    mix_outs, w_out = mix
    n, d = h.shape
    pd = p.shape[-1]
    hb = tm // 8
    const = lambda i: (0, 0)
    layer = lambda i: (li, 0, 0)
    prev8 = lambda i: (jnp.maximum(i * hb - 1, 0), 0)
    resident = dict(pipeline_mode=pl.Buffered(1))
    prev16 = lambda i: (jnp.maximum(i * (tm // BF16_ROWS) - 1, 0), 0)
    mix_specs, mix_args = [], []
    for m in mix_outs:
        mix_specs += [pl.BlockSpec((BF16_ROWS, m.shape[1]), prev16),
                      pl.BlockSpec((tm, m.shape[1]), lambda i: (i, 0))]
        mix_args += [m, m]
    mix_specs.append(pl.BlockSpec(w_out.shape, const, **resident))
    mix_args.append(w_out)
    return pl.pallas_call(
        functools.partial(_mix_ffn_ple_kernel, n_mix=len(mix_outs), tm=tm, seq=seq, tf=tf),
        out_shape=jax.ShapeDtypeStruct((n, d), F32),
        grid=(n // tm,),
        in_specs=[pl.BlockSpec((8, d), prev8),
                  pl.BlockSpec((tm, d), lambda i: (i, 0))] + mix_specs + [
                  pl.BlockSpec((None, 1, d), layer),
                  pl.BlockSpec((None, d, D_FF), lambda i: (li, 0, 0), **resident),
                  pl.BlockSpec((None, d, D_FF), lambda i: (li, 0, 1), **resident),
                  pl.BlockSpec((None, FFN_CONV, D_FF), layer),
                  pl.BlockSpec((None, 1, D_FF), layer),
                  pl.BlockSpec((None, D_FF, d), layer, **resident),
                  pl.BlockSpec((None, tm, pd), lambda i: (li, i, 0)),
                  pl.BlockSpec((None, pd, d), layer, **resident),
                  pl.BlockSpec((None, 1, d), layer),
                  pl.BlockSpec((None, 1, d), layer),
                  pl.BlockSpec((None, d, d), layer, **resident)],
        out_specs=pl.BlockSpec((tm, d), lambda i: (i, 0)),
        scratch_shapes=[pltpu.VMEM((8 + tm, d), BF16),
                        pltpu.VMEM((min(2, D_FF // tf), 8 + tm, tf), F32)],
        compiler_params=_cparams(("parallel",), VMEM_LIMIT_V7X),
        name="mix_ffn_ple",
    )(h, h, *mix_args, g[:, None, :], w_up, w_up, conv_w, conv_b[:, None, :], w_down,
      p, wp, norm_g[:, None, :], gate_norm_g[:, None, :], wg)


def _group_rms(x, g):
    lane = lax.broadcasted_iota(jnp.int32, x.shape, 1)
    x2 = x * x
    ms = jnp.zeros_like(x)
    for grp in range(N_KV_GROUPS):
        in_grp = (lane >= grp * HEAD_DIM) & (lane < (grp + 1) * HEAD_DIM)
        tot = jnp.sum(jnp.where(in_grp, x2, 0.0), axis=-1, keepdims=True)
        ms = jnp.where(in_grp, tot * (1.0 / HEAD_DIM), ms)
    return x * lax.rsqrt(ms + EPS) * g


def _ones_rows(width):
    return (lax.broadcasted_iota(jnp.int32, (BF16_ROWS, width), 0) == 0).astype(BF16)


def _kv_prep_kernel(c_ref, s_ref, w_ref, gs_ref, gw_ref, kc_ref, vc_ref, ks_ref, vs_ref, kw_ref, vw_ref):
    gw = N_KV_GROUPS * HEAD_DIM
    cc = c_ref[...]
    ss = s_ref[...]
    ww = w_ref[...]
    kc_ref[...] = cc[:, 0:gw]
    vc_ref[...] = cc[:, gw:2 * gw]
    ks = _group_rms(ss[:, 0:gw], gs_ref[...]).astype(BF16)
    kw = _group_rms(ww[:, 0:gw], gw_ref[...]).astype(BF16)
    vst = ss[:, gw:2 * gw].T
    vwt = ww[:, gw:2 * gw].T
    for g in range(N_KV_GROUPS):
        lanes = slice(g * HEAD_DIM, (g + 1) * HEAD_DIM)
        ks_ref[g] = ks[:, lanes]
        kw_ref[g] = kw[:, lanes]
        vs_ref[g] = jnp.concatenate([vst[lanes, :].astype(BF16), _ones_rows(SEL_TILE)], axis=0)
        for u in range(SEL_TILE // KEY_TILE):
            vw_ref[g, u] = jnp.concatenate([vwt[lanes, u * KEY_TILE:(u + 1) * KEY_TILE].astype(BF16),
                                            _ones_rows(KEY_TILE)], axis=0)


def kv_prep(z, k_g, b, t):
    n = z.shape[0]
    G = N_KV_GROUPS
    gw = G * HEAD_DIM
    base = N_HEADS * HEAD_DIM // (2 * gw)
    tm = SEL_TILE
    nst = t // tm
    sub = SEL_TILE // KEY_TILE
    row = lambda i, j: i * nst + j
    flat = jax.ShapeDtypeStruct((n, gw), F32)
    keys = jax.ShapeDtypeStruct((b, G, t, HEAD_DIM), BF16)
    flat_spec = pl.BlockSpec((tm, gw), lambda i, j: (row(i, j), 0))
    key_spec = pl.BlockSpec((None, G, tm, HEAD_DIM), lambda i, j: (i, 0, j, 0))
    return pl.pallas_call(
        _kv_prep_kernel,
        out_shape=(flat, flat, keys, jax.ShapeDtypeStruct((b, G, nst, V_ROWS, tm), BF16),
                   keys, jax.ShapeDtypeStruct((b, G, nst * sub, V_ROWS, KEY_TILE), BF16)),
        grid=(b, nst),
        in_specs=[pl.BlockSpec((tm, 2 * gw), lambda i, j: (row(i, j), base)),
                  pl.BlockSpec((tm, 2 * gw), lambda i, j: (row(i, j), base + 1)),
                  pl.BlockSpec((tm, 2 * gw), lambda i, j: (row(i, j), base + 2)),
                  pl.BlockSpec((1, gw), lambda i, j: (0, 0)),
                  pl.BlockSpec((1, gw), lambda i, j: (0, 0))],
        out_specs=(flat_spec, flat_spec, key_spec,
                   pl.BlockSpec((None, G, None, V_ROWS, tm), lambda i, j: (i, 0, j, 0, 0)),
                   key_spec,
                   pl.BlockSpec((None, G, sub, V_ROWS, KEY_TILE), lambda i, j: (i, 0, j, 0, 0))),
        compiler_params=_cparams(("parallel", "parallel")),
        name="kv_prep",
    )(z, z, z, jnp.tile(k_g[1], G).reshape(1, gw), jnp.tile(k_g[2], G).reshape(1, gw))


def _compress_kernel(x_ref, pos_ref, w1_ref, wbd_ref, w2_ref, g_ref, o_ref, *, normalize, transposed):
    G = N_KV_GROUPS
    ncb = x_ref.shape[0] // CMP_STRIDE
    u = jnp.zeros((ncb, G * CMP_HIDDEN), F32)
    v = jnp.zeros((ncb, G * CMP_HIDDEN), F32)
    for r in range(CMP_STRIDE):
        xr = x_ref[pl.ds(r, ncb, stride=CMP_STRIDE), :].astype(BF16)
        u = u + _dot(xr, wbd_ref[0, r])
        v = v + _dot(xr, wbd_ref[1, r])
    posc = _dot(pos_ref[...], w1_ref[...])[0:1, :]
    hid = u + pltpu.roll(v, ncb - 1, 0) + jnp.concatenate([posc] * G, axis=1)
    outs = []
    for grp in range(G):
        out = _dot(jax.nn.gelu(hid[:, grp * CMP_HIDDEN:(grp + 1) * CMP_HIDDEN]).astype(BF16), w2_ref[...])
        outs.append(_rms(out, g_ref[...]) if normalize else out)
    if transposed:
        both = jnp.concatenate(outs, axis=1).T
        for grp in range(G):
            o_ref[grp] = both[grp * HEAD_DIM:(grp + 1) * HEAD_DIM, :].astype(o_ref.dtype)
    else:
        for grp in range(G):
            o_ref[grp] = outs[grp].astype(o_ref.dtype)


def compress(x3, pos, w1, w2, g, normalize, transposed):
    b, t, gw = x3.shape
    G = N_KV_GROUPS
    ncb = t // CMP_STRIDE
    kdim = CMP_STRIDE * HEAD_DIM
    posf = jnp.broadcast_to(pos.reshape(1, -1), (8, 2 * kdim)).astype(BF16)
    w1b = w1.astype(BF16)
    w1r = w1b.reshape(2, CMP_STRIDE, HEAD_DIM, CMP_HIDDEN)
    eye = jnp.eye(G, dtype=BF16)
    wbd = jnp.einsum('gh,srdc->srgdhc', eye, w1r).reshape(2, CMP_STRIDE, G * HEAD_DIM, G * CMP_HIDDEN)
    out_sds = (jax.ShapeDtypeStruct((b, G, HEAD_DIM, ncb), BF16) if transposed
               else jax.ShapeDtypeStruct((b, G, ncb, HEAD_DIM), BF16))
    out_block = (None, G, HEAD_DIM, ncb) if transposed else (None, G, ncb, HEAD_DIM)
    return pl.pallas_call(
        functools.partial(_compress_kernel, normalize=normalize, transposed=transposed),
        out_shape=out_sds,
        grid=(b,),
        in_specs=[pl.BlockSpec((None, t, gw), lambda i: (i, 0, 0)),
                  pl.BlockSpec((8, 2 * kdim), lambda i: (0, 0)),
                  pl.BlockSpec((2 * kdim, CMP_HIDDEN), lambda i: (0, 0)),
                  pl.BlockSpec((2, CMP_STRIDE, G * HEAD_DIM, G * CMP_HIDDEN), lambda i: (0, 0, 0, 0)),
                  pl.BlockSpec((CMP_HIDDEN, HEAD_DIM), lambda i: (0, 0)),
                  pl.BlockSpec((1, HEAD_DIM), lambda i: (0, 0))],
        out_specs=pl.BlockSpec(out_block, lambda i: (i, 0, 0, 0)),
        compiler_params=_cparams(("parallel",)),
        name="compress",
    )(x3, posf, w1b, wbd, w2.astype(BF16), g.reshape(1, HEAD_DIM))


def _nsa_kernel(bnd_ref, *refs):
    safe = bnd_ref[0, 0] <= MAX_SAFE_SCORE_BOUND

    @pl.when(safe)
    def _():
        _nsa_body(bnd_ref, *refs, bounded=True)

    @pl.when(jnp.logical_not(safe))
    def _():
        _nsa_body(bnd_ref, *refs, bounded=False)


def _nsa_body(bnd_ref, zq_ref, gt_ref, gb_ref, qg_ref, ovt_ref, kc_ref, vct_ref, ks_ref, vst_ref, kw_ref,
              vwt_ref, o_ref, q_scr, sel_scr, m_scr, acc_scr, oc_scr, *, bounded):
    grp = pl.program_id(1)
    qb = pl.program_id(2)
    QB = Q_BLOCK
    ncp = kc_ref.shape[0]
    ns = ovt_ref.shape[0]
    tpos = qb * QB + lax.broadcasted_iota(jnp.int32, (1, QB), 1)

    def lane_tile(x):
        return jnp.concatenate([x] * HG, axis=1)

    zt = zq_ref[...].T
    qg = qg_ref[...] * (HEAD_DIM ** -0.5 * LOG2E)
    heads = []
    for h in range(HG):
        xh = zt[h * HEAD_DIM:(h + 1) * HEAD_DIM, :]
        ms = jnp.mean(xh * xh, axis=0, keepdims=True)
        heads.append(xh * lax.rsqrt(ms + EPS) * qg)
    q_scr[...] = jnp.concatenate(heads, axis=1).astype(BF16)
    q = q_scr[...]

    gates = jax.nn.sigmoid(gt_ref[...].T + gb_ref[...])
    per_grp = HG * 3
    gsel = gates[0:per_grp, :]
    for g2 in range(1, N_KV_GROUPS):
        gsel = jnp.where(grp == g2, gates[g2 * per_grp:(g2 + 1) * per_grp, :], gsel)

    def gate_row(c):
        return jnp.concatenate([gsel[3 * h + c:3 * h + c + 1, :] for h in range(HG)], axis=1)

    cmp_end = lax.broadcasted_iota(jnp.int32, (ncp, 1), 0) * CMP_STRIDE + (CMP_BLOCK - 1)
    keep = -bnd_ref[0, 0] if bounded else 0.0
    cbias = jnp.where(cmp_end <= tpos, keep, NEG)
    s = _dot(kc_ref[...], q) + lane_tile(cbias)
    e = jnp.exp2(s) if bounded else jnp.exp2(s - jnp.max(s, axis=0, keepdims=True))
    inv = jnp.where(lane_tile(tpos) >= CMP_BLOCK - 1, 1.0 / jnp.sum(e, axis=0, keepdims=True), 0.0)
    p = e * inv
    ocmp = _dot(vct_ref[...], p.astype(BF16))
    psum = p[:, 0:QB]
    for h in range(1, HG):
        psum = psum + p[:, h * QB:(h + 1) * QB]

    n_win = (WINDOW + QB) // KEY_TILE
    first_tile = qb * (QB // KEY_TILE) - WINDOW // KEY_TILE
    win_sub = lax.broadcasted_iota(jnp.int32, (n_win * KEY_TILE, 1), 0)
    tiles = [jnp.maximum(first_tile + u, 0) for u in range(n_win)]
    kwin = jnp.concatenate([kw_ref[j] for j in tiles], axis=0)
    vwin = jnp.concatenate([vwt_ref[j] for j in tiles], axis=1)
    kpos = first_tile * KEY_TILE + win_sub
    wbias = jnp.where((kpos <= tpos) & (kpos > tpos - WINDOW) & (kpos >= 0), keep, NEG)
    sw = _dot(kwin, q) + lane_tile(wbias)
    pw = jnp.exp2(sw) if bounded else jnp.exp2(sw - jnp.max(sw, axis=0, keepdims=True))
    ow = _dot(vwin, pw.astype(BF16))
    oc_scr[...] = gate_row(0) * ocmp + (gate_row(2) / ow[HEAD_DIM:HEAD_DIM + 1, :]) * ow[0:HEAD_DIM, :]

    p_hi = psum.astype(BF16)
    p_lo = (psum - p_hi.astype(F32)).astype(BF16)
    ovt = ovt_ref[...]
    imp = _dot(ovt, p_hi) + _dot(ovt, p_lo)
    blk = lax.broadcasted_iota(jnp.int32, (ns, 1), 0)
    blk_f = blk.astype(F32)
    cur = jnp.right_shift(tpos, SLC_BLOCK.bit_length() - 1)
    forced = (blk == 0) | (blk == cur) | (blk == cur - 1)
    bvalid = blk <= cur
    score = jnp.where(forced, -jnp.inf, jnp.where(bvalid, imp, NEG))
    sel = jnp.where(forced, 1.0, 0.0)
    for _ in range(max(min(N_SELECT, ns) - 3, 0)):
        mx = jnp.max(score, axis=0, keepdims=True)
        first = jnp.min(jnp.where(score == mx, blk_f, float(ns)), axis=0, keepdims=True)
        pick = blk_f == first
        sel = jnp.where(pick, 1.0, sel)
        score = jnp.where(pick, -jnp.inf, score)
    sel_scr[...] = jnp.where(bvalid, sel, 0.0)

    m_scr[...] = jnp.full(m_scr.shape, NEG, F32)
    acc_scr[...] = jnp.zeros(acc_scr.shape, F32)
    blocks_per_tile = SEL_TILE // SLC_BLOCK
    sel_sub = lax.broadcasted_iota(jnp.int32, (SEL_TILE, 1), 0)

    def sel_body(j, carry):
        rows = [jnp.broadcast_to(sel_scr[pl.ds(j * blocks_per_tile + bi, 1), :], (SLC_BLOCK, QB))
                for bi in range(blocks_per_tile)]
        chosen = jnp.concatenate(rows, axis=0)
        kpos = j * SEL_TILE + sel_sub
        bias = jnp.where((chosen > 0.5) & (kpos <= tpos), keep, NEG)
        st = _dot(ks_ref[j], q_scr[...]) + lane_tile(bias)
        if bounded:
            pt = jnp.exp2(st)
            acc_scr[...] += _dot(vst_ref[j], pt.astype(BF16))
        else:
            m_old = m_scr[...]
            m_new = jnp.maximum(m_old, jnp.max(st, axis=0, keepdims=True))
            alpha = jnp.exp2(m_old - m_new)
            pt = jnp.exp2(st - m_new)
            acc_scr[...] = alpha * acc_scr[...] + _dot(vst_ref[j], pt.astype(BF16))
            m_scr[...] = m_new
        return carry

    lax.fori_loop(0, qb // (SEL_TILE // QB) + 1, sel_body, 0)
    ot = (oc_scr[...]
          + (gate_row(1) / acc_scr[HEAD_DIM:HEAD_DIM + 1, :]) * acc_scr[0:HEAD_DIM, :])
    stacked = jnp.concatenate([ot[:, h * QB:(h + 1) * QB] for h in range(HG)], axis=0)
    o_ref[...] = stacked.T.astype(o_ref.dtype)


def nsa_attention(bound, z3, gate_b, q_g, overlap_t, kc, vct, ks, vst, kw, vwt):
    b, t, _ = z3.shape
    nq = t // Q_BLOCK
    nt = t // KEY_TILE
    qw = HG * HEAD_DIM
    ns, ncp = overlap_t.shape
    gate_blk = (NSA_COLS_PAD // LANES_V7X) - 1
    gb = jnp.zeros((LANES_V7X, 1), F32).at[:N_HEADS * 3, 0].set(gate_b)
    full5 = lambda i, g, q: (i, g, 0, 0, 0)
    lanes = HG * Q_BLOCK
    return pl.pallas_call(
        _nsa_kernel,
        out_shape=jax.ShapeDtypeStruct((b, t, N_HEADS * HEAD_DIM), BF16),
        grid=(b, N_KV_GROUPS, nq),
        in_specs=[pl.BlockSpec(memory_space=pltpu.SMEM),
                  pl.BlockSpec((None, Q_BLOCK, qw), lambda i, g, q: (i, q, g)),
                  pl.BlockSpec((None, Q_BLOCK, LANES_V7X), lambda i, g, q: (i, q, gate_blk)),
                  pl.BlockSpec((LANES_V7X, 1), lambda i, g, q: (0, 0)),
                  pl.BlockSpec((HEAD_DIM, 1), lambda i, g, q: (0, 0)),
                  pl.BlockSpec((ns, ncp), lambda i, g, q: (0, 0)),
                  pl.BlockSpec((None, None, ncp, HEAD_DIM), lambda i, g, q: (i, g, 0, 0)),
                  pl.BlockSpec((None, None, HEAD_DIM, ncp), lambda i, g, q: (i, g, 0, 0)),
                  pl.BlockSpec((None, None, t // SEL_TILE, SEL_TILE, HEAD_DIM), full5),
                  pl.BlockSpec((None, None, t // SEL_TILE, V_ROWS, SEL_TILE), full5),
                  pl.BlockSpec((None, None, nt, KEY_TILE, HEAD_DIM), full5),
                  pl.BlockSpec((None, None, nt, V_ROWS, KEY_TILE), full5)],
        out_specs=pl.BlockSpec((None, Q_BLOCK, qw), lambda i, g, q: (i, q, g)),
        scratch_shapes=[pltpu.VMEM((HEAD_DIM, lanes), BF16),
                        pltpu.VMEM((ns, Q_BLOCK), F32),
                        pltpu.VMEM((1, lanes), F32),
                        pltpu.VMEM((V_ROWS, lanes), F32),
                        pltpu.VMEM((HEAD_DIM, lanes), F32)],
        compiler_params=_cparams(("parallel", "parallel", "arbitrary"), VMEM_LIMIT_V7X),
        name="nsa_attention",
    )(bound.reshape(1, 1), z3, z3, gb, q_g.reshape(HEAD_DIM, 1), overlap_t, kc, vct, ks, vst, kw, vwt)


def _overlap_matrix_t(t):
    ncp = t // CMP_STRIDE
    ns = t // SLC_BLOCK
    c_start = np.arange(ncp) * CMP_STRIDE
    sj = np.arange(ns)
    ov = ((c_start[None, :] < (sj[:, None] + 1) * SLC_BLOCK)
          & (c_start[None, :] + CMP_BLOCK > sj[:, None] * SLC_BLOCK)
          & (c_start[None, :] + CMP_BLOCK <= t))
    return jnp.asarray(ov, dtype=BF16)


def ab_layer(h, b, t, norm_g, w_in, conv_w, conv_b, ret_g, ig_b, fg_b, m_g, w_out):
    n = b * t
    w_in_p = jnp.pad(w_in, ((0, 0), (0, AB_COLS_PAD - AB_COLS))).astype(BF16)
    z = norm_matmul(h, norm_g, w_in_p)
    z3 = z.reshape(b, t, AB_COLS_PAD)
    cos_t, sin_t = rope_tables(t)
    ret = retention(z3, cos_t, sin_t, ret_g)
    g0 = AB_COLS - 2 * M_HEADS
    gates_r = z3[:, :, g0:AB_COLS].reshape(b, t // M_CHUNK, M_CHUNK, 2 * M_HEADS).transpose(0, 1, 3, 2)
    ml = mlstm(z3, gates_r, conv_w, conv_b, ig_b, fg_b, m_g)
    w_out_b = w_out.astype(BF16)
    rw = R_HEADS * R_DV
    return [ret.reshape(n, rw), ml.reshape(n, -1)], w_out_b


def nsa_layer(h, b, t, norm_g, w_in, q_g, k_g, pos_k, pos_v, w1k, w2k, w1v, w2v, gate_b, w_out):
    n = b * t
    G = N_KV_GROUPS
    w_in_p = jnp.pad(w_in, ((0, 0), (0, NSA_COLS_PAD - NSA_COLS))).astype(BF16)
    z = norm_matmul(h, norm_g, w_in_p)
    kc_in, vc_in, ks, vst, kw, vwt = kv_prep(z, k_g, b, t)
    gw = G * HEAD_DIM
    kc = compress(kc_in.reshape(b, t, gw), pos_k, w1k, w2k, k_g[0], True, False)
    vct = compress(vc_in.reshape(b, t, gw), pos_v, w1v, w2v, k_g[0], False, True)

    def key_tiles(x, kt):
        return x.reshape(b, G, t // kt, kt, HEAD_DIM)

    bound = 1.02 * LOG2E * math.sqrt(HEAD_DIM) * jnp.max(jnp.abs(q_g)) * jnp.max(jnp.abs(k_g))
    args = (bound, z.reshape(b, t, NSA_COLS_PAD), gate_b, q_g, _overlap_matrix_t(t), kc, vct,
            key_tiles(ks, SEL_TILE), vst, key_tiles(kw, KEY_TILE), vwt)
    o = nsa_attention(*args)
    return [o.reshape(n, -1)], w_out.astype(BF16)


def kernel(x, p, ab_norm_g, ab_w_in, ab_conv_w, ab_conv_b, ab_ret_norm_g, ab_ig_b, ab_fg_b, ab_m_norm_g, ab_w_out, nsa_norm_g, nsa_w_in, nsa_q_norm_g, nsa_k_norm_g, nsa_cmp_pos_k, nsa_cmp_pos_v, nsa_cmp_w1k, nsa_cmp_w2k, nsa_cmp_w1v, nsa_cmp_w2v, nsa_gate_b, nsa_w_out, ffn_norm_g, ffn_w_up, ffn_conv_w, ffn_conv_b, ffn_w_down, ple_w, ple_norm_g, ple_gate_norm_g, ple_w_gate):
    b, t, d = x.shape
    n = b * t
    depth = p.shape[0]
    h = x.reshape(n, d)
    p2 = p.reshape(depth, n, -1)
    ffn_w_up_b, ffn_w_down_b = ffn_w_up.astype(BF16), ffn_w_down.astype(BF16)
    ple_w_b, ple_w_gate_b = ple_w.astype(BF16), ple_w_gate.astype(BF16)
    for i in range(depth):
        j = i // 2
        if i % 2 == 0:
            mix = ab_layer(h, b, t, ab_norm_g[j], ab_w_in[j], ab_conv_w[j], ab_conv_b[j], ab_ret_norm_g[j],
                           ab_ig_b[j], ab_fg_b[j], ab_m_norm_g[j], ab_w_out[j])
        else:
            mix = nsa_layer(h, b, t, nsa_norm_g[j], nsa_w_in[j], nsa_q_norm_g[j], nsa_k_norm_g[j],
                            nsa_cmp_pos_k[j], nsa_cmp_pos_v[j], nsa_cmp_w1k[j], nsa_cmp_w2k[j],
                            nsa_cmp_w1v[j], nsa_cmp_w2v[j], nsa_gate_b[j], nsa_w_out[j])
        h = mix_ffn_ple(h, mix, i, ffn_norm_g, ffn_w_up_b, ffn_conv_w, ffn_conv_b, ffn_w_down_b, p2, ple_w_b,
                        ple_norm_g, ple_gate_norm_g, ple_w_gate_b, t)
    return h.reshape(b, t, d)
```

```python
import functools
import math

import numpy as np
import jax
import jax.numpy as jnp
from jax import lax
from jax.experimental import pallas as pl
from jax.experimental.pallas import tpu as pltpu

F32 = jnp.float32
BF16 = jnp.bfloat16

LANES_V7X = 128
BF16_ROWS = 16
VMEM_LIMIT_V7X = 56 * 1024 * 1024

D_MODEL = 1024
PLE_DIM = 256
R_HEADS, R_DK, R_DV, R_CHUNK = 4, 64, 128, 128
ROPE_BASE = 10000.0
M_HEADS, M_DK, M_DV, M_CHUNK, M_CONV = 4, 64, 128, 64, 4
AB_SIZES = (R_HEADS * R_DK, R_HEADS * R_DK, R_HEADS * R_DV, R_HEADS * R_DV,
            M_HEADS * M_DK, M_HEADS * M_DK, M_HEADS * M_DV, M_HEADS * M_DV, M_HEADS, M_HEADS)
AB_COLS = sum(AB_SIZES)
AB_COLS_PAD = 3200
N_HEADS, N_KV_GROUPS, HEAD_DIM = 16, 2, 64
HG = N_HEADS // N_KV_GROUPS
CMP_BLOCK, CMP_STRIDE, CMP_HIDDEN = 32, 16, 256
SLC_BLOCK, N_SELECT, WINDOW = 64, 16, 512
Q_BLOCK = 256
NSA_COLS = N_HEADS * HEAD_DIM + 6 * N_KV_GROUPS * HEAD_DIM + N_HEADS * 3
NSA_COLS_PAD = 1920
D_FF = 2816
FFN_CONV = 3
NEG = -1e30
EPS = 1e-6
KEY_TILE = 128
SEL_TILE = 512
V_ROWS = HEAD_DIM + BF16_ROWS
LOG2E = math.log2(math.e)
MAX_SAFE_SCORE_BOUND = 56.0


def _cparams(sem, vmem=None):
    return pltpu.CompilerParams(dimension_semantics=sem, vmem_limit_bytes=vmem)


def _rms(x, g):
    ms = jnp.mean(x * x, axis=-1, keepdims=True)
    return x * lax.rsqrt(ms + EPS) * g


def _dot(a, b):
    return jnp.dot(a, b, preferred_element_type=F32)


def _dot_nt(a, b):
    return lax.dot_general(a, b, (((1,), (1,)), ((), ())), preferred_element_type=F32)


def _dot_f32(a, b):
    return jnp.dot(a, b, preferred_element_type=F32, precision=lax.Precision.HIGHEST)


def _gelu_tanh(x):
    k1 = -2.0 * math.sqrt(2.0 / math.pi) * LOG2E
    return x / (1.0 + jnp.exp2(x * (k1 + (k1 * 0.044715) * (x * x))))


def _norm_matmul_kernel(x_ref, g_ref, w_ref, o_ref):
    xn = _rms(x_ref[...], g_ref[...]).astype(BF16)
    o_ref[...] = _dot(xn, w_ref[...]).astype(o_ref.dtype)


def norm_matmul(x, g, w, tm=1024, out_dtype=F32):
    n, d = x.shape
    nc = w.shape[1]
    return pl.pallas_call(
        _norm_matmul_kernel,
        out_shape=jax.ShapeDtypeStruct((n, nc), out_dtype),
        grid=(n // tm,),
        in_specs=[pl.BlockSpec((tm, d), lambda i: (i, 0)),
                  pl.BlockSpec((1, d), lambda i: (0, 0)),
                  pl.BlockSpec((d, nc), lambda i: (0, 0))],
        out_specs=pl.BlockSpec((tm, nc), lambda i: (i, 0)),
        compiler_params=_cparams(("parallel",), VMEM_LIMIT_V7X),
        name="norm_matmul",
    )(x, g.reshape(1, d), w)


def _rope_table_kernel(inv_ref, cos_ref, sin_ref):
    c = pl.program_id(0)
    rows, width = cos_ref.shape
    pos = (c * rows + lax.broadcasted_iota(jnp.int32, (rows, LANES_V7X), 0)).astype(F32)
    lane = lax.broadcasted_iota(jnp.int32, (rows, LANES_V7X), 1)
    ang = pos * inv_ref[:, 0:LANES_V7X]
    cs = jnp.cos(ang)
    sn = jnp.sin(ang)
    sn = jnp.where(lane % R_DK < R_DK // 2, -sn, sn)
    reps = width // LANES_V7X
    cos_ref[...] = jnp.concatenate([cs] * reps, axis=1)
    sin_ref[...] = jnp.concatenate([sn] * reps, axis=1)


def rope_tables(t):
    half = R_DK // 2
    inv = ROPE_BASE ** (-jnp.arange(half, dtype=F32) / half)
    inv = jnp.tile(inv, 2 * R_HEADS).reshape(1, R_HEADS * R_DK)
    width = R_HEADS * R_DK
    shp = jax.ShapeDtypeStruct((t, width), F32)
    return pl.pallas_call(
        _rope_table_kernel,
        out_shape=(shp, shp),
        grid=(t // R_CHUNK,),
        in_specs=[pl.BlockSpec((1, width), lambda c: (0, 0))],
        out_specs=(pl.BlockSpec((R_CHUNK, width), lambda c: (c, 0)),
                   pl.BlockSpec((R_CHUNK, width), lambda c: (c, 0))),
        compiler_params=_cparams(("parallel",)),
        name="rope_tables",
    )(inv)


def _retention_kernel(cos_ref, sin_ref, q_ref, k_ref, v_ref, g_ref, gain_ref, o_ref, r_ref):
    c = pl.program_id(0)
    nb = q_ref.shape[0]
    L = R_CHUNK

    @pl.when(c == 0)
    def _():
        r_ref[...] = jnp.zeros_like(r_ref)

    cos = cos_ref[...]
    sin = sin_ref[...]
    lane = lax.broadcasted_iota(jnp.int32, cos.shape, 1)
    first_half = lane % R_DK < R_DK // 2
    width = R_HEADS * R_DK

    def rot(x):
        swapped = jnp.where(first_half, pltpu.roll(x, width - R_DK // 2, 1), pltpu.roll(x, R_DK // 2, 1))
        return x * cos + swapped * sin

    gain = gain_ref[...]
    ri = lax.broadcasted_iota(jnp.int32, (L, L), 0)
    ci = lax.broadcasted_iota(jnp.int32, (L, L), 1)
    diff = (ri - ci).astype(F32)
    causal = ri >= ci
    idx = lax.broadcasted_iota(jnp.int32, (L, 1), 0).astype(F32)
    decays = []
    for h in range(R_HEADS):
        log_g = math.log1p(-2.0 ** (-5.0 - h))
        decays.append(dict(
            dmask=jnp.where(causal, jnp.exp(jnp.where(causal, diff, 0.0) * log_g), 0.0),
            q_dec=jnp.exp((idx + 1.0) * log_g), k_dec=jnp.exp((L - 1.0 - idx) * log_g),
            c_dec=math.exp(L * log_g)))

    chains = []
    for bi in range(nb):
        q = rot(q_ref[bi])
        k = rot(k_ref[bi]) * (R_DK ** -0.5)
        for h in range(R_HEADS):
            chains.append(dict(bi=bi, h=h, u=bi * R_HEADS + h,
                               qb=q[:, h * R_DK:(h + 1) * R_DK].astype(BF16),
                               kh=k[:, h * R_DK:(h + 1) * R_DK]))

    for ch in chains:
        ch['rstate'] = r_ref[ch['u']]
        ch['s'] = _dot_nt(ch['qb'], ch['kh'].astype(BF16)) * decays[ch['h']]['dmask']
        ch['qr'] = _dot(ch['qb'], ch['rstate'].astype(BF16))

    for ch in chains:
        d = decays[ch['h']]
        vh = v_ref[ch['bi'], :, ch['h'] * R_DV:(ch['h'] + 1) * R_DV].astype(BF16)
        ch['o'] = _dot(ch['s'].astype(BF16), vh) + ch['qr'] * d['q_dec']
        kd = (ch['kh'] * d['k_dec']).T.astype(BF16)
        r_ref[ch['u']] = d['c_dec'] * ch['rstate'] + _dot(kd, vh)

    for ch in chains:
        sl = slice(ch['h'] * R_DV, (ch['h'] + 1) * R_DV)
        gh = g_ref[ch['bi'], :, sl]
        o_ref[ch['bi'], :, sl] = (_rms(ch['o'], gain[:, sl]) * (gh * jax.nn.sigmoid(gh))).astype(o_ref.dtype)


def retention(z3, cos_t, sin_t, gain):
    b, t, _ = z3.shape
    nc = t // R_CHUNK
    qk_w = R_HEADS * R_DK
    v_w = R_HEADS * R_DV
    return pl.pallas_call(
        _retention_kernel,
        out_shape=jax.ShapeDtypeStruct((b, t, v_w), BF16),
        grid=(nc,),
        in_specs=[pl.BlockSpec((R_CHUNK, qk_w), lambda c: (c, 0)),
                  pl.BlockSpec((R_CHUNK, qk_w), lambda c: (c, 0)),
                  pl.BlockSpec((b, R_CHUNK, qk_w), lambda c: (0, c, 0)),
                  pl.BlockSpec((b, R_CHUNK, qk_w), lambda c: (0, c, 1)),
                  pl.BlockSpec((b, R_CHUNK, v_w), lambda c: (0, c, 1)),
                  pl.BlockSpec((b, R_CHUNK, v_w), lambda c: (0, c, 2)),
                  pl.BlockSpec((1, v_w), lambda c: (0, 0))],
        out_specs=pl.BlockSpec((b, R_CHUNK, v_w), lambda c: (0, c, 0)),
        scratch_shapes=[pltpu.VMEM((b * R_HEADS, R_DK, R_DV), F32)],
        compiler_params=_cparams(("arbitrary",)),
        name="retention",
    )(cos_t, sin_t, z3, z3, z3, z3, gain.reshape(1, v_w))


def _mlstm_kernel(q_ref, k_ref, v_ref, og_ref, gc_ref, cw_ref, cb_ref, bc_ref, br_ref, gain_ref,
                  o_ref, xbuf, c_ref, n_ref, m_ref):
    c = pl.program_id(0)
    nb = q_ref.shape[0]
    L = M_CHUNK
    H = M_HEADS
    qk_w = H * M_DK
    halo = 8

    @pl.when(c == 0)
    def _():
        xbuf[:, 0:halo, :] = jnp.zeros((nb, halo, 2 * qk_w), F32)
        c_ref[...] = jnp.zeros_like(c_ref)
        n_ref[...] = jnp.zeros_like(n_ref)
        m_ref[...] = jnp.zeros_like(m_ref)

    gain = gain_ref[...]
    ri = lax.broadcasted_iota(jnp.int32, (L, L), 0)
    ci = lax.broadcasted_iota(jnp.int32, (L, L), 1)
    causal = ri >= ci
    tril = causal.astype(F32)
    triu = (ri <= ci).astype(F32)

    chains = []
    for bi in range(nb):
        xbuf[bi, halo:halo + L, 0:qk_w] = q_ref[bi]
        xbuf[bi, halo:halo + L, qk_w:2 * qk_w] = k_ref[bi]
        conv = cb_ref[...]
        for j in range(M_CONV):
            conv = conv + xbuf[bi, pl.ds(halo - (M_CONV - 1) + j, L), :] * cw_ref[j:j + 1, :]
        tail = xbuf[bi, L:L + halo, :]
        xbuf[bi, 0:halo, :] = tail
        act = conv * jax.nn.sigmoid(conv)
        q = act[:, 0:qk_w] * (M_DK ** -0.5)
        k = act[:, qk_w:2 * qk_w]
        gblock = gc_ref[bi]
        gc = gblock[:, 0:2 * H] + bc_ref[...]
        gr = gblock.T[0:2 * H, :] + br_ref[...]
        ig_c = gc[:, 0:H]
        ig_r = gr[0:H, :]
        b_c = _dot_f32(tril, jax.nn.log_sigmoid(gc[:, H:2 * H]))
        b_r = _dot_f32(jax.nn.log_sigmoid(gr[H:2 * H, :]), triu)
        for h in range(H):
            chains.append(dict(
                bi=bi, h=h, u=bi * H + h,
                qh=q[:, h * M_DK:(h + 1) * M_DK], kh=k[:, h * M_DK:(h + 1) * M_DK],
                bh=b_c[:, h:h + 1], brow=b_r[h:h + 1, :], irow=ig_r[h:h + 1, :], icol=ig_c[:, h:h + 1]))

    def stack(parts):
        return jnp.concatenate(parts, axis=0)

    def rows(x):
        return jnp.broadcast_to(x, (L, x.shape[1]))

    m_prev_u = [m_ref[ch['u']][:, 0:1] for ch in chains]
    b_last_u = [ch['bh'][L - 1:L, :] for ch in chains]
    bh = stack([ch['bh'] for ch in chains])
    icol = stack([ch['icol'] for ch in chains])
    brow = stack([rows(ch['brow']) for ch in chains])
    irow = stack([rows(ch['irow']) for ch in chains])
    m_prev = stack([rows(m) for m in m_prev_u])
    b_last = stack([rows(x) for x in b_last_u])
    causal_all = stack([causal] * len(chains))
    qs = stack([ch['qh'] for ch in chains])
    ks = stack([ch['kh'] for ch in chains])
    qb = qs.astype(BF16)
    kb = ks.astype(BF16)
    cstates = [c_ref[ch['u']] for ch in chains]
    nstates = [n_ref[ch['u']] for ch in chains]

    def chain_rows(x, i):
        return x[i * L:(i + 1) * L]

    s_raw = stack([_dot_nt(chain_rows(qb, i), chain_rows(kb, i)) for i in range(len(chains))])
    qc = stack([_dot(chain_rows(qb, i), cstates[i].astype(BF16)) for i in range(len(chains))])

    dlog = jnp.where(causal_all, bh - brow + irow, NEG)
    inter = bh + m_prev
    m_t = jnp.maximum(inter, jnp.max(dlog, axis=-1, keepdims=True))
    s = s_raw * jnp.exp(dlog - m_t)
    w_inter = jnp.exp(inter - m_t)
    wlog = b_last - bh + icol
    m_new_u = [jnp.maximum(b_last_u[i] + m_prev_u[i], jnp.max(chain_rows(wlog, i), axis=0, keepdims=True))
               for i in range(len(chains))]
    m_new = stack([rows(m) for m in m_new_u])
    wk = ks * jnp.exp(wlog - m_new)

    sb = s.astype(BF16)
    vhs = [v_ref[ch['bi'], :, ch['h'] * M_DV:(ch['h'] + 1) * M_DV].astype(BF16) for ch in chains]
    sv = stack([_dot(chain_rows(sb, i), vhs[i]) for i in range(len(chains))])
    kv = [_dot(chain_rows(wk, i).T.astype(BF16), vhs[i]) for i in range(len(chains))]

    qn = jnp.sum(qs * stack([rows(n) for n in nstates]), axis=-1, keepdims=True)
    den = jnp.sum(s, axis=-1, keepdims=True) + w_inter * qn
    hh = (sv + w_inter * qc) / jnp.maximum(jnp.abs(den), jnp.exp(-m_t))
    og = stack([og_ref[ch['bi'], :, ch['h'] * M_DV:(ch['h'] + 1) * M_DV] for ch in chains])
    gains = stack([rows(gain[:, ch['h'] * M_DV:(ch['h'] + 1) * M_DV]) for ch in chains])
    out = jax.nn.sigmoid(og) * _rms(hh, gains)

    for i, ch in enumerate(chains):
        u = ch['u']
        decay = jnp.exp(b_last_u[i] + m_prev_u[i] - m_new_u[i])
        c_ref[u] = decay * cstates[i] + kv[i]
        n_ref[u] = decay * nstates[i] + jnp.sum(chain_rows(wk, i), axis=0, keepdims=True)
        m_ref[u] = jnp.broadcast_to(m_new_u[i], (1, LANES_V7X))
        o_ref[ch['bi'], :, ch['h'] * M_DV:(ch['h'] + 1) * M_DV] = chain_rows(out, i).astype(o_ref.dtype)


def mlstm(z3, conv_w, conv_b, ig_b, fg_b, gain):
    b, t, _ = z3.shape
    nc = t // M_CHUNK
    H = M_HEADS
    qk_w = H * M_DK
    v_w = H * M_DV
    bias = jnp.concatenate([ig_b, fg_b])
    gate_blk = AB_COLS_PAD // LANES_V7X - 1
    return pl.pallas_call(
        _mlstm_kernel,
        out_shape=jax.ShapeDtypeStruct((b, t, v_w), BF16),
        grid=(nc,),
        in_specs=[pl.BlockSpec((b, M_CHUNK, qk_w), lambda c: (0, c, 6)),
                  pl.BlockSpec((b, M_CHUNK, qk_w), lambda c: (0, c, 7)),
                  pl.BlockSpec((b, M_CHUNK, v_w), lambda c: (0, c, 4)),
                  pl.BlockSpec((b, M_CHUNK, v_w), lambda c: (0, c, 5)),
                  pl.BlockSpec((b, M_CHUNK, LANES_V7X), lambda c: (0, c, gate_blk)),
                  pl.BlockSpec((M_CONV, 2 * qk_w), lambda c: (0, 0)),
                  pl.BlockSpec((1, 2 * qk_w), lambda c: (0, 0)),
                  pl.BlockSpec((1, 2 * H), lambda c: (0, 0)),
                  pl.BlockSpec((2 * H, 1), lambda c: (0, 0)),
                  pl.BlockSpec((1, v_w), lambda c: (0, 0))],
        out_specs=pl.BlockSpec((b, M_CHUNK, v_w), lambda c: (0, c, 0)),
        scratch_shapes=[pltpu.VMEM((b, 8 + M_CHUNK, 2 * qk_w), F32),
                        pltpu.VMEM((b * H, M_DK, M_DV), F32),
                        pltpu.VMEM((b * H, 1, M_DK), F32),
                        pltpu.VMEM((b * H, 1, LANES_V7X), F32)],
        compiler_params=_cparams(("arbitrary",)),
        name="mlstm",
    )(z3, z3, z3, z3, z3, conv_w, conv_b.reshape(1, -1), bias.reshape(1, -1), bias.reshape(-1, 1),
      gain.reshape(1, v_w))


def _mix_ffn_ple_kernel(*refs, n_mix, tm, seq, tf):
    hp_ref, h_ref = refs[0], refs[1]
    mix_refs = refs[2:2 + 2 * n_mix]
    (wo_ref, g_ref, wa_ref, wb_ref, cw_ref, cb_ref, wd_ref, p_ref, wp_ref, ng_ref, gg_ref, wg_ref,
     o_ref, xn_ref, a_ref) = refs[2 + 2 * n_mix:]
    i = pl.program_id(0)
    halo = 8
    g = g_ref[...]
    m_prev = jnp.concatenate([mix_refs[2 * k][...] for k in range(n_mix)], axis=1)
    m_tile = jnp.concatenate([mix_refs[2 * k + 1][...] for k in range(n_mix)], axis=1)
    x = h_ref[...] + _dot(m_tile, wo_ref[...])
    xp = hp_ref[...] + _dot(m_prev, wo_ref[...])[BF16_ROWS - halo:, :]
    xn_ref[halo:halo + tm, :] = _rms(x, g).astype(BF16)
    keep = ((i * tm) % seq != 0).astype(F32)
    xn_ref[0:halo, :] = (_rms(xp, g) * keep).astype(BF16)
    n_chunks = D_FF // tf

    def up_proj(c):
        cs = slice(c * tf, (c + 1) * tf)
        a_ref[c % 2] = _dot(xn_ref[...], wa_ref[:, cs])
        return _dot(xn_ref[halo:halo + tm, :], wb_ref[:, cs])

    h2 = x
    bgate = up_proj(0)
    for c in range(n_chunks):
        cs = slice(c * tf, (c + 1) * tf)
        bgate_next = up_proj(c + 1) if c + 1 < n_chunks else None
        conv = cb_ref[:, cs]
        for t in range(FFN_CONV):
            conv = conv + a_ref[c % 2, pl.ds(halo - (FFN_CONV - 1) + t, tm), :] * cw_ref[t:t + 1, cs]
        act = (_gelu_tanh(conv) * bgate).astype(BF16)
        h2 = h2 + _dot(act, wd_ref[cs, :])
        bgate = bgate_next
    e = _rms(_dot(p_ref[...].astype(BF16), wp_ref[...]), ng_ref[...])
    gate = jax.nn.sigmoid(_dot(_rms(h2, gg_ref[...]).astype(BF16), wg_ref[...]))
    o_ref[...] = h2 + gate * e


def mix_ffn_ple(h, mix, li, g, w_up, conv_w, conv_b, w_down, p, wp, norm_g, gate_norm_g, wg, seq, tm=512, tf=2816):
    mix_outs, w_out = mix
    n, d = h.shape
    pd = p.shape[-1]
    hb = tm // 8
    const = lambda i: (0, 0)
    layer = lambda i: (li, 0, 0)
    prev8 = lambda i: (jnp.maximum(i * hb - 1, 0), 0)
    resident = dict(pipeline_mode=pl.Buffered(1))
    prev16 = lambda i: (jnp.maximum(i * (tm // BF16_ROWS) - 1, 0), 0)
    mix_specs, mix_args = [], []
    for m in mix_outs:
        mix_specs += [pl.BlockSpec((BF16_ROWS, m.shape[1]), prev16),
                      pl.BlockSpec((tm, m.shape[1]), lambda i: (i, 0))]
        mix_args += [m, m]
    mix_specs.append(pl.BlockSpec(w_out.shape, const, **resident))
    mix_args.append(w_out)
    return pl.pallas_call(
        functools.partial(_mix_ffn_ple_kernel, n_mix=len(mix_outs), tm=tm, seq=seq, tf=tf),
        out_shape=jax.ShapeDtypeStruct((n, d), F32),
        grid=(n // tm,),
        in_specs=[pl.BlockSpec((8, d), prev8),
                  pl.BlockSpec((tm, d), lambda i: (i, 0))] + mix_specs + [
                  pl.BlockSpec((None, 1, d), layer),
                  pl.BlockSpec((None, d, D_FF), lambda i: (li, 0, 0), **resident),
                  pl.BlockSpec((None, d, D_FF), lambda i: (li, 0, 1), **resident),
                  pl.BlockSpec((None, FFN_CONV, D_FF), layer),
                  pl.BlockSpec((None, 1, D_FF), layer),
                  pl.BlockSpec((None, D_FF, d), layer, **resident),
                  pl.BlockSpec((None, tm, pd), lambda i: (li, i, 0)),
                  pl.BlockSpec((None, pd, d), layer, **resident),
                  pl.BlockSpec((None, 1, d), layer),
                  pl.BlockSpec((None, 1, d), layer),
                  pl.BlockSpec((None, d, d), layer, **resident)],
        out_specs=pl.BlockSpec((tm, d), lambda i: (i, 0)),
        scratch_shapes=[pltpu.VMEM((8 + tm, d), BF16),
                        pltpu.VMEM((min(2, D_FF // tf), 8 + tm, tf), F32)],
        compiler_params=_cparams(("parallel",), VMEM_LIMIT_V7X),
        name="mix_ffn_ple",
    )(h, h, *mix_args, g[:, None, :], w_up, w_up, conv_w, conv_b[:, None, :], w_down,
      p, wp, norm_g[:, None, :], gate_norm_g[:, None, :], wg)


def _group_rms(x, g):
    lane = lax.broadcasted_iota(jnp.int32, x.shape, 1)
    x2 = x * x
    ms = jnp.zeros_like(x)
    for grp in range(N_KV_GROUPS):
        in_grp = (lane >= grp * HEAD_DIM) & (lane < (grp + 1) * HEAD_DIM)
        tot = jnp.sum(jnp.where(in_grp, x2, 0.0), axis=-1, keepdims=True)
        ms = jnp.where(in_grp, tot * (1.0 / HEAD_DIM), ms)
    return x * lax.rsqrt(ms + EPS) * g


def _ones_rows(width):
    return (lax.broadcasted_iota(jnp.int32, (BF16_ROWS, width), 0) == 0).astype(BF16)


def _kv_prep_kernel(c_ref, s_ref, w_ref, gs_ref, gw_ref, kc_ref, vc_ref, ks_ref, vs_ref, kw_ref, vw_ref):
    gw = N_KV_GROUPS * HEAD_DIM
    cc = c_ref[...]
    ss = s_ref[...]
    ww = w_ref[...]
    kc_ref[...] = cc[:, 0:gw]
    vc_ref[...] = cc[:, gw:2 * gw]
    ks = _group_rms(ss[:, 0:gw], gs_ref[...]).astype(BF16)
    kw = _group_rms(ww[:, 0:gw], gw_ref[...]).astype(BF16)
    vst = ss[:, gw:2 * gw].T
    vwt = ww[:, gw:2 * gw].T
    for g in range(N_KV_GROUPS):
        lanes = slice(g * HEAD_DIM, (g + 1) * HEAD_DIM)
        ks_ref[g] = ks[:, lanes]
        kw_ref[g] = kw[:, lanes]
        vs_ref[g] = jnp.concatenate([vst[lanes, :].astype(BF16), _ones_rows(SEL_TILE)], axis=0)
        for u in range(SEL_TILE // KEY_TILE):
            vw_ref[g, u] = jnp.concatenate([vwt[lanes, u * KEY_TILE:(u + 1) * KEY_TILE].astype(BF16),
                                            _ones_rows(KEY_TILE)], axis=0)


def kv_prep(z, k_g, b, t):
    n = z.shape[0]
    G = N_KV_GROUPS
    gw = G * HEAD_DIM
    base = N_HEADS * HEAD_DIM // (2 * gw)
    tm = SEL_TILE
    nst = t // tm
    sub = SEL_TILE // KEY_TILE
    row = lambda i, j: i * nst + j
    flat = jax.ShapeDtypeStruct((n, gw), F32)
    keys = jax.ShapeDtypeStruct((b, G, t, HEAD_DIM), BF16)
    flat_spec = pl.BlockSpec((tm, gw), lambda i, j: (row(i, j), 0))
    key_spec = pl.BlockSpec((None, G, tm, HEAD_DIM), lambda i, j: (i, 0, j, 0))
    return pl.pallas_call(
        _kv_prep_kernel,
        out_shape=(flat, flat, keys, jax.ShapeDtypeStruct((b, G, nst, V_ROWS, tm), BF16),
                   keys, jax.ShapeDtypeStruct((b, G, nst * sub, V_ROWS, KEY_TILE), BF16)),
        grid=(b, nst),
        in_specs=[pl.BlockSpec((tm, 2 * gw), lambda i, j: (row(i, j), base)),
                  pl.BlockSpec((tm, 2 * gw), lambda i, j: (row(i, j), base + 1)),
                  pl.BlockSpec((tm, 2 * gw), lambda i, j: (row(i, j), base + 2)),
                  pl.BlockSpec((1, gw), lambda i, j: (0, 0)),
                  pl.BlockSpec((1, gw), lambda i, j: (0, 0))],
        out_specs=(flat_spec, flat_spec, key_spec,
                   pl.BlockSpec((None, G, None, V_ROWS, tm), lambda i, j: (i, 0, j, 0, 0)),
                   key_spec,
                   pl.BlockSpec((None, G, sub, V_ROWS, KEY_TILE), lambda i, j: (i, 0, j, 0, 0))),
        compiler_params=_cparams(("parallel", "parallel")),
        name="kv_prep",
    )(z, z, z, jnp.tile(k_g[1], G).reshape(1, gw), jnp.tile(k_g[2], G).reshape(1, gw))


def _compress_kernel(x_ref, pos_ref, w1_ref, wbd_ref, w2_ref, g_ref, o_ref, *, normalize, transposed):
    G = N_KV_GROUPS
    ncb = x_ref.shape[0] // CMP_STRIDE
    u = jnp.zeros((ncb, G * CMP_HIDDEN), F32)
    v = jnp.zeros((ncb, G * CMP_HIDDEN), F32)
    for r in range(CMP_STRIDE):
        xr = x_ref[pl.ds(r, ncb, stride=CMP_STRIDE), :].astype(BF16)
        u = u + _dot(xr, wbd_ref[0, r])
        v = v + _dot(xr, wbd_ref[1, r])
    posc = _dot(pos_ref[...], w1_ref[...])[0:1, :]
    hid = u + pltpu.roll(v, ncb - 1, 0) + jnp.concatenate([posc] * G, axis=1)
    outs = []
    for grp in range(G):
        out = _dot(jax.nn.gelu(hid[:, grp * CMP_HIDDEN:(grp + 1) * CMP_HIDDEN]).astype(BF16), w2_ref[...])
        outs.append(_rms(out, g_ref[...]) if normalize else out)
    if transposed:
        both = jnp.concatenate(outs, axis=1).T
        for grp in range(G):
            o_ref[grp] = both[grp * HEAD_DIM:(grp + 1) * HEAD_DIM, :].astype(o_ref.dtype)
    else:
        for grp in range(G):
            o_ref[grp] = outs[grp].astype(o_ref.dtype)


def compress(x3, pos, w1, w2, g, normalize, transposed):
    b, t, gw = x3.shape
    G = N_KV_GROUPS
    ncb = t // CMP_STRIDE
    kdim = CMP_STRIDE * HEAD_DIM
    posf = jnp.broadcast_to(pos.reshape(1, -1), (8, 2 * kdim)).astype(BF16)
    w1b = w1.astype(BF16)
    w1r = w1b.reshape(2, CMP_STRIDE, HEAD_DIM, CMP_HIDDEN)
    eye = jnp.eye(G, dtype=BF16)
    wbd = jnp.einsum('gh,srdc->srgdhc', eye, w1r).reshape(2, CMP_STRIDE, G * HEAD_DIM, G * CMP_HIDDEN)
    out_sds = (jax.ShapeDtypeStruct((b, G, HEAD_DIM, ncb), BF16) if transposed
               else jax.ShapeDtypeStruct((b, G, ncb, HEAD_DIM), BF16))
    out_block = (None, G, HEAD_DIM, ncb) if transposed else (None, G, ncb, HEAD_DIM)
    return pl.pallas_call(
        functools.partial(_compress_kernel, normalize=normalize, transposed=transposed),
        out_shape=out_sds,
        grid=(b,),
        in_specs=[pl.BlockSpec((None, t, gw), lambda i: (i, 0, 0)),
                  pl.BlockSpec((8, 2 * kdim), lambda i: (0, 0)),
                  pl.BlockSpec((2 * kdim, CMP_HIDDEN), lambda i: (0, 0)),
                  pl.BlockSpec((2, CMP_STRIDE, G * HEAD_DIM, G * CMP_HIDDEN), lambda i: (0, 0, 0, 0)),
                  pl.BlockSpec((CMP_HIDDEN, HEAD_DIM), lambda i: (0, 0)),
                  pl.BlockSpec((1, HEAD_DIM), lambda i: (0, 0))],
        out_specs=pl.BlockSpec(out_block, lambda i: (i, 0, 0, 0)),
        compiler_params=_cparams(("parallel",)),
        name="compress",
    )(x3, posf, w1b, wbd, w2.astype(BF16), g.reshape(1, HEAD_DIM))


def _nsa_kernel(bnd_ref, *refs):
    safe = bnd_ref[0, 0] <= MAX_SAFE_SCORE_BOUND

    @pl.when(safe)
    def _():
        _nsa_body(bnd_ref, *refs, bounded=True)

    @pl.when(jnp.logical_not(safe))
    def _():
        _nsa_body(bnd_ref, *refs, bounded=False)


def _nsa_body(bnd_ref, zq_ref, gt_ref, gb_ref, qg_ref, ovt_ref, kc_ref, vct_ref, ks_ref, vst_ref, kw_ref,
              vwt_ref, o_ref, q_scr, sel_scr, m_scr, acc_scr, oc_scr, *, bounded):
    grp = pl.program_id(1)
    qb = pl.program_id(2)
    QB = Q_BLOCK
    ncp = kc_ref.shape[0]
    ns = ovt_ref.shape[0]
    tpos = qb * QB + lax.broadcasted_iota(jnp.int32, (1, QB), 1)

    def lane_tile(x):
        return jnp.concatenate([x] * HG, axis=1)

    zt = zq_ref[...].T
    qg = qg_ref[...] * (HEAD_DIM ** -0.5 * LOG2E)
    heads = []
    for h in range(HG):
        xh = zt[h * HEAD_DIM:(h + 1) * HEAD_DIM, :]
        ms = jnp.mean(xh * xh, axis=0, keepdims=True)
        heads.append(xh * lax.rsqrt(ms + EPS) * qg)
    q_scr[...] = jnp.concatenate(heads, axis=1).astype(BF16)
    q = q_scr[...]

    gates = jax.nn.sigmoid(gt_ref[...].T + gb_ref[...])
    per_grp = HG * 3
    gsel = gates[0:per_grp, :]
    for g2 in range(1, N_KV_GROUPS):
        gsel = jnp.where(grp == g2, gates[g2 * per_grp:(g2 + 1) * per_grp, :], gsel)

    def gate_row(c):
        return jnp.concatenate([gsel[3 * h + c:3 * h + c + 1, :] for h in range(HG)], axis=1)

    cmp_end = lax.broadcasted_iota(jnp.int32, (ncp, 1), 0) * CMP_STRIDE + (CMP_BLOCK - 1)
    keep = -bnd_ref[0, 0] if bounded else 0.0
    cbias = jnp.where(cmp_end <= tpos, keep, NEG)
    s = _dot(kc_ref[...], q) + lane_tile(cbias)
    e = jnp.exp2(s) if bounded else jnp.exp2(s - jnp.max(s, axis=0, keepdims=True))
    cmp_lhs = jnp.concatenate([vct_ref[...], _ones_rows(ncp), ovt_ref[...]], axis=0)
    r = _dot(cmp_lhs, e.astype(BF16))
    inv = jnp.where(lane_tile(tpos) >= CMP_BLOCK - 1, 1.0 / r[HEAD_DIM:HEAD_DIM + 1, :], 0.0)
    ocmp = r[0:HEAD_DIM, :] * inv
    imp_h = r[V_ROWS:V_ROWS + ns, :] * inv
    imp = imp_h[:, 0:QB]
    for h in range(1, HG):
        imp = imp + imp_h[:, h * QB:(h + 1) * QB]

    n_win = (WINDOW + QB) // KEY_TILE
    first_tile = qb * (QB // KEY_TILE) - WINDOW // KEY_TILE
    win_sub = lax.broadcasted_iota(jnp.int32, (n_win * KEY_TILE, 1), 0)
    tiles = [jnp.maximum(first_tile + u, 0) for u in range(n_win)]
    kwin = jnp.concatenate([kw_ref[j] for j in tiles], axis=0)
    vwin = jnp.concatenate([vwt_ref[j] for j in tiles], axis=1)
    kpos = first_tile * KEY_TILE + win_sub
    wbias = jnp.where((kpos <= tpos) & (kpos > tpos - WINDOW) & (kpos >= 0), keep, NEG)
    sw = _dot(kwin, q) + lane_tile(wbias)
    pw = jnp.exp2(sw) if bounded else jnp.exp2(sw - jnp.max(sw, axis=0, keepdims=True))
    ow = _dot(vwin, pw.astype(BF16))
    oc_scr[...] = gate_row(0) * ocmp + (gate_row(2) / ow[HEAD_DIM:HEAD_DIM + 1, :]) * ow[0:HEAD_DIM, :]

    blk = lax.broadcasted_iota(jnp.int32, (ns, 1), 0)
    blk_f = blk.astype(F32)
    cur = jnp.right_shift(tpos, SLC_BLOCK.bit_length() - 1)
    forced = (blk == 0) | (blk == cur) | (blk == cur - 1)
    bvalid = blk <= cur
    score = jnp.where(forced, -jnp.inf, jnp.where(bvalid, imp, NEG))
    sel = jnp.where(forced, 1.0, 0.0)
    for _ in range(max(min(N_SELECT, ns) - 3, 0)):
        mx = jnp.max(score, axis=0, keepdims=True)
        first = jnp.min(jnp.where(score == mx, blk_f, float(ns)), axis=0, keepdims=True)
        pick = blk_f == first
        sel = jnp.where(pick, 1.0, sel)
        score = jnp.where(pick, -jnp.inf, score)
    sel_scr[...] = jnp.where(bvalid, sel, 0.0)

    m_scr[...] = jnp.full(m_scr.shape, NEG, F32)
    acc_scr[...] = jnp.zeros(acc_scr.shape, F32)
    blocks_per_tile = SEL_TILE // SLC_BLOCK
    sel_sub = lax.broadcasted_iota(jnp.int32, (SEL_TILE, 1), 0)

    def sel_body(j, carry):
        rows = [jnp.broadcast_to(sel_scr[pl.ds(j * blocks_per_tile + bi, 1), :], (SLC_BLOCK, QB))
                for bi in range(blocks_per_tile)]
        chosen = jnp.concatenate(rows, axis=0)
        kpos = j * SEL_TILE + sel_sub
        bias = jnp.where((chosen > 0.5) & (kpos <= tpos), keep, NEG)
        st = _dot(ks_ref[j], q_scr[...]) + lane_tile(bias)
        if bounded:
            pt = jnp.exp2(st)
            acc_scr[...] += _dot(vst_ref[j], pt.astype(BF16))
        else:
            m_old = m_scr[...]
            m_new = jnp.maximum(m_old, jnp.max(st, axis=0, keepdims=True))
            alpha = jnp.exp2(m_old - m_new)
            pt = jnp.exp2(st - m_new)
            acc_scr[...] = alpha * acc_scr[...] + _dot(vst_ref[j], pt.astype(BF16))
            m_scr[...] = m_new
        return carry

    lax.fori_loop(0, qb // (SEL_TILE // QB) + 1, sel_body, 0)
    ot = (oc_scr[...]
          + (gate_row(1) / acc_scr[HEAD_DIM:HEAD_DIM + 1, :]) * acc_scr[0:HEAD_DIM, :])
    stacked = jnp.concatenate([ot[:, h * QB:(h + 1) * QB] for h in range(HG)], axis=0)
    o_ref[...] = stacked.T.astype(o_ref.dtype)


def nsa_attention(bound, z3, gate_b, q_g, overlap_t, kc, vct, ks, vst, kw, vwt):
    b, t, _ = z3.shape
    nq = t // Q_BLOCK
    nt = t // KEY_TILE
    qw = HG * HEAD_DIM
    ns, ncp = overlap_t.shape
    gate_blk = (NSA_COLS_PAD // LANES_V7X) - 1
    gb = jnp.zeros((LANES_V7X, 1), F32).at[:N_HEADS * 3, 0].set(gate_b)
    full5 = lambda i, g, q: (i, g, 0, 0, 0)
    lanes = HG * Q_BLOCK
    return pl.pallas_call(
        _nsa_kernel,
        out_shape=jax.ShapeDtypeStruct((b, t, N_HEADS * HEAD_DIM), BF16),
        grid=(b, N_KV_GROUPS, nq),
        in_specs=[pl.BlockSpec(memory_space=pltpu.SMEM),
                  pl.BlockSpec((None, Q_BLOCK, qw), lambda i, g, q: (i, q, g)),
                  pl.BlockSpec((None, Q_BLOCK, LANES_V7X), lambda i, g, q: (i, q, gate_blk)),
                  pl.BlockSpec((LANES_V7X, 1), lambda i, g, q: (0, 0)),
                  pl.BlockSpec((HEAD_DIM, 1), lambda i, g, q: (0, 0)),
                  pl.BlockSpec((ns, ncp), lambda i, g, q: (0, 0)),
                  pl.BlockSpec((None, None, ncp, HEAD_DIM), lambda i, g, q: (i, g, 0, 0)),
                  pl.BlockSpec((None, None, HEAD_DIM, ncp), lambda i, g, q: (i, g, 0, 0)),
                  pl.BlockSpec((None, None, t // SEL_TILE, SEL_TILE, HEAD_DIM), full5),
                  pl.BlockSpec((None, None, t // SEL_TILE, V_ROWS, SEL_TILE), full5),
                  pl.BlockSpec((None, None, nt, KEY_TILE, HEAD_DIM), full5),
                  pl.BlockSpec((None, None, nt, V_ROWS, KEY_TILE), full5)],
        out_specs=pl.BlockSpec((None, Q_BLOCK, qw), lambda i, g, q: (i, q, g)),
        scratch_shapes=[pltpu.VMEM((HEAD_DIM, lanes), BF16),
                        pltpu.VMEM((ns, Q_BLOCK), F32),
                        pltpu.VMEM((1, lanes), F32),
                        pltpu.VMEM((V_ROWS, lanes), F32),
                        pltpu.VMEM((HEAD_DIM, lanes), F32)],
        compiler_params=_cparams(("parallel", "parallel", "arbitrary"), VMEM_LIMIT_V7X),
        name="nsa_attention",
    )(bound.reshape(1, 1), z3, z3, gb, q_g.reshape(HEAD_DIM, 1), overlap_t, kc, vct, ks, vst, kw, vwt)


def _overlap_matrix_t(t):
    ncp = t // CMP_STRIDE
    ns = t // SLC_BLOCK
    c_start = np.arange(ncp) * CMP_STRIDE
    sj = np.arange(ns)
    ov = ((c_start[None, :] < (sj[:, None] + 1) * SLC_BLOCK)
          & (c_start[None, :] + CMP_BLOCK > sj[:, None] * SLC_BLOCK)
          & (c_start[None, :] + CMP_BLOCK <= t))
    return jnp.asarray(ov, dtype=BF16)


def ab_layer(h, b, t, norm_g, w_in, conv_w, conv_b, ret_g, ig_b, fg_b, m_g, w_out):
    n = b * t
    w_in_p = jnp.pad(w_in, ((0, 0), (0, AB_COLS_PAD - AB_COLS))).astype(BF16)
    z = norm_matmul(h, norm_g, w_in_p)
    z3 = z.reshape(b, t, AB_COLS_PAD)
    cos_t, sin_t = rope_tables(t)
    ret = retention(z3, cos_t, sin_t, ret_g)
    ml = mlstm(z3, conv_w, conv_b, ig_b, fg_b, m_g)
    w_out_b = w_out.astype(BF16)
    rw = R_HEADS * R_DV
    return [ret.reshape(n, rw), ml.reshape(n, -1)], w_out_b


def nsa_layer(h, b, t, norm_g, w_in, q_g, k_g, pos_k, pos_v, w1k, w2k, w1v, w2v, gate_b, w_out):
    n = b * t
    G = N_KV_GROUPS
    w_in_p = jnp.pad(w_in, ((0, 0), (0, NSA_COLS_PAD - NSA_COLS))).astype(BF16)
    z = norm_matmul(h, norm_g, w_in_p)
    kc_in, vc_in, ks, vst, kw, vwt = kv_prep(z, k_g, b, t)
    gw = G * HEAD_DIM
    kc = compress(kc_in.reshape(b, t, gw), pos_k, w1k, w2k, k_g[0], True, False)
    vct = compress(vc_in.reshape(b, t, gw), pos_v, w1v, w2v, k_g[0], False, True)

    def key_tiles(x, kt):
        return x.reshape(b, G, t // kt, kt, HEAD_DIM)

    bound = 1.02 * LOG2E * math.sqrt(HEAD_DIM) * jnp.max(jnp.abs(q_g)) * jnp.max(jnp.abs(k_g))
    args = (bound, z.reshape(b, t, NSA_COLS_PAD), gate_b, q_g, _overlap_matrix_t(t), kc, vct,
            key_tiles(ks, SEL_TILE), vst, key_tiles(kw, KEY_TILE), vwt)
    o = nsa_attention(*args)
    return [o.reshape(n, -1)], w_out.astype(BF16)


def kernel(x, p, ab_norm_g, ab_w_in, ab_conv_w, ab_conv_b, ab_ret_norm_g, ab_ig_b, ab_fg_b, ab_m_norm_g, ab_w_out, nsa_norm_g, nsa_w_in, nsa_q_norm_g, nsa_k_norm_g, nsa_cmp_pos_k, nsa_cmp_pos_v, nsa_cmp_w1k, nsa_cmp_w2k, nsa_cmp_w1v, nsa_cmp_w2v, nsa_gate_b, nsa_w_out, ffn_norm_g, ffn_w_up, ffn_conv_w, ffn_conv_b, ffn_w_down, ple_w, ple_norm_g, ple_gate_norm_g, ple_w_gate):
    b, t, d = x.shape
    n = b * t
    depth = p.shape[0]
    h = x.reshape(n, d)
    p2 = p.reshape(depth, n, -1)
    ffn_w_up_b, ffn_w_down_b = ffn_w_up.astype(BF16), ffn_w_down.astype(BF16)
    ple_w_b, ple_w_gate_b = ple_w.astype(BF16), ple_w_gate.astype(BF16)
    for i in range(depth):
        j = i // 2
        if i % 2 == 0:
            mix = ab_layer(h, b, t, ab_norm_g[j], ab_w_in[j], ab_conv_w[j], ab_conv_b[j], ab_ret_norm_g[j],
                           ab_ig_b[j], ab_fg_b[j], ab_m_norm_g[j], ab_w_out[j])
        else:
            mix = nsa_layer(h, b, t, nsa_norm_g[j], nsa_w_in[j], nsa_q_norm_g[j], nsa_k_norm_g[j],
                            nsa_cmp_pos_k[j], nsa_cmp_pos_v[j], nsa_cmp_w1k[j], nsa_cmp_w2k[j],
                            nsa_cmp_w1v[j], nsa_cmp_w2v[j], nsa_gate_b[j], nsa_w_out[j])
        h = mix_ffn_ple(h, mix, i, ffn_norm_g, ffn_w_up_b, ffn_conv_w, ffn_conv_b, ffn_w_down_b, p2, ple_w_b,
                        ple_norm_g, ple_gate_norm_g, ple_w_gate_b, t)
    return h.reshape(b, t, d)
```

```python
import functools
import math

import numpy as np
import jax
import jax.numpy as jnp
from jax import lax
from jax.experimental import pallas as pl
from jax.experimental.pallas import tpu as pltpu

F32 = jnp.float32
BF16 = jnp.bfloat16

LANES_V7X = 128
BF16_ROWS = 16
VMEM_LIMIT_V7X = 56 * 1024 * 1024

D_MODEL = 1024
PLE_DIM = 256
R_HEADS, R_DK, R_DV, R_CHUNK = 4, 64, 128, 128
ROPE_BASE = 10000.0
M_HEADS, M_DK, M_DV, M_CHUNK, M_CONV = 4, 64, 128, 64, 4
AB_SIZES = (R_HEADS * R_DK, R_HEADS * R_DK, R_HEADS * R_DV, R_HEADS * R_DV,
            M_HEADS * M_DK, M_HEADS * M_DK, M_HEADS * M_DV, M_HEADS * M_DV, M_HEADS, M_HEADS)
AB_COLS = sum(AB_SIZES)
AB_COLS_PAD = 3200
N_HEADS, N_KV_GROUPS, HEAD_DIM = 16, 2, 64
HG = N_HEADS // N_KV_GROUPS
CMP_BLOCK, CMP_STRIDE, CMP_HIDDEN = 32, 16, 256
SLC_BLOCK, N_SELECT, WINDOW = 64, 16, 512
Q_BLOCK = 256
NSA_COLS = N_HEADS * HEAD_DIM + 6 * N_KV_GROUPS * HEAD_DIM + N_HEADS * 3
NSA_COLS_PAD = 1920
D_FF = 2816
FFN_CONV = 3
NEG = -1e30
EPS = 1e-6
KEY_TILE = 128
SEL_TILE = 512
V_ROWS = HEAD_DIM + BF16_ROWS
LOG2E = math.log2(math.e)
MAX_SAFE_SCORE_BOUND = 56.0


def _cparams(sem, vmem=None):
    return pltpu.CompilerParams(dimension_semantics=sem, vmem_limit_bytes=vmem)


def _rms(x, g):
    ms = jnp.mean(x * x, axis=-1, keepdims=True)
    return x * lax.rsqrt(ms + EPS) * g


def _dot(a, b):
    return jnp.dot(a, b, preferred_element_type=F32)


def _dot_nt(a, b):
    return lax.dot_general(a, b, (((1,), (1,)), ((), ())), preferred_element_type=F32)


def _dot_f32(a, b):
    return jnp.dot(a, b, preferred_element_type=F32, precision=lax.Precision.HIGHEST)


def _gelu_tanh(x):
    k1 = -2.0 * math.sqrt(2.0 / math.pi) * LOG2E
    return x / (1.0 + jnp.exp2(x * (k1 + (k1 * 0.044715) * (x * x))))


def _norm_matmul_kernel(x_ref, g_ref, w_ref, o_ref):
    xn = _rms(x_ref[...], g_ref[...]).astype(BF16)
    o_ref[...] = _dot(xn, w_ref[...]).astype(o_ref.dtype)


def norm_matmul(x, g, w, tm=1024, out_dtype=F32):
    n, d = x.shape
    nc = w.shape[1]
    return pl.pallas_call(
        _norm_matmul_kernel,
        out_shape=jax.ShapeDtypeStruct((n, nc), out_dtype),
        grid=(n // tm,),
        in_specs=[pl.BlockSpec((tm, d), lambda i: (i, 0)),
                  pl.BlockSpec((1, d), lambda i: (0, 0)),
                  pl.BlockSpec((d, nc), lambda i: (0, 0))],
        out_specs=pl.BlockSpec((tm, nc), lambda i: (i, 0)),
        compiler_params=_cparams(("parallel",), VMEM_LIMIT_V7X),
        name="norm_matmul",
    )(x, g.reshape(1, d), w)


def _rope_table_kernel(inv_ref, cos_ref, sin_ref):
    c = pl.program_id(0)
    rows, width = cos_ref.shape
    pos = (c * rows + lax.broadcasted_iota(jnp.int32, (rows, LANES_V7X), 0)).astype(F32)
    lane = lax.broadcasted_iota(jnp.int32, (rows, LANES_V7X), 1)
    ang = pos * inv_ref[:, 0:LANES_V7X]
    cs = jnp.cos(ang)
    sn = jnp.sin(ang)
    sn = jnp.where(lane % R_DK < R_DK // 2, -sn, sn)
    reps = width // LANES_V7X
    cos_ref[...] = jnp.concatenate([cs] * reps, axis=1)
    sin_ref[...] = jnp.concatenate([sn] * reps, axis=1)


def rope_tables(t):
    half = R_DK // 2
    inv = ROPE_BASE ** (-jnp.arange(half, dtype=F32) / half)
    inv = jnp.tile(inv, 2 * R_HEADS).reshape(1, R_HEADS * R_DK)
    width = R_HEADS * R_DK
    shp = jax.ShapeDtypeStruct((t, width), F32)
    return pl.pallas_call(
        _rope_table_kernel,
        out_shape=(shp, shp),
        grid=(t // R_CHUNK,),
        in_specs=[pl.BlockSpec((1, width), lambda c: (0, 0))],
        out_specs=(pl.BlockSpec((R_CHUNK, width), lambda c: (c, 0)),
                   pl.BlockSpec((R_CHUNK, width), lambda c: (c, 0))),
        compiler_params=_cparams(("parallel",)),
        name="rope_tables",
    )(inv)


def _retention_kernel(cos_ref, sin_ref, q_ref, k_ref, v_ref, g_ref, gain_ref, o_ref, r_ref):
    c = pl.program_id(0)
    nb = q_ref.shape[0]
    L = R_CHUNK

    @pl.when(c == 0)
    def _():
        r_ref[...] = jnp.zeros_like(r_ref)

    cos = cos_ref[...]
    sin = sin_ref[...]
    lane = lax.broadcasted_iota(jnp.int32, cos.shape, 1)
    first_half = lane % R_DK < R_DK // 2
    width = R_HEADS * R_DK

    def rot(x):
        swapped = jnp.where(first_half, pltpu.roll(x, width - R_DK // 2, 1), pltpu.roll(x, R_DK // 2, 1))
        return x * cos + swapped * sin

    gain = gain_ref[...]
    ri = lax.broadcasted_iota(jnp.int32, (L, L), 0)
    ci = lax.broadcasted_iota(jnp.int32, (L, L), 1)
    diff = (ri - ci).astype(F32)
    causal = ri >= ci
    idx = lax.broadcasted_iota(jnp.int32, (L, 1), 0).astype(F32)
    decays = []
    for h in range(R_HEADS):
        log_g = math.log1p(-2.0 ** (-5.0 - h))
        decays.append(dict(
            dmask=jnp.where(causal, jnp.exp(jnp.where(causal, diff, 0.0) * log_g), 0.0),
            q_dec=jnp.exp((idx + 1.0) * log_g), k_dec=jnp.exp((L - 1.0 - idx) * log_g),
            c_dec=math.exp(L * log_g)))

    chains = []
    for bi in range(nb):
        q = rot(q_ref[bi])
        k = rot(k_ref[bi]) * (R_DK ** -0.5)
        for h in range(R_HEADS):
            chains.append(dict(bi=bi, h=h, u=bi * R_HEADS + h,
                               qb=q[:, h * R_DK:(h + 1) * R_DK].astype(BF16),
                               kh=k[:, h * R_DK:(h + 1) * R_DK]))

    for ch in chains:
        ch['rstate'] = r_ref[ch['u']]
        ch['s'] = _dot_nt(ch['qb'], ch['kh'].astype(BF16)) * decays[ch['h']]['dmask']
        ch['qr'] = _dot(ch['qb'], ch['rstate'].astype(BF16))

    for ch in chains:
        d = decays[ch['h']]
        vh = v_ref[ch['bi'], :, ch['h'] * R_DV:(ch['h'] + 1) * R_DV].astype(BF16)
        ch['o'] = _dot(ch['s'].astype(BF16), vh) + ch['qr'] * d['q_dec']
        kd = (ch['kh'] * d['k_dec']).T.astype(BF16)
        r_ref[ch['u']] = d['c_dec'] * ch['rstate'] + _dot(kd, vh)

    for ch in chains:
        sl = slice(ch['h'] * R_DV, (ch['h'] + 1) * R_DV)
        gh = g_ref[ch['bi'], :, sl]
        o_ref[ch['bi'], :, sl] = (_rms(ch['o'], gain[:, sl]) * (gh * jax.nn.sigmoid(gh))).astype(o_ref.dtype)


def retention(z3, cos_t, sin_t, gain):
    b, t, _ = z3.shape
    nc = t // R_CHUNK
    qk_w = R_HEADS * R_DK
    v_w = R_HEADS * R_DV
    return pl.pallas_call(
        _retention_kernel,
        out_shape=jax.ShapeDtypeStruct((b, t, v_w), BF16),
        grid=(nc,),
        in_specs=[pl.BlockSpec((R_CHUNK, qk_w), lambda c: (c, 0)),
                  pl.BlockSpec((R_CHUNK, qk_w), lambda c: (c, 0)),
                  pl.BlockSpec((b, R_CHUNK, qk_w), lambda c: (0, c, 0)),
                  pl.BlockSpec((b, R_CHUNK, qk_w), lambda c: (0, c, 1)),
                  pl.BlockSpec((b, R_CHUNK, v_w), lambda c: (0, c, 1)),
                  pl.BlockSpec((b, R_CHUNK, v_w), lambda c: (0, c, 2)),
                  pl.BlockSpec((1, v_w), lambda c: (0, 0))],
        out_specs=pl.BlockSpec((b, R_CHUNK, v_w), lambda c: (0, c, 0)),
        scratch_shapes=[pltpu.VMEM((b * R_HEADS, R_DK, R_DV), F32)],
        compiler_params=_cparams(("arbitrary",)),
        name="retention",
    )(cos_t, sin_t, z3, z3, z3, z3, gain.reshape(1, v_w))


def _mlstm_kernel(q_ref, k_ref, v_ref, og_ref, gc_ref, cw_ref, cb_ref, bc_ref, br_ref, gain_ref,
                  o_ref, xbuf, c_ref, n_ref, m_ref):
    c = pl.program_id(0)
    nb = q_ref.shape[0]
    L = M_CHUNK
    H = M_HEADS
    qk_w = H * M_DK
    halo = 8

    @pl.when(c == 0)
    def _():
        xbuf[:, 0:halo, :] = jnp.zeros((nb, halo, 2 * qk_w), F32)
        c_ref[...] = jnp.zeros_like(c_ref)
        n_ref[...] = jnp.zeros_like(n_ref)
        m_ref[...] = jnp.zeros_like(m_ref)

    gain = gain_ref[...]
    ri = lax.broadcasted_iota(jnp.int32, (L, L), 0)
    ci = lax.broadcasted_iota(jnp.int32, (L, L), 1)
    causal = ri >= ci
    tril = causal.astype(F32)
    triu = (ri <= ci).astype(F32)

    chains = []
    for bi in range(nb):
        xbuf[bi, halo:halo + L, 0:qk_w] = q_ref[bi]
        xbuf[bi, halo:halo + L, qk_w:2 * qk_w] = k_ref[bi]
        conv = cb_ref[...]
        for j in range(M_CONV):
            conv = conv + xbuf[bi, pl.ds(halo - (M_CONV - 1) + j, L), :] * cw_ref[j:j + 1, :]
        tail = xbuf[bi, L:L + halo, :]
        xbuf[bi, 0:halo, :] = tail
        act = conv * jax.nn.sigmoid(conv)
        q = act[:, 0:qk_w] * (M_DK ** -0.5)
        k = act[:, qk_w:2 * qk_w]
        gblock = gc_ref[bi]
        gc = gblock[:, 0:2 * H] + bc_ref[...]
        gr = gblock.T[0:2 * H, :] + br_ref[...]
        ig_c = gc[:, 0:H]
        ig_r = gr[0:H, :]
        b_c = _dot_f32(tril, jax.nn.log_sigmoid(gc[:, H:2 * H]))
        b_r = _dot_f32(jax.nn.log_sigmoid(gr[H:2 * H, :]), triu)
        for h in range(H):
            chains.append(dict(
                bi=bi, h=h, u=bi * H + h,
                qh=q[:, h * M_DK:(h + 1) * M_DK], kh=k[:, h * M_DK:(h + 1) * M_DK],
                bh=b_c[:, h:h + 1], brow=b_r[h:h + 1, :], irow=ig_r[h:h + 1, :], icol=ig_c[:, h:h + 1]))

    def stack(parts):
        return jnp.concatenate(parts, axis=0)

    def rows(x):
        return jnp.broadcast_to(x, (L, x.shape[1]))

    m_prev_u = [m_ref[ch['u']][:, 0:1] for ch in chains]
    b_last_u = [ch['bh'][L - 1:L, :] for ch in chains]
    bh = stack([ch['bh'] for ch in chains])
    icol = stack([ch['icol'] for ch in chains])
    brow = stack([rows(ch['brow']) for ch in chains])
    irow = stack([rows(ch['irow']) for ch in chains])
    m_prev = stack([rows(m) for m in m_prev_u])
    b_last = stack([rows(x) for x in b_last_u])
    causal_all = stack([causal] * len(chains))
    qs = stack([ch['qh'] for ch in chains])
    ks = stack([ch['kh'] for ch in chains])
    qb = qs.astype(BF16)
    kb = ks.astype(BF16)
    cstates = [c_ref[ch['u']] for ch in chains]
    nstates = [n_ref[ch['u']] for ch in chains]

    def chain_rows(x, i):
        return x[i * L:(i + 1) * L]

    s_raw = stack([_dot_nt(chain_rows(qb, i), chain_rows(kb, i)) for i in range(len(chains))])
    qc = stack([_dot(chain_rows(qb, i), cstates[i].astype(BF16)) for i in range(len(chains))])

    glog = jnp.where(causal_all, irow - brow, NEG)
    c = jnp.maximum(m_prev, jnp.max(glog, axis=-1, keepdims=True))
    m_t = bh + c
    s = s_raw * jnp.exp(glog - c)
    w_inter = jnp.exp(m_prev - c)
    wlog = b_last - bh + icol
    m_new_u = [jnp.maximum(b_last_u[i] + m_prev_u[i], jnp.max(chain_rows(wlog, i), axis=0, keepdims=True))
               for i in range(len(chains))]
    m_new = stack([rows(m) for m in m_new_u])
    wk = ks * jnp.exp(wlog - m_new)

    sb = s.astype(BF16)
    vhs = [v_ref[ch['bi'], :, ch['h'] * M_DV:(ch['h'] + 1) * M_DV].astype(BF16) for ch in chains]
    sv = stack([_dot(chain_rows(sb, i), vhs[i]) for i in range(len(chains))])
    kv = [_dot(chain_rows(wk, i).T.astype(BF16), vhs[i]) for i in range(len(chains))]

    qn = jnp.sum(qs * stack([rows(n) for n in nstates]), axis=-1, keepdims=True)
    den = jnp.sum(s, axis=-1, keepdims=True) + w_inter * qn
    hh = (sv + w_inter * qc) / jnp.maximum(jnp.abs(den), jnp.exp(-m_t))
    og = stack([og_ref[ch['bi'], :, ch['h'] * M_DV:(ch['h'] + 1) * M_DV] for ch in chains])
    gains = stack([rows(gain[:, ch['h'] * M_DV:(ch['h'] + 1) * M_DV]) for ch in chains])
    out = jax.nn.sigmoid(og) * _rms(hh, gains)

    for i, ch in enumerate(chains):
        u = ch['u']
        decay = jnp.exp(b_last_u[i] + m_prev_u[i] - m_new_u[i])
        c_ref[u] = decay * cstates[i] + kv[i]
        n_ref[u] = decay * nstates[i] + jnp.sum(chain_rows(wk, i), axis=0, keepdims=True)
        m_ref[u] = jnp.broadcast_to(m_new_u[i], (1, LANES_V7X))
        o_ref[ch['bi'], :, ch['h'] * M_DV:(ch['h'] + 1) * M_DV] = chain_rows(out, i).astype(o_ref.dtype)


def mlstm(z3, conv_w, conv_b, ig_b, fg_b, gain):
    b, t, _ = z3.shape
    nc = t // M_CHUNK
    H = M_HEADS
    qk_w = H * M_DK
    v_w = H * M_DV
    bias = jnp.concatenate([ig_b, fg_b])
    gate_blk = AB_COLS_PAD // LANES_V7X - 1
    return pl.pallas_call(
        _mlstm_kernel,
        out_shape=jax.ShapeDtypeStruct((b, t, v_w), BF16),
        grid=(nc,),
        in_specs=[pl.BlockSpec((b, M_CHUNK, qk_w), lambda c: (0, c, 6)),
                  pl.BlockSpec((b, M_CHUNK, qk_w), lambda c: (0, c, 7)),
                  pl.BlockSpec((b, M_CHUNK, v_w), lambda c: (0, c, 4)),
                  pl.BlockSpec((b, M_CHUNK, v_w), lambda c: (0, c, 5)),
                  pl.BlockSpec((b, M_CHUNK, LANES_V7X), lambda c: (0, c, gate_blk)),
                  pl.BlockSpec((M_CONV, 2 * qk_w), lambda c: (0, 0)),
                  pl.BlockSpec((1, 2 * qk_w), lambda c: (0, 0)),
                  pl.BlockSpec((1, 2 * H), lambda c: (0, 0)),
                  pl.BlockSpec((2 * H, 1), lambda c: (0, 0)),
                  pl.BlockSpec((1, v_w), lambda c: (0, 0))],
        out_specs=pl.BlockSpec((b, M_CHUNK, v_w), lambda c: (0, c, 0)),
        scratch_shapes=[pltpu.VMEM((b, 8 + M_CHUNK, 2 * qk_w), F32),
                        pltpu.VMEM((b * H, M_DK, M_DV), F32),
                        pltpu.VMEM((b * H, 1, M_DK), F32),
                        pltpu.VMEM((b * H, 1, LANES_V7X), F32)],
        compiler_params=_cparams(("arbitrary",)),
        name="mlstm",
    )(z3, z3, z3, z3, z3, conv_w, conv_b.reshape(1, -1), bias.reshape(1, -1), bias.reshape(-1, 1),
      gain.reshape(1, v_w))


def _mix_ffn_ple_kernel(*refs, n_mix, tm, seq, tf):
    hp_ref, h_ref = refs[0], refs[1]
    mix_refs = refs[2:2 + 2 * n_mix]
    (wo_ref, g_ref, wa_ref, wb_ref, cw_ref, cb_ref, wd_ref, p_ref, wp_ref, ng_ref, gg_ref, wg_ref,
     o_ref, xn_ref, a_ref) = refs[2 + 2 * n_mix:]
    i = pl.program_id(0)
    halo = 8
    g = g_ref[...]
    m_prev = jnp.concatenate([mix_refs[2 * k][...] for k in range(n_mix)], axis=1)
    m_tile = jnp.concatenate([mix_refs[2 * k + 1][...] for k in range(n_mix)], axis=1)
    x = h_ref[...] + _dot(m_tile, wo_ref[...])
    xp = hp_ref[...] + _dot(m_prev, wo_ref[...])[BF16_ROWS - halo:, :]
    xn_ref[halo:halo + tm, :] = _rms(x, g).astype(BF16)
    keep = ((i * tm) % seq != 0).astype(F32)
    xn_ref[0:halo, :] = (_rms(xp, g) * keep).astype(BF16)
    n_chunks = D_FF // tf

    def up_proj(c):
        cs = slice(c * tf, (c + 1) * tf)
        a_ref[c % 2] = _dot(xn_ref[...], wa_ref[:, cs])
        return _dot(xn_ref[halo:halo + tm, :], wb_ref[:, cs])

    h2 = x
    bgate = up_proj(0)
    for c in range(n_chunks):
        cs = slice(c * tf, (c + 1) * tf)
        bgate_next = up_proj(c + 1) if c + 1 < n_chunks else None
        conv = cb_ref[:, cs]
        for t in range(FFN_CONV):
            conv = conv + a_ref[c % 2, pl.ds(halo - (FFN_CONV - 1) + t, tm), :] * cw_ref[t:t + 1, cs]
        act = (_gelu_tanh(conv) * bgate).astype(BF16)
        h2 = h2 + _dot(act, wd_ref[cs, :])
        bgate = bgate_next
    e = _rms(_dot(p_ref[...].astype(BF16), wp_ref[...]), ng_ref[...])
    gate = jax.nn.sigmoid(_dot(_rms(h2, gg_ref[...]).astype(BF16), wg_ref[...]))
    o_ref[...] = h2 + gate * e


def mix_ffn_ple(h, mix, li, g, w_up, conv_w, conv_b, w_down, p, wp, norm_g, gate_norm_g, wg, seq, tm=512, tf=2816):
    mix_outs, w_out = mix
    n, d = h.shape
    pd = p.shape[-1]
    hb = tm // 8
    const = lambda i: (0, 0)
    layer = lambda i: (li, 0, 0)
    prev8 = lambda i: (jnp.maximum(i * hb - 1, 0), 0)
    resident = dict(pipeline_mode=pl.Buffered(1))
    prev16 = lambda i: (jnp.maximum(i * (tm // BF16_ROWS) - 1, 0), 0)
    mix_specs, mix_args = [], []
    for m in mix_outs:
        mix_specs += [pl.BlockSpec((BF16_ROWS, m.shape[1]), prev16),
                      pl.BlockSpec((tm, m.shape[1]), lambda i: (i, 0))]
        mix_args += [m, m]
    mix_specs.append(pl.BlockSpec(w_out.shape, const, **resident))
    mix_args.append(w_out)
    return pl.pallas_call(
        functools.partial(_mix_ffn_ple_kernel, n_mix=len(mix_outs), tm=tm, seq=seq, tf=tf),
        out_shape=jax.ShapeDtypeStruct((n, d), F32),
        grid=(n // tm,),
        in_specs=[pl.BlockSpec((8, d), prev8),
                  pl.BlockSpec((tm, d), lambda i: (i, 0))] + mix_specs + [
                  pl.BlockSpec((None, 1, d), layer),
                  pl.BlockSpec((None, d, D_FF), lambda i: (li, 0, 0), **resident),
                  pl.BlockSpec((None, d, D_FF), lambda i: (li, 0, 1), **resident),
                  pl.BlockSpec((None, FFN_CONV, D_FF), layer),
                  pl.BlockSpec((None, 1, D_FF), layer),
                  pl.BlockSpec((None, D_FF, d), layer, **resident),
                  pl.BlockSpec((None, tm, pd), lambda i: (li, i, 0)),
                  pl.BlockSpec((None, pd, d), layer, **resident),
                  pl.BlockSpec((None, 1, d), layer),
                  pl.BlockSpec((None, 1, d), layer),
                  pl.BlockSpec((None, d, d), layer, **resident)],
        out_specs=pl.BlockSpec((tm, d), lambda i: (i, 0)),
        scratch_shapes=[pltpu.VMEM((8 + tm, d), BF16),
                        pltpu.VMEM((min(2, D_FF // tf), 8 + tm, tf), F32)],
        compiler_params=_cparams(("parallel",), VMEM_LIMIT_V7X),
        name="mix_ffn_ple",
    )(h, h, *mix_args, g[:, None, :], w_up, w_up, conv_w, conv_b[:, None, :], w_down,
      p, wp, norm_g[:, None, :], gate_norm_g[:, None, :], wg)


def _group_rms(x, g):
    lane = lax.broadcasted_iota(jnp.int32, x.shape, 1)
    x2 = x * x
    ms = jnp.zeros_like(x)
    for grp in range(N_KV_GROUPS):
        in_grp = (lane >= grp * HEAD_DIM) & (lane < (grp + 1) * HEAD_DIM)
        tot = jnp.sum(jnp.where(in_grp, x2, 0.0), axis=-1, keepdims=True)
        ms = jnp.where(in_grp, tot * (1.0 / HEAD_DIM), ms)
    return x * lax.rsqrt(ms + EPS) * g


def _ones_rows(width):
    return (lax.broadcasted_iota(jnp.int32, (BF16_ROWS, width), 0) == 0).astype(BF16)


def _kv_prep_kernel(c_ref, s_ref, w_ref, gs_ref, gw_ref, kc_ref, vc_ref, ks_ref, vs_ref, kw_ref, vw_ref):
    gw = N_KV_GROUPS * HEAD_DIM
    cc = c_ref[...]
    ss = s_ref[...]
    ww = w_ref[...]
    kc_ref[...] = cc[:, 0:gw]
    vc_ref[...] = cc[:, gw:2 * gw]
    ks = _group_rms(ss[:, 0:gw], gs_ref[...]).astype(BF16)
    kw = _group_rms(ww[:, 0:gw], gw_ref[...]).astype(BF16)
    vst = ss[:, gw:2 * gw].T
    vwt = ww[:, gw:2 * gw].T
    for g in range(N_KV_GROUPS):
        lanes = slice(g * HEAD_DIM, (g + 1) * HEAD_DIM)
        ks_ref[g] = ks[:, lanes]
        kw_ref[g] = kw[:, lanes]
        vs_ref[g] = jnp.concatenate([vst[lanes, :].astype(BF16), _ones_rows(SEL_TILE)], axis=0)
        for u in range(SEL_TILE // KEY_TILE):
            vw_ref[g, u] = jnp.concatenate([vwt[lanes, u * KEY_TILE:(u + 1) * KEY_TILE].astype(BF16),
                                            _ones_rows(KEY_TILE)], axis=0)


def kv_prep(z, k_g, b, t):
    n = z.shape[0]
    G = N_KV_GROUPS
    gw = G * HEAD_DIM
    base = N_HEADS * HEAD_DIM // (2 * gw)
    tm = SEL_TILE
    nst = t // tm
    sub = SEL_TILE // KEY_TILE
    row = lambda i, j: i * nst + j
    flat = jax.ShapeDtypeStruct((n, gw), F32)
    keys = jax.ShapeDtypeStruct((b, G, t, HEAD_DIM), BF16)
    flat_spec = pl.BlockSpec((tm, gw), lambda i, j: (row(i, j), 0))
    key_spec = pl.BlockSpec((None, G, tm, HEAD_DIM), lambda i, j: (i, 0, j, 0))
    return pl.pallas_call(
        _kv_prep_kernel,
        out_shape=(flat, flat, keys, jax.ShapeDtypeStruct((b, G, nst, V_ROWS, tm), BF16),
                   keys, jax.ShapeDtypeStruct((b, G, nst * sub, V_ROWS, KEY_TILE), BF16)),
        grid=(b, nst),
        in_specs=[pl.BlockSpec((tm, 2 * gw), lambda i, j: (row(i, j), base)),
                  pl.BlockSpec((tm, 2 * gw), lambda i, j: (row(i, j), base + 1)),
                  pl.BlockSpec((tm, 2 * gw), lambda i, j: (row(i, j), base + 2)),
                  pl.BlockSpec((1, gw), lambda i, j: (0, 0)),
                  pl.BlockSpec((1, gw), lambda i, j: (0, 0))],
        out_specs=(flat_spec, flat_spec, key_spec,
                   pl.BlockSpec((None, G, None, V_ROWS, tm), lambda i, j: (i, 0, j, 0, 0)),
                   key_spec,
                   pl.BlockSpec((None, G, sub, V_ROWS, KEY_TILE), lambda i, j: (i, 0, j, 0, 0))),
        compiler_params=_cparams(("parallel", "parallel")),
        name="kv_prep",
    )(z, z, z, jnp.tile(k_g[1], G).reshape(1, gw), jnp.tile(k_g[2], G).reshape(1, gw))


def _compress_kernel(x_ref, pos_ref, w1_ref, w2_ref, g_ref, o_ref, *, normalize, transposed):
    G = N_KV_GROUPS
    ncb = x_ref.shape[0] // CMP_STRIDE
    half = CMP_STRIDE * HEAD_DIM
    u = [jnp.zeros((ncb, CMP_HIDDEN), F32) for _ in range(G)]
    v = [jnp.zeros((ncb, CMP_HIDDEN), F32) for _ in range(G)]
    for r in range(CMP_STRIDE):
        xr = x_ref[pl.ds(r, ncb, stride=CMP_STRIDE), :].astype(BF16)
        wa = w1_ref[r * HEAD_DIM:(r + 1) * HEAD_DIM, :]
        wb = w1_ref[half + r * HEAD_DIM:half + (r + 1) * HEAD_DIM, :]
        for grp in range(G):
            xg = xr[:, grp * HEAD_DIM:(grp + 1) * HEAD_DIM]
            u[grp] = u[grp] + _dot(xg, wa)
            v[grp] = v[grp] + _dot(xg, wb)
    posc = _dot(pos_ref[...], w1_ref[...])[0:1, :]
    outs = []
    for grp in range(G):
        hid = u[grp] + pltpu.roll(v[grp], ncb - 1, 0) + posc
        out = _dot(jax.nn.gelu(hid).astype(BF16), w2_ref[...])
        outs.append(_rms(out, g_ref[...]) if normalize else out)
    if transposed:
        both = jnp.concatenate(outs, axis=1).T
        for grp in range(G):
            o_ref[grp] = both[grp * HEAD_DIM:(grp + 1) * HEAD_DIM, :].astype(o_ref.dtype)
    else:
        for grp in range(G):
            o_ref[grp] = outs[grp].astype(o_ref.dtype)


def compress(x3, pos, w1, w2, g, normalize, transposed):
    b, t, gw = x3.shape
    G = N_KV_GROUPS
    ncb = t // CMP_STRIDE
    kdim = CMP_STRIDE * HEAD_DIM
    posf = jnp.broadcast_to(pos.reshape(1, -1), (8, 2 * kdim)).astype(BF16)
    out_sds = (jax.ShapeDtypeStruct((b, G, HEAD_DIM, ncb), BF16) if transposed
               else jax.ShapeDtypeStruct((b, G, ncb, HEAD_DIM), BF16))
    out_block = (None, G, HEAD_DIM, ncb) if transposed else (None, G, ncb, HEAD_DIM)
    return pl.pallas_call(
        functools.partial(_compress_kernel, normalize=normalize, transposed=transposed),
        out_shape=out_sds,
        grid=(b,),
        in_specs=[pl.BlockSpec((None, t, gw), lambda i: (i, 0, 0)),
                  pl.BlockSpec((8, 2 * kdim), lambda i: (0, 0)),
                  pl.BlockSpec((2 * kdim, CMP_HIDDEN), lambda i: (0, 0)),
                  pl.BlockSpec((CMP_HIDDEN, HEAD_DIM), lambda i: (0, 0)),
                  pl.BlockSpec((1, HEAD_DIM), lambda i: (0, 0))],
        out_specs=pl.BlockSpec(out_block, lambda i: (i, 0, 0, 0)),
        compiler_params=_cparams(("parallel",)),
        name="compress",
    )(x3, posf, w1.astype(BF16), w2.astype(BF16), g.reshape(1, HEAD_DIM))


def _nsa_kernel(bnd_ref, *refs):
    safe = bnd_ref[0, 0] <= MAX_SAFE_SCORE_BOUND

    @pl.when(safe)
    def _():
        _nsa_body(bnd_ref, *refs, bounded=True)

    @pl.when(jnp.logical_not(safe))
    def _():
        _nsa_body(bnd_ref, *refs, bounded=False)


def _nsa_body(bnd_ref, zq_ref, gt_ref, gb_ref, qg_ref, ovt_ref, kc_ref, vct_ref, ks_ref, vst_ref, kw_ref,
              vwt_ref, o_ref, q_scr, sel_scr, m_scr, acc_scr, oc_scr, *, bounded):
    grp = pl.program_id(1)
    qb = pl.program_id(2)
    QB = Q_BLOCK
    ncp = kc_ref.shape[0]
    ns = ovt_ref.shape[0]
    tpos = qb * QB + lax.broadcasted_iota(jnp.int32, (1, QB), 1)

    def lane_tile(x):
        return jnp.concatenate([x] * HG, axis=1)

    zt = zq_ref[...].T
    qg = qg_ref[...] * (HEAD_DIM ** -0.5 * LOG2E)
    heads = []
    for h in range(HG):
        xh = zt[h * HEAD_DIM:(h + 1) * HEAD_DIM, :]
        ms = jnp.mean(xh * xh, axis=0, keepdims=True)
        heads.append(xh * lax.rsqrt(ms + EPS) * qg)
    q_scr[...] = jnp.concatenate(heads, axis=1).astype(BF16)
    q = q_scr[...]

    gates = jax.nn.sigmoid(gt_ref[...].T + gb_ref[...])
    per_grp = HG * 3
    gsel = gates[0:per_grp, :]
    for g2 in range(1, N_KV_GROUPS):
        gsel = jnp.where(grp == g2, gates[g2 * per_grp:(g2 + 1) * per_grp, :], gsel)

    def gate_row(c):
        return jnp.concatenate([gsel[3 * h + c:3 * h + c + 1, :] for h in range(HG)], axis=1)

    cmp_end = lax.broadcasted_iota(jnp.int32, (ncp, 1), 0) * CMP_STRIDE + (CMP_BLOCK - 1)
    keep = -bnd_ref[0, 0] if bounded else 0.0
    cbias = jnp.where(cmp_end <= tpos, keep, NEG)
    s = _dot(kc_ref[...], q) + lane_tile(cbias)
    e = jnp.exp2(s) if bounded else jnp.exp2(s - jnp.max(s, axis=0, keepdims=True))
    cmp_lhs = jnp.concatenate([vct_ref[...], _ones_rows(ncp), ovt_ref[...]], axis=0)
    r = _dot(cmp_lhs, e.astype(BF16))
    inv = jnp.where(lane_tile(tpos) >= CMP_BLOCK - 1, 1.0 / r[HEAD_DIM:HEAD_DIM + 1, :], 0.0)
    ocmp = r[0:HEAD_DIM, :] * inv
    imp_h = r[V_ROWS:V_ROWS + ns, :] * inv
    imp = imp_h[:, 0:QB]
    for h in range(1, HG):
        imp = imp + imp_h[:, h * QB:(h + 1) * QB]

    n_win = (WINDOW + QB) // KEY_TILE
    first_tile = qb * (QB // KEY_TILE) - WINDOW // KEY_TILE
    win_sub = lax.broadcasted_iota(jnp.int32, (n_win * KEY_TILE, 1), 0)
    tiles = [jnp.maximum(first_tile + u, 0) for u in range(n_win)]
    kwin = jnp.concatenate([kw_ref[j] for j in tiles], axis=0)
    vwin = jnp.concatenate([vwt_ref[j] for j in tiles], axis=1)
    kpos = first_tile * KEY_TILE + win_sub
    wbias = jnp.where((kpos <= tpos) & (kpos > tpos - WINDOW) & (kpos >= 0), keep, NEG)
    sw = _dot(kwin, q) + lane_tile(wbias)
    pw = jnp.exp2(sw) if bounded else jnp.exp2(sw - jnp.max(sw, axis=0, keepdims=True))
    ow = _dot(vwin, pw.astype(BF16))
    oc_scr[...] = gate_row(0) * ocmp + (gate_row(2) / ow[HEAD_DIM:HEAD_DIM + 1, :]) * ow[0:HEAD_DIM, :]

    blk = lax.broadcasted_iota(jnp.int32, (ns, 1), 0)
    blk_f = blk.astype(F32)
    cur = jnp.right_shift(tpos, SLC_BLOCK.bit_length() - 1)
    forced = (blk == 0) | (blk == cur) | (blk == cur - 1)
    bvalid = blk <= cur
    score = jnp.where(forced, -jnp.inf, jnp.where(bvalid, imp, NEG))
    sel = jnp.where(forced, 1.0, 0.0)
    for _ in range(max(min(N_SELECT, ns) - 3, 0)):
        mx = jnp.max(score, axis=0, keepdims=True)
        first = jnp.min(jnp.where(score == mx, blk_f, float(ns)), axis=0, keepdims=True)
        pick = blk_f == first
        sel = jnp.where(pick, 1.0, sel)
        score = jnp.where(pick, -jnp.inf, score)
    sel_scr[...] = jnp.where(bvalid, sel, 0.0)

    m_scr[...] = jnp.full(m_scr.shape, NEG, F32)
    acc_scr[...] = jnp.zeros(acc_scr.shape, F32)
    blocks_per_tile = SEL_TILE // SLC_BLOCK
    sel_sub = lax.broadcasted_iota(jnp.int32, (SEL_TILE, 1), 0)

    def sel_body(j, carry):
        rows = [jnp.broadcast_to(sel_scr[pl.ds(j * blocks_per_tile + bi, 1), :], (SLC_BLOCK, QB))
                for bi in range(blocks_per_tile)]
        chosen = jnp.concatenate(rows, axis=0)
        kpos = j * SEL_TILE + sel_sub
        bias = jnp.where((chosen > 0.5) & (kpos <= tpos), keep, NEG)
        st = _dot(ks_ref[j], q_scr[...]) + lane_tile(bias)
        if bounded:
            pt = jnp.exp2(st)
            acc_scr[...] += _dot(vst_ref[j], pt.astype(BF16))
        else:
            m_old = m_scr[...]
            m_new = jnp.maximum(m_old, jnp.max(st, axis=0, keepdims=True))
            alpha = jnp.exp2(m_old - m_new)
            pt = jnp.exp2(st - m_new)
            acc_scr[...] = alpha * acc_scr[...] + _dot(vst_ref[j], pt.astype(BF16))
            m_scr[...] = m_new
        return carry

    lax.fori_loop(0, qb // (SEL_TILE // QB) + 1, sel_body, 0)
    ot = (oc_scr[...]
          + (gate_row(1) / acc_scr[HEAD_DIM:HEAD_DIM + 1, :]) * acc_scr[0:HEAD_DIM, :])
    stacked = jnp.concatenate([ot[:, h * QB:(h + 1) * QB] for h in range(HG)], axis=0)
    o_ref[...] = stacked.T.astype(o_ref.dtype)


def nsa_attention(bound, z3, gate_b, q_g, overlap_t, kc, vct, ks, vst, kw, vwt):
    b, t, _ = z3.shape
    nq = t // Q_BLOCK
    nt = t // KEY_TILE
    qw = HG * HEAD_DIM
    ns, ncp = overlap_t.shape
    gate_blk = (NSA_COLS_PAD // LANES_V7X) - 1
    gb = jnp.zeros((LANES_V7X, 1), F32).at[:N_HEADS * 3, 0].set(gate_b)
    full5 = lambda i, g, q: (i, g, 0, 0, 0)
    lanes = HG * Q_BLOCK
    return pl.pallas_call(
        _nsa_kernel,
        out_shape=jax.ShapeDtypeStruct((b, t, N_HEADS * HEAD_DIM), BF16),
        grid=(b, N_KV_GROUPS, nq),
        in_specs=[pl.BlockSpec(memory_space=pltpu.SMEM),
                  pl.BlockSpec((None, Q_BLOCK, qw), lambda i, g, q: (i, q, g)),
                  pl.BlockSpec((None, Q_BLOCK, LANES_V7X), lambda i, g, q: (i, q, gate_blk)),
                  pl.BlockSpec((LANES_V7X, 1), lambda i, g, q: (0, 0)),
                  pl.BlockSpec((HEAD_DIM, 1), lambda i, g, q: (0, 0)),
                  pl.BlockSpec((ns, ncp), lambda i, g, q: (0, 0)),
                  pl.BlockSpec((None, None, ncp, HEAD_DIM), lambda i, g, q: (i, g, 0, 0)),
                  pl.BlockSpec((None, None, HEAD_DIM, ncp), lambda i, g, q: (i, g, 0, 0)),
                  pl.BlockSpec((None, None, t // SEL_TILE, SEL_TILE, HEAD_DIM), full5),
                  pl.BlockSpec((None, None, t // SEL_TILE, V_ROWS, SEL_TILE), full5),
                  pl.BlockSpec((None, None, nt, KEY_TILE, HEAD_DIM), full5),
                  pl.BlockSpec((None, None, nt, V_ROWS, KEY_TILE), full5)],
        out_specs=pl.BlockSpec((None, Q_BLOCK, qw), lambda i, g, q: (i, q, g)),
        scratch_shapes=[pltpu.VMEM((HEAD_DIM, lanes), BF16),
                        pltpu.VMEM((ns, Q_BLOCK), F32),
                        pltpu.VMEM((1, lanes), F32),
                        pltpu.VMEM((V_ROWS, lanes), F32),
                        pltpu.VMEM((HEAD_DIM, lanes), F32)],
        compiler_params=_cparams(("parallel", "parallel", "arbitrary"), VMEM_LIMIT_V7X),
        name="nsa_attention",
    )(bound.reshape(1, 1), z3, z3, gb, q_g.reshape(HEAD_DIM, 1), overlap_t, kc, vct, ks, vst, kw, vwt)


def _overlap_matrix_t(t):
    ncp = t // CMP_STRIDE
    ns = t // SLC_BLOCK
    c_start = np.arange(ncp) * CMP_STRIDE
    sj = np.arange(ns)
    ov = ((c_start[None, :] < (sj[:, None] + 1) * SLC_BLOCK)
          & (c_start[None, :] + CMP_BLOCK > sj[:, None] * SLC_BLOCK)
          & (c_start[None, :] + CMP_BLOCK <= t))
    return jnp.asarray(ov, dtype=BF16)


def ab_layer(h, b, t, norm_g, w_in, conv_w, conv_b, ret_g, ig_b, fg_b, m_g, w_out):
    n = b * t
    w_in_p = jnp.pad(w_in, ((0, 0), (0, AB_COLS_PAD - AB_COLS))).astype(BF16)
    z = norm_matmul(h, norm_g, w_in_p)
    z3 = z.reshape(b, t, AB_COLS_PAD)
    cos_t, sin_t = rope_tables(t)
    ret = retention(z3, cos_t, sin_t, ret_g)
    ml = mlstm(z3, conv_w, conv_b, ig_b, fg_b, m_g)
    w_out_b = w_out.astype(BF16)
    rw = R_HEADS * R_DV
    return [ret.reshape(n, rw), ml.reshape(n, -1)], w_out_b


def nsa_layer(h, b, t, norm_g, w_in, q_g, k_g, pos_k, pos_v, w1k, w2k, w1v, w2v, gate_b, w_out):
    n = b * t
    G = N_KV_GROUPS
    w_in_p = jnp.pad(w_in, ((0, 0), (0, NSA_COLS_PAD - NSA_COLS))).astype(BF16)
    z = norm_matmul(h, norm_g, w_in_p)
    kc_in, vc_in, ks, vst, kw, vwt = kv_prep(z, k_g, b, t)
    gw = G * HEAD_DIM
    kc = compress(kc_in.reshape(b, t, gw), pos_k, w1k, w2k, k_g[0], True, False)
    vct = compress(vc_in.reshape(b, t, gw), pos_v, w1v, w2v, k_g[0], False, True)

    def key_tiles(x, kt):
        return x.reshape(b, G, t // kt, kt, HEAD_DIM)

    bound = 1.02 * LOG2E * math.sqrt(HEAD_DIM) * jnp.max(jnp.abs(q_g)) * jnp.max(jnp.abs(k_g))
    args = (bound, z.reshape(b, t, NSA_COLS_PAD), gate_b, q_g, _overlap_matrix_t(t), kc, vct,
            key_tiles(ks, SEL_TILE), vst, key_tiles(kw, KEY_TILE), vwt)
    o = nsa_attention(*args)
    return [o.reshape(n, -1)], w_out.astype(BF16)


def kernel(x, p, ab_norm_g, ab_w_in, ab_conv_w, ab_conv_b, ab_ret_norm_g, ab_ig_b, ab_fg_b, ab_m_norm_g, ab_w_out, nsa_norm_g, nsa_w_in, nsa_q_norm_g, nsa_k_norm_g, nsa_cmp_pos_k, nsa_cmp_pos_v, nsa_cmp_w1k, nsa_cmp_w2k, nsa_cmp_w1v, nsa_cmp_w2v, nsa_gate_b, nsa_w_out, ffn_norm_g, ffn_w_up, ffn_conv_w, ffn_conv_b, ffn_w_down, ple_w, ple_norm_g, ple_gate_norm_g, ple_w_gate):
    b, t, d = x.shape
    n = b * t
    depth = p.shape[0]
    h = x.reshape(n, d)
    p2 = p.reshape(depth, n, -1)
    ffn_w_up_b, ffn_w_down_b = ffn_w_up.astype(BF16), ffn_w_down.astype(BF16)
    ple_w_b, ple_w_gate_b = ple_w.astype(BF16), ple_w_gate.astype(BF16)
    for i in range(depth):
        j = i // 2
        if i % 2 == 0:
            mix = ab_layer(h, b, t, ab_norm_g[j], ab_w_in[j], ab_conv_w[j], ab_conv_b[j], ab_ret_norm_g[j],
                           ab_ig_b[j], ab_fg_b[j], ab_m_norm_g[j], ab_w_out[j])
        else:
            mix = nsa_layer(h, b, t, nsa_norm_g[j], nsa_w_in[j], nsa_q_norm_g[j], nsa_k_norm_g[j],
                            nsa_cmp_pos_k[j], nsa_cmp_pos_v[j], nsa_cmp_w1k[j], nsa_cmp_w2k[j],
                            nsa_cmp_w1v[j], nsa_cmp_w2v[j], nsa_gate_b[j], nsa_w_out[j])
        h = mix_ffn_ple(h, mix, i, ffn_norm_g, ffn_w_up_b, ffn_conv_w, ffn_conv_b, ffn_w_down_b, p2, ple_w_b,
                        ple_norm_g, ple_gate_norm_g, ple_w_gate_b, t)
    return h.reshape(b, t, d)
```

```python
import functools
import math

import numpy as np
import jax
import jax.numpy as jnp
from jax import lax
from jax.experimental import pallas as pl
from jax.experimental.pallas import tpu as pltpu

F32 = jnp.float32
BF16 = jnp.bfloat16

LANES_V7X = 128
BF16_ROWS = 16
VMEM_LIMIT_V7X = 56 * 1024 * 1024

R_HEADS, R_DK, R_DV, R_CHUNK = 4, 64, 128, 128
ROPE_BASE = 10000.0
M_HEADS, M_DK, M_DV, M_CHUNK, M_CONV = 4, 64, 128, 64, 4
AB_SIZES = (R_HEADS * R_DK, R_HEADS * R_DK, R_HEADS * R_DV, R_HEADS * R_DV,
            M_HEADS * M_DK, M_HEADS * M_DK, M_HEADS * M_DV, M_HEADS * M_DV, M_HEADS, M_HEADS)
AB_COLS = sum(AB_SIZES)
AB_COLS_PAD = 3200
N_HEADS, N_KV_GROUPS, HEAD_DIM = 16, 2, 64
HG = N_HEADS // N_KV_GROUPS
CMP_BLOCK, CMP_STRIDE, CMP_HIDDEN = 32, 16, 256
SLC_BLOCK, N_SELECT, WINDOW = 64, 16, 512
Q_BLOCK = 256
NSA_COLS = N_HEADS * HEAD_DIM + 6 * N_KV_GROUPS * HEAD_DIM + N_HEADS * 3
NSA_COLS_PAD = 1920
D_FF = 2816
FFN_CONV = 3
NEG = -1e30
EPS = 1e-6
KEY_TILE = 128
SEL_TILE = 512
V_ROWS = HEAD_DIM + BF16_ROWS
LOG2E = math.log2(math.e)
MAX_SAFE_SCORE_BOUND = 56.0


def _cparams(sem, vmem=None):
    return pltpu.CompilerParams(dimension_semantics=sem, vmem_limit_bytes=vmem)


def _rms(x, g):
    ms = jnp.mean(x * x, axis=-1, keepdims=True)
    return x * lax.rsqrt(ms + EPS) * g


def _dot(a, b):
    return jnp.dot(a, b, preferred_element_type=F32)


def _dot_nt(a, b):
    return lax.dot_general(a, b, (((1,), (1,)), ((), ())), preferred_element_type=F32)


def _dot_f32(a, b):
    return jnp.dot(a, b, preferred_element_type=F32, precision=lax.Precision.HIGHEST)


def _gelu_tanh(x):
    k1 = -2.0 * math.sqrt(2.0 / math.pi) * LOG2E
    return x / (1.0 + jnp.exp2(x * (k1 + (k1 * 0.044715) * (x * x))))


def _norm_matmul_kernel(x_ref, g_ref, w_ref, o_ref):
    xn = _rms(x_ref[...], g_ref[...]).astype(BF16)
    o_ref[...] = _dot(xn, w_ref[...]).astype(o_ref.dtype)


def norm_matmul(x, g, w, tm=1024, out_dtype=F32):
    n, d = x.shape
    nc = w.shape[1]
    return pl.pallas_call(
        _norm_matmul_kernel,
        out_shape=jax.ShapeDtypeStruct((n, nc), out_dtype),
        grid=(n // tm,),
        in_specs=[pl.BlockSpec((tm, d), lambda i: (i, 0)),
                  pl.BlockSpec((1, d), lambda i: (0, 0)),
                  pl.BlockSpec((d, nc), lambda i: (0, 0))],
        out_specs=pl.BlockSpec((tm, nc), lambda i: (i, 0)),
        compiler_params=_cparams(("parallel",), VMEM_LIMIT_V7X),
        name="norm_matmul",
    )(x, g.reshape(1, d), w)


def _rope_table_kernel(inv_ref, cos_ref, sin_ref):
    c = pl.program_id(0)
    rows, width = cos_ref.shape
    pos = (c * rows + lax.broadcasted_iota(jnp.int32, (rows, LANES_V7X), 0)).astype(F32)
    lane = lax.broadcasted_iota(jnp.int32, (rows, LANES_V7X), 1)
    ang = pos * inv_ref[:, 0:LANES_V7X]
    cs = jnp.cos(ang)
    sn = jnp.sin(ang)
    sn = jnp.where(lane % R_DK < R_DK // 2, -sn, sn)
    reps = width // LANES_V7X
    cos_ref[...] = jnp.concatenate([cs] * reps, axis=1)
    sin_ref[...] = jnp.concatenate([sn] * reps, axis=1)


def rope_tables(t):
    half = R_DK // 2
    inv = ROPE_BASE ** (-jnp.arange(half, dtype=F32) / half)
    inv = jnp.tile(inv, 2 * R_HEADS).reshape(1, R_HEADS * R_DK)
    width = R_HEADS * R_DK
    shp = jax.ShapeDtypeStruct((t, width), F32)
    return pl.pallas_call(
        _rope_table_kernel,
        out_shape=(shp, shp),
        grid=(t // R_CHUNK,),
        in_specs=[pl.BlockSpec((1, width), lambda c: (0, 0))],
        out_specs=(pl.BlockSpec((R_CHUNK, width), lambda c: (c, 0)),
                   pl.BlockSpec((R_CHUNK, width), lambda c: (c, 0))),
        compiler_params=_cparams(("parallel",)),
        name="rope_tables",
    )(inv)


def _retention_kernel(cos_ref, sin_ref, q_ref, k_ref, v_ref, g_ref, gain_ref, o_ref, r_ref):
    c = pl.program_id(0)
    nb = q_ref.shape[0]
    L = R_CHUNK

    @pl.when(c == 0)
    def _():
        r_ref[...] = jnp.zeros_like(r_ref)

    cos = cos_ref[...]
    sin = sin_ref[...]
    lane = lax.broadcasted_iota(jnp.int32, cos.shape, 1)
    first_half = lane % R_DK < R_DK // 2
    width = R_HEADS * R_DK

    def rot(x):
        swapped = jnp.where(first_half, pltpu.roll(x, width - R_DK // 2, 1), pltpu.roll(x, R_DK // 2, 1))
        return x * cos + swapped * sin

    gain = gain_ref[...]
    ri = lax.broadcasted_iota(jnp.int32, (L, L), 0)
    ci = lax.broadcasted_iota(jnp.int32, (L, L), 1)
    diff = (ri - ci).astype(F32)
    causal = ri >= ci
    idx = lax.broadcasted_iota(jnp.int32, (L, 1), 0).astype(F32)
    decays = []
    for h in range(R_HEADS):
        log_g = math.log1p(-2.0 ** (-5.0 - h))
        decays.append(dict(
            dmask=jnp.where(causal, jnp.exp(jnp.where(causal, diff, 0.0) * log_g), 0.0),
            q_dec=jnp.exp((idx + 1.0) * log_g), k_dec=jnp.exp((L - 1.0 - idx) * log_g),
            c_dec=math.exp(L * log_g)))

    chains = []
    for bi in range(nb):
        q = rot(q_ref[bi])
        k = rot(k_ref[bi]) * (R_DK ** -0.5)
        for h in range(R_HEADS):
            chains.append(dict(bi=bi, h=h, u=bi * R_HEADS + h,
                               qb=q[:, h * R_DK:(h + 1) * R_DK].astype(BF16),
                               kh=k[:, h * R_DK:(h + 1) * R_DK]))

    for ch in chains:
        ch['rstate'] = r_ref[ch['u']]
        ch['s'] = _dot_nt(ch['qb'], ch['kh'].astype(BF16)) * decays[ch['h']]['dmask']
        ch['qr'] = _dot(ch['qb'], ch['rstate'].astype(BF16))

    for ch in chains:
        d = decays[ch['h']]
        vh = v_ref[ch['bi'], :, ch['h'] * R_DV:(ch['h'] + 1) * R_DV].astype(BF16)
        ch['o'] = _dot(ch['s'].astype(BF16), vh) + ch['qr'] * d['q_dec']
        kd = (ch['kh'] * d['k_dec']).T.astype(BF16)
        r_ref[ch['u']] = d['c_dec'] * ch['rstate'] + _dot(kd, vh)

    for ch in chains:
        sl = slice(ch['h'] * R_DV, (ch['h'] + 1) * R_DV)
        gh = g_ref[ch['bi'], :, sl]
        o_ref[ch['bi'], :, sl] = (_rms(ch['o'], gain[:, sl]) * (gh * jax.nn.sigmoid(gh))).astype(o_ref.dtype)


def retention(z3, cos_t, sin_t, gain):
    b, t, _ = z3.shape
    nc = t // R_CHUNK
    qk_w = R_HEADS * R_DK
    v_w = R_HEADS * R_DV
    return pl.pallas_call(
        _retention_kernel,
        out_shape=jax.ShapeDtypeStruct((b, t, v_w), BF16),
        grid=(nc,),
        in_specs=[pl.BlockSpec((R_CHUNK, qk_w), lambda c: (c, 0)),
                  pl.BlockSpec((R_CHUNK, qk_w), lambda c: (c, 0)),
                  pl.BlockSpec((b, R_CHUNK, qk_w), lambda c: (0, c, 0)),
                  pl.BlockSpec((b, R_CHUNK, qk_w), lambda c: (0, c, 1)),
                  pl.BlockSpec((b, R_CHUNK, v_w), lambda c: (0, c, 1)),
                  pl.BlockSpec((b, R_CHUNK, v_w), lambda c: (0, c, 2)),
                  pl.BlockSpec((1, v_w), lambda c: (0, 0))],
        out_specs=pl.BlockSpec((b, R_CHUNK, v_w), lambda c: (0, c, 0)),
        scratch_shapes=[pltpu.VMEM((b * R_HEADS, R_DK, R_DV), F32)],
        compiler_params=_cparams(("arbitrary",)),
        name="retention",
    )(cos_t, sin_t, z3, z3, z3, z3, gain.reshape(1, v_w))


def _mlstm_kernel(q_ref, k_ref, v_ref, og_ref, gc_ref, cw_ref, cb_ref, bc_ref, br_ref, gain_ref,
                  o_ref, xbuf, c_ref, n_ref, m_ref):
    c = pl.program_id(0)
    nb = q_ref.shape[0]
    L = M_CHUNK
    H = M_HEADS
    qk_w = H * M_DK
    halo = 8

    @pl.when(c == 0)
    def _():
        xbuf[:, 0:halo, :] = jnp.zeros((nb, halo, 2 * qk_w), F32)
        c_ref[...] = jnp.zeros_like(c_ref)
        n_ref[...] = jnp.zeros_like(n_ref)
        m_ref[...] = jnp.zeros_like(m_ref)

    gain = gain_ref[...]
    ri = lax.broadcasted_iota(jnp.int32, (L, L), 0)
    ci = lax.broadcasted_iota(jnp.int32, (L, L), 1)
    causal = ri >= ci
    tril = causal.astype(F32)
    triu = (ri <= ci).astype(F32)

    chains = []
    for bi in range(nb):
        xbuf[bi, halo:halo + L, 0:qk_w] = q_ref[bi]
        xbuf[bi, halo:halo + L, qk_w:2 * qk_w] = k_ref[bi]
        conv = cb_ref[...]
        for j in range(M_CONV):
            conv = conv + xbuf[bi, pl.ds(halo - (M_CONV - 1) + j, L), :] * cw_ref[j:j + 1, :]
        tail = xbuf[bi, L:L + halo, :]
        xbuf[bi, 0:halo, :] = tail
        act = conv * jax.nn.sigmoid(conv)
        q = act[:, 0:qk_w] * (M_DK ** -0.5)
        k = act[:, qk_w:2 * qk_w]
        gblock = gc_ref[bi]
        gc = gblock[:, 0:2 * H] + bc_ref[...]
        gr = gblock.T[0:2 * H, :] + br_ref[...]
        ig_c = gc[:, 0:H]
        ig_r = gr[0:H, :]
        b_c = _dot_f32(tril, jax.nn.log_sigmoid(gc[:, H:2 * H]))
        b_r = _dot_f32(jax.nn.log_sigmoid(gr[H:2 * H, :]), triu)
        for h in range(H):
            chains.append(dict(
                bi=bi, h=h, u=bi * H + h,
                qh=q[:, h * M_DK:(h + 1) * M_DK], kh=k[:, h * M_DK:(h + 1) * M_DK],
                bh=b_c[:, h:h + 1], brow=b_r[h:h + 1, :], irow=ig_r[h:h + 1, :], icol=ig_c[:, h:h + 1]))

    def stack(parts):
        return jnp.concatenate(parts, axis=0)

    def rows(x):
        return jnp.broadcast_to(x, (L, x.shape[1]))

    m_prev_u = [m_ref[ch['u']][:, 0:1] for ch in chains]
    b_last_u = [ch['bh'][L - 1:L, :] for ch in chains]
    bh = stack([ch['bh'] for ch in chains])
    icol = stack([ch['icol'] for ch in chains])
    brow = stack([rows(ch['brow']) for ch in chains])
    irow = stack([rows(ch['irow']) for ch in chains])
    m_prev = stack([rows(m) for m in m_prev_u])
    b_last = stack([rows(x) for x in b_last_u])
    causal_all = stack([causal] * len(chains))
    qs = stack([ch['qh'] for ch in chains])
    ks = stack([ch['kh'] for ch in chains])
    qb = qs.astype(BF16)
    kb = ks.astype(BF16)
    cstates = [c_ref[ch['u']] for ch in chains]
    nstates = [n_ref[ch['u']] for ch in chains]

    def chain_rows(x, i):
        return x[i * L:(i + 1) * L]

    s_raw = stack([_dot_nt(chain_rows(qb, i), chain_rows(kb, i)) for i in range(len(chains))])
    qc = stack([_dot(chain_rows(qb, i), cstates[i].astype(BF16)) for i in range(len(chains))])

    glog = jnp.where(causal_all, irow - brow, NEG)
    c = jnp.maximum(m_prev, jnp.max(glog, axis=-1, keepdims=True))
    m_t = bh + c
    s = s_raw * jnp.exp(glog - c)
    w_inter = jnp.exp(m_prev - c)
    wlog = b_last - bh + icol
    m_new_u = [jnp.maximum(b_last_u[i] + m_prev_u[i], jnp.max(chain_rows(wlog, i), axis=0, keepdims=True))
               for i in range(len(chains))]
    m_new = stack([rows(m) for m in m_new_u])
    wk = ks * jnp.exp(wlog - m_new)

    sb = s.astype(BF16)
    vhs = [v_ref[ch['bi'], :, ch['h'] * M_DV:(ch['h'] + 1) * M_DV].astype(BF16) for ch in chains]
    sv = stack([_dot(chain_rows(sb, i), vhs[i]) for i in range(len(chains))])
    kv = [_dot(chain_rows(wk, i).T.astype(BF16), vhs[i]) for i in range(len(chains))]

    qn = jnp.sum(qs * stack([rows(n) for n in nstates]), axis=-1, keepdims=True)
    den = jnp.sum(s, axis=-1, keepdims=True) + w_inter * qn
    hh = (sv + w_inter * qc) / jnp.maximum(jnp.abs(den), jnp.exp(-m_t))
    og = stack([og_ref[ch['bi'], :, ch['h'] * M_DV:(ch['h'] + 1) * M_DV] for ch in chains])
    gains = stack([rows(gain[:, ch['h'] * M_DV:(ch['h'] + 1) * M_DV]) for ch in chains])
    out = jax.nn.sigmoid(og) * _rms(hh, gains)

    for i, ch in enumerate(chains):
        u = ch['u']
        decay = jnp.exp(b_last_u[i] + m_prev_u[i] - m_new_u[i])
        c_ref[u] = decay * cstates[i] + kv[i]
        n_ref[u] = decay * nstates[i] + jnp.sum(chain_rows(wk, i), axis=0, keepdims=True)
        m_ref[u] = jnp.broadcast_to(m_new_u[i], (1, LANES_V7X))
        o_ref[ch['bi'], :, ch['h'] * M_DV:(ch['h'] + 1) * M_DV] = chain_rows(out, i).astype(o_ref.dtype)


def mlstm(z3, conv_w, conv_b, ig_b, fg_b, gain):
    b, t, _ = z3.shape
    nc = t // M_CHUNK
    H = M_HEADS
    qk_w = H * M_DK
    v_w = H * M_DV
    bias = jnp.concatenate([ig_b, fg_b])
    gate_blk = AB_COLS_PAD // LANES_V7X - 1
    return pl.pallas_call(
        _mlstm_kernel,
        out_shape=jax.ShapeDtypeStruct((b, t, v_w), BF16),
        grid=(nc,),
        in_specs=[pl.BlockSpec((b, M_CHUNK, qk_w), lambda c: (0, c, 6)),
                  pl.BlockSpec((b, M_CHUNK, qk_w), lambda c: (0, c, 7)),
                  pl.BlockSpec((b, M_CHUNK, v_w), lambda c: (0, c, 4)),
                  pl.BlockSpec((b, M_CHUNK, v_w), lambda c: (0, c, 5)),
                  pl.BlockSpec((b, M_CHUNK, LANES_V7X), lambda c: (0, c, gate_blk)),
                  pl.BlockSpec((M_CONV, 2 * qk_w), lambda c: (0, 0)),
                  pl.BlockSpec((1, 2 * qk_w), lambda c: (0, 0)),
                  pl.BlockSpec((1, 2 * H), lambda c: (0, 0)),
                  pl.BlockSpec((2 * H, 1), lambda c: (0, 0)),
                  pl.BlockSpec((1, v_w), lambda c: (0, 0))],
        out_specs=pl.BlockSpec((b, M_CHUNK, v_w), lambda c: (0, c, 0)),
        scratch_shapes=[pltpu.VMEM((b, 8 + M_CHUNK, 2 * qk_w), F32),
                        pltpu.VMEM((b * H, M_DK, M_DV), F32),
                        pltpu.VMEM((b * H, 1, M_DK), F32),
                        pltpu.VMEM((b * H, 1, LANES_V7X), F32)],
        compiler_params=_cparams(("arbitrary",)),
        name="mlstm",
    )(z3, z3, z3, z3, z3, conv_w, conv_b.reshape(1, -1), bias.reshape(1, -1), bias.reshape(-1, 1),
      gain.reshape(1, v_w))


def _mix_ffn_ple_kernel(*refs, n_mix, tm, seq, tf):
    hp_ref, h_ref = refs[0], refs[1]
    mix_refs = refs[2:2 + 2 * n_mix]
    (wo_ref, g_ref, wa_ref, wb_ref, cw_ref, cb_ref, wd_ref, p_ref, wp_ref, ng_ref, gg_ref, wg_ref,
     o_ref, xn_ref, a_ref) = refs[2 + 2 * n_mix:]
    i = pl.program_id(0)
    halo = 8
    g = g_ref[...]
    m_prev = jnp.concatenate([mix_refs[2 * k][...] for k in range(n_mix)], axis=1)
    m_tile = jnp.concatenate([mix_refs[2 * k + 1][...] for k in range(n_mix)], axis=1)
    x = h_ref[...] + _dot(m_tile, wo_ref[...])
    xp = hp_ref[...] + _dot(m_prev, wo_ref[...])[BF16_ROWS - halo:, :]
    xn_ref[halo:halo + tm, :] = _rms(x, g).astype(BF16)
    keep = ((i * tm) % seq != 0).astype(F32)
    xn_ref[0:halo, :] = (_rms(xp, g) * keep).astype(BF16)
    n_chunks = D_FF // tf

    def up_proj(c):
        cs = slice(c * tf, (c + 1) * tf)
        a_ref[c % 2] = _dot(xn_ref[...], wa_ref[:, cs])
        return _dot(xn_ref[halo:halo + tm, :], wb_ref[:, cs])

    h2 = x
    bgate = up_proj(0)
    for c in range(n_chunks):
        cs = slice(c * tf, (c + 1) * tf)
        bgate_next = up_proj(c + 1) if c + 1 < n_chunks else None
        conv = cb_ref[:, cs]
        for t in range(FFN_CONV):
            conv = conv + a_ref[c % 2, pl.ds(halo - (FFN_CONV - 1) + t, tm), :] * cw_ref[t:t + 1, cs]
        act = (_gelu_tanh(conv) * bgate).astype(BF16)
        h2 = h2 + _dot(act, wd_ref[cs, :])
        bgate = bgate_next
    e = _rms(_dot(p_ref[...].astype(BF16), wp_ref[...]), ng_ref[...])
    gate = jax.nn.sigmoid(_dot(_rms(h2, gg_ref[...]).astype(BF16), wg_ref[...]))
    o_ref[...] = h2 + gate * e


def mix_ffn_ple(h, mix, li, g, w_up, conv_w, conv_b, w_down, p, wp, norm_g, gate_norm_g, wg, seq, tm=512, tf=2816):
    mix_outs, w_out = mix
    n, d = h.shape
    pd = p.shape[-1]
    hb = tm // 8
    const = lambda i: (0, 0)
    layer = lambda i: (li, 0, 0)
    prev8 = lambda i: (jnp.maximum(i * hb - 1, 0), 0)
    resident = dict(pipeline_mode=pl.Buffered(1))
    prev16 = lambda i: (jnp.maximum(i * (tm // BF16_ROWS) - 1, 0), 0)
    mix_specs, mix_args = [], []
    for m in mix_outs:
        mix_specs += [pl.BlockSpec((BF16_ROWS, m.shape[1]), prev16),
                      pl.BlockSpec((tm, m.shape[1]), lambda i: (i, 0))]
        mix_args += [m, m]
    mix_specs.append(pl.BlockSpec(w_out.shape, const, **resident))
    mix_args.append(w_out)
    return pl.pallas_call(
        functools.partial(_mix_ffn_ple_kernel, n_mix=len(mix_outs), tm=tm, seq=seq, tf=tf),
        out_shape=jax.ShapeDtypeStruct((n, d), F32),
        grid=(n // tm,),
        in_specs=[pl.BlockSpec((8, d), prev8),
                  pl.BlockSpec((tm, d), lambda i: (i, 0))] + mix_specs + [
                  pl.BlockSpec((None, 1, d), layer),
                  pl.BlockSpec((None, d, D_FF), lambda i: (li, 0, 0), **resident),
                  pl.BlockSpec((None, d, D_FF), lambda i: (li, 0, 1), **resident),
                  pl.BlockSpec((None, FFN_CONV, D_FF), layer),
                  pl.BlockSpec((None, 1, D_FF), layer),
                  pl.BlockSpec((None, D_FF, d), layer, **resident),
                  pl.BlockSpec((None, tm, pd), lambda i: (li, i, 0)),
                  pl.BlockSpec((None, pd, d), layer, **resident),
                  pl.BlockSpec((None, 1, d), layer),
                  pl.BlockSpec((None, 1, d), layer),
                  pl.BlockSpec((None, d, d), layer, **resident)],
        out_specs=pl.BlockSpec((tm, d), lambda i: (i, 0)),
        scratch_shapes=[pltpu.VMEM((8 + tm, d), BF16),
                        pltpu.VMEM((min(2, D_FF // tf), 8 + tm, tf), F32)],
        compiler_params=_cparams(("parallel",), VMEM_LIMIT_V7X),
        name="mix_ffn_ple",
    )(h, h, *mix_args, g[:, None, :], w_up, w_up, conv_w, conv_b[:, None, :], w_down,
      p, wp, norm_g[:, None, :], gate_norm_g[:, None, :], wg)


def _group_rms(x, g):
    lane = lax.broadcasted_iota(jnp.int32, x.shape, 1)
    x2 = x * x
    ms = jnp.zeros_like(x)
    for grp in range(N_KV_GROUPS):
        in_grp = (lane >= grp * HEAD_DIM) & (lane < (grp + 1) * HEAD_DIM)
        tot = jnp.sum(jnp.where(in_grp, x2, 0.0), axis=-1, keepdims=True)
        ms = jnp.where(in_grp, tot * (1.0 / HEAD_DIM), ms)
    return x * lax.rsqrt(ms + EPS) * g


def _ones_rows(width):
    return (lax.broadcasted_iota(jnp.int32, (BF16_ROWS, width), 0) == 0).astype(BF16)


def _kv_prep_kernel(c_ref, s_ref, w_ref, gs_ref, gw_ref, kc_ref, vc_ref, ks_ref, vs_ref, kw_ref, vw_ref):
    gw = N_KV_GROUPS * HEAD_DIM
    cc = c_ref[...]
    ss = s_ref[...]
    ww = w_ref[...]
    kc_ref[...] = cc[:, 0:gw]
    vc_ref[...] = cc[:, gw:2 * gw]
    ks = _group_rms(ss[:, 0:gw], gs_ref[...]).astype(BF16)
    kw = _group_rms(ww[:, 0:gw], gw_ref[...]).astype(BF16)
    vst = ss[:, gw:2 * gw].T
    vwt = ww[:, gw:2 * gw].T
    for g in range(N_KV_GROUPS):
        lanes = slice(g * HEAD_DIM, (g + 1) * HEAD_DIM)
        ks_ref[g] = ks[:, lanes]
        kw_ref[g] = kw[:, lanes]
        vs_ref[g] = jnp.concatenate([vst[lanes, :].astype(BF16), _ones_rows(SEL_TILE)], axis=0)
        for u in range(SEL_TILE // KEY_TILE):
            vw_ref[g, u] = jnp.concatenate([vwt[lanes, u * KEY_TILE:(u + 1) * KEY_TILE].astype(BF16),
                                            _ones_rows(KEY_TILE)], axis=0)


def kv_prep(z, k_g, b, t):
    n = z.shape[0]
    G = N_KV_GROUPS
    gw = G * HEAD_DIM
    base = N_HEADS * HEAD_DIM // (2 * gw)
    tm = SEL_TILE
    nst = t // tm
    sub = SEL_TILE // KEY_TILE
    row = lambda i, j: i * nst + j
    flat = jax.ShapeDtypeStruct((n, gw), F32)
    keys = jax.ShapeDtypeStruct((b, G, t, HEAD_DIM), BF16)
    flat_spec = pl.BlockSpec((tm, gw), lambda i, j: (row(i, j), 0))
    key_spec = pl.BlockSpec((None, G, tm, HEAD_DIM), lambda i, j: (i, 0, j, 0))
    return pl.pallas_call(
        _kv_prep_kernel,
        out_shape=(flat, flat, keys, jax.ShapeDtypeStruct((b, G, nst, V_ROWS, tm), BF16),
                   keys, jax.ShapeDtypeStruct((b, G, nst * sub, V_ROWS, KEY_TILE), BF16)),
        grid=(b, nst),
        in_specs=[pl.BlockSpec((tm, 2 * gw), lambda i, j: (row(i, j), base)),
                  pl.BlockSpec((tm, 2 * gw), lambda i, j: (row(i, j), base + 1)),
                  pl.BlockSpec((tm, 2 * gw), lambda i, j: (row(i, j), base + 2)),
                  pl.BlockSpec((1, gw), lambda i, j: (0, 0)),
                  pl.BlockSpec((1, gw), lambda i, j: (0, 0))],
        out_specs=(flat_spec, flat_spec, key_spec,
                   pl.BlockSpec((None, G, None, V_ROWS, tm), lambda i, j: (i, 0, j, 0, 0)),
                   key_spec,
                   pl.BlockSpec((None, G, sub, V_ROWS, KEY_TILE), lambda i, j: (i, 0, j, 0, 0))),
        compiler_params=_cparams(("parallel", "parallel")),
        name="kv_prep",
    )(z, z, z, jnp.tile(k_g[1], G).reshape(1, gw), jnp.tile(k_g[2], G).reshape(1, gw))


def _compress_kernel(x_ref, pos_ref, w1_ref, w2_ref, g_ref, o_ref, *, normalize, transposed):
    G = N_KV_GROUPS
    ncb = x_ref.shape[0] // CMP_STRIDE
    half = CMP_STRIDE * HEAD_DIM
    u = [jnp.zeros((ncb, CMP_HIDDEN), F32) for _ in range(G)]
    v = [jnp.zeros((ncb, CMP_HIDDEN), F32) for _ in range(G)]
    for r in range(CMP_STRIDE):
        xr = x_ref[pl.ds(r, ncb, stride=CMP_STRIDE), :].astype(BF16)
        wa = w1_ref[r * HEAD_DIM:(r + 1) * HEAD_DIM, :]
        wb = w1_ref[half + r * HEAD_DIM:half + (r + 1) * HEAD_DIM, :]
        for grp in range(G):
            xg = xr[:, grp * HEAD_DIM:(grp + 1) * HEAD_DIM]
            u[grp] = u[grp] + _dot(xg, wa)
            v[grp] = v[grp] + _dot(xg, wb)
    posc = _dot(pos_ref[...], w1_ref[...])[0:1, :]
    outs = []
    for grp in range(G):
        hid = u[grp] + pltpu.roll(v[grp], ncb - 1, 0) + posc
        out = _dot(jax.nn.gelu(hid).astype(BF16), w2_ref[...])
        outs.append(_rms(out, g_ref[...]) if normalize else out)
    if transposed:
        both = jnp.concatenate(outs, axis=1).T
        for grp in range(G):
            o_ref[grp] = both[grp * HEAD_DIM:(grp + 1) * HEAD_DIM, :].astype(o_ref.dtype)
    else:
        for grp in range(G):
            o_ref[grp] = outs[grp].astype(o_ref.dtype)


def compress(x3, pos, w1, w2, g, normalize, transposed):
    b, t, gw = x3.shape
    G = N_KV_GROUPS
    ncb = t // CMP_STRIDE
    kdim = CMP_STRIDE * HEAD_DIM
    posf = jnp.broadcast_to(pos.reshape(1, -1), (8, 2 * kdim)).astype(BF16)
    out_sds = (jax.ShapeDtypeStruct((b, G, HEAD_DIM, ncb), BF16) if transposed
               else jax.ShapeDtypeStruct((b, G, ncb, HEAD_DIM), BF16))
    out_block = (None, G, HEAD_DIM, ncb) if transposed else (None, G, ncb, HEAD_DIM)
    return pl.pallas_call(
        functools.partial(_compress_kernel, normalize=normalize, transposed=transposed),
        out_shape=out_sds,
        grid=(b,),
        in_specs=[pl.BlockSpec((None, t, gw), lambda i: (i, 0, 0)),
                  pl.BlockSpec((8, 2 * kdim), lambda i: (0, 0)),
                  pl.BlockSpec((2 * kdim, CMP_HIDDEN), lambda i: (0, 0)),
                  pl.BlockSpec((CMP_HIDDEN, HEAD_DIM), lambda i: (0, 0)),
                  pl.BlockSpec((1, HEAD_DIM), lambda i: (0, 0))],
        out_specs=pl.BlockSpec(out_block, lambda i: (i, 0, 0, 0)),
        compiler_params=_cparams(("parallel",)),
        name="compress",
    )(x3, posf, w1.astype(BF16), w2.astype(BF16), g.reshape(1, HEAD_DIM))


def _nsa_kernel(bnd_ref, *refs):
    safe = bnd_ref[0, 0] <= MAX_SAFE_SCORE_BOUND

    @pl.when(safe)
    def _():
        _nsa_body(bnd_ref, *refs, bounded=True)

    @pl.when(jnp.logical_not(safe))
    def _():
        _nsa_body(bnd_ref, *refs, bounded=False)


def _nsa_body(bnd_ref, zq_ref, gt_ref, gb_ref, qg_ref, ovt_ref, kc_ref, vct_ref, ks_ref, vst_ref, kw_ref,
              vwt_ref, o_ref, q_scr, sel_scr, m_scr, acc_scr, oc_scr, *, bounded):
    grp = pl.program_id(1)
    qb = pl.program_id(2)
    QB = Q_BLOCK
    ncp = kc_ref.shape[0]
    ns = ovt_ref.shape[0]
    tpos = qb * QB + lax.broadcasted_iota(jnp.int32, (1, QB), 1)

    def lane_tile(x):
        return jnp.concatenate([x] * HG, axis=1)

    zt = zq_ref[...].T
    qg = qg_ref[...] * (HEAD_DIM ** -0.5 * LOG2E)
    heads = []
    for h in range(HG):
        xh = zt[h * HEAD_DIM:(h + 1) * HEAD_DIM, :]
        ms = jnp.mean(xh * xh, axis=0, keepdims=True)
        heads.append(xh * lax.rsqrt(ms + EPS) * qg)
    q_scr[...] = jnp.concatenate(heads, axis=1).astype(BF16)
    q = q_scr[...]

    gates = jax.nn.sigmoid(gt_ref[...].T + gb_ref[...])
    per_grp = HG * 3
    gsel = gates[0:per_grp, :]
    for g2 in range(1, N_KV_GROUPS):
        gsel = jnp.where(grp == g2, gates[g2 * per_grp:(g2 + 1) * per_grp, :], gsel)

    def gate_row(c):
        return jnp.concatenate([gsel[3 * h + c:3 * h + c + 1, :] for h in range(HG)], axis=1)

    cmp_end = lax.broadcasted_iota(jnp.int32, (ncp, 1), 0) * CMP_STRIDE + (CMP_BLOCK - 1)
    keep = -bnd_ref[0, 0] if bounded else 0.0
    cbias = jnp.where(cmp_end <= tpos, keep, NEG)
    s = _dot(kc_ref[...], q) + lane_tile(cbias)
    e = jnp.exp2(s) if bounded else jnp.exp2(s - jnp.max(s, axis=0, keepdims=True))
    cmp_lhs = jnp.concatenate([vct_ref[...], _ones_rows(ncp), ovt_ref[...]], axis=0)
    r = _dot(cmp_lhs, e.astype(BF16))
    inv = jnp.where(lane_tile(tpos) >= CMP_BLOCK - 1, 1.0 / r[HEAD_DIM:HEAD_DIM + 1, :], 0.0)
    ocmp = r[0:HEAD_DIM, :] * inv
    imp_h = r[V_ROWS:V_ROWS + ns, :] * inv
    imp = imp_h[:, 0:QB]
    for h in range(1, HG):
        imp = imp + imp_h[:, h * QB:(h + 1) * QB]

    n_win = (WINDOW + QB) // KEY_TILE
    first_tile = qb * (QB // KEY_TILE) - WINDOW // KEY_TILE
    win_sub = lax.broadcasted_iota(jnp.int32, (n_win * KEY_TILE, 1), 0)
    tiles = [jnp.maximum(first_tile + u, 0) for u in range(n_win)]
    kwin = jnp.concatenate([kw_ref[j] for j in tiles], axis=0)
    vwin = jnp.concatenate([vwt_ref[j] for j in tiles], axis=1)
    kpos = first_tile * KEY_TILE + win_sub
    wbias = jnp.where((kpos <= tpos) & (kpos > tpos - WINDOW) & (kpos >= 0), keep, NEG)
    sw = _dot(kwin, q) + lane_tile(wbias)
    pw = jnp.exp2(sw) if bounded else jnp.exp2(sw - jnp.max(sw, axis=0, keepdims=True))
    ow = _dot(vwin, pw.astype(BF16))
    oc_scr[...] = gate_row(0) * ocmp + (gate_row(2) / ow[HEAD_DIM:HEAD_DIM + 1, :]) * ow[0:HEAD_DIM, :]

    blk = lax.broadcasted_iota(jnp.int32, (ns, 1), 0)
    blk_f = blk.astype(F32)
    cur = jnp.right_shift(tpos, SLC_BLOCK.bit_length() - 1)
    forced = (blk == 0) | (blk == cur) | (blk == cur - 1)
    bvalid = blk <= cur
    score = jnp.where(forced, -jnp.inf, jnp.where(bvalid, imp, NEG))
    sel = jnp.where(forced, 1.0, 0.0)
    for _ in range(max(min(N_SELECT, ns) - 3, 0)):
        mx = jnp.max(score, axis=0, keepdims=True)
        first = jnp.min(jnp.where(score == mx, blk_f, float(ns)), axis=0, keepdims=True)
        pick = blk_f == first
        sel = jnp.where(pick, 1.0, sel)
        score = jnp.where(pick, -jnp.inf, score)
    sel_scr[...] = jnp.where(bvalid, sel, 0.0)

    m_scr[...] = jnp.full(m_scr.shape, NEG, F32)
    acc_scr[...] = jnp.zeros(acc_scr.shape, F32)
    blocks_per_tile = SEL_TILE // SLC_BLOCK
    sel_sub = lax.broadcasted_iota(jnp.int32, (SEL_TILE, 1), 0)

    def sel_body(j, carry):
        rows = [jnp.broadcast_to(sel_scr[pl.ds(j * blocks_per_tile + bi, 1), :], (SLC_BLOCK, QB))
                for bi in range(blocks_per_tile)]
        chosen = jnp.concatenate(rows, axis=0)
        kpos = j * SEL_TILE + sel_sub
        bias = jnp.where((chosen > 0.5) & (kpos <= tpos), keep, NEG)
        st = _dot(ks_ref[j], q_scr[...]) + lane_tile(bias)
        if bounded:
            pt = jnp.exp2(st)
            acc_scr[...] += _dot(vst_ref[j], pt.astype(BF16))
        else:
            m_old = m_scr[...]
            m_new = jnp.maximum(m_old, jnp.max(st, axis=0, keepdims=True))
            alpha = jnp.exp2(m_old - m_new)
            pt = jnp.exp2(st - m_new)
            acc_scr[...] = alpha * acc_scr[...] + _dot(vst_ref[j], pt.astype(BF16))
            m_scr[...] = m_new
        return carry

    lax.fori_loop(0, qb // (SEL_TILE // QB) + 1, sel_body, 0)
    ot = (oc_scr[...]
          + (gate_row(1) / acc_scr[HEAD_DIM:HEAD_DIM + 1, :]) * acc_scr[0:HEAD_DIM, :])
    stacked = jnp.concatenate([ot[:, h * QB:(h + 1) * QB] for h in range(HG)], axis=0)
    o_ref[...] = stacked.T.astype(o_ref.dtype)


def nsa_attention(bound, z3, gate_b, q_g, overlap_t, kc, vct, ks, vst, kw, vwt):
    b, t, _ = z3.shape
    nq = t // Q_BLOCK
    nt = t // KEY_TILE
    qw = HG * HEAD_DIM
    ns, ncp = overlap_t.shape
    gate_blk = (NSA_COLS_PAD // LANES_V7X) - 1
    gb = jnp.zeros((LANES_V7X, 1), F32).at[:N_HEADS * 3, 0].set(gate_b)
    full5 = lambda i, g, q: (i, g, 0, 0, 0)
    lanes = HG * Q_BLOCK
    return pl.pallas_call(
        _nsa_kernel,
        out_shape=jax.ShapeDtypeStruct((b, t, N_HEADS * HEAD_DIM), BF16),
        grid=(b, N_KV_GROUPS, nq),
        in_specs=[pl.BlockSpec(memory_space=pltpu.SMEM),
                  pl.BlockSpec((None, Q_BLOCK, qw), lambda i, g, q: (i, q, g)),
                  pl.BlockSpec((None, Q_BLOCK, LANES_V7X), lambda i, g, q: (i, q, gate_blk)),
                  pl.BlockSpec((LANES_V7X, 1), lambda i, g, q: (0, 0)),
                  pl.BlockSpec((HEAD_DIM, 1), lambda i, g, q: (0, 0)),
                  pl.BlockSpec((ns, ncp), lambda i, g, q: (0, 0)),
                  pl.BlockSpec((None, None, ncp, HEAD_DIM), lambda i, g, q: (i, g, 0, 0)),
                  pl.BlockSpec((None, None, HEAD_DIM, ncp), lambda i, g, q: (i, g, 0, 0)),
                  pl.BlockSpec((None, None, t // SEL_TILE, SEL_TILE, HEAD_DIM), full5),
                  pl.BlockSpec((None, None, t // SEL_TILE, V_ROWS, SEL_TILE), full5),
                  pl.BlockSpec((None, None, nt, KEY_TILE, HEAD_DIM), full5),
                  pl.BlockSpec((None, None, nt, V_ROWS, KEY_TILE), full5)],
        out_specs=pl.BlockSpec((None, Q_BLOCK, qw), lambda i, g, q: (i, q, g)),
        scratch_shapes=[pltpu.VMEM((HEAD_DIM, lanes), BF16),
                        pltpu.VMEM((ns, Q_BLOCK), F32),
                        pltpu.VMEM((1, lanes), F32),
                        pltpu.VMEM((V_ROWS, lanes), F32),
                        pltpu.VMEM((HEAD_DIM, lanes), F32)],
        compiler_params=_cparams(("parallel", "parallel", "arbitrary"), VMEM_LIMIT_V7X),
        name="nsa_attention",
    )(bound.reshape(1, 1), z3, z3, gb, q_g.reshape(HEAD_DIM, 1), overlap_t, kc, vct, ks, vst, kw, vwt)


def _overlap_matrix_t(t):
    ncp = t // CMP_STRIDE
    ns = t // SLC_BLOCK
    c_start = np.arange(ncp) * CMP_STRIDE
    sj = np.arange(ns)
    ov = ((c_start[None, :] < (sj[:, None] + 1) * SLC_BLOCK)
          & (c_start[None, :] + CMP_BLOCK > sj[:, None] * SLC_BLOCK)
          & (c_start[None, :] + CMP_BLOCK <= t))
    return jnp.asarray(ov, dtype=BF16)


def ab_layer(h, b, t, norm_g, w_in, conv_w, conv_b, ret_g, ig_b, fg_b, m_g, w_out):
    n = b * t
    w_in_p = jnp.pad(w_in, ((0, 0), (0, AB_COLS_PAD - AB_COLS))).astype(BF16)
    z = norm_matmul(h, norm_g, w_in_p)
    z3 = z.reshape(b, t, AB_COLS_PAD)
    cos_t, sin_t = rope_tables(t)
    ret = retention(z3, cos_t, sin_t, ret_g)
    ml = mlstm(z3, conv_w, conv_b, ig_b, fg_b, m_g)
    w_out_b = w_out.astype(BF16)
    rw = R_HEADS * R_DV
    return [ret.reshape(n, rw), ml.reshape(n, -1)], w_out_b


def nsa_layer(h, b, t, norm_g, w_in, q_g, k_g, pos_k, pos_v, w1k, w2k, w1v, w2v, gate_b, w_out):
    n = b * t
    G = N_KV_GROUPS
    w_in_p = jnp.pad(w_in, ((0, 0), (0, NSA_COLS_PAD - NSA_COLS))).astype(BF16)
    z = norm_matmul(h, norm_g, w_in_p)
    kc_in, vc_in, ks, vst, kw, vwt = kv_prep(z, k_g, b, t)
    gw = G * HEAD_DIM
    kc = compress(kc_in.reshape(b, t, gw), pos_k, w1k, w2k, k_g[0], True, False)
    vct = compress(vc_in.reshape(b, t, gw), pos_v, w1v, w2v, k_g[0], False, True)

    def key_tiles(x, kt):
        return x.reshape(b, G, t // kt, kt, HEAD_DIM)

    bound = 1.02 * LOG2E * math.sqrt(HEAD_DIM) * jnp.max(jnp.abs(q_g)) * jnp.max(jnp.abs(k_g))
    args = (bound, z.reshape(b, t, NSA_COLS_PAD), gate_b, q_g, _overlap_matrix_t(t), kc, vct,
            key_tiles(ks, SEL_TILE), vst, key_tiles(kw, KEY_TILE), vwt)
    o = nsa_attention(*args)
    return [o.reshape(n, -1)], w_out.astype(BF16)


def kernel(x, p, ab_norm_g, ab_w_in, ab_conv_w, ab_conv_b, ab_ret_norm_g, ab_ig_b, ab_fg_b, ab_m_norm_g, ab_w_out, nsa_norm_g, nsa_w_in, nsa_q_norm_g, nsa_k_norm_g, nsa_cmp_pos_k, nsa_cmp_pos_v, nsa_cmp_w1k, nsa_cmp_w2k, nsa_cmp_w1v, nsa_cmp_w2v, nsa_gate_b, nsa_w_out, ffn_norm_g, ffn_w_up, ffn_conv_w, ffn_conv_b, ffn_w_down, ple_w, ple_norm_g, ple_gate_norm_g, ple_w_gate):
    b, t, d = x.shape
    n = b * t
    depth = p.shape[0]
    h = x.reshape(n, d)
    p2 = p.reshape(depth, n, -1)
    ffn_w_up_b, ffn_w_down_b = ffn_w_up.astype(BF16), ffn_w_down.astype(BF16)
    ple_w_b, ple_w_gate_b = ple_w.astype(BF16), ple_w_gate.astype(BF16)
    for i in range(depth):
        j = i // 2
        if i % 2 == 0:
            mix = ab_layer(h, b, t, ab_norm_g[j], ab_w_in[j], ab_conv_w[j], ab_conv_b[j], ab_ret_norm_g[j],
                           ab_ig_b[j], ab_fg_b[j], ab_m_norm_g[j], ab_w_out[j])
        else:
            mix = nsa_layer(h, b, t, nsa_norm_g[j], nsa_w_in[j], nsa_q_norm_g[j], nsa_k_norm_g[j],
                            nsa_cmp_pos_k[j], nsa_cmp_pos_v[j], nsa_cmp_w1k[j], nsa_cmp_w2k[j],
                            nsa_cmp_w1v[j], nsa_cmp_w2v[j], nsa_gate_b[j], nsa_w_out[j])
        h = mix_ffn_ple(h, mix, i, ffn_norm_g, ffn_w_up_b, ffn_conv_w, ffn_conv_b, ffn_w_down_b, p2, ple_w_b,
                        ple_norm_g, ple_gate_norm_g, ple_w_gate_b, t)
    return h.reshape(b, t, d)
```

```python
import functools
import math

import numpy as np
import jax
import jax.numpy as jnp
from jax import lax
from jax.experimental import pallas as pl
from jax.experimental.pallas import tpu as pltpu

F32 = jnp.float32
BF16 = jnp.bfloat16

LANES_V7X = 128
BF16_ROWS = 16
VMEM_LIMIT_V7X = 56 * 1024 * 1024

R_HEADS, R_DK, R_DV, R_CHUNK = 4, 64, 128, 128
ROPE_BASE = 10000.0
M_HEADS, M_DK, M_DV, M_CHUNK, M_CONV = 4, 64, 128, 64, 4
AB_SIZES = (R_HEADS * R_DK, R_HEADS * R_DK, R_HEADS * R_DV, R_HEADS * R_DV,
            M_HEADS * M_DK, M_HEADS * M_DK, M_HEADS * M_DV, M_HEADS * M_DV, M_HEADS, M_HEADS)
AB_COLS = sum(AB_SIZES)
AB_COLS_PAD = 3200
N_HEADS, N_KV_GROUPS, HEAD_DIM = 16, 2, 64
HG = N_HEADS // N_KV_GROUPS
CMP_BLOCK, CMP_STRIDE, CMP_HIDDEN = 32, 16, 256
SLC_BLOCK, N_SELECT, WINDOW = 64, 16, 512
Q_BLOCK = 256
NSA_COLS = N_HEADS * HEAD_DIM + 6 * N_KV_GROUPS * HEAD_DIM + N_HEADS * 3
NSA_COLS_PAD = 1920
D_FF = 2816
FFN_CONV = 3
NEG = -1e30
EPS = 1e-6
KEY_TILE = 128
SEL_TILE = 512
V_ROWS = HEAD_DIM + BF16_ROWS
LOG2E = math.log2(math.e)
MAX_SAFE_SCORE_BOUND = 56.0


def _cparams(sem, vmem=None):
    return pltpu.CompilerParams(dimension_semantics=sem, vmem_limit_bytes=vmem)


def _rms(x, g):
    ms = jnp.mean(x * x, axis=-1, keepdims=True)
    return x * lax.rsqrt(ms + EPS) * g


def _dot(a, b):
    return jnp.dot(a, b, preferred_element_type=F32)


def _dot_nt(a, b):
    return lax.dot_general(a, b, (((1,), (1,)), ((), ())), preferred_element_type=F32)


def _dot_f32(a, b):
    return jnp.dot(a, b, preferred_element_type=F32, precision=lax.Precision.HIGHEST)


def _gelu_tanh(x):
    k1 = -2.0 * math.sqrt(2.0 / math.pi) * LOG2E
    return x / (1.0 + jnp.exp2(x * (k1 + (k1 * 0.044715) * (x * x))))


def _norm_matmul_kernel(x_ref, g_ref, w_ref, o_ref):
    xn = _rms(x_ref[...], g_ref[...]).astype(BF16)
    o_ref[...] = _dot(xn, w_ref[...]).astype(o_ref.dtype)


def norm_matmul(x, g, w, tm=1024, out_dtype=F32):
    n, d = x.shape
    nc = w.shape[1]
    return pl.pallas_call(
        _norm_matmul_kernel,
        out_shape=jax.ShapeDtypeStruct((n, nc), out_dtype),
        grid=(n // tm,),
        in_specs=[pl.BlockSpec((tm, d), lambda i: (i, 0)),
                  pl.BlockSpec((1, d), lambda i: (0, 0)),
                  pl.BlockSpec((d, nc), lambda i: (0, 0))],
        out_specs=pl.BlockSpec((tm, nc), lambda i: (i, 0)),
        compiler_params=_cparams(("parallel",), VMEM_LIMIT_V7X),
        name="norm_matmul",
    )(x, g.reshape(1, d), w)


def _rope_table_kernel(inv_ref, cos_ref, sin_ref):
    c = pl.program_id(0)
    rows, width = cos_ref.shape
    pos = (c * rows + lax.broadcasted_iota(jnp.int32, (rows, LANES_V7X), 0)).astype(F32)
    lane = lax.broadcasted_iota(jnp.int32, (rows, LANES_V7X), 1)
    ang = pos * inv_ref[:, 0:LANES_V7X]
    cs = jnp.cos(ang)
    sn = jnp.sin(ang)
    sn = jnp.where(lane % R_DK < R_DK // 2, -sn, sn)
    reps = width // LANES_V7X
    cos_ref[...] = jnp.concatenate([cs] * reps, axis=1)
    sin_ref[...] = jnp.concatenate([sn] * reps, axis=1)


def rope_tables(t):
    half = R_DK // 2
    inv = ROPE_BASE ** (-jnp.arange(half, dtype=F32) / half)
    inv = jnp.tile(inv, 2 * R_HEADS).reshape(1, R_HEADS * R_DK)
    width = R_HEADS * R_DK
    shp = jax.ShapeDtypeStruct((t, width), F32)
    return pl.pallas_call(
        _rope_table_kernel,
        out_shape=(shp, shp),
        grid=(t // R_CHUNK,),
        in_specs=[pl.BlockSpec((1, width), lambda c: (0, 0))],
        out_specs=(pl.BlockSpec((R_CHUNK, width), lambda c: (c, 0)),
                   pl.BlockSpec((R_CHUNK, width), lambda c: (c, 0))),
        compiler_params=_cparams(("parallel",)),
        name="rope_tables",
    )(inv)


def _retention_kernel(cos_ref, sin_ref, q_ref, k_ref, v_ref, g_ref, gain_ref, o_ref, r_ref):
    c = pl.program_id(0)
    nb = q_ref.shape[0]
    L = R_CHUNK

    @pl.when(c == 0)
    def _():
        r_ref[...] = jnp.zeros_like(r_ref)

    cos = cos_ref[...]
    sin = sin_ref[...]
    lane = lax.broadcasted_iota(jnp.int32, cos.shape, 1)
    first_half = lane % R_DK < R_DK // 2
    width = R_HEADS * R_DK

    def rot(x):
        swapped = jnp.where(first_half, pltpu.roll(x, width - R_DK // 2, 1), pltpu.roll(x, R_DK // 2, 1))
        return x * cos + swapped * sin

    gain = gain_ref[...]
    ri = lax.broadcasted_iota(jnp.int32, (L, L), 0)
    ci = lax.broadcasted_iota(jnp.int32, (L, L), 1)
    diff = (ri - ci).astype(F32)
    causal = ri >= ci
    idx = lax.broadcasted_iota(jnp.int32, (L, 1), 0).astype(F32)
    decays = []
    for h in range(R_HEADS):
        log_g = math.log1p(-2.0 ** (-5.0 - h))
        decays.append(dict(
            dmask=jnp.where(causal, jnp.exp(jnp.where(causal, diff, 0.0) * log_g), 0.0),
            q_dec=jnp.exp((idx + 1.0) * log_g), k_dec=jnp.exp((L - 1.0 - idx) * log_g),
            c_dec=math.exp(L * log_g)))

    chains = []
    for bi in range(nb):
        q = rot(q_ref[bi])
        k = rot(k_ref[bi]) * (R_DK ** -0.5)
        for h in range(R_HEADS):
            chains.append(dict(bi=bi, h=h, u=bi * R_HEADS + h,
                               qb=q[:, h * R_DK:(h + 1) * R_DK].astype(BF16),
                               kh=k[:, h * R_DK:(h + 1) * R_DK]))

    for ch in chains:
        ch['rstate'] = r_ref[ch['u']]
        ch['s'] = _dot_nt(ch['qb'], ch['kh'].astype(BF16)) * decays[ch['h']]['dmask']
        ch['qr'] = _dot(ch['qb'], ch['rstate'].astype(BF16))

    for ch in chains:
        d = decays[ch['h']]
        vh = v_ref[ch['bi'], :, ch['h'] * R_DV:(ch['h'] + 1) * R_DV].astype(BF16)
        ch['o'] = _dot(ch['s'].astype(BF16), vh) + ch['qr'] * d['q_dec']
        kd = (ch['kh'] * d['k_dec']).T.astype(BF16)
        r_ref[ch['u']] = d['c_dec'] * ch['rstate'] + _dot(kd, vh)

    for ch in chains:
        sl = slice(ch['h'] * R_DV, (ch['h'] + 1) * R_DV)
        gh = g_ref[ch['bi'], :, sl]
        o_ref[ch['bi'], :, sl] = (_rms(ch['o'], gain[:, sl]) * (gh * jax.nn.sigmoid(gh))).astype(o_ref.dtype)


def retention(z3, cos_t, sin_t, gain):
    b, t, _ = z3.shape
    nc = t // R_CHUNK
    qk_w = R_HEADS * R_DK
    v_w = R_HEADS * R_DV
    return pl.pallas_call(
        _retention_kernel,
        out_shape=jax.ShapeDtypeStruct((b, t, v_w), BF16),
        grid=(nc,),
        in_specs=[pl.BlockSpec((R_CHUNK, qk_w), lambda c: (c, 0)),
                  pl.BlockSpec((R_CHUNK, qk_w), lambda c: (c, 0)),
                  pl.BlockSpec((b, R_CHUNK, qk_w), lambda c: (0, c, 0)),
                  pl.BlockSpec((b, R_CHUNK, qk_w), lambda c: (0, c, 1)),
                  pl.BlockSpec((b, R_CHUNK, v_w), lambda c: (0, c, 1)),
                  pl.BlockSpec((b, R_CHUNK, v_w), lambda c: (0, c, 2)),
                  pl.BlockSpec((1, v_w), lambda c: (0, 0))],
        out_specs=pl.BlockSpec((b, R_CHUNK, v_w), lambda c: (0, c, 0)),
        scratch_shapes=[pltpu.VMEM((b * R_HEADS, R_DK, R_DV), F32)],
        compiler_params=_cparams(("arbitrary",)),
        name="retention",
    )(cos_t, sin_t, z3, z3, z3, z3, gain.reshape(1, v_w))


def _mlstm_kernel(q_ref, k_ref, v_ref, og_ref, gc_ref, cw_ref, cb_ref, bc_ref, br_ref, gain_ref,
                  o_ref, xbuf, c_ref, n_ref, m_ref):
    c = pl.program_id(0)
    nb = q_ref.shape[0]
    L = M_CHUNK
    H = M_HEADS
    qk_w = H * M_DK
    halo = 8

    @pl.when(c == 0)
    def _():
        xbuf[:, 0:halo, :] = jnp.zeros((nb, halo, 2 * qk_w), F32)
        c_ref[...] = jnp.zeros_like(c_ref)
        n_ref[...] = jnp.zeros_like(n_ref)
        m_ref[...] = jnp.zeros_like(m_ref)

    gain = gain_ref[...]
    ri = lax.broadcasted_iota(jnp.int32, (L, L), 0)
    ci = lax.broadcasted_iota(jnp.int32, (L, L), 1)
    causal = ri >= ci
    tril = causal.astype(F32)
    triu = (ri <= ci).astype(F32)

    chains = []
    for bi in range(nb):
        xbuf[bi, halo:halo + L, 0:qk_w] = q_ref[bi]
        xbuf[bi, halo:halo + L, qk_w:2 * qk_w] = k_ref[bi]
        conv = cb_ref[...]
        for j in range(M_CONV):
            conv = conv + xbuf[bi, pl.ds(halo - (M_CONV - 1) + j, L), :] * cw_ref[j:j + 1, :]
        tail = xbuf[bi, L:L + halo, :]
        xbuf[bi, 0:halo, :] = tail
        act = conv * jax.nn.sigmoid(conv)
        q = act[:, 0:qk_w] * (M_DK ** -0.5)
        k = act[:, qk_w:2 * qk_w]
        gblock = gc_ref[bi]
        gc = gblock[:, 0:2 * H] + bc_ref[...]
        gr = gblock.T[0:2 * H, :] + br_ref[...]
        ig_c = gc[:, 0:H]
        ig_r = gr[0:H, :]
        b_c = _dot_f32(tril, jax.nn.log_sigmoid(gc[:, H:2 * H]))
        b_r = _dot_f32(jax.nn.log_sigmoid(gr[H:2 * H, :]), triu)
        for h in range(H):
            chains.append(dict(
                bi=bi, h=h, u=bi * H + h,
                qh=q[:, h * M_DK:(h + 1) * M_DK], kh=k[:, h * M_DK:(h + 1) * M_DK],
                bh=b_c[:, h:h + 1], brow=b_r[h:h + 1, :], irow=ig_r[h:h + 1, :], icol=ig_c[:, h:h + 1]))

    def stack(parts):
        return jnp.concatenate(parts, axis=0)

    def rows(x):
        return jnp.broadcast_to(x, (L, x.shape[1]))

    m_prev_u = [m_ref[ch['u']][:, 0:1] for ch in chains]
    b_last_u = [ch['bh'][L - 1:L, :] for ch in chains]
    bh = stack([ch['bh'] for ch in chains])
    icol = stack([ch['icol'] for ch in chains])
    brow = stack([rows(ch['brow']) for ch in chains])
    irow = stack([rows(ch['irow']) for ch in chains])
    m_prev = stack([rows(m) for m in m_prev_u])
    b_last = stack([rows(x) for x in b_last_u])
    causal_all = stack([causal] * len(chains))
    qs = stack([ch['qh'] for ch in chains])
    ks = stack([ch['kh'] for ch in chains])
    qb = qs.astype(BF16)
    kb = ks.astype(BF16)
    cstates = [c_ref[ch['u']] for ch in chains]
    nstates = [n_ref[ch['u']] for ch in chains]

    def chain_rows(x, i):
        return x[i * L:(i + 1) * L]

    s_raw = stack([_dot_nt(chain_rows(qb, i), chain_rows(kb, i)) for i in range(len(chains))])
    qc = stack([_dot(chain_rows(qb, i), cstates[i].astype(BF16)) for i in range(len(chains))])

    glog = jnp.where(causal_all, irow - brow, NEG)
    c = jnp.maximum(m_prev, jnp.max(glog, axis=-1, keepdims=True))
    m_t = bh + c
    s = s_raw * jnp.exp(glog - c)
    w_inter = jnp.exp(m_prev - c)
    wlog = b_last - bh + icol
    m_new_u = [jnp.maximum(b_last_u[i] + m_prev_u[i], jnp.max(chain_rows(wlog, i), axis=0, keepdims=True))
               for i in range(len(chains))]
    m_new = stack([rows(m) for m in m_new_u])
    wk = ks * jnp.exp(wlog - m_new)

    sb = s.astype(BF16)
    vhs = [v_ref[ch['bi'], :, ch['h'] * M_DV:(ch['h'] + 1) * M_DV].astype(BF16) for ch in chains]
    sv = stack([_dot(chain_rows(sb, i), vhs[i]) for i in range(len(chains))])
    kv = [_dot(chain_rows(wk, i).T.astype(BF16), vhs[i]) for i in range(len(chains))]

    qn = jnp.sum(qs * stack([rows(n) for n in nstates]), axis=-1, keepdims=True)
    den = jnp.sum(s, axis=-1, keepdims=True) + w_inter * qn
    hh = (sv + w_inter * qc) / jnp.maximum(jnp.abs(den), jnp.exp(-m_t))
    og = stack([og_ref[ch['bi'], :, ch['h'] * M_DV:(ch['h'] + 1) * M_DV] for ch in chains])
    gains = stack([rows(gain[:, ch['h'] * M_DV:(ch['h'] + 1) * M_DV]) for ch in chains])
    out = jax.nn.sigmoid(og) * _rms(hh, gains)

    for i, ch in enumerate(chains):
        u = ch['u']
        decay = jnp.exp(b_last_u[i] + m_prev_u[i] - m_new_u[i])
        c_ref[u] = decay * cstates[i] + kv[i]
        n_ref[u] = decay * nstates[i] + jnp.sum(chain_rows(wk, i), axis=0, keepdims=True)
        m_ref[u] = jnp.broadcast_to(m_new_u[i], (1, LANES_V7X))
        o_ref[ch['bi'], :, ch['h'] * M_DV:(ch['h'] + 1) * M_DV] = chain_rows(out, i).astype(o_ref.dtype)


def mlstm(z3, conv_w, conv_b, ig_b, fg_b, gain):
    b, t, _ = z3.shape
    nc = t // M_CHUNK
    H = M_HEADS
    qk_w = H * M_DK
    v_w = H * M_DV
    bias = jnp.concatenate([ig_b, fg_b])
    gate_blk = AB_COLS_PAD // LANES_V7X - 1
    return pl.pallas_call(
        _mlstm_kernel,
        out_shape=jax.ShapeDtypeStruct((b, t, v_w), BF16),
        grid=(nc,),
        in_specs=[pl.BlockSpec((b, M_CHUNK, qk_w), lambda c: (0, c, 6)),
                  pl.BlockSpec((b, M_CHUNK, qk_w), lambda c: (0, c, 7)),
                  pl.BlockSpec((b, M_CHUNK, v_w), lambda c: (0, c, 4)),
                  pl.BlockSpec((b, M_CHUNK, v_w), lambda c: (0, c, 5)),
                  pl.BlockSpec((b, M_CHUNK, LANES_V7X), lambda c: (0, c, gate_blk)),
                  pl.BlockSpec((M_CONV, 2 * qk_w), lambda c: (0, 0)),
                  pl.BlockSpec((1, 2 * qk_w), lambda c: (0, 0)),
                  pl.BlockSpec((1, 2 * H), lambda c: (0, 0)),
                  pl.BlockSpec((2 * H, 1), lambda c: (0, 0)),
                  pl.BlockSpec((1, v_w), lambda c: (0, 0))],
        out_specs=pl.BlockSpec((b, M_CHUNK, v_w), lambda c: (0, c, 0)),
        scratch_shapes=[pltpu.VMEM((b, 8 + M_CHUNK, 2 * qk_w), F32),
                        pltpu.VMEM((b * H, M_DK, M_DV), F32),
                        pltpu.VMEM((b * H, 1, M_DK), F32),
                        pltpu.VMEM((b * H, 1, LANES_V7X), F32)],
        compiler_params=_cparams(("arbitrary",)),
        name="mlstm",
    )(z3, z3, z3, z3, z3, conv_w, conv_b.reshape(1, -1), bias.reshape(1, -1), bias.reshape(-1, 1),
      gain.reshape(1, v_w))


def _mix_ffn_ple_kernel(*refs, n_mix, tm, seq, tf):
    hp_ref, h_ref = refs[0], refs[1]
    mix_refs = refs[2:2 + 2 * n_mix]
    (wo_ref, g_ref, wa_ref, wb_ref, cw_ref, cb_ref, wd_ref, p_ref, wp_ref, ng_ref, gg_ref, wg_ref,
     o_ref, xn_ref, a_ref) = refs[2 + 2 * n_mix:]
    i = pl.program_id(0)
    halo = 8
    g = g_ref[...]
    m_prev = jnp.concatenate([mix_refs[2 * k][...] for k in range(n_mix)], axis=1)
    m_tile = jnp.concatenate([mix_refs[2 * k + 1][...] for k in range(n_mix)], axis=1)
    x = h_ref[...] + _dot(m_tile, wo_ref[...])
    xp = hp_ref[...] + _dot(m_prev, wo_ref[...])[BF16_ROWS - halo:, :]
    xn_ref[halo:halo + tm, :] = _rms(x, g).astype(BF16)
    keep = ((i * tm) % seq != 0).astype(F32)
    xn_ref[0:halo, :] = (_rms(xp, g) * keep).astype(BF16)
    n_chunks = D_FF // tf

    def up_proj(c):
        cs = slice(c * tf, (c + 1) * tf)
        a_ref[c % 2] = _dot(xn_ref[...], wa_ref[:, cs])
        return _dot(xn_ref[halo:halo + tm, :], wb_ref[:, cs])

    h2 = x
    bgate = up_proj(0)
    for c in range(n_chunks):
        cs = slice(c * tf, (c + 1) * tf)
        bgate_next = up_proj(c + 1) if c + 1 < n_chunks else None
        conv = cb_ref[:, cs]
        for t in range(FFN_CONV):
            conv = conv + a_ref[c % 2, pl.ds(halo - (FFN_CONV - 1) + t, tm), :] * cw_ref[t:t + 1, cs]
        act = (_gelu_tanh(conv) * bgate).astype(BF16)
        h2 = h2 + _dot(act, wd_ref[cs, :])
        bgate = bgate_next
    e = _rms(_dot(p_ref[...].astype(BF16), wp_ref[...]), ng_ref[...])
    gate = jax.nn.sigmoid(_dot(_rms(h2, gg_ref[...]).astype(BF16), wg_ref[...]))
    o_ref[...] = h2 + gate * e


def mix_ffn_ple(h, mix, li, g, w_up, conv_w, conv_b, w_down, p, wp, norm_g, gate_norm_g, wg, seq, tm=512, tf=2816):
    mix_outs, w_out = mix
    n, d = h.shape
    pd = p.shape[-1]
    hb = tm // 8
    const = lambda i: (0, 0)
    layer = lambda i: (li, 0, 0)
    prev8 = lambda i: (jnp.maximum(i * hb - 1, 0), 0)
    resident = dict(pipeline_mode=pl.Buffered(1))
    prev16 = lambda i: (jnp.maximum(i * (tm // BF16_ROWS) - 1, 0), 0)
    mix_specs, mix_args = [], []
    for m in mix_outs:
        mix_specs += [pl.BlockSpec((BF16_ROWS, m.shape[1]), prev16),
                      pl.BlockSpec((tm, m.shape[1]), lambda i: (i, 0))]
        mix_args += [m, m]
    mix_specs.append(pl.BlockSpec(w_out.shape, const, **resident))
    mix_args.append(w_out)
    return pl.pallas_call(
        functools.partial(_mix_ffn_ple_kernel, n_mix=len(mix_outs), tm=tm, seq=seq, tf=tf),
        out_shape=jax.ShapeDtypeStruct((n, d), F32),
        grid=(n // tm,),
        in_specs=[pl.BlockSpec((8, d), prev8),
                  pl.BlockSpec((tm, d), lambda i: (i, 0))] + mix_specs + [
                  pl.BlockSpec((None, 1, d), layer),
                  pl.BlockSpec((None, d, D_FF), lambda i: (li, 0, 0), **resident),
                  pl.BlockSpec((None, d, D_FF), lambda i: (li, 0, 1), **resident),
                  pl.BlockSpec((None, FFN_CONV, D_FF), layer),
                  pl.BlockSpec((None, 1, D_FF), layer),
                  pl.BlockSpec((None, D_FF, d), layer, **resident),
                  pl.BlockSpec((None, tm, pd), lambda i: (li, i, 0)),
                  pl.BlockSpec((None, pd, d), layer, **resident),
                  pl.BlockSpec((None, 1, d), layer),
                  pl.BlockSpec((None, 1, d), layer),
                  pl.BlockSpec((None, d, d), layer, **resident)],
        out_specs=pl.BlockSpec((tm, d), lambda i: (i, 0)),
        scratch_shapes=[pltpu.VMEM((8 + tm, d), BF16),
                        pltpu.VMEM((min(2, D_FF // tf), 8 + tm, tf), F32)],
        compiler_params=_cparams(("parallel",), VMEM_LIMIT_V7X),
        name="mix_ffn_ple",
    )(h, h, *mix_args, g[:, None, :], w_up, w_up, conv_w, conv_b[:, None, :], w_down,
      p, wp, norm_g[:, None, :], gate_norm_g[:, None, :], wg)


def _group_rms(x, g):
    lane = lax.broadcasted_iota(jnp.int32, x.shape, 1)
    x2 = x * x
    ms = jnp.zeros_like(x)
    for grp in range(N_KV_GROUPS):
        in_grp = (lane >= grp * HEAD_DIM) & (lane < (grp + 1) * HEAD_DIM)
        tot = jnp.sum(jnp.where(in_grp, x2, 0.0), axis=-1, keepdims=True)
        ms = jnp.where(in_grp, tot * (1.0 / HEAD_DIM), ms)
    return x * lax.rsqrt(ms + EPS) * g


def _ones_rows(width):
    return (lax.broadcasted_iota(jnp.int32, (BF16_ROWS, width), 0) == 0).astype(BF16)


def _kv_prep_kernel(c_ref, s_ref, w_ref, gs_ref, gw_ref, kc_ref, vc_ref, ks_ref, vs_ref, kw_ref, vw_ref):
    gw = N_KV_GROUPS * HEAD_DIM
    cc = c_ref[...].astype(F32)
    ss = s_ref[...].astype(F32)
    ww = w_ref[...].astype(F32)
    kc_ref[...] = cc[:, 0:gw]
    vc_ref[...] = cc[:, gw:2 * gw]
    ks = _group_rms(ss[:, 0:gw], gs_ref[...]).astype(BF16)
    kw = _group_rms(ww[:, 0:gw], gw_ref[...]).astype(BF16)
    vst = ss[:, gw:2 * gw].T
    vwt = ww[:, gw:2 * gw].T
    for g in range(N_KV_GROUPS):
        lanes = slice(g * HEAD_DIM, (g + 1) * HEAD_DIM)
        ks_ref[g] = ks[:, lanes]
        kw_ref[g] = kw[:, lanes]
        vs_ref[g] = jnp.concatenate([vst[lanes, :].astype(BF16), _ones_rows(SEL_TILE)], axis=0)
        for u in range(SEL_TILE // KEY_TILE):
            vw_ref[g, u] = jnp.concatenate([vwt[lanes, u * KEY_TILE:(u + 1) * KEY_TILE].astype(BF16),
                                            _ones_rows(KEY_TILE)], axis=0)


def kv_prep(z, k_g, b, t):
    n = z.shape[0]
    G = N_KV_GROUPS
    gw = G * HEAD_DIM
    base = N_HEADS * HEAD_DIM // (2 * gw)
    tm = SEL_TILE
    nst = t // tm
    sub = SEL_TILE // KEY_TILE
    row = lambda i, j: i * nst + j
    flat = jax.ShapeDtypeStruct((n, gw), F32)
    keys = jax.ShapeDtypeStruct((b, G, t, HEAD_DIM), BF16)
    flat_spec = pl.BlockSpec((tm, gw), lambda i, j: (row(i, j), 0))
    key_spec = pl.BlockSpec((None, G, tm, HEAD_DIM), lambda i, j: (i, 0, j, 0))
    return pl.pallas_call(
        _kv_prep_kernel,
        out_shape=(flat, flat, keys, jax.ShapeDtypeStruct((b, G, nst, V_ROWS, tm), BF16),
                   keys, jax.ShapeDtypeStruct((b, G, nst * sub, V_ROWS, KEY_TILE), BF16)),
        grid=(b, nst),
        in_specs=[pl.BlockSpec((tm, 2 * gw), lambda i, j: (row(i, j), base)),
                  pl.BlockSpec((tm, 2 * gw), lambda i, j: (row(i, j), base + 1)),
                  pl.BlockSpec((tm, 2 * gw), lambda i, j: (row(i, j), base + 2)),
                  pl.BlockSpec((1, gw), lambda i, j: (0, 0)),
                  pl.BlockSpec((1, gw), lambda i, j: (0, 0))],
        out_specs=(flat_spec, flat_spec, key_spec,
                   pl.BlockSpec((None, G, None, V_ROWS, tm), lambda i, j: (i, 0, j, 0, 0)),
                   key_spec,
                   pl.BlockSpec((None, G, sub, V_ROWS, KEY_TILE), lambda i, j: (i, 0, j, 0, 0))),
        compiler_params=_cparams(("parallel", "parallel")),
        name="kv_prep",
    )(z, z, z, jnp.tile(k_g[1], G).reshape(1, gw), jnp.tile(k_g[2], G).reshape(1, gw))


def _compress_kernel(x_ref, pos_ref, w1_ref, w2_ref, g_ref, o_ref, *, normalize, transposed):
    G = N_KV_GROUPS
    ncb = x_ref.shape[0] // CMP_STRIDE
    half = CMP_STRIDE * HEAD_DIM
    u = [jnp.zeros((ncb, CMP_HIDDEN), F32) for _ in range(G)]
    v = [jnp.zeros((ncb, CMP_HIDDEN), F32) for _ in range(G)]
    for r in range(CMP_STRIDE):
        xr = x_ref[pl.ds(r, ncb, stride=CMP_STRIDE), :].astype(BF16)
        wa = w1_ref[r * HEAD_DIM:(r + 1) * HEAD_DIM, :]
        wb = w1_ref[half + r * HEAD_DIM:half + (r + 1) * HEAD_DIM, :]
        for grp in range(G):
            xg = xr[:, grp * HEAD_DIM:(grp + 1) * HEAD_DIM]
            u[grp] = u[grp] + _dot(xg, wa)
            v[grp] = v[grp] + _dot(xg, wb)
    posc = _dot(pos_ref[...], w1_ref[...])[0:1, :]
    outs = []
    for grp in range(G):
        hid = u[grp] + pltpu.roll(v[grp], ncb - 1, 0) + posc
        out = _dot(jax.nn.gelu(hid).astype(BF16), w2_ref[...])
        outs.append(_rms(out, g_ref[...]) if normalize else out)
    if transposed:
        both = jnp.concatenate(outs, axis=1).T
        for grp in range(G):
            o_ref[grp] = both[grp * HEAD_DIM:(grp + 1) * HEAD_DIM, :].astype(o_ref.dtype)
    else:
        for grp in range(G):
            o_ref[grp] = outs[grp].astype(o_ref.dtype)


def compress(x3, pos, w1, w2, g, normalize, transposed):
    b, t, gw = x3.shape
    G = N_KV_GROUPS
    ncb = t // CMP_STRIDE
    kdim = CMP_STRIDE * HEAD_DIM
    posf = jnp.broadcast_to(pos.reshape(1, -1), (8, 2 * kdim)).astype(BF16)
    out_sds = (jax.ShapeDtypeStruct((b, G, HEAD_DIM, ncb), BF16) if transposed
               else jax.ShapeDtypeStruct((b, G, ncb, HEAD_DIM), BF16))
    out_block = (None, G, HEAD_DIM, ncb) if transposed else (None, G, ncb, HEAD_DIM)
    return pl.pallas_call(
        functools.partial(_compress_kernel, normalize=normalize, transposed=transposed),
        out_shape=out_sds,
        grid=(b,),
        in_specs=[pl.BlockSpec((None, t, gw), lambda i: (i, 0, 0)),
                  pl.BlockSpec((8, 2 * kdim), lambda i: (0, 0)),
                  pl.BlockSpec((2 * kdim, CMP_HIDDEN), lambda i: (0, 0)),
                  pl.BlockSpec((CMP_HIDDEN, HEAD_DIM), lambda i: (0, 0)),
                  pl.BlockSpec((1, HEAD_DIM), lambda i: (0, 0))],
        out_specs=pl.BlockSpec(out_block, lambda i: (i, 0, 0, 0)),
        compiler_params=_cparams(("parallel",)),
        name="compress",
    )(x3, posf, w1.astype(BF16), w2.astype(BF16), g.reshape(1, HEAD_DIM))


def _nsa_kernel(bnd_ref, *refs):
    safe = bnd_ref[0, 0] <= MAX_SAFE_SCORE_BOUND

    @pl.when(safe)
    def _():
        _nsa_body(bnd_ref, *refs, bounded=True)

    @pl.when(jnp.logical_not(safe))
    def _():
        _nsa_body(bnd_ref, *refs, bounded=False)


def _nsa_body(bnd_ref, zq_ref, gt_ref, gb_ref, qg_ref, ovt_ref, kc_ref, vct_ref, ks_ref, vst_ref, kw_ref,
              vwt_ref, o_ref, q_scr, sel_scr, m_scr, acc_scr, oc_scr, *, bounded):
    grp = pl.program_id(1)
    qb = pl.program_id(2)
    QB = Q_BLOCK
    ncp = kc_ref.shape[0]
    ns = ovt_ref.shape[0]
    tpos = qb * QB + lax.broadcasted_iota(jnp.int32, (1, QB), 1)

    def lane_tile(x):
        return jnp.concatenate([x] * HG, axis=1)

    zt = zq_ref[...].astype(F32).T
    qg = qg_ref[...] * (HEAD_DIM ** -0.5 * LOG2E)
    heads = []
    for h in range(HG):
        xh = zt[h * HEAD_DIM:(h + 1) * HEAD_DIM, :]
        ms = jnp.mean(xh * xh, axis=0, keepdims=True)
        heads.append(xh * lax.rsqrt(ms + EPS) * qg)
    q_scr[...] = jnp.concatenate(heads, axis=1).astype(BF16)
    q = q_scr[...]

    gates = jax.nn.sigmoid(gt_ref[...].astype(F32).T + gb_ref[...])
    per_grp = HG * 3
    gsel = gates[0:per_grp, :]
    for g2 in range(1, N_KV_GROUPS):
        gsel = jnp.where(grp == g2, gates[g2 * per_grp:(g2 + 1) * per_grp, :], gsel)

    def gate_row(c):
        return jnp.concatenate([gsel[3 * h + c:3 * h + c + 1, :] for h in range(HG)], axis=1)

    cmp_end = lax.broadcasted_iota(jnp.int32, (ncp, 1), 0) * CMP_STRIDE + (CMP_BLOCK - 1)
    keep = -bnd_ref[0, 0] if bounded else 0.0
    cbias = jnp.where(cmp_end <= tpos, keep, NEG)
    s = _dot(kc_ref[...], q) + lane_tile(cbias)
    e = jnp.exp2(s) if bounded else jnp.exp2(s - jnp.max(s, axis=0, keepdims=True))
    cmp_lhs = jnp.concatenate([vct_ref[...], _ones_rows(ncp), ovt_ref[...]], axis=0)
    r = _dot(cmp_lhs, e.astype(BF16))
    inv = jnp.where(lane_tile(tpos) >= CMP_BLOCK - 1, 1.0 / r[HEAD_DIM:HEAD_DIM + 1, :], 0.0)
    ocmp = r[0:HEAD_DIM, :] * inv
    imp_h = r[V_ROWS:V_ROWS + ns, :] * inv
    imp = imp_h[:, 0:QB]
    for h in range(1, HG):
        imp = imp + imp_h[:, h * QB:(h + 1) * QB]

    n_win = (WINDOW + QB) // KEY_TILE
    first_tile = qb * (QB // KEY_TILE) - WINDOW // KEY_TILE
    win_sub = lax.broadcasted_iota(jnp.int32, (n_win * KEY_TILE, 1), 0)
    tiles = [jnp.maximum(first_tile + u, 0) for u in range(n_win)]
    kwin = jnp.concatenate([kw_ref[j] for j in tiles], axis=0)
    vwin = jnp.concatenate([vwt_ref[j] for j in tiles], axis=1)
    kpos = first_tile * KEY_TILE + win_sub
    wbias = jnp.where((kpos <= tpos) & (kpos > tpos - WINDOW) & (kpos >= 0), keep, NEG)
    sw = _dot(kwin, q) + lane_tile(wbias)
    pw = jnp.exp2(sw) if bounded else jnp.exp2(sw - jnp.max(sw, axis=0, keepdims=True))
    ow = _dot(vwin, pw.astype(BF16))
    oc_scr[...] = gate_row(0) * ocmp + (gate_row(2) / ow[HEAD_DIM:HEAD_DIM + 1, :]) * ow[0:HEAD_DIM, :]

    blk = lax.broadcasted_iota(jnp.int32, (ns, 1), 0)
    blk_f = blk.astype(F32)
    cur = jnp.right_shift(tpos, SLC_BLOCK.bit_length() - 1)
    forced = (blk == 0) | (blk == cur) | (blk == cur - 1)
    bvalid = blk <= cur
    score = jnp.where(forced, -jnp.inf, jnp.where(bvalid, imp, NEG))
    sel = jnp.where(forced, 1.0, 0.0)
    for _ in range(max(min(N_SELECT, ns) - 3, 0)):
        mx = jnp.max(score, axis=0, keepdims=True)
        first = jnp.min(jnp.where(score == mx, blk_f, float(ns)), axis=0, keepdims=True)
        pick = blk_f == first
        sel = jnp.where(pick, 1.0, sel)
        score = jnp.where(pick, -jnp.inf, score)
    sel_scr[...] = jnp.where(bvalid, sel, 0.0)

    m_scr[...] = jnp.full(m_scr.shape, NEG, F32)
    acc_scr[...] = jnp.zeros(acc_scr.shape, F32)
    blocks_per_tile = SEL_TILE // SLC_BLOCK
    sel_sub = lax.broadcasted_iota(jnp.int32, (SEL_TILE, 1), 0)

    def sel_body(j, carry):
        rows = [jnp.broadcast_to(sel_scr[pl.ds(j * blocks_per_tile + bi, 1), :], (SLC_BLOCK, QB))
                for bi in range(blocks_per_tile)]
        chosen = jnp.concatenate(rows, axis=0)
        kpos = j * SEL_TILE + sel_sub
        bias = jnp.where((chosen > 0.5) & (kpos <= tpos), keep, NEG)
        st = _dot(ks_ref[j], q_scr[...]) + lane_tile(bias)
        if bounded:
            pt = jnp.exp2(st)
            acc_scr[...] += _dot(vst_ref[j], pt.astype(BF16))
        else:
            m_old = m_scr[...]
            m_new = jnp.maximum(m_old, jnp.max(st, axis=0, keepdims=True))
            alpha = jnp.exp2(m_old - m_new)
            pt = jnp.exp2(st - m_new)
            acc_scr[...] = alpha * acc_scr[...] + _dot(vst_ref[j], pt.astype(BF16))
            m_scr[...] = m_new
        return carry

    lax.fori_loop(0, qb // (SEL_TILE // QB) + 1, sel_body, 0)
    ot = (oc_scr[...]
          + (gate_row(1) / acc_scr[HEAD_DIM:HEAD_DIM + 1, :]) * acc_scr[0:HEAD_DIM, :])
    stacked = jnp.concatenate([ot[:, h * QB:(h + 1) * QB] for h in range(HG)], axis=0)
    o_ref[...] = stacked.T.astype(o_ref.dtype)


def nsa_attention(bound, z3, gate_b, q_g, overlap_t, kc, vct, ks, vst, kw, vwt):
    b, t, _ = z3.shape
    nq = t // Q_BLOCK
    nt = t // KEY_TILE
    qw = HG * HEAD_DIM
    ns, ncp = overlap_t.shape
    gate_blk = (NSA_COLS_PAD // LANES_V7X) - 1
    gb = jnp.zeros((LANES_V7X, 1), F32).at[:N_HEADS * 3, 0].set(gate_b)
    full5 = lambda i, g, q: (i, g, 0, 0, 0)
    lanes = HG * Q_BLOCK
    return pl.pallas_call(
        _nsa_kernel,
        out_shape=jax.ShapeDtypeStruct((b, t, N_HEADS * HEAD_DIM), BF16),
        grid=(b, N_KV_GROUPS, nq),
        in_specs=[pl.BlockSpec(memory_space=pltpu.SMEM),
                  pl.BlockSpec((None, Q_BLOCK, qw), lambda i, g, q: (i, q, g)),
                  pl.BlockSpec((None, Q_BLOCK, LANES_V7X), lambda i, g, q: (i, q, gate_blk)),
                  pl.BlockSpec((LANES_V7X, 1), lambda i, g, q: (0, 0)),
                  pl.BlockSpec((HEAD_DIM, 1), lambda i, g, q: (0, 0)),
                  pl.BlockSpec((ns, ncp), lambda i, g, q: (0, 0)),
                  pl.BlockSpec((None, None, ncp, HEAD_DIM), lambda i, g, q: (i, g, 0, 0)),
                  pl.BlockSpec((None, None, HEAD_DIM, ncp), lambda i, g, q: (i, g, 0, 0)),
                  pl.BlockSpec((None, None, t // SEL_TILE, SEL_TILE, HEAD_DIM), full5),
                  pl.BlockSpec((None, None, t // SEL_TILE, V_ROWS, SEL_TILE), full5),
                  pl.BlockSpec((None, None, nt, KEY_TILE, HEAD_DIM), full5),
                  pl.BlockSpec((None, None, nt, V_ROWS, KEY_TILE), full5)],
        out_specs=pl.BlockSpec((None, Q_BLOCK, qw), lambda i, g, q: (i, q, g)),
        scratch_shapes=[pltpu.VMEM((HEAD_DIM, lanes), BF16),
                        pltpu.VMEM((ns, Q_BLOCK), F32),
                        pltpu.VMEM((1, lanes), F32),
                        pltpu.VMEM((V_ROWS, lanes), F32),
                        pltpu.VMEM((HEAD_DIM, lanes), F32)],
        compiler_params=_cparams(("parallel", "parallel", "arbitrary"), VMEM_LIMIT_V7X),
        name="nsa_attention",
    )(bound.reshape(1, 1), z3, z3, gb, q_g.reshape(HEAD_DIM, 1), overlap_t, kc, vct, ks, vst, kw, vwt)


def _overlap_matrix_t(t):
    ncp = t // CMP_STRIDE
    ns = t // SLC_BLOCK
    c_start = np.arange(ncp) * CMP_STRIDE
    sj = np.arange(ns)
    ov = ((c_start[None, :] < (sj[:, None] + 1) * SLC_BLOCK)
          & (c_start[None, :] + CMP_BLOCK > sj[:, None] * SLC_BLOCK)
          & (c_start[None, :] + CMP_BLOCK <= t))
    return jnp.asarray(ov, dtype=BF16)


def ab_layer(h, b, t, norm_g, w_in, conv_w, conv_b, ret_g, ig_b, fg_b, m_g, w_out):
    n = b * t
    w_in_p = jnp.pad(w_in, ((0, 0), (0, AB_COLS_PAD - AB_COLS))).astype(BF16)
    z = norm_matmul(h, norm_g, w_in_p)
    z3 = z.reshape(b, t, AB_COLS_PAD)
    cos_t, sin_t = rope_tables(t)
    ret = retention(z3, cos_t, sin_t, ret_g)
    ml = mlstm(z3, conv_w, conv_b, ig_b, fg_b, m_g)
    w_out_b = w_out.astype(BF16)
    rw = R_HEADS * R_DV
    return [ret.reshape(n, rw), ml.reshape(n, -1)], w_out_b


def nsa_layer(h, b, t, norm_g, w_in, q_g, k_g, pos_k, pos_v, w1k, w2k, w1v, w2v, gate_b, w_out):
    n = b * t
    G = N_KV_GROUPS
    w_in_p = jnp.pad(w_in, ((0, 0), (0, NSA_COLS_PAD - NSA_COLS))).astype(BF16)
    z = norm_matmul(h, norm_g, w_in_p, out_dtype=BF16)
    kc_in, vc_in, ks, vst, kw, vwt = kv_prep(z, k_g, b, t)
    gw = G * HEAD_DIM
    kc = compress(kc_in.reshape(b, t, gw), pos_k, w1k, w2k, k_g[0], True, False)
    vct = compress(vc_in.reshape(b, t, gw), pos_v, w1v, w2v, k_g[0], False, True)

    def key_tiles(x, kt):
        return x.reshape(b, G, t // kt, kt, HEAD_DIM)

    bound = 1.02 * LOG2E * math.sqrt(HEAD_DIM) * jnp.max(jnp.abs(q_g)) * jnp.max(jnp.abs(k_g))
    args = (bound, z.reshape(b, t, NSA_COLS_PAD), gate_b, q_g, _overlap_matrix_t(t), kc, vct,
            key_tiles(ks, SEL_TILE), vst, key_tiles(kw, KEY_TILE), vwt)
    o = nsa_attention(*args)
    return [o.reshape(n, -1)], w_out.astype(BF16)


def kernel(x, p, ab_norm_g, ab_w_in, ab_conv_w, ab_conv_b, ab_ret_norm_g, ab_ig_b, ab_fg_b, ab_m_norm_g, ab_w_out, nsa_norm_g, nsa_w_in, nsa_q_norm_g, nsa_k_norm_g, nsa_cmp_pos_k, nsa_cmp_pos_v, nsa_cmp_w1k, nsa_cmp_w2k, nsa_cmp_w1v, nsa_cmp_w2v, nsa_gate_b, nsa_w_out, ffn_norm_g, ffn_w_up, ffn_conv_w, ffn_conv_b, ffn_w_down, ple_w, ple_norm_g, ple_gate_norm_g, ple_w_gate):
    b, t, d = x.shape
    n = b * t
    depth = p.shape[0]
    h = x.reshape(n, d)
    p2 = p.reshape(depth, n, -1)
    ffn_w_up_b, ffn_w_down_b = ffn_w_up.astype(BF16), ffn_w_down.astype(BF16)
    ple_w_b, ple_w_gate_b = ple_w.astype(BF16), ple_w_gate.astype(BF16)
    for i in range(depth):
        j = i // 2
        if i % 2 == 0:
            mix = ab_layer(h, b, t, ab_norm_g[j], ab_w_in[j], ab_conv_w[j], ab_conv_b[j], ab_ret_norm_g[j],
                           ab_ig_b[j], ab_fg_b[j], ab_m_norm_g[j], ab_w_out[j])
        else:
            mix = nsa_layer(h, b, t, nsa_norm_g[j], nsa_w_in[j], nsa_q_norm_g[j], nsa_k_norm_g[j],
                            nsa_cmp_pos_k[j], nsa_cmp_pos_v[j], nsa_cmp_w1k[j], nsa_cmp_w2k[j],
                            nsa_cmp_w1v[j], nsa_cmp_w2v[j], nsa_gate_b[j], nsa_w_out[j])
        h = mix_ffn_ple(h, mix, i, ffn_norm_g, ffn_w_up_b, ffn_conv_w, ffn_conv_b, ffn_w_down_b, p2, ple_w_b,
                        ple_norm_g, ple_gate_norm_g, ple_w_gate_b, t)
    return h.reshape(b, t, d)
```

```python
import functools
import math

import numpy as np
import jax
import jax.numpy as jnp
from jax import lax
from jax.experimental import pallas as pl
from jax.experimental.pallas import tpu as pltpu

F32 = jnp.float32
BF16 = jnp.bfloat16

LANES_V7X = 128
BF16_ROWS = 16
VMEM_LIMIT_V7X = 56 * 1024 * 1024

R_HEADS, R_DK, R_DV, R_CHUNK = 4, 64, 128, 128
ROPE_BASE = 10000.0
M_HEADS, M_DK, M_DV, M_CHUNK, M_CONV = 4, 64, 128, 64, 4
AB_SIZES = (R_HEADS * R_DK, R_HEADS * R_DK, R_HEADS * R_DV, R_HEADS * R_DV,
            M_HEADS * M_DK, M_HEADS * M_DK, M_HEADS * M_DV, M_HEADS * M_DV, M_HEADS, M_HEADS)
AB_COLS = sum(AB_SIZES)
AB_COLS_PAD = 3200
N_HEADS, N_KV_GROUPS, HEAD_DIM = 16, 2, 64
HG = N_HEADS // N_KV_GROUPS
CMP_BLOCK, CMP_STRIDE, CMP_HIDDEN = 32, 16, 256
SLC_BLOCK, N_SELECT, WINDOW = 64, 16, 512
Q_BLOCK = 256
NSA_COLS = N_HEADS * HEAD_DIM + 6 * N_KV_GROUPS * HEAD_DIM + N_HEADS * 3
NSA_COLS_PAD = 1920
D_FF = 2816
FFN_CONV = 3
NEG = -1e30
EPS = 1e-6
KEY_TILE = 128
SEL_TILE = 512
V_ROWS = HEAD_DIM + BF16_ROWS
LOG2E = math.log2(math.e)
MAX_SAFE_SCORE_BOUND = 56.0


def _cparams(sem, vmem=None):
    return pltpu.CompilerParams(dimension_semantics=sem, vmem_limit_bytes=vmem)


def _rms(x, g):
    ms = jnp.mean(x * x, axis=-1, keepdims=True)
    return x * lax.rsqrt(ms + EPS) * g


def _dot(a, b):
    return jnp.dot(a, b, preferred_element_type=F32)


def _dot_nt(a, b):
    return lax.dot_general(a, b, (((1,), (1,)), ((), ())), preferred_element_type=F32)


def _dot_f32(a, b):
    return jnp.dot(a, b, preferred_element_type=F32, precision=lax.Precision.HIGHEST)


def _gelu_tanh(x):
    k1 = -2.0 * math.sqrt(2.0 / math.pi) * LOG2E
    return x / (1.0 + jnp.exp2(x * (k1 + (k1 * 0.044715) * (x * x))))


def _norm_matmul_kernel(x_ref, g_ref, w_ref, o_ref):
    xn = _rms(x_ref[...], g_ref[...]).astype(BF16)
    o_ref[...] = _dot(xn, w_ref[...]).astype(o_ref.dtype)


def norm_matmul(x, g, w, tm=1024, out_dtype=F32):
    n, d = x.shape
    nc = w.shape[1]
    return pl.pallas_call(
        _norm_matmul_kernel,
        out_shape=jax.ShapeDtypeStruct((n, nc), out_dtype),
        grid=(n // tm,),
        in_specs=[pl.BlockSpec((tm, d), lambda i: (i, 0)),
                  pl.BlockSpec((1, d), lambda i: (0, 0)),
                  pl.BlockSpec((d, nc), lambda i: (0, 0))],
        out_specs=pl.BlockSpec((tm, nc), lambda i: (i, 0)),
        compiler_params=_cparams(("parallel",), VMEM_LIMIT_V7X),
        name="norm_matmul",
    )(x, g.reshape(1, d), w)


def _rope_table_kernel(inv_ref, cos_ref, sin_ref):
    c = pl.program_id(0)
    rows, width = cos_ref.shape
    pos = (c * rows + lax.broadcasted_iota(jnp.int32, (rows, LANES_V7X), 0)).astype(F32)
    lane = lax.broadcasted_iota(jnp.int32, (rows, LANES_V7X), 1)
    ang = pos * inv_ref[:, 0:LANES_V7X]
    cs = jnp.cos(ang)
    sn = jnp.sin(ang)
    sn = jnp.where(lane % R_DK < R_DK // 2, -sn, sn)
    reps = width // LANES_V7X
    cos_ref[...] = jnp.concatenate([cs] * reps, axis=1)
    sin_ref[...] = jnp.concatenate([sn] * reps, axis=1)


def rope_tables(t):
    half = R_DK // 2
    inv = ROPE_BASE ** (-jnp.arange(half, dtype=F32) / half)
    inv = jnp.tile(inv, 2 * R_HEADS).reshape(1, R_HEADS * R_DK)
    width = R_HEADS * R_DK
    shp = jax.ShapeDtypeStruct((t, width), F32)
    return pl.pallas_call(
        _rope_table_kernel,
        out_shape=(shp, shp),
        grid=(t // R_CHUNK,),
        in_specs=[pl.BlockSpec((1, width), lambda c: (0, 0))],
        out_specs=(pl.BlockSpec((R_CHUNK, width), lambda c: (c, 0)),
                   pl.BlockSpec((R_CHUNK, width), lambda c: (c, 0))),
        compiler_params=_cparams(("parallel",)),
        name="rope_tables",
    )(inv)


def _retention_kernel(cos_ref, sin_ref, q_ref, k_ref, v_ref, g_ref, gain_ref, o_ref, r_ref):
    c = pl.program_id(0)
    nb = q_ref.shape[0]
    L = R_CHUNK

    @pl.when(c == 0)
    def _():
        r_ref[...] = jnp.zeros_like(r_ref)

    cos = cos_ref[...]
    sin = sin_ref[...]
    lane = lax.broadcasted_iota(jnp.int32, cos.shape, 1)
    first_half = lane % R_DK < R_DK // 2
    width = R_HEADS * R_DK

    def rot(x):
        swapped = jnp.where(first_half, pltpu.roll(x, width - R_DK // 2, 1), pltpu.roll(x, R_DK // 2, 1))
        return x * cos + swapped * sin

    gain = gain_ref[...]
    ri = lax.broadcasted_iota(jnp.int32, (L, L), 0)
    ci = lax.broadcasted_iota(jnp.int32, (L, L), 1)
    diff = (ri - ci).astype(F32)
    causal = ri >= ci
    idx = lax.broadcasted_iota(jnp.int32, (L, 1), 0).astype(F32)
    decays = []
    for h in range(R_HEADS):
        log_g = math.log1p(-2.0 ** (-5.0 - h))
        decays.append(dict(
            dmask=jnp.where(causal, jnp.exp(jnp.where(causal, diff, 0.0) * log_g), 0.0),
            q_dec=jnp.exp((idx + 1.0) * log_g), k_dec=jnp.exp((L - 1.0 - idx) * log_g),
            c_dec=math.exp(L * log_g)))

    chains = []
    for bi in range(nb):
        q = rot(q_ref[bi])
        k = rot(k_ref[bi]) * (R_DK ** -0.5)
        for h in range(R_HEADS):
            chains.append(dict(bi=bi, h=h, u=bi * R_HEADS + h,
                               qb=q[:, h * R_DK:(h + 1) * R_DK].astype(BF16),
                               kh=k[:, h * R_DK:(h + 1) * R_DK]))

    for ch in chains:
        ch['rstate'] = r_ref[ch['u']]
        ch['s'] = _dot_nt(ch['qb'], ch['kh'].astype(BF16)) * decays[ch['h']]['dmask']
        ch['qr'] = _dot(ch['qb'], ch['rstate'].astype(BF16))

    for ch in chains:
        d = decays[ch['h']]
        vh = v_ref[ch['bi'], :, ch['h'] * R_DV:(ch['h'] + 1) * R_DV].astype(BF16)
        ch['o'] = _dot(ch['s'].astype(BF16), vh) + ch['qr'] * d['q_dec']
        kd = (ch['kh'] * d['k_dec']).T.astype(BF16)
        r_ref[ch['u']] = d['c_dec'] * ch['rstate'] + _dot(kd, vh)

    for ch in chains:
        sl = slice(ch['h'] * R_DV, (ch['h'] + 1) * R_DV)
        gh = g_ref[ch['bi'], :, sl]
        o_ref[ch['bi'], :, sl] = (_rms(ch['o'], gain[:, sl]) * (gh * jax.nn.sigmoid(gh))).astype(o_ref.dtype)


def retention(z3, cos_t, sin_t, gain):
    b, t, _ = z3.shape
    nc = t // R_CHUNK
    qk_w = R_HEADS * R_DK
    v_w = R_HEADS * R_DV
    return pl.pallas_call(
        _retention_kernel,
        out_shape=jax.ShapeDtypeStruct((b, t, v_w), BF16),
        grid=(nc,),
        in_specs=[pl.BlockSpec((R_CHUNK, qk_w), lambda c: (c, 0)),
                  pl.BlockSpec((R_CHUNK, qk_w), lambda c: (c, 0)),
                  pl.BlockSpec((b, R_CHUNK, qk_w), lambda c: (0, c, 0)),
                  pl.BlockSpec((b, R_CHUNK, qk_w), lambda c: (0, c, 1)),
                  pl.BlockSpec((b, R_CHUNK, v_w), lambda c: (0, c, 1)),
                  pl.BlockSpec((b, R_CHUNK, v_w), lambda c: (0, c, 2)),
                  pl.BlockSpec((1, v_w), lambda c: (0, 0))],
        out_specs=pl.BlockSpec((b, R_CHUNK, v_w), lambda c: (0, c, 0)),
        scratch_shapes=[pltpu.VMEM((b * R_HEADS, R_DK, R_DV), F32)],
        compiler_params=_cparams(("arbitrary",)),
        name="retention",
    )(cos_t, sin_t, z3, z3, z3, z3, gain.reshape(1, v_w))


def _mlstm_kernel(q_ref, k_ref, v_ref, og_ref, gc_ref, cw_ref, cb_ref, bc_ref, br_ref, gain_ref,
                  o_ref, xbuf, c_ref, n_ref, m_ref):
    c = pl.program_id(0)
    nb = q_ref.shape[0]
    L = M_CHUNK
    H = M_HEADS
    qk_w = H * M_DK
    halo = 8

    @pl.when(c == 0)
    def _():
        xbuf[:, 0:halo, :] = jnp.zeros((nb, halo, 2 * qk_w), F32)
        c_ref[...] = jnp.zeros_like(c_ref)
        n_ref[...] = jnp.zeros_like(n_ref)
        m_ref[...] = jnp.zeros_like(m_ref)

    gain = gain_ref[...]
    ri = lax.broadcasted_iota(jnp.int32, (L, L), 0)
    ci = lax.broadcasted_iota(jnp.int32, (L, L), 1)
    causal = ri >= ci
    tril = causal.astype(F32)
    triu = (ri <= ci).astype(F32)

    chains = []
    for bi in range(nb):
        xbuf[bi, halo:halo + L, 0:qk_w] = q_ref[bi]
        xbuf[bi, halo:halo + L, qk_w:2 * qk_w] = k_ref[bi]
        conv = cb_ref[...]
        for j in range(M_CONV):
            conv = conv + xbuf[bi, pl.ds(halo - (M_CONV - 1) + j, L), :] * cw_ref[j:j + 1, :]
        tail = xbuf[bi, L:L + halo, :]
        xbuf[bi, 0:halo, :] = tail
        act = conv * jax.nn.sigmoid(conv)
        q = act[:, 0:qk_w] * (M_DK ** -0.5)
        k = act[:, qk_w:2 * qk_w]
        gblock = gc_ref[bi]
        gc = gblock[:, 0:2 * H] + bc_ref[...]
        gr = gblock.T[0:2 * H, :] + br_ref[...]
        ig_c = gc[:, 0:H]
        ig_r = gr[0:H, :]
        b_c = _dot_f32(tril, jax.nn.log_sigmoid(gc[:, H:2 * H]))
        b_r = _dot_f32(jax.nn.log_sigmoid(gr[H:2 * H, :]), triu)
        for h in range(H):
            chains.append(dict(
                bi=bi, h=h, u=bi * H + h,
                qh=q[:, h * M_DK:(h + 1) * M_DK], kh=k[:, h * M_DK:(h + 1) * M_DK],
                bh=b_c[:, h:h + 1], brow=b_r[h:h + 1, :], irow=ig_r[h:h + 1, :], icol=ig_c[:, h:h + 1]))

    def stack(parts):
        return jnp.concatenate(parts, axis=0)

    def rows(x):
        return jnp.broadcast_to(x, (L, x.shape[1]))

    m_prev_u = [m_ref[ch['u']][:, 0:1] for ch in chains]
    b_last_u = [ch['bh'][L - 1:L, :] for ch in chains]
    bh = stack([ch['bh'] for ch in chains])
    icol = stack([ch['icol'] for ch in chains])
    brow = stack([rows(ch['brow']) for ch in chains])
    irow = stack([rows(ch['irow']) for ch in chains])
    m_prev = stack([rows(m) for m in m_prev_u])
    b_last = stack([rows(x) for x in b_last_u])
    causal_all = stack([causal] * len(chains))
    qs = stack([ch['qh'] for ch in chains])
    ks = stack([ch['kh'] for ch in chains])
    qb = qs.astype(BF16)
    kb = ks.astype(BF16)
    cstates = [c_ref[ch['u']] for ch in chains]
    nstates = [n_ref[ch['u']] for ch in chains]

    def chain_rows(x, i):
        return x[i * L:(i + 1) * L]

    s_raw = stack([_dot_nt(chain_rows(qb, i), chain_rows(kb, i)) for i in range(len(chains))])
    qc = stack([_dot(chain_rows(qb, i), cstates[i].astype(BF16)) for i in range(len(chains))])

    glog = jnp.where(causal_all, irow - brow, NEG)
    c = jnp.maximum(m_prev, jnp.max(glog, axis=-1, keepdims=True))
    m_t = bh + c
    s = s_raw * jnp.exp(glog - c)
    w_inter = jnp.exp(m_prev - c)
    wlog = b_last - bh + icol
    m_new_u = [jnp.maximum(b_last_u[i] + m_prev_u[i], jnp.max(chain_rows(wlog, i), axis=0, keepdims=True))
               for i in range(len(chains))]
    m_new = stack([rows(m) for m in m_new_u])
    wk = ks * jnp.exp(wlog - m_new)

    sb = s.astype(BF16)
    vhs = [v_ref[ch['bi'], :, ch['h'] * M_DV:(ch['h'] + 1) * M_DV].astype(BF16) for ch in chains]
    sv = stack([_dot(chain_rows(sb, i), vhs[i]) for i in range(len(chains))])
    kv = [_dot(chain_rows(wk, i).T.astype(BF16), vhs[i]) for i in range(len(chains))]

    qn = jnp.sum(qs * stack([rows(n) for n in nstates]), axis=-1, keepdims=True)
    den = jnp.sum(s, axis=-1, keepdims=True) + w_inter * qn
    hh = (sv + w_inter * qc) / jnp.maximum(jnp.abs(den), jnp.exp(-m_t))
    og = stack([og_ref[ch['bi'], :, ch['h'] * M_DV:(ch['h'] + 1) * M_DV] for ch in chains])
    gains = stack([rows(gain[:, ch['h'] * M_DV:(ch['h'] + 1) * M_DV]) for ch in chains])
    out = jax.nn.sigmoid(og) * _rms(hh, gains)

    for i, ch in enumerate(chains):
        u = ch['u']
        decay = jnp.exp(b_last_u[i] + m_prev_u[i] - m_new_u[i])
        c_ref[u] = decay * cstates[i] + kv[i]
        n_ref[u] = decay * nstates[i] + jnp.sum(chain_rows(wk, i), axis=0, keepdims=True)
        m_ref[u] = jnp.broadcast_to(m_new_u[i], (1, LANES_V7X))
        o_ref[ch['bi'], :, ch['h'] * M_DV:(ch['h'] + 1) * M_DV] = chain_rows(out, i).astype(o_ref.dtype)


def mlstm(z3, conv_w, conv_b, ig_b, fg_b, gain):
    b, t, _ = z3.shape
    nc = t // M_CHUNK
    H = M_HEADS
    qk_w = H * M_DK
    v_w = H * M_DV
    bias = jnp.concatenate([ig_b, fg_b])
    gate_blk = AB_COLS_PAD // LANES_V7X - 1
    return pl.pallas_call(
        _mlstm_kernel,
        out_shape=jax.ShapeDtypeStruct((b, t, v_w), BF16),
        grid=(nc,),
        in_specs=[pl.BlockSpec((b, M_CHUNK, qk_w), lambda c: (0, c, 6)),
                  pl.BlockSpec((b, M_CHUNK, qk_w), lambda c: (0, c, 7)),
                  pl.BlockSpec((b, M_CHUNK, v_w), lambda c: (0, c, 4)),
                  pl.BlockSpec((b, M_CHUNK, v_w), lambda c: (0, c, 5)),
                  pl.BlockSpec((b, M_CHUNK, LANES_V7X), lambda c: (0, c, gate_blk)),
                  pl.BlockSpec((M_CONV, 2 * qk_w), lambda c: (0, 0)),
                  pl.BlockSpec((1, 2 * qk_w), lambda c: (0, 0)),
                  pl.BlockSpec((1, 2 * H), lambda c: (0, 0)),
                  pl.BlockSpec((2 * H, 1), lambda c: (0, 0)),
                  pl.BlockSpec((1, v_w), lambda c: (0, 0))],
        out_specs=pl.BlockSpec((b, M_CHUNK, v_w), lambda c: (0, c, 0)),
        scratch_shapes=[pltpu.VMEM((b, 8 + M_CHUNK, 2 * qk_w), F32),
                        pltpu.VMEM((b * H, M_DK, M_DV), F32),
                        pltpu.VMEM((b * H, 1, M_DK), F32),
                        pltpu.VMEM((b * H, 1, LANES_V7X), F32)],
        compiler_params=_cparams(("arbitrary",)),
        name="mlstm",
    )(z3, z3, z3, z3, z3, conv_w, conv_b.reshape(1, -1), bias.reshape(1, -1), bias.reshape(-1, 1),
      gain.reshape(1, v_w))


def _mix_ffn_ple_kernel(*refs, n_mix, tm, seq, tf):
    hp_ref, h_ref = refs[0], refs[1]
    mix_refs = refs[2:2 + 2 * n_mix]
    (wo_ref, g_ref, wa_ref, wb_ref, cw_ref, cb_ref, wd_ref, p_ref, wp_ref, ng_ref, gg_ref, wg_ref,
     o_ref, xn_ref, a_ref) = refs[2 + 2 * n_mix:]
    i = pl.program_id(0)
    halo = 8
    g = g_ref[...]
    m_prev = jnp.concatenate([mix_refs[2 * k][...] for k in range(n_mix)], axis=1)
    m_tile = jnp.concatenate([mix_refs[2 * k + 1][...] for k in range(n_mix)], axis=1)
    x = h_ref[...] + _dot(m_tile, wo_ref[...])
    xp = hp_ref[...] + _dot(m_prev, wo_ref[...])[BF16_ROWS - halo:, :]
    xn_ref[halo:halo + tm, :] = _rms(x, g).astype(BF16)
    keep = ((i * tm) % seq != 0).astype(F32)
    xn_ref[0:halo, :] = (_rms(xp, g) * keep).astype(BF16)
    n_chunks = D_FF // tf

    def up_proj(c):
        cs = slice(c * tf, (c + 1) * tf)
        a_ref[c % 2] = _dot(xn_ref[...], wa_ref[:, cs])
        return _dot(xn_ref[halo:halo + tm, :], wb_ref[:, cs])

    h2 = x
    bgate = up_proj(0)
    for c in range(n_chunks):
        cs = slice(c * tf, (c + 1) * tf)
        bgate_next = up_proj(c + 1) if c + 1 < n_chunks else None
        conv = cb_ref[:, cs]
        for t in range(FFN_CONV):
            conv = conv + a_ref[c % 2, pl.ds(halo - (FFN_CONV - 1) + t, tm), :] * cw_ref[t:t + 1, cs]
        act = (_gelu_tanh(conv) * bgate).astype(BF16)
        h2 = h2 + _dot(act, wd_ref[cs, :])
        bgate = bgate_next
    e = _rms(_dot(p_ref[...].astype(BF16), wp_ref[...]), ng_ref[...])
    gate = jax.nn.sigmoid(_dot(_rms(h2, gg_ref[...]).astype(BF16), wg_ref[...]))
    o_ref[...] = h2 + gate * e


def mix_ffn_ple(h, mix, li, g, w_up, conv_w, conv_b, w_down, p, wp, norm_g, gate_norm_g, wg, seq, tm=512, tf=2816):
    mix_outs, w_out = mix
    n, d = h.shape
    pd = p.shape[-1]
    hb = tm // 8
    const = lambda i: (0, 0)
    layer = lambda i: (li, 0, 0)
    prev8 = lambda i: (jnp.maximum(i * hb - 1, 0), 0)
    resident = dict(pipeline_mode=pl.Buffered(1))
    prev16 = lambda i: (jnp.maximum(i * (tm // BF16_ROWS) - 1, 0), 0)
    mix_specs, mix_args = [], []
    for m in mix_outs:
        mix_specs += [pl.BlockSpec((BF16_ROWS, m.shape[1]), prev16),
                      pl.BlockSpec((tm, m.shape[1]), lambda i: (i, 0))]
        mix_args += [m, m]
    mix_specs.append(pl.BlockSpec(w_out.shape, const, **resident))
    mix_args.append(w_out)
    return pl.pallas_call(
        functools.partial(_mix_ffn_ple_kernel, n_mix=len(mix_outs), tm=tm, seq=seq, tf=tf),
        out_shape=jax.ShapeDtypeStruct((n, d), F32),
        grid=(n // tm,),
        in_specs=[pl.BlockSpec((8, d), prev8),
                  pl.BlockSpec((tm, d), lambda i: (i, 0))] + mix_specs + [
                  pl.BlockSpec((None, 1, d), layer),
                  pl.BlockSpec((None, d, D_FF), lambda i: (li, 0, 0), **resident),
                  pl.BlockSpec((None, d, D_FF), lambda i: (li, 0, 1), **resident),
                  pl.BlockSpec((None, FFN_CONV, D_FF), layer),
                  pl.BlockSpec((None, 1, D_FF), layer),
                  pl.BlockSpec((None, D_FF, d), layer, **resident),
                  pl.BlockSpec((None, tm, pd), lambda i: (li, i, 0)),
                  pl.BlockSpec((None, pd, d), layer, **resident),
                  pl.BlockSpec((None, 1, d), layer),
                  pl.BlockSpec((None, 1, d), layer),
                  pl.BlockSpec((None, d, d), layer, **resident)],
        out_specs=pl.BlockSpec((tm, d), lambda i: (i, 0)),
        scratch_shapes=[pltpu.VMEM((8 + tm, d), BF16),
                        pltpu.VMEM((min(2, D_FF // tf), 8 + tm, tf), F32)],
        compiler_params=_cparams(("parallel",), VMEM_LIMIT_V7X),
        name="mix_ffn_ple",
    )(h, h, *mix_args, g[:, None, :], w_up, w_up, conv_w, conv_b[:, None, :], w_down,
      p, wp, norm_g[:, None, :], gate_norm_g[:, None, :], wg)


def _group_rms(x, g):
    lane = lax.broadcasted_iota(jnp.int32, x.shape, 1)
    x2 = x * x
    ms = jnp.zeros_like(x)
    for grp in range(N_KV_GROUPS):
        in_grp = (lane >= grp * HEAD_DIM) & (lane < (grp + 1) * HEAD_DIM)
        tot = jnp.sum(jnp.where(in_grp, x2, 0.0), axis=-1, keepdims=True)
        ms = jnp.where(in_grp, tot * (1.0 / HEAD_DIM), ms)
    return x * lax.rsqrt(ms + EPS) * g


def _ones_rows(width):
    return (lax.broadcasted_iota(jnp.int32, (BF16_ROWS, width), 0) == 0).astype(BF16)


def _kv_prep_kernel(c_ref, s_ref, w_ref, gs_ref, gw_ref, kc_ref, vc_ref, ks_ref, vs_ref, kw_ref, vw_ref):
    gw = N_KV_GROUPS * HEAD_DIM
    cc = c_ref[...]
    ss = s_ref[...]
    ww = w_ref[...]
    kc_ref[...] = cc[:, 0:gw]
    vc_ref[...] = cc[:, gw:2 * gw]
    ks = _group_rms(ss[:, 0:gw], gs_ref[...]).astype(BF16)
    kw = _group_rms(ww[:, 0:gw], gw_ref[...]).astype(BF16)
    vst = ss[:, gw:2 * gw].T
    vwt = ww[:, gw:2 * gw].T
    for g in range(N_KV_GROUPS):
        lanes = slice(g * HEAD_DIM, (g + 1) * HEAD_DIM)
        ks_ref[g] = ks[:, lanes]
        kw_ref[g] = kw[:, lanes]
        vs_ref[g] = jnp.concatenate([vst[lanes, :].astype(BF16), _ones_rows(SEL_TILE)], axis=0)
        for u in range(SEL_TILE // KEY_TILE):
            vw_ref[g, u] = jnp.concatenate([vwt[lanes, u * KEY_TILE:(u + 1) * KEY_TILE].astype(BF16),
                                            _ones_rows(KEY_TILE)], axis=0)


def kv_prep(z, k_g, b, t):
    n = z.shape[0]
    G = N_KV_GROUPS
    gw = G * HEAD_DIM
    base = N_HEADS * HEAD_DIM // (2 * gw)
    tm = SEL_TILE
    nst = t // tm
    sub = SEL_TILE // KEY_TILE
    row = lambda i, j: i * nst + j
    flat = jax.ShapeDtypeStruct((n, gw), F32)
    keys = jax.ShapeDtypeStruct((b, G, t, HEAD_DIM), BF16)
    flat_spec = pl.BlockSpec((tm, gw), lambda i, j: (row(i, j), 0))
    key_spec = pl.BlockSpec((None, G, tm, HEAD_DIM), lambda i, j: (i, 0, j, 0))
    return pl.pallas_call(
        _kv_prep_kernel,
        out_shape=(flat, flat, keys, jax.ShapeDtypeStruct((b, G, nst, V_ROWS, tm), BF16),
                   keys, jax.ShapeDtypeStruct((b, G, nst * sub, V_ROWS, KEY_TILE), BF16)),
        grid=(b, nst),
        in_specs=[pl.BlockSpec((tm, 2 * gw), lambda i, j: (row(i, j), base)),
                  pl.BlockSpec((tm, 2 * gw), lambda i, j: (row(i, j), base + 1)),
                  pl.BlockSpec((tm, 2 * gw), lambda i, j: (row(i, j), base + 2)),
                  pl.BlockSpec((1, gw), lambda i, j: (0, 0)),
                  pl.BlockSpec((1, gw), lambda i, j: (0, 0))],
        out_specs=(flat_spec, flat_spec, key_spec,
                   pl.BlockSpec((None, G, None, V_ROWS, tm), lambda i, j: (i, 0, j, 0, 0)),
                   key_spec,
                   pl.BlockSpec((None, G, sub, V_ROWS, KEY_TILE), lambda i, j: (i, 0, j, 0, 0))),
        compiler_params=_cparams(("parallel", "parallel")),
        name="kv_prep",
    )(z, z, z, jnp.tile(k_g[1], G).reshape(1, gw), jnp.tile(k_g[2], G).reshape(1, gw))


def _compress_kernel(x_ref, pos_ref, w1_ref, w2_ref, g_ref, o_ref, *, normalize, transposed):
    G = N_KV_GROUPS
    ncb = x_ref.shape[0] // CMP_STRIDE
    half = CMP_STRIDE * HEAD_DIM
    u = [jnp.zeros((ncb, CMP_HIDDEN), F32) for _ in range(G)]
    v = [jnp.zeros((ncb, CMP_HIDDEN), F32) for _ in range(G)]
    for r in range(CMP_STRIDE):
        xr = x_ref[pl.ds(r, ncb, stride=CMP_STRIDE), :].astype(BF16)
        wa = w1_ref[r * HEAD_DIM:(r + 1) * HEAD_DIM, :]
        wb = w1_ref[half + r * HEAD_DIM:half + (r + 1) * HEAD_DIM, :]
        for grp in range(G):
            xg = xr[:, grp * HEAD_DIM:(grp + 1) * HEAD_DIM]
            u[grp] = u[grp] + _dot(xg, wa)
            v[grp] = v[grp] + _dot(xg, wb)
    posc = _dot(pos_ref[...], w1_ref[...])[0:1, :]
    outs = []
    for grp in range(G):
        hid = u[grp] + pltpu.roll(v[grp], ncb - 1, 0) + posc
        out = _dot(jax.nn.gelu(hid).astype(BF16), w2_ref[...])
        outs.append(_rms(out, g_ref[...]) if normalize else out)
    if transposed:
        both = jnp.concatenate(outs, axis=1).T
        for grp in range(G):
            o_ref[grp] = both[grp * HEAD_DIM:(grp + 1) * HEAD_DIM, :].astype(o_ref.dtype)
    else:
        for grp in range(G):
            o_ref[grp] = outs[grp].astype(o_ref.dtype)


def compress(x3, pos, w1, w2, g, normalize, transposed):
    b, t, gw = x3.shape
    G = N_KV_GROUPS
    ncb = t // CMP_STRIDE
    kdim = CMP_STRIDE * HEAD_DIM
    posf = jnp.broadcast_to(pos.reshape(1, -1), (8, 2 * kdim)).astype(BF16)
    out_sds = (jax.ShapeDtypeStruct((b, G, HEAD_DIM, ncb), BF16) if transposed
               else jax.ShapeDtypeStruct((b, G, ncb, HEAD_DIM), BF16))
    out_block = (None, G, HEAD_DIM, ncb) if transposed else (None, G, ncb, HEAD_DIM)
    return pl.pallas_call(
        functools.partial(_compress_kernel, normalize=normalize, transposed=transposed),
        out_shape=out_sds,
        grid=(b,),
        in_specs=[pl.BlockSpec((None, t, gw), lambda i: (i, 0, 0)),
                  pl.BlockSpec((8, 2 * kdim), lambda i: (0, 0)),
                  pl.BlockSpec((2 * kdim, CMP_HIDDEN), lambda i: (0, 0)),
                  pl.BlockSpec((CMP_HIDDEN, HEAD_DIM), lambda i: (0, 0)),
                  pl.BlockSpec((1, HEAD_DIM), lambda i: (0, 0))],
        out_specs=pl.BlockSpec(out_block, lambda i: (i, 0, 0, 0)),
        compiler_params=_cparams(("parallel",)),
        name="compress",
    )(x3, posf, w1.astype(BF16), w2.astype(BF16), g.reshape(1, HEAD_DIM))


def _nsa_kernel(bnd_ref, *refs):
    safe = bnd_ref[0, 0] <= MAX_SAFE_SCORE_BOUND

    @pl.when(safe)
    def _():
        _nsa_body(bnd_ref, *refs, bounded=True)

    @pl.when(jnp.logical_not(safe))
    def _():
        _nsa_body(bnd_ref, *refs, bounded=False)


def _nsa_body(bnd_ref, zq_ref, gt_ref, gb_ref, qg_ref, ovt_ref, kc_ref, vct_ref, ks_ref, vst_ref, kw_ref,
              vwt_ref, o_ref, q_scr, sel_scr, m_scr, acc_scr, oc_scr, *, bounded):
    grp = pl.program_id(1)
    qb = pl.program_id(2)
    QB = Q_BLOCK
    ncp = kc_ref.shape[0]
    ns = ovt_ref.shape[0]
    tpos = qb * QB + lax.broadcasted_iota(jnp.int32, (1, QB), 1)

    def lane_tile(x):
        return jnp.concatenate([x] * HG, axis=1)

    zt = zq_ref[...].T
    qg = qg_ref[...] * (HEAD_DIM ** -0.5 * LOG2E)
    heads = []
    for h in range(HG):
        xh = zt[h * HEAD_DIM:(h + 1) * HEAD_DIM, :]
        ms = jnp.mean(xh * xh, axis=0, keepdims=True)
        heads.append(xh * lax.rsqrt(ms + EPS) * qg)
    q_scr[...] = jnp.concatenate(heads, axis=1).astype(BF16)
    q = q_scr[...]

    gates = jax.nn.sigmoid(gt_ref[...].T + gb_ref[...])
    per_grp = HG * 3
    gsel = gates[0:per_grp, :]
    for g2 in range(1, N_KV_GROUPS):
        gsel = jnp.where(grp == g2, gates[g2 * per_grp:(g2 + 1) * per_grp, :], gsel)

    def gate_row(c):
        return jnp.concatenate([gsel[3 * h + c:3 * h + c + 1, :] for h in range(HG)], axis=1)

    cmp_end = lax.broadcasted_iota(jnp.int32, (ncp, 1), 0) * CMP_STRIDE + (CMP_BLOCK - 1)
    keep = -bnd_ref[0, 0] if bounded else 0.0
    cbias = jnp.where(cmp_end <= tpos, keep, NEG)
    s = _dot(kc_ref[...], q) + lane_tile(cbias)
    e = jnp.exp2(s) if bounded else jnp.exp2(s - jnp.max(s, axis=0, keepdims=True))
    cmp_lhs = jnp.concatenate([vct_ref[...], _ones_rows(ncp), ovt_ref[...]], axis=0)
    r = _dot(cmp_lhs, e.astype(BF16))
    inv = jnp.where(lane_tile(tpos) >= CMP_BLOCK - 1, 1.0 / r[HEAD_DIM:HEAD_DIM + 1, :], 0.0)
    ocmp = r[0:HEAD_DIM, :] * inv
    imp_h = r[V_ROWS:V_ROWS + ns, :] * inv
    imp = imp_h[:, 0:QB]
    for h in range(1, HG):
        imp = imp + imp_h[:, h * QB:(h + 1) * QB]

    n_win = (WINDOW + QB) // KEY_TILE
    first_tile = qb * (QB // KEY_TILE) - WINDOW // KEY_TILE
    win_sub = lax.broadcasted_iota(jnp.int32, (n_win * KEY_TILE, 1), 0)
    tiles = [jnp.maximum(first_tile + u, 0) for u in range(n_win)]
    kwin = jnp.concatenate([kw_ref[j] for j in tiles], axis=0)
    vwin = jnp.concatenate([vwt_ref[j] for j in tiles], axis=1)
    kpos = first_tile * KEY_TILE + win_sub
    wbias = jnp.where((kpos <= tpos) & (kpos > tpos - WINDOW) & (kpos >= 0), keep, NEG)
    sw = _dot(kwin, q) + lane_tile(wbias)
    pw = jnp.exp2(sw) if bounded else jnp.exp2(sw - jnp.max(sw, axis=0, keepdims=True))
    ow = _dot(vwin, pw.astype(BF16))
    oc_scr[...] = gate_row(0) * ocmp + (gate_row(2) / ow[HEAD_DIM:HEAD_DIM + 1, :]) * ow[0:HEAD_DIM, :]

    blk = lax.broadcasted_iota(jnp.int32, (ns, 1), 0)
    blk_f = blk.astype(F32)
    cur = jnp.right_shift(tpos, SLC_BLOCK.bit_length() - 1)
    forced = (blk == 0) | (blk == cur) | (blk == cur - 1)
    bvalid = blk <= cur
    score = jnp.where(forced, -jnp.inf, jnp.where(bvalid, imp, NEG))
    sel = jnp.where(forced, 1.0, 0.0)
    for _ in range(max(min(N_SELECT, ns) - 3, 0)):
        mx = jnp.max(score, axis=0, keepdims=True)
        first = jnp.min(jnp.where(score == mx, blk_f, float(ns)), axis=0, keepdims=True)
        pick = blk_f == first
        sel = jnp.where(pick, 1.0, sel)
        score = jnp.where(pick, -jnp.inf, score)
    sel_scr[...] = jnp.where(bvalid, sel, 0.0)

    m_scr[...] = jnp.full(m_scr.shape, NEG, F32)
    acc_scr[...] = jnp.zeros(acc_scr.shape, F32)
    blocks_per_tile = SEL_TILE // SLC_BLOCK

    def sel_step(first, width):
        keys = width * SEL_TILE
        rows = [jnp.broadcast_to(sel_scr[pl.ds(first * blocks_per_tile + bi, 1), :], (SLC_BLOCK, QB))
                for bi in range(width * blocks_per_tile)]
        chosen = jnp.concatenate(rows, axis=0)
        kpos = first * SEL_TILE + lax.broadcasted_iota(jnp.int32, (keys, 1), 0)
        bias = jnp.where((chosen > 0.5) & (kpos <= tpos), keep, NEG)
        k = jnp.concatenate([ks_ref[first + w] for w in range(width)], axis=0)
        vt = jnp.concatenate([vst_ref[first + w] for w in range(width)], axis=1)
        st = _dot(k, q_scr[...]) + lane_tile(bias)
        if bounded:
            pt = jnp.exp2(st)
            acc_scr[...] += _dot(vt, pt.astype(BF16))
        else:
            m_old = m_scr[...]
            m_new = jnp.maximum(m_old, jnp.max(st, axis=0, keepdims=True))
            alpha = jnp.exp2(m_old - m_new)
            pt = jnp.exp2(st - m_new)
            acc_scr[...] = alpha * acc_scr[...] + _dot(vt, pt.astype(BF16))
            m_scr[...] = m_new

    n_tiles = qb // (SEL_TILE // QB) + 1

    def sel_pair(i, carry):
        sel_step(2 * i, 2)
        return carry

    lax.fori_loop(0, n_tiles // 2, sel_pair, 0)

    @pl.when(n_tiles % 2 == 1)
    def _():
        sel_step(n_tiles - 1, 1)
    ot = (oc_scr[...]
          + (gate_row(1) / acc_scr[HEAD_DIM:HEAD_DIM + 1, :]) * acc_scr[0:HEAD_DIM, :])
    stacked = jnp.concatenate([ot[:, h * QB:(h + 1) * QB] for h in range(HG)], axis=0)
    o_ref[...] = stacked.T.astype(o_ref.dtype)


def nsa_attention(bound, z3, gate_b, q_g, overlap_t, kc, vct, ks, vst, kw, vwt):
    b, t, _ = z3.shape
    nq = t // Q_BLOCK
    nt = t // KEY_TILE
    qw = HG * HEAD_DIM
    ns, ncp = overlap_t.shape
    gate_blk = (NSA_COLS_PAD // LANES_V7X) - 1
    gb = jnp.zeros((LANES_V7X, 1), F32).at[:N_HEADS * 3, 0].set(gate_b)
    full5 = lambda i, g, q: (i, g, 0, 0, 0)
    lanes = HG * Q_BLOCK
    return pl.pallas_call(
        _nsa_kernel,
        out_shape=jax.ShapeDtypeStruct((b, t, N_HEADS * HEAD_DIM), BF16),
        grid=(b, N_KV_GROUPS, nq),
        in_specs=[pl.BlockSpec(memory_space=pltpu.SMEM),
                  pl.BlockSpec((None, Q_BLOCK, qw), lambda i, g, q: (i, q, g)),
                  pl.BlockSpec((None, Q_BLOCK, LANES_V7X), lambda i, g, q: (i, q, gate_blk)),
                  pl.BlockSpec((LANES_V7X, 1), lambda i, g, q: (0, 0)),
                  pl.BlockSpec((HEAD_DIM, 1), lambda i, g, q: (0, 0)),
                  pl.BlockSpec((ns, ncp), lambda i, g, q: (0, 0)),
                  pl.BlockSpec((None, None, ncp, HEAD_DIM), lambda i, g, q: (i, g, 0, 0)),
                  pl.BlockSpec((None, None, HEAD_DIM, ncp), lambda i, g, q: (i, g, 0, 0)),
                  pl.BlockSpec((None, None, t // SEL_TILE, SEL_TILE, HEAD_DIM), full5),
                  pl.BlockSpec((None, None, t // SEL_TILE, V_ROWS, SEL_TILE), full5),
                  pl.BlockSpec((None, None, nt, KEY_TILE, HEAD_DIM), full5),
                  pl.BlockSpec((None, None, nt, V_ROWS, KEY_TILE), full5)],
        out_specs=pl.BlockSpec((None, Q_BLOCK, qw), lambda i, g, q: (i, q, g)),
        scratch_shapes=[pltpu.VMEM((HEAD_DIM, lanes), BF16),
                        pltpu.VMEM((ns, Q_BLOCK), F32),
                        pltpu.VMEM((1, lanes), F32),
                        pltpu.VMEM((V_ROWS, lanes), F32),
                        pltpu.VMEM((HEAD_DIM, lanes), F32)],
        compiler_params=_cparams(("parallel", "parallel", "arbitrary"), VMEM_LIMIT_V7X),
        name="nsa_attention",
    )(bound.reshape(1, 1), z3, z3, gb, q_g.reshape(HEAD_DIM, 1), overlap_t, kc, vct, ks, vst, kw, vwt)


def _overlap_matrix_t(t):
    ncp = t // CMP_STRIDE
    ns = t // SLC_BLOCK
    c_start = np.arange(ncp) * CMP_STRIDE
    sj = np.arange(ns)
    ov = ((c_start[None, :] < (sj[:, None] + 1) * SLC_BLOCK)
          & (c_start[None, :] + CMP_BLOCK > sj[:, None] * SLC_BLOCK)
          & (c_start[None, :] + CMP_BLOCK <= t))
    return jnp.asarray(ov, dtype=BF16)


def ab_layer(h, b, t, norm_g, w_in, conv_w, conv_b, ret_g, ig_b, fg_b, m_g, w_out):
    n = b * t
    w_in_p = jnp.pad(w_in, ((0, 0), (0, AB_COLS_PAD - AB_COLS))).astype(BF16)
    z = norm_matmul(h, norm_g, w_in_p)
    z3 = z.reshape(b, t, AB_COLS_PAD)
    cos_t, sin_t = rope_tables(t)
    ret = retention(z3, cos_t, sin_t, ret_g)
    ml = mlstm(z3, conv_w, conv_b, ig_b, fg_b, m_g)
    w_out_b = w_out.astype(BF16)
    rw = R_HEADS * R_DV
    return [ret.reshape(n, rw), ml.reshape(n, -1)], w_out_b


def nsa_layer(h, b, t, norm_g, w_in, q_g, k_g, pos_k, pos_v, w1k, w2k, w1v, w2v, gate_b, w_out):
    n = b * t
    G = N_KV_GROUPS
    w_in_p = jnp.pad(w_in, ((0, 0), (0, NSA_COLS_PAD - NSA_COLS))).astype(BF16)
    z = norm_matmul(h, norm_g, w_in_p)
    kc_in, vc_in, ks, vst, kw, vwt = kv_prep(z, k_g, b, t)
    gw = G * HEAD_DIM
    kc = compress(kc_in.reshape(b, t, gw), pos_k, w1k, w2k, k_g[0], True, False)
    vct = compress(vc_in.reshape(b, t, gw), pos_v, w1v, w2v, k_g[0], False, True)

    def key_tiles(x, kt):
        return x.reshape(b, G, t // kt, kt, HEAD_DIM)

    bound = 1.02 * LOG2E * math.sqrt(HEAD_DIM) * jnp.max(jnp.abs(q_g)) * jnp.max(jnp.abs(k_g))
    args = (bound, z.reshape(b, t, NSA_COLS_PAD), gate_b, q_g, _overlap_matrix_t(t), kc, vct,
            key_tiles(ks, SEL_TILE), vst, key_tiles(kw, KEY_TILE), vwt)
    o = nsa_attention(*args)
    return [o.reshape(n, -1)], w_out.astype(BF16)


def kernel(x, p, ab_norm_g, ab_w_in, ab_conv_w, ab_conv_b, ab_ret_norm_g, ab_ig_b, ab_fg_b, ab_m_norm_g, ab_w_out, nsa_norm_g, nsa_w_in, nsa_q_norm_g, nsa_k_norm_g, nsa_cmp_pos_k, nsa_cmp_pos_v, nsa_cmp_w1k, nsa_cmp_w2k, nsa_cmp_w1v, nsa_cmp_w2v, nsa_gate_b, nsa_w_out, ffn_norm_g, ffn_w_up, ffn_conv_w, ffn_conv_b, ffn_w_down, ple_w, ple_norm_g, ple_gate_norm_g, ple_w_gate):
    b, t, d = x.shape
    n = b * t
    depth = p.shape[0]
    h = x.reshape(n, d)
    p2 = p.reshape(depth, n, -1)
    ffn_w_up_b, ffn_w_down_b = ffn_w_up.astype(BF16), ffn_w_down.astype(BF16)
    ple_w_b, ple_w_gate_b = ple_w.astype(BF16), ple_w_gate.astype(BF16)
    for i in range(depth):
        j = i // 2
        if i % 2 == 0:
            mix = ab_layer(h, b, t, ab_norm_g[j], ab_w_in[j], ab_conv_w[j], ab_conv_b[j], ab_ret_norm_g[j],
                           ab_ig_b[j], ab_fg_b[j], ab_m_norm_g[j], ab_w_out[j])
        else:
            mix = nsa_layer(h, b, t, nsa_norm_g[j], nsa_w_in[j], nsa_q_norm_g[j], nsa_k_norm_g[j],
                            nsa_cmp_pos_k[j], nsa_cmp_pos_v[j], nsa_cmp_w1k[j], nsa_cmp_w2k[j],
                            nsa_cmp_w1v[j], nsa_cmp_w2v[j], nsa_gate_b[j], nsa_w_out[j])
        h = mix_ffn_ple(h, mix, i, ffn_norm_g, ffn_w_up_b, ffn_conv_w, ffn_conv_b, ffn_w_down_b, p2, ple_w_b,
                        ple_norm_g, ple_gate_norm_g, ple_w_gate_b, t)
    return h.reshape(b, t, d)
```

```python
import functools
import math

import numpy as np
import jax
import jax.numpy as jnp
from jax import lax
from jax.experimental import pallas as pl
from jax.experimental.pallas import tpu as pltpu

F32 = jnp.float32
BF16 = jnp.bfloat16

LANES_V7X = 128
BF16_ROWS = 16
VMEM_LIMIT_V7X = 56 * 1024 * 1024

R_HEADS, R_DK, R_DV, R_CHUNK = 4, 64, 128, 128
ROPE_BASE = 10000.0
M_HEADS, M_DK, M_DV, M_CHUNK, M_CONV = 4, 64, 128, 64, 4
AB_SIZES = (R_HEADS * R_DK, R_HEADS * R_DK, R_HEADS * R_DV, R_HEADS * R_DV,
            M_HEADS * M_DK, M_HEADS * M_DK, M_HEADS * M_DV, M_HEADS * M_DV, M_HEADS, M_HEADS)
AB_COLS = sum(AB_SIZES)
AB_COLS_PAD = 3200
N_HEADS, N_KV_GROUPS, HEAD_DIM = 16, 2, 64
HG = N_HEADS // N_KV_GROUPS
CMP_BLOCK, CMP_STRIDE, CMP_HIDDEN = 32, 16, 256
SLC_BLOCK, N_SELECT, WINDOW = 64, 16, 512
Q_BLOCK = 256
NSA_COLS = N_HEADS * HEAD_DIM + 6 * N_KV_GROUPS * HEAD_DIM + N_HEADS * 3
NSA_COLS_PAD = 1920
D_FF = 2816
FFN_CONV = 3
NEG = -1e30
EPS = 1e-6
KEY_TILE = 128
SEL_TILE = 512
V_ROWS = HEAD_DIM + BF16_ROWS
LOG2E = math.log2(math.e)
MAX_SAFE_SCORE_BOUND = 56.0


def _cparams(sem, vmem=None):
    return pltpu.CompilerParams(dimension_semantics=sem, vmem_limit_bytes=vmem)


def _rms(x, g):
    ms = jnp.mean(x * x, axis=-1, keepdims=True)
    return x * lax.rsqrt(ms + EPS) * g


def _dot(a, b):
    return jnp.dot(a, b, preferred_element_type=F32)


def _dot_nt(a, b):
    return lax.dot_general(a, b, (((1,), (1,)), ((), ())), preferred_element_type=F32)


def _dot_f32(a, b):
    return jnp.dot(a, b, preferred_element_type=F32, precision=lax.Precision.HIGHEST)


def _gelu_tanh(x):
    k1 = -2.0 * math.sqrt(2.0 / math.pi) * LOG2E
    return x / (1.0 + jnp.exp2(x * (k1 + (k1 * 0.044715) * (x * x))))


def _norm_matmul_kernel(x_ref, g_ref, w_ref, o_ref):
    xn = _rms(x_ref[...], g_ref[...]).astype(BF16)
    o_ref[...] = _dot(xn, w_ref[...]).astype(o_ref.dtype)


def norm_matmul(x, g, w, tm=1024, out_dtype=F32):
    n, d = x.shape
    nc = w.shape[1]
    return pl.pallas_call(
        _norm_matmul_kernel,
        out_shape=jax.ShapeDtypeStruct((n, nc), out_dtype),
        grid=(n // tm,),
        in_specs=[pl.BlockSpec((tm, d), lambda i: (i, 0)),
                  pl.BlockSpec((1, d), lambda i: (0, 0)),
                  pl.BlockSpec((d, nc), lambda i: (0, 0))],
        out_specs=pl.BlockSpec((tm, nc), lambda i: (i, 0)),
        compiler_params=_cparams(("parallel",), VMEM_LIMIT_V7X),
        name="norm_matmul",
    )(x, g.reshape(1, d), w)


def _retention_kernel(inv_ref, q_ref, k_ref, v_ref, g_ref, gain_ref, o_ref, r_ref):
    c = pl.program_id(0)
    nb = q_ref.shape[0]
    L = R_CHUNK

    @pl.when(c == 0)
    def _():
        r_ref[...] = jnp.zeros_like(r_ref)

    width = R_HEADS * R_DK
    pos = (c * L + lax.broadcasted_iota(jnp.int32, (L, LANES_V7X), 0)).astype(F32)
    lane1 = lax.broadcasted_iota(jnp.int32, (L, LANES_V7X), 1)
    ang = pos * inv_ref[:, 0:LANES_V7X]
    sn = jnp.sin(ang)
    cos = jnp.concatenate([jnp.cos(ang)] * (width // LANES_V7X), axis=1)
    sin = jnp.concatenate([jnp.where(lane1 % R_DK < R_DK // 2, -sn, sn)] * (width // LANES_V7X), axis=1)
    lane = lax.broadcasted_iota(jnp.int32, cos.shape, 1)
    first_half = lane % R_DK < R_DK // 2

    def rot(x):
        swapped = jnp.where(first_half, pltpu.roll(x, width - R_DK // 2, 1), pltpu.roll(x, R_DK // 2, 1))
        return x * cos + swapped * sin

    gain = gain_ref[...]
    ri = lax.broadcasted_iota(jnp.int32, (L, L), 0)
    ci = lax.broadcasted_iota(jnp.int32, (L, L), 1)
    diff = (ri - ci).astype(F32)
    causal = ri >= ci
    idx = lax.broadcasted_iota(jnp.int32, (L, 1), 0).astype(F32)
    decays = []
    for h in range(R_HEADS):
        log_g = math.log1p(-2.0 ** (-5.0 - h))
        decays.append(dict(
            dmask=jnp.where(causal, jnp.exp(jnp.where(causal, diff, 0.0) * log_g), 0.0),
            q_dec=jnp.exp((idx + 1.0) * log_g), k_dec=jnp.exp((L - 1.0 - idx) * log_g),
            c_dec=math.exp(L * log_g)))

    chains = []
    for bi in range(nb):
        q = rot(q_ref[bi])
        k = rot(k_ref[bi]) * (R_DK ** -0.5)
        for h in range(R_HEADS):
            chains.append(dict(bi=bi, h=h, u=bi * R_HEADS + h,
                               qb=q[:, h * R_DK:(h + 1) * R_DK].astype(BF16),
                               kh=k[:, h * R_DK:(h + 1) * R_DK]))

    for ch in chains:
        ch['rstate'] = r_ref[ch['u']]
        ch['s'] = _dot_nt(ch['qb'], ch['kh'].astype(BF16)) * decays[ch['h']]['dmask']
        ch['qr'] = _dot(ch['qb'], ch['rstate'].astype(BF16))

    for ch in chains:
        d = decays[ch['h']]
        vh = v_ref[ch['bi'], :, ch['h'] * R_DV:(ch['h'] + 1) * R_DV].astype(BF16)
        ch['o'] = _dot(ch['s'].astype(BF16), vh) + ch['qr'] * d['q_dec']
        kd = (ch['kh'] * d['k_dec']).T.astype(BF16)
        r_ref[ch['u']] = d['c_dec'] * ch['rstate'] + _dot(kd, vh)

    for ch in chains:
        sl = slice(ch['h'] * R_DV, (ch['h'] + 1) * R_DV)
        gh = g_ref[ch['bi'], :, sl]
        o_ref[ch['bi'], :, sl] = (_rms(ch['o'], gain[:, sl]) * (gh * jax.nn.sigmoid(gh))).astype(o_ref.dtype)


def retention(z3, gain):
    b, t, _ = z3.shape
    nc = t // R_CHUNK
    qk_w = R_HEADS * R_DK
    v_w = R_HEADS * R_DV
    half = R_DK // 2
    inv = ROPE_BASE ** (-jnp.arange(half, dtype=F32) / half)
    inv = jnp.tile(inv, 2 * R_HEADS).reshape(1, qk_w)
    return pl.pallas_call(
        _retention_kernel,
        out_shape=jax.ShapeDtypeStruct((b, t, v_w), BF16),
        grid=(nc,),
        in_specs=[pl.BlockSpec((1, qk_w), lambda c: (0, 0)),
                  pl.BlockSpec((b, R_CHUNK, qk_w), lambda c: (0, c, 0)),
                  pl.BlockSpec((b, R_CHUNK, qk_w), lambda c: (0, c, 1)),
                  pl.BlockSpec((b, R_CHUNK, v_w), lambda c: (0, c, 1)),
                  pl.BlockSpec((b, R_CHUNK, v_w), lambda c: (0, c, 2)),
                  pl.BlockSpec((1, v_w), lambda c: (0, 0))],
        out_specs=pl.BlockSpec((b, R_CHUNK, v_w), lambda c: (0, c, 0)),
        scratch_shapes=[pltpu.VMEM((b * R_HEADS, R_DK, R_DV), F32)],
        compiler_params=_cparams(("arbitrary",)),
        name="retention",
    )(inv, z3, z3, z3, z3, gain.reshape(1, v_w))


def _mlstm_kernel(q_ref, k_ref, v_ref, og_ref, gc_ref, cw_ref, cb_ref, bc_ref, br_ref, gain_ref,
                  o_ref, xbuf, c_ref, n_ref, m_ref):
    c = pl.program_id(0)
    nb = q_ref.shape[0]
    L = M_CHUNK
    H = M_HEADS
    qk_w = H * M_DK
    halo = 8

    @pl.when(c == 0)
    def _():
        xbuf[:, 0:halo, :] = jnp.zeros((nb, halo, 2 * qk_w), F32)
        c_ref[...] = jnp.zeros_like(c_ref)
        n_ref[...] = jnp.zeros_like(n_ref)
        m_ref[...] = jnp.zeros_like(m_ref)

    gain = gain_ref[...]
    ri = lax.broadcasted_iota(jnp.int32, (L, L), 0)
    ci = lax.broadcasted_iota(jnp.int32, (L, L), 1)
    causal = ri >= ci
    tril = causal.astype(F32)
    triu = (ri <= ci).astype(F32)

    chains = []
    for bi in range(nb):
        xbuf[bi, halo:halo + L, 0:qk_w] = q_ref[bi]
        xbuf[bi, halo:halo + L, qk_w:2 * qk_w] = k_ref[bi]
        conv = cb_ref[...]
        for j in range(M_CONV):
            conv = conv + xbuf[bi, pl.ds(halo - (M_CONV - 1) + j, L), :] * cw_ref[j:j + 1, :]
        tail = xbuf[bi, L:L + halo, :]
        xbuf[bi, 0:halo, :] = tail
        act = conv * jax.nn.sigmoid(conv)
        q = act[:, 0:qk_w] * (M_DK ** -0.5)
        k = act[:, qk_w:2 * qk_w]
        gblock = gc_ref[bi]
        gc = gblock[:, 0:2 * H] + bc_ref[...]
        gr = gblock.T[0:2 * H, :] + br_ref[...]
        ig_c = gc[:, 0:H]
        ig_r = gr[0:H, :]
        b_c = _dot_f32(tril, jax.nn.log_sigmoid(gc[:, H:2 * H]))
        b_r = _dot_f32(jax.nn.log_sigmoid(gr[H:2 * H, :]), triu)
        for h in range(H):
            chains.append(dict(
                bi=bi, h=h, u=bi * H + h,
                qh=q[:, h * M_DK:(h + 1) * M_DK], kh=k[:, h * M_DK:(h + 1) * M_DK],
                bh=b_c[:, h:h + 1], brow=b_r[h:h + 1, :], irow=ig_r[h:h + 1, :], icol=ig_c[:, h:h + 1]))

    def stack(parts):
        return jnp.concatenate(parts, axis=0)

    def rows(x):
        return jnp.broadcast_to(x, (L, x.shape[1]))

    m_prev_u = [m_ref[ch['u']][:, 0:1] for ch in chains]
    b_last_u = [ch['bh'][L - 1:L, :] for ch in chains]
    bh = stack([ch['bh'] for ch in chains])
    icol = stack([ch['icol'] for ch in chains])
    brow = stack([rows(ch['brow']) for ch in chains])
    irow = stack([rows(ch['irow']) for ch in chains])
    m_prev = stack([rows(m) for m in m_prev_u])
    b_last = stack([rows(x) for x in b_last_u])
    causal_all = stack([causal] * len(chains))
    qs = stack([ch['qh'] for ch in chains])
    ks = stack([ch['kh'] for ch in chains])
    qb = qs.astype(BF16)
    kb = ks.astype(BF16)
    cstates = [c_ref[ch['u']] for ch in chains]
    nstates = [n_ref[ch['u']] for ch in chains]

    def chain_rows(x, i):
        return x[i * L:(i + 1) * L]

    s_raw = stack([_dot_nt(chain_rows(qb, i), chain_rows(kb, i)) for i in range(len(chains))])
    qc = stack([_dot(chain_rows(qb, i), cstates[i].astype(BF16)) for i in range(len(chains))])

    glog = jnp.where(causal_all, irow - brow, NEG)
    c = jnp.maximum(m_prev, jnp.max(glog, axis=-1, keepdims=True))
    m_t = bh + c
    s = s_raw * jnp.exp(glog - c)
    w_inter = jnp.exp(m_prev - c)
    wlog = b_last - bh + icol
    m_new_u = [jnp.maximum(b_last_u[i] + m_prev_u[i], jnp.max(chain_rows(wlog, i), axis=0, keepdims=True))
               for i in range(len(chains))]
    m_new = stack([rows(m) for m in m_new_u])
    wk = ks * jnp.exp(wlog - m_new)

    sb = s.astype(BF16)
    vhs = [v_ref[ch['bi'], :, ch['h'] * M_DV:(ch['h'] + 1) * M_DV].astype(BF16) for ch in chains]
    sv = stack([_dot(chain_rows(sb, i), vhs[i]) for i in range(len(chains))])
    kv = [_dot(chain_rows(wk, i).T.astype(BF16), vhs[i]) for i in range(len(chains))]

    qn = jnp.sum(qs * stack([rows(n) for n in nstates]), axis=-1, keepdims=True)
    den = jnp.sum(s, axis=-1, keepdims=True) + w_inter * qn
    hh = (sv + w_inter * qc) / jnp.maximum(jnp.abs(den), jnp.exp(-m_t))
    og = stack([og_ref[ch['bi'], :, ch['h'] * M_DV:(ch['h'] + 1) * M_DV] for ch in chains])
    gains = stack([rows(gain[:, ch['h'] * M_DV:(ch['h'] + 1) * M_DV]) for ch in chains])
    out = jax.nn.sigmoid(og) * _rms(hh, gains)

    for i, ch in enumerate(chains):
        u = ch['u']
        decay = jnp.exp(b_last_u[i] + m_prev_u[i] - m_new_u[i])
        c_ref[u] = decay * cstates[i] + kv[i]
        n_ref[u] = decay * nstates[i] + jnp.sum(chain_rows(wk, i), axis=0, keepdims=True)
        m_ref[u] = jnp.broadcast_to(m_new_u[i], (1, LANES_V7X))
        o_ref[ch['bi'], :, ch['h'] * M_DV:(ch['h'] + 1) * M_DV] = chain_rows(out, i).astype(o_ref.dtype)


def mlstm(z3, conv_w, conv_b, ig_b, fg_b, gain):
    b, t, _ = z3.shape
    nc = t // M_CHUNK
    H = M_HEADS
    qk_w = H * M_DK
    v_w = H * M_DV
    bias = jnp.concatenate([ig_b, fg_b])
    gate_blk = AB_COLS_PAD // LANES_V7X - 1
    return pl.pallas_call(
        _mlstm_kernel,
        out_shape=jax.ShapeDtypeStruct((b, t, v_w), BF16),
        grid=(nc,),
        in_specs=[pl.BlockSpec((b, M_CHUNK, qk_w), lambda c: (0, c, 6)),
                  pl.BlockSpec((b, M_CHUNK, qk_w), lambda c: (0, c, 7)),
                  pl.BlockSpec((b, M_CHUNK, v_w), lambda c: (0, c, 4)),
                  pl.BlockSpec((b, M_CHUNK, v_w), lambda c: (0, c, 5)),
                  pl.BlockSpec((b, M_CHUNK, LANES_V7X), lambda c: (0, c, gate_blk)),
                  pl.BlockSpec((M_CONV, 2 * qk_w), lambda c: (0, 0)),
                  pl.BlockSpec((1, 2 * qk_w), lambda c: (0, 0)),
                  pl.BlockSpec((1, 2 * H), lambda c: (0, 0)),
                  pl.BlockSpec((2 * H, 1), lambda c: (0, 0)),
                  pl.BlockSpec((1, v_w), lambda c: (0, 0))],
        out_specs=pl.BlockSpec((b, M_CHUNK, v_w), lambda c: (0, c, 0)),
        scratch_shapes=[pltpu.VMEM((b, 8 + M_CHUNK, 2 * qk_w), F32),
                        pltpu.VMEM((b * H, M_DK, M_DV), F32),
                        pltpu.VMEM((b * H, 1, M_DK), F32),
                        pltpu.VMEM((b * H, 1, LANES_V7X), F32)],
        compiler_params=_cparams(("arbitrary",)),
        name="mlstm",
    )(z3, z3, z3, z3, z3, conv_w, conv_b.reshape(1, -1), bias.reshape(1, -1), bias.reshape(-1, 1),
      gain.reshape(1, v_w))


def _mix_ffn_ple_kernel(*refs, n_mix, tm, seq, tf):
    hp_ref, h_ref = refs[0], refs[1]
    mix_refs = refs[2:2 + 2 * n_mix]
    (wo_ref, g_ref, wa_ref, wb_ref, cw_ref, cb_ref, wd_ref, p_ref, wp_ref, ng_ref, gg_ref, wg_ref,
     o_ref, xn_ref, a_ref) = refs[2 + 2 * n_mix:]
    i = pl.program_id(0)
    halo = 8
    g = g_ref[...]
    m_prev = jnp.concatenate([mix_refs[2 * k][...] for k in range(n_mix)], axis=1)
    m_tile = jnp.concatenate([mix_refs[2 * k + 1][...] for k in range(n_mix)], axis=1)
    x = h_ref[...] + _dot(m_tile, wo_ref[...])
    xp = hp_ref[...] + _dot(m_prev, wo_ref[...])[BF16_ROWS - halo:, :]
    xn_ref[halo:halo + tm, :] = _rms(x, g).astype(BF16)
    keep = ((i * tm) % seq != 0).astype(F32)
    xn_ref[0:halo, :] = (_rms(xp, g) * keep).astype(BF16)
    n_chunks = D_FF // tf

    def up_proj(c):
        cs = slice(c * tf, (c + 1) * tf)
        a_ref[c % 2] = _dot(xn_ref[...], wa_ref[:, cs])
        return _dot(xn_ref[halo:halo + tm, :], wb_ref[:, cs])

    h2 = x
    bgate = up_proj(0)
    for c in range(n_chunks):
        cs = slice(c * tf, (c + 1) * tf)
        bgate_next = up_proj(c + 1) if c + 1 < n_chunks else None
        conv = cb_ref[:, cs]
        for t in range(FFN_CONV):
            conv = conv + a_ref[c % 2, pl.ds(halo - (FFN_CONV - 1) + t, tm), :] * cw_ref[t:t + 1, cs]
        act = (_gelu_tanh(conv) * bgate).astype(BF16)
        h2 = h2 + _dot(act, wd_ref[cs, :])
        bgate = bgate_next
    e = _rms(_dot(p_ref[...].astype(BF16), wp_ref[...]), ng_ref[...])
    gate = jax.nn.sigmoid(_dot(_rms(h2, gg_ref[...]).astype(BF16), wg_ref[...]))
    o_ref[...] = h2 + gate * e


def mix_ffn_ple(h, mix, li, g, w_up, conv_w, conv_b, w_down, p, wp, norm_g, gate_norm_g, wg, seq, tm=512, tf=2816):
    mix_outs, w_out = mix
    n, d = h.shape
    pd = p.shape[-1]
    hb = tm // 8
    const = lambda i: (0, 0)
    layer = lambda i: (li, 0, 0)
    prev8 = lambda i: (jnp.maximum(i * hb - 1, 0), 0)
    resident = dict(pipeline_mode=pl.Buffered(1))
    prev16 = lambda i: (jnp.maximum(i * (tm // BF16_ROWS) - 1, 0), 0)
    mix_specs, mix_args = [], []
    for m in mix_outs:
        mix_specs += [pl.BlockSpec((BF16_ROWS, m.shape[1]), prev16),
                      pl.BlockSpec((tm, m.shape[1]), lambda i: (i, 0))]
        mix_args += [m, m]
    mix_specs.append(pl.BlockSpec(w_out.shape, const, **resident))
    mix_args.append(w_out)
    return pl.pallas_call(
        functools.partial(_mix_ffn_ple_kernel, n_mix=len(mix_outs), tm=tm, seq=seq, tf=tf),
        out_shape=jax.ShapeDtypeStruct((n, d), F32),
        grid=(n // tm,),
        in_specs=[pl.BlockSpec((8, d), prev8),
                  pl.BlockSpec((tm, d), lambda i: (i, 0))] + mix_specs + [
                  pl.BlockSpec((None, 1, d), layer),
                  pl.BlockSpec((None, d, D_FF), lambda i: (li, 0, 0), **resident),
                  pl.BlockSpec((None, d, D_FF), lambda i: (li, 0, 1), **resident),
                  pl.BlockSpec((None, FFN_CONV, D_FF), layer),
                  pl.BlockSpec((None, 1, D_FF), layer),
                  pl.BlockSpec((None, D_FF, d), layer, **resident),
                  pl.BlockSpec((None, tm, pd), lambda i: (li, i, 0)),
                  pl.BlockSpec((None, pd, d), layer, **resident),
                  pl.BlockSpec((None, 1, d), layer),
                  pl.BlockSpec((None, 1, d), layer),
                  pl.BlockSpec((None, d, d), layer, **resident)],
        out_specs=pl.BlockSpec((tm, d), lambda i: (i, 0)),
        scratch_shapes=[pltpu.VMEM((8 + tm, d), BF16),
                        pltpu.VMEM((min(2, D_FF // tf), 8 + tm, tf), F32)],
        compiler_params=_cparams(("parallel",), VMEM_LIMIT_V7X),
        name="mix_ffn_ple",
    )(h, h, *mix_args, g[:, None, :], w_up, w_up, conv_w, conv_b[:, None, :], w_down,
      p, wp, norm_g[:, None, :], gate_norm_g[:, None, :], wg)


def _group_rms(x, g):
    lane = lax.broadcasted_iota(jnp.int32, x.shape, 1)
    x2 = x * x
    ms = jnp.zeros_like(x)
    for grp in range(N_KV_GROUPS):
        in_grp = (lane >= grp * HEAD_DIM) & (lane < (grp + 1) * HEAD_DIM)
        tot = jnp.sum(jnp.where(in_grp, x2, 0.0), axis=-1, keepdims=True)
        ms = jnp.where(in_grp, tot * (1.0 / HEAD_DIM), ms)
    return x * lax.rsqrt(ms + EPS) * g


def _ones_rows(width):
    return (lax.broadcasted_iota(jnp.int32, (BF16_ROWS, width), 0) == 0).astype(BF16)


def _kv_prep_kernel(c_ref, s_ref, w_ref, gs_ref, gw_ref, kc_ref, vc_ref, ks_ref, vs_ref, kw_ref, vw_ref):
    gw = N_KV_GROUPS * HEAD_DIM
    cc = c_ref[...]
    ss = s_ref[...]
    ww = w_ref[...]
    kc_ref[...] = cc[:, 0:gw]
    vc_ref[...] = cc[:, gw:2 * gw]
    ks = _group_rms(ss[:, 0:gw], gs_ref[...]).astype(BF16)
    kw = _group_rms(ww[:, 0:gw], gw_ref[...]).astype(BF16)
    vst = ss[:, gw:2 * gw].T
    vwt = ww[:, gw:2 * gw].T
    for g in range(N_KV_GROUPS):
        lanes = slice(g * HEAD_DIM, (g + 1) * HEAD_DIM)
        ks_ref[g] = ks[:, lanes]
        kw_ref[g] = kw[:, lanes]
        vs_ref[g] = jnp.concatenate([vst[lanes, :].astype(BF16), _ones_rows(SEL_TILE)], axis=0)
        for u in range(SEL_TILE // KEY_TILE):
            vw_ref[g, u] = jnp.concatenate([vwt[lanes, u * KEY_TILE:(u + 1) * KEY_TILE].astype(BF16),
                                            _ones_rows(KEY_TILE)], axis=0)


def kv_prep(z, k_g, b, t):
    n = z.shape[0]
    G = N_KV_GROUPS
    gw = G * HEAD_DIM
    base = N_HEADS * HEAD_DIM // (2 * gw)
    tm = SEL_TILE
    nst = t // tm
    sub = SEL_TILE // KEY_TILE
    row = lambda i, j: i * nst + j
    flat = jax.ShapeDtypeStruct((n, gw), F32)
    keys = jax.ShapeDtypeStruct((b, G, t, HEAD_DIM), BF16)
    flat_spec = pl.BlockSpec((tm, gw), lambda i, j: (row(i, j), 0))
    key_spec = pl.BlockSpec((None, G, tm, HEAD_DIM), lambda i, j: (i, 0, j, 0))
    return pl.pallas_call(
        _kv_prep_kernel,
        out_shape=(flat, flat, keys, jax.ShapeDtypeStruct((b, G, nst, V_ROWS, tm), BF16),
                   keys, jax.ShapeDtypeStruct((b, G, nst * sub, V_ROWS, KEY_TILE), BF16)),
        grid=(b, nst),
        in_specs=[pl.BlockSpec((tm, 2 * gw), lambda i, j: (row(i, j), base)),
                  pl.BlockSpec((tm, 2 * gw), lambda i, j: (row(i, j), base + 1)),
                  pl.BlockSpec((tm, 2 * gw), lambda i, j: (row(i, j), base + 2)),
                  pl.BlockSpec((1, gw), lambda i, j: (0, 0)),
                  pl.BlockSpec((1, gw), lambda i, j: (0, 0))],
        out_specs=(flat_spec, flat_spec, key_spec,
                   pl.BlockSpec((None, G, None, V_ROWS, tm), lambda i, j: (i, 0, j, 0, 0)),
                   key_spec,
                   pl.BlockSpec((None, G, sub, V_ROWS, KEY_TILE), lambda i, j: (i, 0, j, 0, 0))),
        compiler_params=_cparams(("parallel", "parallel")),
        name="kv_prep",
    )(z, z, z, jnp.tile(k_g[1], G).reshape(1, gw), jnp.tile(k_g[2], G).reshape(1, gw))


def _compress_kernel(x_ref, pos_ref, w1_ref, w2_ref, g_ref, o_ref, *, normalize, transposed):
    G = N_KV_GROUPS
    ncb = x_ref.shape[0] // CMP_STRIDE
    half = CMP_STRIDE * HEAD_DIM
    u = [jnp.zeros((ncb, CMP_HIDDEN), F32) for _ in range(G)]
    v = [jnp.zeros((ncb, CMP_HIDDEN), F32) for _ in range(G)]
    for r in range(CMP_STRIDE):
        xr = x_ref[pl.ds(r, ncb, stride=CMP_STRIDE), :].astype(BF16)
        wa = w1_ref[r * HEAD_DIM:(r + 1) * HEAD_DIM, :]
        wb = w1_ref[half + r * HEAD_DIM:half + (r + 1) * HEAD_DIM, :]
        for grp in range(G):
            xg = xr[:, grp * HEAD_DIM:(grp + 1) * HEAD_DIM]
            u[grp] = u[grp] + _dot(xg, wa)
            v[grp] = v[grp] + _dot(xg, wb)
    posc = _dot(pos_ref[...], w1_ref[...])[0:1, :]
    outs = []
    for grp in range(G):
        hid = u[grp] + pltpu.roll(v[grp], ncb - 1, 0) + posc
        out = _dot(jax.nn.gelu(hid).astype(BF16), w2_ref[...])
        outs.append(_rms(out, g_ref[...]) if normalize else out)
    if transposed:
        both = jnp.concatenate(outs, axis=1).T
        for grp in range(G):
            o_ref[grp] = both[grp * HEAD_DIM:(grp + 1) * HEAD_DIM, :].astype(o_ref.dtype)
    else:
        for grp in range(G):
            o_ref[grp] = outs[grp].astype(o_ref.dtype)


def compress(x3, pos, w1, w2, g, normalize, transposed):
    b, t, gw = x3.shape
    G = N_KV_GROUPS
    ncb = t // CMP_STRIDE
    kdim = CMP_STRIDE * HEAD_DIM
    posf = jnp.broadcast_to(pos.reshape(1, -1), (8, 2 * kdim)).astype(BF16)
    out_sds = (jax.ShapeDtypeStruct((b, G, HEAD_DIM, ncb), BF16) if transposed
               else jax.ShapeDtypeStruct((b, G, ncb, HEAD_DIM), BF16))
    out_block = (None, G, HEAD_DIM, ncb) if transposed else (None, G, ncb, HEAD_DIM)
    return pl.pallas_call(
        functools.partial(_compress_kernel, normalize=normalize, transposed=transposed),
        out_shape=out_sds,
        grid=(b,),
        in_specs=[pl.BlockSpec((None, t, gw), lambda i: (i, 0, 0)),
                  pl.BlockSpec((8, 2 * kdim), lambda i: (0, 0)),
                  pl.BlockSpec((2 * kdim, CMP_HIDDEN), lambda i: (0, 0)),
                  pl.BlockSpec((CMP_HIDDEN, HEAD_DIM), lambda i: (0, 0)),
                  pl.BlockSpec((1, HEAD_DIM), lambda i: (0, 0))],
        out_specs=pl.BlockSpec(out_block, lambda i: (i, 0, 0, 0)),
        compiler_params=_cparams(("parallel",)),
        name="compress",
    )(x3, posf, w1.astype(BF16), w2.astype(BF16), g.reshape(1, HEAD_DIM))


def _nsa_kernel(bnd_ref, *refs):
    safe = bnd_ref[0, 0] <= MAX_SAFE_SCORE_BOUND

    @pl.when(safe)
    def _():
        _nsa_body(bnd_ref, *refs, bounded=True)

    @pl.when(jnp.logical_not(safe))
    def _():
        _nsa_body(bnd_ref, *refs, bounded=False)


def _nsa_body(bnd_ref, zq_ref, gt_ref, gb_ref, qg_ref, ovt_ref, kc_ref, vct_ref, ks_ref, vst_ref, kw_ref,
              vwt_ref, o_ref, q_scr, sel_scr, m_scr, acc_scr, oc_scr, *, bounded):
    grp = pl.program_id(1)
    qb = pl.program_id(2)
    QB = Q_BLOCK
    ncp = kc_ref.shape[0]
    ns = ovt_ref.shape[0]
    tpos = qb * QB + lax.broadcasted_iota(jnp.int32, (1, QB), 1)

    def lane_tile(x):
        return jnp.concatenate([x] * HG, axis=1)

    zt = zq_ref[...].T
    qg = qg_ref[...] * (HEAD_DIM ** -0.5 * LOG2E)
    heads = []
    for h in range(HG):
        xh = zt[h * HEAD_DIM:(h + 1) * HEAD_DIM, :]
        ms = jnp.mean(xh * xh, axis=0, keepdims=True)
        heads.append(xh * lax.rsqrt(ms + EPS) * qg)
    q_scr[...] = jnp.concatenate(heads, axis=1).astype(BF16)
    q = q_scr[...]

    gates = jax.nn.sigmoid(gt_ref[...].T + gb_ref[...])
    per_grp = HG * 3
    gsel = gates[0:per_grp, :]
    for g2 in range(1, N_KV_GROUPS):
        gsel = jnp.where(grp == g2, gates[g2 * per_grp:(g2 + 1) * per_grp, :], gsel)

    def gate_row(c):
        return jnp.concatenate([gsel[3 * h + c:3 * h + c + 1, :] for h in range(HG)], axis=1)

    cmp_end = lax.broadcasted_iota(jnp.int32, (ncp, 1), 0) * CMP_STRIDE + (CMP_BLOCK - 1)
    keep = -bnd_ref[0, 0] if bounded else 0.0
    cbias = jnp.where(cmp_end <= tpos, keep, NEG)
    s = _dot(kc_ref[...], q) + lane_tile(cbias)
    e = jnp.exp2(s) if bounded else jnp.exp2(s - jnp.max(s, axis=0, keepdims=True))
    cmp_lhs = jnp.concatenate([vct_ref[...], _ones_rows(ncp), ovt_ref[...]], axis=0)
    r = _dot(cmp_lhs, e.astype(BF16))
    inv = jnp.where(lane_tile(tpos) >= CMP_BLOCK - 1, 1.0 / r[HEAD_DIM:HEAD_DIM + 1, :], 0.0)
    ocmp = r[0:HEAD_DIM, :] * inv
    imp_h = r[V_ROWS:V_ROWS + ns, :] * inv
    imp = imp_h[:, 0:QB]
    for h in range(1, HG):
        imp = imp + imp_h[:, h * QB:(h + 1) * QB]

    n_win = (WINDOW + QB) // KEY_TILE
    first_tile = qb * (QB // KEY_TILE) - WINDOW // KEY_TILE
    win_sub = lax.broadcasted_iota(jnp.int32, (n_win * KEY_TILE, 1), 0)
    tiles = [jnp.maximum(first_tile + u, 0) for u in range(n_win)]
    kwin = jnp.concatenate([kw_ref[j] for j in tiles], axis=0)
    vwin = jnp.concatenate([vwt_ref[j] for j in tiles], axis=1)
    kpos = first_tile * KEY_TILE + win_sub
    wbias = jnp.where((kpos <= tpos) & (kpos > tpos - WINDOW) & (kpos >= 0), keep, NEG)
    sw = _dot(kwin, q) + lane_tile(wbias)
    pw = jnp.exp2(sw) if bounded else jnp.exp2(sw - jnp.max(sw, axis=0, keepdims=True))
    ow = _dot(vwin, pw.astype(BF16))
    oc_scr[...] = gate_row(0) * ocmp + (gate_row(2) / ow[HEAD_DIM:HEAD_DIM + 1, :]) * ow[0:HEAD_DIM, :]

    blk = lax.broadcasted_iota(jnp.int32, (ns, 1), 0)
    blk_f = blk.astype(F32)
    cur = jnp.right_shift(tpos, SLC_BLOCK.bit_length() - 1)
    forced = (blk == 0) | (blk == cur) | (blk == cur - 1)
    bvalid = blk <= cur
    score = jnp.where(forced, -jnp.inf, jnp.where(bvalid, imp, NEG))
    sel = jnp.where(forced, 1.0, 0.0)
    for _ in range(max(min(N_SELECT, ns) - 3, 0)):
        mx = jnp.max(score, axis=0, keepdims=True)
        first = jnp.min(jnp.where(score == mx, blk_f, float(ns)), axis=0, keepdims=True)
        pick = blk_f == first
        sel = jnp.where(pick, 1.0, sel)
        score = jnp.where(pick, -jnp.inf, score)
    sel_scr[...] = jnp.where(bvalid, sel, 0.0)

    m_scr[...] = jnp.full(m_scr.shape, NEG, F32)
    acc_scr[...] = jnp.zeros(acc_scr.shape, F32)
    blocks_per_tile = SEL_TILE // SLC_BLOCK

    def sel_step(first, width):
        keys = width * SEL_TILE
        rows = [jnp.broadcast_to(sel_scr[pl.ds(first * blocks_per_tile + bi, 1), :], (SLC_BLOCK, QB))
                for bi in range(width * blocks_per_tile)]
        chosen = jnp.concatenate(rows, axis=0)
        kpos = first * SEL_TILE + lax.broadcasted_iota(jnp.int32, (keys, 1), 0)
        bias = jnp.where((chosen > 0.5) & (kpos <= tpos), keep, NEG)
        k = jnp.concatenate([ks_ref[first + w] for w in range(width)], axis=0)
        vt = jnp.concatenate([vst_ref[first + w] for w in range(width)], axis=1)
        st = _dot(k, q_scr[...]) + lane_tile(bias)
        if bounded:
            pt = jnp.exp2(st)
            acc_scr[...] += _dot(vt, pt.astype(BF16))
        else:
            m_old = m_scr[...]
            m_new = jnp.maximum(m_old, jnp.max(st, axis=0, keepdims=True))
            alpha = jnp.exp2(m_old - m_new)
            pt = jnp.exp2(st - m_new)
            acc_scr[...] = alpha * acc_scr[...] + _dot(vt, pt.astype(BF16))
            m_scr[...] = m_new

    n_tiles = qb // (SEL_TILE // QB) + 1

    def sel_pair(i, carry):
        sel_step(2 * i, 2)
        return carry

    lax.fori_loop(0, n_tiles // 2, sel_pair, 0)

    @pl.when(n_tiles % 2 == 1)
    def _():
        sel_step(n_tiles - 1, 1)
    ot = (oc_scr[...]
          + (gate_row(1) / acc_scr[HEAD_DIM:HEAD_DIM + 1, :]) * acc_scr[0:HEAD_DIM, :])
    stacked = jnp.concatenate([ot[:, h * QB:(h + 1) * QB] for h in range(HG)], axis=0)
    o_ref[...] = stacked.T.astype(o_ref.dtype)


def nsa_attention(bound, z3, gate_b, q_g, overlap_t, kc, vct, ks, vst, kw, vwt):
    b, t, _ = z3.shape
    nq = t // Q_BLOCK
    nt = t // KEY_TILE
    qw = HG * HEAD_DIM
    ns, ncp = overlap_t.shape
    gate_blk = (NSA_COLS_PAD // LANES_V7X) - 1
    gb = jnp.zeros((LANES_V7X, 1), F32).at[:N_HEADS * 3, 0].set(gate_b)
    full5 = lambda i, g, q: (i, g, 0, 0, 0)
    lanes = HG * Q_BLOCK
    return pl.pallas_call(
        _nsa_kernel,
        out_shape=jax.ShapeDtypeStruct((b, t, N_HEADS * HEAD_DIM), BF16),
        grid=(b, N_KV_GROUPS, nq),
        in_specs=[pl.BlockSpec(memory_space=pltpu.SMEM),
                  pl.BlockSpec((None, Q_BLOCK, qw), lambda i, g, q: (i, q, g)),
                  pl.BlockSpec((None, Q_BLOCK, LANES_V7X), lambda i, g, q: (i, q, gate_blk)),
                  pl.BlockSpec((LANES_V7X, 1), lambda i, g, q: (0, 0)),
                  pl.BlockSpec((HEAD_DIM, 1), lambda i, g, q: (0, 0)),
                  pl.BlockSpec((ns, ncp), lambda i, g, q: (0, 0)),
                  pl.BlockSpec((None, None, ncp, HEAD_DIM), lambda i, g, q: (i, g, 0, 0)),
                  pl.BlockSpec((None, None, HEAD_DIM, ncp), lambda i, g, q: (i, g, 0, 0)),
                  pl.BlockSpec((None, None, t // SEL_TILE, SEL_TILE, HEAD_DIM), full5),
                  pl.BlockSpec((None, None, t // SEL_TILE, V_ROWS, SEL_TILE), full5),
                  pl.BlockSpec((None, None, nt, KEY_TILE, HEAD_DIM), full5),
                  pl.BlockSpec((None, None, nt, V_ROWS, KEY_TILE), full5)],
        out_specs=pl.BlockSpec((None, Q_BLOCK, qw), lambda i, g, q: (i, q, g)),
        scratch_shapes=[pltpu.VMEM((HEAD_DIM, lanes), BF16),
                        pltpu.VMEM((ns, Q_BLOCK), F32),
                        pltpu.VMEM((1, lanes), F32),
                        pltpu.VMEM((V_ROWS, lanes), F32),
                        pltpu.VMEM((HEAD_DIM, lanes), F32)],
        compiler_params=_cparams(("parallel", "parallel", "arbitrary"), VMEM_LIMIT_V7X),
        name="nsa_attention",
    )(bound.reshape(1, 1), z3, z3, gb, q_g.reshape(HEAD_DIM, 1), overlap_t, kc, vct, ks, vst, kw, vwt)


def _overlap_matrix_t(t):
    ncp = t // CMP_STRIDE
    ns = t // SLC_BLOCK
    c_start = np.arange(ncp) * CMP_STRIDE
    sj = np.arange(ns)
    ov = ((c_start[None, :] < (sj[:, None] + 1) * SLC_BLOCK)
          & (c_start[None, :] + CMP_BLOCK > sj[:, None] * SLC_BLOCK)
          & (c_start[None, :] + CMP_BLOCK <= t))
    return jnp.asarray(ov, dtype=BF16)


def ab_layer(h, b, t, norm_g, w_in, conv_w, conv_b, ret_g, ig_b, fg_b, m_g, w_out):
    n = b * t
    w_in_p = jnp.pad(w_in, ((0, 0), (0, AB_COLS_PAD - AB_COLS))).astype(BF16)
    z = norm_matmul(h, norm_g, w_in_p)
    z3 = z.reshape(b, t, AB_COLS_PAD)
    ret = retention(z3, ret_g)
    ml = mlstm(z3, conv_w, conv_b, ig_b, fg_b, m_g)
    w_out_b = w_out.astype(BF16)
    rw = R_HEADS * R_DV
    return [ret.reshape(n, rw), ml.reshape(n, -1)], w_out_b


def nsa_layer(h, b, t, norm_g, w_in, q_g, k_g, pos_k, pos_v, w1k, w2k, w1v, w2v, gate_b, w_out):
    n = b * t
    G = N_KV_GROUPS
    w_in_p = jnp.pad(w_in, ((0, 0), (0, NSA_COLS_PAD - NSA_COLS))).astype(BF16)
    z = norm_matmul(h, norm_g, w_in_p)
    kc_in, vc_in, ks, vst, kw, vwt = kv_prep(z, k_g, b, t)
    gw = G * HEAD_DIM
    kc = compress(kc_in.reshape(b, t, gw), pos_k, w1k, w2k, k_g[0], True, False)
    vct = compress(vc_in.reshape(b, t, gw), pos_v, w1v, w2v, k_g[0], False, True)

    def key_tiles(x, kt):
        return x.reshape(b, G, t // kt, kt, HEAD_DIM)

    bound = 1.02 * LOG2E * math.sqrt(HEAD_DIM) * jnp.max(jnp.abs(q_g)) * jnp.max(jnp.abs(k_g))
    args = (bound, z.reshape(b, t, NSA_COLS_PAD), gate_b, q_g, _overlap_matrix_t(t), kc, vct,
            key_tiles(ks, SEL_TILE), vst, key_tiles(kw, KEY_TILE), vwt)
    o = nsa_attention(*args)
    return [o.reshape(n, -1)], w_out.astype(BF16)


def kernel(x, p, ab_norm_g, ab_w_in, ab_conv_w, ab_conv_b, ab_ret_norm_g, ab_ig_b, ab_fg_b, ab_m_norm_g, ab_w_out, nsa_norm_g, nsa_w_in, nsa_q_norm_g, nsa_k_norm_g, nsa_cmp_pos_k, nsa_cmp_pos_v, nsa_cmp_w1k, nsa_cmp_w2k, nsa_cmp_w1v, nsa_cmp_w2v, nsa_gate_b, nsa_w_out, ffn_norm_g, ffn_w_up, ffn_conv_w, ffn_conv_b, ffn_w_down, ple_w, ple_norm_g, ple_gate_norm_g, ple_w_gate):
    b, t, d = x.shape
    n = b * t
    depth = p.shape[0]
    h = x.reshape(n, d)
    p2 = p.reshape(depth, n, -1)
    ffn_w_up_b, ffn_w_down_b = ffn_w_up.astype(BF16), ffn_w_down.astype(BF16)
    ple_w_b, ple_w_gate_b = ple_w.astype(BF16), ple_w_gate.astype(BF16)
    for i in range(depth):
        j = i // 2
        if i % 2 == 0:
            mix = ab_layer(h, b, t, ab_norm_g[j], ab_w_in[j], ab_conv_w[j], ab_conv_b[j], ab_ret_norm_g[j],
                           ab_ig_b[j], ab_fg_b[j], ab_m_norm_g[j], ab_w_out[j])
        else:
            mix = nsa_layer(h, b, t, nsa_norm_g[j], nsa_w_in[j], nsa_q_norm_g[j], nsa_k_norm_g[j],
                            nsa_cmp_pos_k[j], nsa_cmp_pos_v[j], nsa_cmp_w1k[j], nsa_cmp_w2k[j],
                            nsa_cmp_w1v[j], nsa_cmp_w2v[j], nsa_gate_b[j], nsa_w_out[j])
        h = mix_ffn_ple(h, mix, i, ffn_norm_g, ffn_w_up_b, ffn_conv_w, ffn_conv_b, ffn_w_down_b, p2, ple_w_b,
                        ple_norm_g, ple_gate_norm_g, ple_w_gate_b, t)
    return h.reshape(b, t, d)
```
